```python
import math
import jax, jax.numpy as jnp
from jax import lax
import numpy as np

D_MODEL = 1024
BATCH = 8
SEQ = 4096
DEPTH = 2
DEC_BATCH = 16
DEC_SEQ = 16
PAST_LEN = 4096

CHUNK = 64
N_EVEN = (DEPTH + 1) // 2
N_ODD = DEPTH // 2
EPS = 1e-6
CONV_W = 4
NEG_INF = -1e30
LRU_WIDTH = D_MODEL // 2
LRU_HEADS = 8
LRU_BLOCK = LRU_WIDTH // LRU_HEADS
LRU_C = 8.0
HEAD_DIM = 64
ATT_HEADS = (D_MODEL // 2) // HEAD_DIM
ATT_KV_HEADS = 2
ATT_GROUP = ATT_HEADS // ATT_KV_HEADS
WINDOW = 128
ROT_DIM = HEAD_DIM // 4
ROPE_THETA = 500000.0
ATT_WIDTH = ATT_HEADS * HEAD_DIM
KV_WIDTH = ATT_KV_HEADS * HEAD_DIM
EVEN_IN = 2 * LRU_WIDTH + ATT_WIDTH + 2 * KV_WIDTH + ATT_WIDTH
EVEN_MIX = LRU_WIDTH + ATT_WIDTH
EVEN_SPLITS = (LRU_WIDTH, 2 * LRU_WIDTH, 2 * LRU_WIDTH + ATT_WIDTH,
               2 * LRU_WIDTH + ATT_WIDTH + KV_WIDTH, 2 * LRU_WIDTH + ATT_WIDTH + 2 * KV_WIDTH)
GDN_DK = 128
GDN_DV = 128
GDN_HEADS = D_MODEL // GDN_DV
GDN_KW = GDN_HEADS * GDN_DK
GDN_VW = GDN_HEADS * GDN_DV
GDN_CONV_CH = 2 * GDN_KW + GDN_VW
ODD_IN = GDN_CONV_CH + GDN_VW + 2 * GDN_HEADS
ODD_SPLITS = (GDN_CONV_CH, GDN_CONV_CH + GDN_VW, GDN_CONV_CH + GDN_VW + GDN_HEADS)

kernel_name = "hybrid_streaming_encoder_step"


def rms_norm(x, g):
    xf = x.astype(jnp.float32)
    y = xf * lax.rsqrt(jnp.mean(xf * xf, axis=-1, keepdims=True) + EPS)
    return (y * g.astype(jnp.float32)).astype(x.dtype)


def l2_normalize(x):
    return x * lax.rsqrt(jnp.sum(x * x, axis=-1, keepdims=True) + EPS)


def modulate(c, mod_w, mod_b):
    mod = jnp.dot(jax.nn.silu(c), mod_w) + mod_b
    shift, scale, gate = jnp.split(mod, 3, axis=-1)
    return shift[:, None], scale[:, None], gate[:, None]


def causal_dwconv(u, buf, w):
    t = u.shape[1]
    ext = jnp.concatenate([buf.astype(u.dtype), u], axis=1)
    y = ext[:, 0:t] * w[0]
    for j in range(1, CONV_W):
        y = y + ext[:, j:j + t] * w[j]
    return y, ext[:, t:]


def partial_rope(x, pos):
    half = ROT_DIM // 2
    inv = ROPE_THETA ** (-(jnp.arange(half, dtype=jnp.float32) * 2.0 / ROT_DIM))
    ang = pos.astype(jnp.float32)[:, None] * inv[None, :]
    cos = jnp.cos(ang)[None, :, None, :]
    sin = jnp.sin(ang)[None, :, None, :]
    xf = x.astype(jnp.float32)
    x1 = xf[..., :half]
    x2 = xf[..., half:ROT_DIM]
    out = jnp.concatenate([x1 * cos - x2 * sin, x2 * cos + x1 * sin, xf[..., ROT_DIM:]], axis=-1)
    return out.astype(x.dtype)


def rg_lru(x, h0, w_a, b_a, w_x, b_x, lam):
    bsz, t, _ = x.shape
    xf = x.astype(jnp.float32)
    xb = xf.reshape(bsz, t, LRU_HEADS, LRU_BLOCK)
    r = jax.nn.sigmoid(jnp.einsum('bthi,hij->bthj', xb, w_a.astype(jnp.float32)).reshape(bsz, t, LRU_WIDTH) + b_a)
    i = jax.nn.sigmoid(jnp.einsum('bthi,hij->bthj', xb, w_x.astype(jnp.float32)).reshape(bsz, t, LRU_WIDTH) + b_x)
    log_a = -LRU_C * r * jax.nn.softplus(-lam.astype(jnp.float32))
    a = jnp.exp(log_a)
    gated_x = jnp.sqrt(-jnp.expm1(2.0 * log_a)) * (i * xf)

    def combine(lhs, rhs):
        a_l, b_l = lhs
        a_r, b_r = rhs
        return a_l * a_r, a_r * b_l + b_r

    a_cum, h_zero = lax.associative_scan(combine, (a, gated_x), axis=1)
    h = h_zero + a_cum * h0.astype(jnp.float32)[:, None]
    return h.astype(x.dtype), h[:, -1].astype(x.dtype)


def sink_window_attention(q, k, v, k_past, v_past, n_past_valid, sinks):
    bsz, t = q.shape[:2]
    cb = min(CHUNK, t)
    nb = t // cb
    m = WINDOW + cb
    k_ext = jnp.concatenate([k_past.astype(k.dtype), k], axis=1)
    v_ext = jnp.concatenate([v_past.astype(v.dtype), v], axis=1)
    idx = jnp.arange(nb)[:, None] * cb + jnp.arange(m)[None, :]
    kb = jnp.take(k_ext, idx, axis=1)
    vb = jnp.take(v_ext, idx, axis=1)
    qb = q.reshape(bsz, nb, cb, ATT_KV_HEADS, ATT_GROUP, HEAD_DIM)
    s = jnp.einsum('bnqhgd,bnmhd->bnhgqm', qb, kb,
                   preferred_element_type=jnp.float32) * (HEAD_DIM ** -0.5)
    valid = idx >= (WINDOW - n_past_valid)
    s = jnp.where(valid[None, :, None, None, None, :], s, NEG_INF)
    sink = sinks.astype(jnp.float32).reshape(ATT_KV_HEADS, ATT_GROUP)[None, None, :, :, None, None]
    mx = jnp.maximum(jnp.max(s, axis=-1, keepdims=True), sink)
    p = jnp.exp(s - mx)
    p = p / (jnp.sum(p, axis=-1, keepdims=True) + jnp.exp(sink - mx))
    o = jnp.einsum('bnhgqm,bnmhd->bnqhgd', p.astype(v.dtype), vb)
    return o.reshape(bsz, t, ATT_WIDTH)


def unit_lower_inverse(a, n):
    t = jnp.broadcast_to(jnp.eye(n, dtype=a.dtype), a.shape)
    p = -a
    n_steps = (n - 1).bit_length()
    for step in range(n_steps):
        t = t + jnp.matmul(t, p)
        if step + 1 < n_steps:
            p = jnp.matmul(p, p)
    return t


def gated_delta_rule(q, k, v, g, beta, s0):
    bsz, t, h, _ = q.shape
    dv = v.shape[-1]
    cb = min(CHUNK, t)
    nc = t // cb

    def chunks(a):
        a = a.reshape(bsz, nc, cb, h, -1)
        return jnp.transpose(a, (1, 0, 3, 2, 4))

    qc, kc, vc = chunks(q), chunks(k), chunks(v)
    gc = chunks(g[..., None])[..., 0]
    bc = chunks(beta[..., None])[..., 0]
    gcum = jnp.cumsum(gc, axis=-1)
    strict = jnp.tril(jnp.ones((cb, cb), dtype=bool), -1)
    incl = jnp.tril(jnp.ones((cb, cb), dtype=bool))
    diff = gcum[..., :, None] - gcum[..., None, :]
    decay_strict = jnp.where(strict, jnp.exp(jnp.where(strict, diff, 0.0)), 0.0)
    decay_incl = jnp.where(incl, jnp.exp(jnp.where(incl, diff, 0.0)), 0.0)
    k_beta = kc * bc[..., None]
    a_mat = jnp.einsum('nbhid,nbhjd->nbhij', k_beta, kc) * decay_strict
    t_mat = unit_lower_inverse(a_mat, cb)
    w = jnp.matmul(t_mat, k_beta * jnp.exp(gcum)[..., None])
    u = jnp.matmul(t_mat, vc * bc[..., None])
    qk = jnp.einsum('nbhid,nbhjd->nbhij', qc, kc) * decay_incl

    def step(state, inp):
        q_i, k_i, w_i, u_i, qk_i, g_i = inp
        v_new = u_i - jnp.einsum('bhck,bhkv->bhcv', w_i, state)
        o_i = (jnp.einsum('bhck,bhkv->bhcv', q_i * jnp.exp(g_i)[..., None], state)
               + jnp.einsum('bhcj,bhjv->bhcv', qk_i, v_new))
        g_last = g_i[..., -1:]
        k_dec = k_i * jnp.exp(g_last - g_i)[..., None]
        state = state * jnp.exp(g_last)[..., None] + jnp.einsum('bhck,bhcv->bhkv', k_dec, v_new)
        return state, o_i

    s_final, o = lax.scan(step, s0, (qc, kc, w, u, qk, gcum))
    o = jnp.transpose(o, (1, 0, 3, 2, 4)).reshape(bsz, t, h, dv)
    return o, s_final


def even_layer(x, c, conv_buf, h0, k_past, v_past, n_past_valid, pos0,
               mod_w, mod_b, norm_pre, norm_post, w_in, conv_w, conv_b,
               w_a, b_a, w_x, b_x, lam, sinks, w_out):
    bsz, t, _ = x.shape
    shift, scale, gate = modulate(c, mod_w, mod_b)
    hmod = rms_norm(x, norm_pre) * (1.0 + scale) + shift
    u = jnp.dot(hmod, w_in)
    x_a, g_a, q, k, v, g_b = jnp.split(u, EVEN_SPLITS, axis=-1)
    xc, new_buf = causal_dwconv(x_a, conv_buf, conv_w)
    h, h_last = rg_lru(xc + conv_b, h0, w_a, b_a, w_x, b_x, lam)
    y_a = h * jax.nn.silu(g_a)
    pos = pos0 + jnp.arange(t, dtype=jnp.int32)
    q = partial_rope(q.reshape(bsz, t, ATT_HEADS, HEAD_DIM), pos)
    k = partial_rope(k.reshape(bsz, t, ATT_KV_HEADS, HEAD_DIM), pos)
    v = v.reshape(bsz, t, ATT_KV_HEADS, HEAD_DIM)
    y_b = sink_window_attention(q, k, v, k_past, v_past, n_past_valid, sinks) * jax.nn.silu(g_b)
    y = jnp.dot(jnp.concatenate([y_a, y_b], axis=-1), w_out)
    x = x + gate * rms_norm(y, norm_post)
    return x, new_buf, h_last, k, v


def odd_layer(x, c, conv_buf, s0, mod_w, mod_b, norm_pre, norm_post, w_in,
              conv_w, a_log, dt_bias, head_norm, w_out):
    bsz, t, _ = x.shape
    shift, scale, gate = modulate(c, mod_w, mod_b)
    hmod = rms_norm(x, norm_pre) * (1.0 + scale) + shift
    u = jnp.dot(hmod, w_in)
    qkv, z, b, a = jnp.split(u, ODD_SPLITS, axis=-1)
    qkv_c, new_buf = causal_dwconv(qkv, conv_buf, conv_w)
    qkv_c = jax.nn.silu(qkv_c).astype(jnp.float32)
    q, k, v = jnp.split(qkv_c, (GDN_KW, 2 * GDN_KW), axis=-1)
    q = l2_normalize(q.reshape(bsz, t, GDN_HEADS, GDN_DK)) * (GDN_DK ** -0.5)
    k = l2_normalize(k.reshape(bsz, t, GDN_HEADS, GDN_DK))
    v = v.reshape(bsz, t, GDN_HEADS, GDN_DV)
    beta = jax.nn.sigmoid(b.astype(jnp.float32))
    g = -jnp.exp(a_log.astype(jnp.float32)) * jax.nn.softplus(a.astype(jnp.float32) + dt_bias)
    o, s_new = gated_delta_rule(q, k, v, g, beta, s0.astype(jnp.float32))
    o = rms_norm(o, head_norm) * jax.nn.silu(z.reshape(bsz, t, GDN_HEADS, GDN_DV).astype(jnp.float32))
    y = jnp.dot(o.reshape(bsz, t, GDN_VW).astype(x.dtype), w_out)
    x = x + gate * rms_norm(y, norm_post)
    return x, new_buf, s_new.astype(x.dtype)


def setup_inputs(seed: int = 0) -> dict:
    key = jax.random.key(seed)
    ks = iter(jax.random.split(key, 48))
    f32 = jnp.float32

    def nrm(shape, scale):
        return jax.random.normal(next(ks), shape, f32) * scale

    x_prompt = nrm((BATCH, SEQ, D_MODEL), 1.0)
    x_sample = nrm((DEC_BATCH, DEC_SEQ, D_MODEL), 1.0)
    state_lru_conv = nrm((N_EVEN, DEC_BATCH, CONV_W - 1, LRU_WIDTH), 1.0)
    state_lru_h = nrm((N_EVEN, DEC_BATCH, LRU_WIDTH), 0.5)
    cache_swa_k = nrm((N_EVEN, DEC_BATCH, WINDOW, ATT_KV_HEADS, HEAD_DIM), 1.0)
    cache_swa_v = nrm((N_EVEN, DEC_BATCH, WINDOW, ATT_KV_HEADS, HEAD_DIM), 1.0)
    state_gdn_conv = nrm((N_ODD, DEC_BATCH, CONV_W - 1, GDN_CONV_CH), 1.0)
    state_gdn_s = nrm((N_ODD, DEC_BATCH, GDN_HEADS, GDN_DK, GDN_DV), GDN_DK ** -0.5)
    c_prompt = nrm((BATCH, D_MODEL), 1.0)
    c_sample = nrm((DEC_BATCH, D_MODEL), 1.0)
    ev_mod_w = nrm((N_EVEN, D_MODEL, 3 * D_MODEL), 0.2 * D_MODEL ** -0.5)
    ev_mod_b = nrm((N_EVEN, 3 * D_MODEL), 0.02)
    ev_norm_pre = 1.0 + nrm((N_EVEN, D_MODEL), 0.02)
    ev_norm_post = 1.0 + nrm((N_EVEN, D_MODEL), 0.02)
    ev_w_in = nrm((N_EVEN, D_MODEL, EVEN_IN), D_MODEL ** -0.5)
    lru_conv_w = nrm((N_EVEN, CONV_W, LRU_WIDTH), CONV_W ** -0.5)
    lru_conv_b = nrm((N_EVEN, LRU_WIDTH), 0.02)
    lru_w_a = nrm((N_EVEN, LRU_HEADS, LRU_BLOCK, LRU_BLOCK), LRU_BLOCK ** -0.5)
    lru_b_a = nrm((N_EVEN, LRU_WIDTH), 0.02)
    lru_w_x = nrm((N_EVEN, LRU_HEADS, LRU_BLOCK, LRU_BLOCK), LRU_BLOCK ** -0.5)
    lru_b_x = nrm((N_EVEN, LRU_WIDTH), 0.02)
    u_lam = jax.random.uniform(next(ks), (N_EVEN, LRU_WIDTH), f32, 0.9, 0.999)
    lru_lambda = jnp.log(u_lam) - jnp.log1p(-u_lam)
    swa_sinks = nrm((N_EVEN, ATT_HEADS), 1.0)
    ev_w_out = nrm((N_EVEN, EVEN_MIX, D_MODEL), EVEN_MIX ** -0.5)
    od_mod_w = nrm((N_ODD, D_MODEL, 3 * D_MODEL), 0.2 * D_MODEL ** -0.5)
    od_mod_b = nrm((N_ODD, 3 * D_MODEL), 0.02)
    od_norm_pre = 1.0 + nrm((N_ODD, D_MODEL), 0.02)
    od_norm_post = 1.0 + nrm((N_ODD, D_MODEL), 0.02)
    od_w_in = nrm((N_ODD, D_MODEL, ODD_IN), D_MODEL ** -0.5)
    gdn_conv_w = nrm((N_ODD, CONV_W, GDN_CONV_CH), CONV_W ** -0.5)
    gdn_a_log = jnp.log(jax.random.uniform(next(ks), (N_ODD, GDN_HEADS), f32, 1.0, 16.0))
    dt = jnp.exp(jax.random.uniform(next(ks), (N_ODD, GDN_HEADS), f32, math.log(1e-3), math.log(1e-1)))
    gdn_dt_bias = dt + jnp.log(-jnp.expm1(-dt))
    gdn_head_norm = 1.0 + nrm((N_ODD, GDN_DV), 0.02)
    od_w_out = nrm((N_ODD, GDN_VW, D_MODEL), GDN_VW ** -0.5)
    return {
        'x_prompt': x_prompt, 'x_sample': x_sample,
        'state_lru_conv': state_lru_conv, 'state_lru_h': state_lru_h,
        'cache_swa_k': cache_swa_k, 'cache_swa_v': cache_swa_v,
        'state_gdn_conv': state_gdn_conv, 'state_gdn_s': state_gdn_s,
        'c_prompt': c_prompt, 'c_sample': c_sample,
        'ev_mod_w': ev_mod_w, 'ev_mod_b': ev_mod_b, 'ev_norm_pre': ev_norm_pre, 'ev_norm_post': ev_norm_post,
        'ev_w_in': ev_w_in, 'lru_conv_w': lru_conv_w, 'lru_conv_b': lru_conv_b,
        'lru_w_a': lru_w_a, 'lru_b_a': lru_b_a, 'lru_w_x': lru_w_x, 'lru_b_x': lru_b_x,
        'lru_lambda': lru_lambda, 'swa_sinks': swa_sinks, 'ev_w_out': ev_w_out,
        'od_mod_w': od_mod_w, 'od_mod_b': od_mod_b, 'od_norm_pre': od_norm_pre, 'od_norm_post': od_norm_post,
        'od_w_in': od_w_in, 'gdn_conv_w': gdn_conv_w, 'gdn_a_log': gdn_a_log, 'gdn_dt_bias': gdn_dt_bias,
        'gdn_head_norm': gdn_head_norm, 'od_w_out': od_w_out,
    }


def reference(x_prompt, x_sample, state_lru_conv, state_lru_h, cache_swa_k, cache_swa_v,
              state_gdn_conv, state_gdn_s, c_prompt, c_sample,
              ev_mod_w, ev_mod_b, ev_norm_pre, ev_norm_post, ev_w_in, lru_conv_w, lru_conv_b,
              lru_w_a, lru_b_a, lru_w_x, lru_b_x, lru_lambda, swa_sinks, ev_w_out,
              od_mod_w, od_mod_b, od_norm_pre, od_norm_post, od_w_in, gdn_conv_w, gdn_a_log,
              gdn_dt_bias, gdn_head_norm, od_w_out):
    xp, xs = x_prompt, x_sample
    bp = xp.shape[0]
    lru_conv_p, lru_conv_s, lru_h_p, lru_h_s = [], [], [], []
    swa_k_p, swa_k_s, swa_v_p, swa_v_s = [], [], [], []
    gdn_conv_p, gdn_conv_s, gdn_s_p, gdn_s_s = [], [], [], []
    for layer in range(DEPTH):
        if layer % 2 == 0:
            e = layer // 2
            ew = (ev_mod_w[e], ev_mod_b[e], ev_norm_pre[e], ev_norm_post[e], ev_w_in[e],
                  lru_conv_w[e], lru_conv_b[e], lru_w_a[e], lru_b_a[e], lru_w_x[e], lru_b_x[e],
                  lru_lambda[e], swa_sinks[e], ev_w_out[e])
            zero_buf = jnp.zeros((bp, CONV_W - 1, LRU_WIDTH), xp.dtype)
            zero_h = jnp.zeros((bp, LRU_WIDTH), xp.dtype)
            zero_kv = jnp.zeros((bp, WINDOW, ATT_KV_HEADS, HEAD_DIM), xp.dtype)
            xp, buf_p, h_p, k_p, v_p = even_layer(xp, c_prompt, zero_buf, zero_h, zero_kv, zero_kv, 0, 0, *ew)
            xs, buf_s, h_s, k_s, v_s = even_layer(xs, c_sample, state_lru_conv[e], state_lru_h[e],
                                                  cache_swa_k[e], cache_swa_v[e], WINDOW, PAST_LEN, *ew)
            lru_conv_p.append(buf_p)
            lru_conv_s.append(buf_s)
            lru_h_p.append(h_p)
            lru_h_s.append(h_s)
            swa_k_p.append(k_p[:, -WINDOW:])
            swa_v_p.append(v_p[:, -WINDOW:])
            swa_k_s.append(k_s)
            swa_v_s.append(v_s)
        else:
            o = layer // 2
            ow = (od_mod_w[o], od_mod_b[o], od_norm_pre[o], od_norm_post[o], od_w_in[o],
                  gdn_conv_w[o], gdn_a_log[o], gdn_dt_bias[o], gdn_head_norm[o], od_w_out[o])
            zero_buf = jnp.zeros((bp, CONV_W - 1, GDN_CONV_CH), xp.dtype)
            zero_s = jnp.zeros((bp, GDN_HEADS, GDN_DK, GDN_DV), jnp.float32)
            xp, buf_p, st_p = odd_layer(xp, c_prompt, zero_buf, zero_s, *ow)
            xs, buf_s, st_s = odd_layer(xs, c_sample, state_gdn_conv[o], state_gdn_s[o], *ow)
            gdn_conv_p.append(buf_p)
            gdn_conv_s.append(buf_s)
            gdn_s_p.append(st_p)
            gdn_s_s.append(st_s)
    return (xp, xs,
            jnp.stack(lru_conv_p), jnp.stack(lru_conv_s),
            jnp.stack(lru_h_p), jnp.stack(lru_h_s),
            jnp.stack(swa_k_p), jnp.stack(swa_k_s),
            jnp.stack(swa_v_p), jnp.stack(swa_v_s),
            jnp.stack(gdn_conv_p), jnp.stack(gdn_conv_s),
            jnp.stack(gdn_s_p), jnp.stack(gdn_s_s))
```

```python
import functools

import jax
import jax.numpy as jnp
from jax import lax
from jax.experimental import pallas as pl
from jax.experimental.pallas import tpu as pltpu

F32 = jnp.float32
BF16 = jnp.bfloat16

CHUNK = 64
EPS = 1e-6
CONV_W = 4
NEG_INF = -1e30
LRU_HEADS = 8
LRU_C = 8.0
HEAD_DIM = 64
ATT_KV_HEADS = 2
WINDOW = 128
ROT_DIM = HEAD_DIM // 4
ROPE_THETA = 500000.0
GDN_DK = 128
GDN_DV = 128
PAST_LEN = 4096

LANES = 128
SUBLANES = 8
MXU_TILE = 256
VMEM_LIMIT_BYTES = 56 * 1024 * 1024

PROMPT_TILE = 256
MOD_TILE = 512
CONV_PAD = SUBLANES


def _silu(x):
    return x * jax.nn.sigmoid(x)


def _expm1(x):
    u = jnp.exp(x)
    d = u - 1.0
    return jnp.where(u == 1.0, x, jnp.where(d == -1.0, -1.0, d * x / jnp.log(u)))


def _rms(x, g):
    return x * lax.rsqrt(jnp.mean(x * x, axis=-1, keepdims=True) + EPS) * g


def _dot(a, b):
    return jnp.dot(a, b, preferred_element_type=F32)


def _dot_nt(a, b):
    return lax.dot_general(a, b, (((1,), (1,)), ((), ())), preferred_element_type=F32)


def _dot_tn(a, b):
    return lax.dot_general(a, b, (((0,), (0,)), ((), ())), preferred_element_type=F32)


def _for_each(n, body):
    if n == 1:
        body(0)
    else:
        def step(i, carry):
            body(i)
            return carry
        lax.fori_loop(0, n, step, 0)


def _rows(i, n, size=None):
    size = n if size is None else size
    if isinstance(i, int):
        return pl.ds(i * n, size)
    return pl.ds(pl.multiple_of(i * n, n), size)


def _split_segment(s, n_batch, chunks_per_tile):
    if n_batch == 1:
        return 0, s
    if chunks_per_tile == 1:
        return s, 0
    return s // chunks_per_tile, s % chunks_per_tile


def _mod_body(c_ref, w0_ref, b0_ref, w1_ref, b1_ref, o0_ref, o1_ref):
    c = _silu(c_ref[...]).astype(BF16)
    o0_ref[...] = _dot(c, w0_ref[...].astype(BF16)) + b0_ref[...]
    o1_ref[...] = _dot(c, w1_ref[...].astype(BF16)) + b1_ref[...]


def _modulation(c_all, w0, b0, w1, b1):
    n, d = c_all.shape
    d3 = w0.shape[1]
    wspec = pl.BlockSpec((d, MOD_TILE), lambda j: (0, j))
    bspec = pl.BlockSpec((1, MOD_TILE), lambda j: (0, j))
    ospec = pl.BlockSpec((n, MOD_TILE), lambda j: (0, j))
    return pl.pallas_call(
        _mod_body,
        grid=(d3 // MOD_TILE,),
        in_specs=[pl.BlockSpec((n, d), lambda j: (0, 0)), wspec, bspec, wspec, bspec],
        out_specs=[ospec, ospec],
        out_shape=[jax.ShapeDtypeStruct((n, d3), F32)] * 2,
        name="adaln_modulation",
    )(c_all, w0, b0.reshape(1, d3), w1, b1.reshape(1, d3))


def _pre_norm(x_ref, mod_ref, norm_pre_ref):
    bb, tc, d = x_ref.shape
    x = x_ref[...]
    h = _rms(x, norm_pre_ref[...]) * (1.0 + mod_ref[:, 1:2, :]) + mod_ref[:, 0:1, :]
    return h.reshape(bb * tc, d)


def _post_residual(x_ref, mod_ref, norm_post_ref, y, o_ref):
    bb, tc, d = x_ref.shape
    yn = _rms(y, norm_post_ref[...]).reshape(bb, tc, d)
    o_ref[...] = x_ref[...] + mod_ref[:, 2:3, :] * yn


def _causal_conv(ext_ref, bi, tc, w_ref):
    acc = ext_ref[bi, CONV_PAD - 3:CONV_PAD - 3 + tc, :] * w_ref[0:1, :]
    for j in range(1, CONV_W):
        lo = CONV_PAD - 3 + j
        acc = acc + ext_ref[bi, lo:lo + tc, :] * w_ref[j:j + 1, :]
    return acc


def _even_body(*refs, has_state, n_past_valid, pos0, chunk, n_tiles, k_out_rows):
    if has_state:
        (x_ref, mod_ref, npre_ref, npost_ref, win_ref, cw_ref, cb_ref, wg_ref, ba_ref, bx_ref,
         lam_ref, inv_ref, sinks_ref, wout_ref, conv0_ref, h0_ref, k0_ref, v0_ref,
         y_ref, nconv_ref, nh_ref, nk_ref, nv_ref,
         u_ref, ext_ref, a_ref, b_ref, hs_ref, knat_ref, vnat_ref, q_ref, mix_ref) = refs
    else:
        (x_ref, mod_ref, npre_ref, npost_ref, win_ref, cw_ref, cb_ref, wg_ref, ba_ref, bx_ref,
         lam_ref, inv_ref, sinks_ref, wout_ref,
         y_ref, nconv_ref, nh_ref, nk_ref, nv_ref,
         u_ref, ext_ref, a_ref, b_ref, hs_ref, knat_ref, vnat_ref, q_ref, mix_ref) = refs
    bb, tc, _ = x_ref.shape
    lw = a_ref.shape[1]
    aw = q_ref.shape[1]
    kvw = knat_ref.shape[2]
    t = pl.program_id(1)
    cpt = tc // chunk
    m = WINDOW + chunk
    o_q, o_k, o_v, o_gb = 2 * lw, 2 * lw + aw, 2 * lw + aw + kvw, 2 * lw + aw + 2 * kvw

    @pl.when(t == 0)
    def _():
        if has_state:
            ext_ref[:, CONV_PAD - 3:CONV_PAD, :] = conv0_ref[...]
            nh_ref[...] = h0_ref[...]
            knat_ref[:, 0:WINDOW, :] = k0_ref[...]
            vnat_ref[:, 0:WINDOW, :] = v0_ref[...]
        else:
            ext_ref[:, 0:CONV_PAD, :] = jnp.zeros((bb, CONV_PAD, lw), F32)
            nh_ref[...] = jnp.zeros(nh_ref.shape, F32)
            knat_ref[:, 0:WINDOW, :] = jnp.zeros((bb, WINDOW, kvw), F32)
            vnat_ref[:, 0:WINDOW, :] = jnp.zeros((bb, WINDOW, kvw), F32)

    hmod = _pre_norm(x_ref, mod_ref, npre_ref).astype(BF16)
    u_ref[...] = _dot(hmod, win_ref[...])

    pos = (pos0 + t * tc + lax.broadcasted_iota(jnp.int32, (tc, LANES), 0)).astype(F32)
    ang = pos * inv_ref[...]
    lane = lax.broadcasted_iota(jnp.int32, (tc, LANES), 1) % HEAD_DIM
    cos_t = jnp.cos(ang)
    sin_t = jnp.sin(ang)
    half = ROT_DIM // 2
    sin_a = jnp.where(lane < half, -sin_t, 0.0)
    sin_b = jnp.where(lane >= half, sin_t, 0.0)

    def rope(xcol):
        return (xcol * cos_t + pltpu.roll(xcol, LANES - half, axis=1) * sin_a
                + pltpu.roll(xcol, half, axis=1) * sin_b)

    neg_c_softplus = -LRU_C * jax.nn.softplus(-lam_ref[...])
    row8 = lax.broadcasted_iota(jnp.int32, (SUBLANES, lw), 0)

    def per_batch(bi):
        rows = _rows(bi, tc)
        ext_ref[bi, CONV_PAD:CONV_PAD + tc, :] = u_ref[rows, 0:lw]
        xc = _causal_conv(ext_ref, bi, tc, cw_ref) + cb_ref[...]
        nconv_ref[bi, :, :] = ext_ref[bi, CONV_PAD + tc - 3:CONV_PAD + tc, :]
        ext_ref[bi, CONV_PAD - 3:CONV_PAD, :] = ext_ref[bi, CONV_PAD + tc - 3:CONV_PAD + tc, :]
        xcb = xc.astype(BF16)
        halves = [_dot(xcb[:, i * MXU_TILE:(i + 1) * MXU_TILE], wg_ref[i]) for i in range(lw // MXU_TILE)]
        r = jax.nn.sigmoid(jnp.concatenate([g[:, :MXU_TILE] for g in halves], axis=1) + ba_ref[...])
        ig = jax.nn.sigmoid(jnp.concatenate([g[:, MXU_TILE:] for g in halves], axis=1) + bx_ref[...])
        log_a = r * neg_c_softplus
        a_ref[...] = jnp.exp(log_a)
        b_ref[...] = jnp.sqrt(-_expm1(2.0 * log_a)) * (ig * xc)

        def scan_block(j, hc):
            r0 = pl.multiple_of(j * SUBLANES, SUBLANES)
            a = a_ref[pl.ds(r0, SUBLANES), :]
            b = b_ref[pl.ds(r0, SUBLANES), :]
            for sft in (1, 2, 4):
                keep = row8 >= sft
                b = jnp.where(keep, a * pltpu.roll(b, sft, axis=0) + b, b)
                a = jnp.where(keep, a * pltpu.roll(a, sft, axis=0), a)
            h = a * hc + b
            hs_ref[pl.ds(r0, SUBLANES), :] = h
            return h[SUBLANES - 1:SUBLANES, :]

        nh_ref[bi, :, :] = lax.fori_loop(0, tc // SUBLANES, scan_block, nh_ref[bi, :, :])
        mix_ref[rows, 0:lw] = (hs_ref[...] * _silu(u_ref[rows, lw:2 * lw])).astype(BF16)
        for j in range(aw // LANES):
            qcol = rope(u_ref[rows, o_q + j * LANES:o_q + (j + 1) * LANES])
            q_ref[rows, j * LANES:(j + 1) * LANES] = (qcol * (HEAD_DIM ** -0.5)).astype(BF16)
        knat_ref[bi, WINDOW:WINDOW + tc, :] = rope(u_ref[rows, o_k:o_k + kvw])
        vnat_ref[bi, WINDOW:WINDOW + tc, :] = u_ref[rows, o_v:o_v + kvw]
        nk_ref[bi, :, :] = knat_ref[bi, WINDOW + tc - k_out_rows:WINDOW + tc, :]
        nv_ref[bi, :, :] = vnat_ref[bi, WINDOW + tc - k_out_rows:WINDOW + tc, :]

    _for_each(bb, per_batch)

    lo_lane = lax.broadcasted_iota(jnp.int32, (m, LANES), 1) < HEAD_DIM
    lo_q = lax.broadcasted_iota(jnp.int32, (chunk, LANES), 1) < HEAD_DIM
    group = (aw // HEAD_DIM) // ATT_KV_HEADS

    def per_segment(s):
        bi, c = _split_segment(s, bb, cpt)
        rows = _rows(s, chunk)
        krows = _rows(c, chunk, m)
        kseg = knat_ref[bi, krows, :]
        vseg = vnat_ref[bi, krows, :]
        krot = pltpu.roll(kseg, HEAD_DIM, axis=1)
        vrot = pltpu.roll(vseg, HEAD_DIM, axis=1)
        if n_past_valid < WINDOW:
            key_pos = t * tc + c * chunk - WINDOW + lax.broadcasted_iota(jnp.int32, (1, m), 1)
            valid = key_pos >= -n_past_valid
        for h in range(ATT_KV_HEADS):
            kd = (jnp.where(lo_lane, kseg, krot) if h == 0 else jnp.where(lo_lane, krot, kseg)).astype(BF16)
            v_lo = (jnp.where(lo_lane, vseg, 0.0) if h == 0 else jnp.where(lo_lane, vrot, 0.0)).astype(BF16)
            v_hi = (jnp.where(lo_lane, 0.0, vrot) if h == 0 else jnp.where(lo_lane, 0.0, vseg)).astype(BF16)
            cols = [q_ref[rows, (h * group // 2 + j) * LANES:(h * group // 2 + j + 1) * LANES]
                    for j in range(group // 2)]
            zero = jnp.zeros_like(cols[0])
            qstack = jnp.concatenate(
                [part for qc in cols for part in (jnp.where(lo_q, qc, zero), jnp.where(lo_q, zero, qc))], axis=0)
            sc = _dot_nt(qstack, kd)
            if n_past_valid < WINDOW:
                sc = jnp.where(valid, sc, NEG_INF)
            sink = jnp.concatenate(
                [jnp.full((chunk, 1), sinks_ref[h * group + g], F32) for g in range(group)], axis=0)
            mx = jnp.maximum(jnp.max(sc, axis=-1, keepdims=True), sink)
            p = jnp.exp(sc - mx)
            p = (p / (jnp.sum(p, axis=-1, keepdims=True) + jnp.exp(sink - mx))).astype(BF16)
            for j in range(group // 2):
                o = (_dot(p[(2 * j) * chunk:(2 * j + 1) * chunk], v_lo)
                     + _dot(p[(2 * j + 1) * chunk:(2 * j + 2) * chunk], v_hi))
                col = h * group // 2 + j
                gate = _silu(u_ref[rows, o_gb + col * LANES:o_gb + (col + 1) * LANES])
                mix_ref[rows, lw + col * LANES:lw + (col + 1) * LANES] = (o * gate).astype(BF16)

    _for_each(bb * cpt, per_segment)

    if n_tiles > 1:
        knat_ref[:, 0:WINDOW, :] = knat_ref[:, tc:tc + WINDOW, :]
        vnat_ref[:, 0:WINDOW, :] = vnat_ref[:, tc:tc + WINDOW, :]

    y = _dot(mix_ref[...], wout_ref[...])
    _post_residual(x_ref, mod_ref, npost_ref, y, y_ref)


def _const_spec(shape):
    zeros = (0,) * len(shape)
    return pl.BlockSpec(shape, lambda b, t: zeros)


def _even_layer(x, mod, consts, state, *, tile, chunk, batch_block, n_past_valid, pos0, k_out_rows):
    bsz, seq, d = x.shape
    (npre, npost, win, cw, cb, wg, b_a, b_x, lam, inv_row, sinks, wout) = consts
    lw = cw.shape[1]
    kvw = ATT_KV_HEADS * HEAD_DIM
    aw = (win.shape[1] - 2 * lw - 2 * kvw) // 2
    n_tiles = seq // tile
    assert seq % tile == 0 and tile % chunk == 0 and bsz % batch_block == 0
    assert n_tiles == 1 or tile >= WINDOW
    rows = batch_block * tile
    has_state = state is not None

    def bspec(shape):
        nd = len(shape)
        return pl.BlockSpec((batch_block,) + shape, lambda b, t: (b,) + (0,) * nd)

    in_specs = [
        pl.BlockSpec((batch_block, tile, d), lambda b, t: (b, t, 0)),
        bspec((3, d)),
        _const_spec(npre.shape), _const_spec(npost.shape), _const_spec(win.shape),
        _const_spec(cw.shape), _const_spec(cb.shape), _const_spec(wg.shape),
        _const_spec(b_a.shape), _const_spec(b_x.shape), _const_spec(lam.shape), _const_spec(inv_row.shape),
        pl.BlockSpec(memory_space=pltpu.SMEM),
        _const_spec(wout.shape),
    ]
    args = [x, mod, npre, npost, win, cw, cb, wg, b_a, b_x, lam, inv_row, sinks, wout]
    if has_state:
        in_specs += [bspec((CONV_W - 1, lw)), bspec((1, lw)), bspec((WINDOW, kvw)), bspec((WINDOW, kvw))]
        args += list(state)
    out_specs = [
        pl.BlockSpec((batch_block, tile, d), lambda b, t: (b, t, 0)),
        bspec((CONV_W - 1, lw)), bspec((1, lw)), bspec((k_out_rows, kvw)), bspec((k_out_rows, kvw)),
    ]
    out_shape = [
        jax.ShapeDtypeStruct((bsz, seq, d), F32),
        jax.ShapeDtypeStruct((bsz, CONV_W - 1, lw), F32),
        jax.ShapeDtypeStruct((bsz, 1, lw), F32),
        jax.ShapeDtypeStruct((bsz, k_out_rows, kvw), F32),
        jax.ShapeDtypeStruct((bsz, k_out_rows, kvw), F32),
    ]
    scratch = [
        pltpu.VMEM((rows, win.shape[1]), F32),
        pltpu.VMEM((batch_block, CONV_PAD + tile, lw), F32),
        pltpu.VMEM((tile, lw), F32), pltpu.VMEM((tile, lw), F32), pltpu.VMEM((tile, lw), F32),
        pltpu.VMEM((batch_block, WINDOW + tile, kvw), F32),
        pltpu.VMEM((batch_block, WINDOW + tile, kvw), F32),
        pltpu.VMEM((rows, aw), BF16),
        pltpu.VMEM((rows, lw + aw), BF16),
    ]
    body = functools.partial(_even_body, has_state=has_state, n_past_valid=n_past_valid, pos0=pos0,
                             chunk=chunk, n_tiles=n_tiles, k_out_rows=k_out_rows)
    return pl.pallas_call(
        body,
        grid=(bsz // batch_block, n_tiles),
        in_specs=in_specs,
        out_specs=out_specs,
        out_shape=out_shape,
        scratch_shapes=scratch,
        compiler_params=pltpu.CompilerParams(
            dimension_semantics=("arbitrary", "arbitrary"), vmem_limit_bytes=VMEM_LIMIT_BYTES),
        name="even_layer_state" if has_state else "even_layer_prompt",
    )(*args)


def _odd_body(*refs, has_state, chunk):
    if has_state:
        (x_ref, mod_ref, npre_ref, npost_ref, win_ref, wba_ref, cw_ref, alog_ref, dtb_ref, hn_ref, wout_ref,
         conv0_ref, s0_ref,
         y_ref, nconv_ref, ns_ref,
         u_ref, ext_ref, qn_ref, kn_ref, v_ref, beta_ref, gcum_ref, o_ref) = refs
    else:
        (x_ref, mod_ref, npre_ref, npost_ref, win_ref, wba_ref, cw_ref, alog_ref, dtb_ref, hn_ref, wout_ref,
         y_ref, nconv_ref, ns_ref,
         u_ref, ext_ref, qn_ref, kn_ref, v_ref, beta_ref, gcum_ref, o_ref) = refs
    bb, tc, _ = x_ref.shape
    n_heads = ns_ref.shape[1]
    kw = n_heads * GDN_DK
    vw = n_heads * GDN_DV
    cch = 2 * kw + vw
    rows_all = bb * tc
    t = pl.program_id(1)
    cpt = tc // chunk
    n_steps = (chunk - 1).bit_length()

    @pl.when(t == 0)
    def _():
        if has_state:
            ext_ref[:, CONV_PAD - 3:CONV_PAD, :] = conv0_ref[...]
            ns_ref[...] = s0_ref[...]
        else:
            ext_ref[:, 0:CONV_PAD, :] = jnp.zeros((bb, CONV_PAD, cch), F32)
            ns_ref[...] = jnp.zeros(ns_ref.shape, F32)

    hmod = _pre_norm(x_ref, mod_ref, npre_ref).astype(BF16)
    u_ref[...] = _dot(hmod, win_ref[...])
    ba = _dot(hmod, wba_ref[...])
    beta_ref[...] = jax.nn.sigmoid(ba)
    g = -jnp.exp(alog_ref[...]) * jax.nn.softplus(ba + dtb_ref[...])
    row_in_chunk = lax.broadcasted_iota(jnp.int32, (rows_all, LANES), 0) % chunk
    sft = 1
    while sft < chunk:
        g = jnp.where(row_in_chunk >= sft, g + pltpu.roll(g, sft, axis=0), g)
        sft *= 2
    gcum_ref[...] = g

    def per_batch(bi):
        rows = _rows(bi, tc)
        ext_ref[bi, CONV_PAD:CONV_PAD + tc, :] = u_ref[rows, 0:cch]
        qkv = _silu(_causal_conv(ext_ref, bi, tc, cw_ref))
        nconv_ref[bi, :, :] = ext_ref[bi, CONV_PAD + tc - 3:CONV_PAD + tc, :]
        ext_ref[bi, CONV_PAD - 3:CONV_PAD, :] = ext_ref[bi, CONV_PAD + tc - 3:CONV_PAD + tc, :]
        for h in range(n_heads):
            qh = qkv[:, h * GDN_DK:(h + 1) * GDN_DK]
            kh = qkv[:, kw + h * GDN_DK:kw + (h + 1) * GDN_DK]
            qn_ref[rows, h * GDN_DK:(h + 1) * GDN_DK] = (
                qh * lax.rsqrt(jnp.sum(qh * qh, axis=-1, keepdims=True) + EPS) * (GDN_DK ** -0.5))
            kn_ref[rows, h * GDN_DK:(h + 1) * GDN_DK] = (
                kh * lax.rsqrt(jnp.sum(kh * kh, axis=-1, keepdims=True) + EPS))
        v_ref[rows, :] = qkv[:, 2 * kw:]

    _for_each(bb, per_batch)

    ri = lax.broadcasted_iota(jnp.int32, (chunk, chunk), 0)
    ci = lax.broadcasted_iota(jnp.int32, (chunk, chunk), 1)
    eye = ri == ci
    incl = ri >= ci
    strict = ri > ci
    eye_f = jnp.where(eye, 1.0, 0.0).astype(F32)

    def per_segment(s):
        bi, _ = _split_segment(s, bb, cpt)
        rows = _rows(s, chunk)
        beta_t = beta_ref[rows, :]
        gcum_t = gcum_ref[rows, :]
        for h in range(n_heads):
            q = qn_ref[rows, h * GDN_DK:(h + 1) * GDN_DK]
            k = kn_ref[rows, h * GDN_DK:(h + 1) * GDN_DK]
            v = v_ref[rows, h * GDN_DV:(h + 1) * GDN_DV]
            beta = beta_t[:, h:h + 1]
            gc = gcum_t[:, n_heads + h:n_heads + h + 1]
            g_row = jnp.sum(jnp.where(eye, gc, 0.0), axis=0, keepdims=True)
            decay = jnp.exp(jnp.where(incl, gc - g_row, 0.0))
            eg = jnp.exp(gc)
            kb = k * beta
            kk = _dot_nt(jnp.concatenate([kb, q], axis=0).astype(BF16), k.astype(BF16))
            p_mat = -(kk[:chunk] * jnp.where(strict, decay, 0.0))
            qk = kk[chunk:] * jnp.where(incl, decay, 0.0)
            p_b = p_mat.astype(BF16)
            t_mat = eye_f + p_b.astype(F32)
            for step in range(1, n_steps):
                p_mat = _dot(p_b, p_b)
                p_b = p_mat.astype(BF16)
                t_mat = t_mat + _dot(t_mat.astype(BF16), p_b)
            wu = _dot(t_mat.astype(BF16), jnp.concatenate([kb * eg, v * beta], axis=1).astype(BF16))
            state = ns_ref[bi, h]
            ws_qs = _dot(jnp.concatenate([wu[:, :GDN_DK], q * eg], axis=0).astype(BF16), state.astype(BF16))
            v_new = wu[:, GDN_DK:] - ws_qs[:chunk]
            v_new_b = v_new.astype(BF16)
            o_ref[rows, h * GDN_DV:(h + 1) * GDN_DV] = ws_qs[chunk:] + _dot(qk.astype(BF16), v_new_b)
            g_last = gc[chunk - 1:chunk, :]
            k_dec = (k * jnp.exp(g_last - gc)).astype(BF16)
            ns_ref[bi, h] = state * jnp.exp(g_last) + _dot_tn(k_dec, v_new_b)

    _for_each(bb * cpt, per_segment)

    for h in range(n_heads):
        cols = slice(h * GDN_DV, (h + 1) * GDN_DV)
        gate = _silu(u_ref[:, cch + h * GDN_DV:cch + (h + 1) * GDN_DV])
        qn_ref[:, cols] = _rms(o_ref[:, cols], hn_ref[...]) * gate
    y = _dot(qn_ref[...].astype(BF16), wout_ref[...])
    _post_residual(x_ref, mod_ref, npost_ref, y, y_ref)


def _odd_layer(x, mod, consts, state, *, tile, chunk, batch_block):
    bsz, seq, d = x.shape
    (npre, npost, win, wba, cw, alog, dtb, hn, wout) = consts
    cch = cw.shape[1]
    vw = wout.shape[0]
    n_heads = vw // GDN_DV
    n_tiles = seq // tile
    assert seq % tile == 0 and tile % chunk == 0 and bsz % batch_block == 0
    rows = batch_block * tile
    has_state = state is not None

    def bspec(shape):
        nd = len(shape)
        return pl.BlockSpec((batch_block,) + shape, lambda b, t: (b,) + (0,) * nd)

    in_specs = [
        pl.BlockSpec((batch_block, tile, d), lambda b, t: (b, t, 0)),
        bspec((3, d)),
        _const_spec(npre.shape), _const_spec(npost.shape), _const_spec(win.shape), _const_spec(wba.shape),
        _const_spec(cw.shape), _const_spec(alog.shape), _const_spec(dtb.shape), _const_spec(hn.shape),
        _const_spec(wout.shape),
    ]
    args = [x, mod, npre, npost, win, wba, cw, alog, dtb, hn, wout]
    if has_state:
        in_specs += [bspec((CONV_W - 1, cch)), bspec((n_heads, GDN_DK, GDN_DV))]
        args += list(state)
    out_specs = [
        pl.BlockSpec((batch_block, tile, d), lambda b, t: (b, t, 0)),
        bspec((CONV_W - 1, cch)), bspec((n_heads, GDN_DK, GDN_DV)),
    ]
    out_shape = [
        jax.ShapeDtypeStruct((bsz, seq, d), F32),
        jax.ShapeDtypeStruct((bsz, CONV_W - 1, cch), F32),
        jax.ShapeDtypeStruct((bsz, n_heads, GDN_DK, GDN_DV), F32),
    ]
    scratch = [
        pltpu.VMEM((rows, win.shape[1]), F32),
        pltpu.VMEM((batch_block, CONV_PAD + tile, cch), F32),
        pltpu.VMEM((rows, vw), F32), pltpu.VMEM((rows, vw), F32), pltpu.VMEM((rows, vw), F32),
        pltpu.VMEM((rows, LANES), F32), pltpu.VMEM((rows, LANES), F32),
        pltpu.VMEM((rows, vw), F32),
    ]
    body = functools.partial(_odd_body, has_state=has_state, chunk=chunk)
    return pl.pallas_call(
        body,
        grid=(bsz // batch_block, n_tiles),
        in_specs=in_specs,
        out_specs=out_specs,
        out_shape=out_shape,
        scratch_shapes=scratch,
        compiler_params=pltpu.CompilerParams(
            dimension_semantics=("arbitrary", "arbitrary"), vmem_limit_bytes=VMEM_LIMIT_BYTES),
        name="odd_layer_state" if has_state else "odd_layer_prompt",
    )(*args)


def _gate_weights(w_a, w_x):
    heads, blk, _ = w_a.shape
    per_tile = MXU_TILE // blk
    tiles = []
    for i in range(heads // per_tile):
        sl = slice(i * per_tile, (i + 1) * per_tile)
        tiles.append(jnp.concatenate(
            [jax.scipy.linalg.block_diag(*w_a[sl]), jax.scipy.linalg.block_diag(*w_x[sl])], axis=1))
    return jnp.stack(tiles).astype(BF16)


def _rope_inv_row():
    half = ROT_DIM // 2
    inv = ROPE_THETA ** (-(jnp.arange(half, dtype=F32) * 2.0 / ROT_DIM))
    per_head = jnp.concatenate([inv, inv, jnp.zeros((HEAD_DIM - ROT_DIM,), F32)])
    return jnp.tile(per_head, LANES // HEAD_DIM).reshape(1, LANES)


def kernel(x_prompt, x_sample, state_lru_conv, state_lru_h, cache_swa_k, cache_swa_v, state_gdn_conv, state_gdn_s, c_prompt, c_sample, ev_mod_w, ev_mod_b, ev_norm_pre, ev_norm_post, ev_w_in, lru_conv_w, lru_conv_b, lru_w_a, lru_b_a, lru_w_x, lru_b_x, lru_lambda, swa_sinks, ev_w_out, od_mod_w, od_mod_b, od_norm_pre, od_norm_post, od_w_in, gdn_conv_w, gdn_a_log, gdn_dt_bias, gdn_head_norm, od_w_out):
    bp, seq, d = x_prompt.shape
    bs, dec_seq, _ = x_sample.shape
    lw = lru_conv_w.shape[-1]
    kvw = ATT_KV_HEADS * HEAD_DIM
    n_heads = gdn_a_log.shape[-1]
    cch = gdn_conv_w.shape[-1]
    vw = od_w_out.shape[1]

    n_c = bp + bs
    n_pad = -n_c % 16
    c_all = jnp.concatenate([c_prompt, c_sample, jnp.zeros((n_pad, d), F32)], axis=0)
    mod_ev, mod_od = _modulation(c_all, ev_mod_w[0], ev_mod_b[0], od_mod_w[0], od_mod_b[0])
    mod_ev = mod_ev.reshape(n_c + n_pad, 3, d)
    mod_od = mod_od.reshape(n_c + n_pad, 3, d)

    row = lambda a: a.reshape(1, -1)
    ev_consts = (row(ev_norm_pre[0]), row(ev_norm_post[0]), ev_w_in[0].astype(BF16), lru_conv_w[0],
                 row(lru_conv_b[0]), _gate_weights(lru_w_a[0], lru_w_x[0]), row(lru_b_a[0]), row(lru_b_x[0]),
                 row(lru_lambda[0]), _rope_inv_row(), swa_sinks[0], ev_w_out[0].astype(BF16))
    w_in_od = od_w_in[0]
    pad_lanes = lambda a: jnp.pad(a, ((0, 0), (0, LANES - a.shape[1])))
    head_row = lambda a: jnp.pad(a.reshape(1, -1), ((0, 0), (n_heads, LANES - 2 * n_heads)))
    od_consts = (row(od_norm_pre[0]), row(od_norm_post[0]), w_in_od[:, :cch + vw].astype(BF16),
                 pad_lanes(w_in_od[:, cch + vw:]).astype(BF16), gdn_conv_w[0],
                 head_row(gdn_a_log[0]), head_row(gdn_dt_bias[0]), row(gdn_head_norm[0]),
                 od_w_out[0].astype(BF16))

    tile = min(PROMPT_TILE, seq)
    xp, lru_conv_p, lru_h_p, swa_k_p, swa_v_p = _even_layer(
        x_prompt, mod_ev[:bp], ev_consts, None, tile=tile, chunk=min(CHUNK, seq), batch_block=1,
        n_past_valid=0, pos0=0, k_out_rows=WINDOW)
    ev_state = (state_lru_conv[0], state_lru_h[0].reshape(bs, 1, lw),
                cache_swa_k[0].reshape(bs, WINDOW, kvw), cache_swa_v[0].reshape(bs, WINDOW, kvw))
    xs, lru_conv_s, lru_h_s, swa_k_s, swa_v_s = _even_layer(
        x_sample, mod_ev[bp:n_c], ev_consts, ev_state, tile=dec_seq, chunk=min(CHUNK, dec_seq), batch_block=bs,
        n_past_valid=WINDOW, pos0=PAST_LEN, k_out_rows=dec_seq)

    xp, gdn_conv_p, gdn_s_p = _odd_layer(
        xp, mod_od[:bp], od_consts, None, tile=tile, chunk=min(CHUNK, seq), batch_block=1)
    xs, gdn_conv_s, gdn_s_s = _odd_layer(
        xs, mod_od[bp:n_c], od_consts, (state_gdn_conv[0], state_gdn_s[0]),
        tile=dec_seq, chunk=min(CHUNK, dec_seq), batch_block=bs)

    kv_shape = lambda a: a.reshape(1, a.shape[0], a.shape[1], ATT_KV_HEADS, HEAD_DIM)
    return (xp, xs,
            lru_conv_p[None], lru_conv_s[None],
            lru_h_p.reshape(1, bp, lw), lru_h_s.reshape(1, bs, lw),
            kv_shape(swa_k_p), kv_shape(swa_k_s), kv_shape(swa_v_p), kv_shape(swa_v_s),
            gdn_conv_p[None], gdn_conv_s[None], gdn_s_p[None], gdn_s_s[None])
```

```python
import functools

import jax
import jax.numpy as jnp
from jax import lax
from jax.experimental import pallas as pl
from jax.experimental.pallas import tpu as pltpu

F32 = jnp.float32
BF16 = jnp.bfloat16

CHUNK = 64
EPS = 1e-6
CONV_W = 4
NEG_INF = -1e30
LRU_HEADS = 8
LRU_C = 8.0
HEAD_DIM = 64
ATT_KV_HEADS = 2
WINDOW = 128
ROT_DIM = HEAD_DIM // 4
ROPE_THETA = 500000.0
GDN_DK = 128
GDN_DV = 128
PAST_LEN = 4096

LANES = 128
SUBLANES = 8
MXU_TILE = 256
VMEM_LIMIT_BYTES = 56 * 1024 * 1024

PROMPT_TILE = 256
MOD_TILE = 512
CONV_PAD = SUBLANES


def _silu(x):
    return x * jax.nn.sigmoid(x)


def _expm1(x):
    u = jnp.exp(x)
    d = u - 1.0
    return jnp.where(u == 1.0, x, jnp.where(d == -1.0, -1.0, d * x / jnp.log(u)))


def _rms(x, g):
    return x * lax.rsqrt(jnp.mean(x * x, axis=-1, keepdims=True) + EPS) * g


def _dot(a, b):
    return jnp.dot(a, b, preferred_element_type=F32)


def _dot_nt(a, b):
    return lax.dot_general(a, b, (((1,), (1,)), ((), ())), preferred_element_type=F32)


def _dot_tn(a, b):
    return lax.dot_general(a, b, (((0,), (0,)), ((), ())), preferred_element_type=F32)


def _for_each(n, body):
    if n == 1:
        body(0)
    else:
        def step(i, carry):
            body(i)
            return carry
        lax.fori_loop(0, n, step, 0)


def _rows(i, n, size=None):
    size = n if size is None else size
    if isinstance(i, int):
        return pl.ds(i * n, size)
    return pl.ds(pl.multiple_of(i * n, n), size)


def _split_segment(s, n_batch, chunks_per_tile):
    if n_batch == 1:
        return 0, s
    if chunks_per_tile == 1:
        return s, 0
    return s // chunks_per_tile, s % chunks_per_tile


def _mod_body(c_ref, w0_ref, b0_ref, w1_ref, b1_ref, o0_ref, o1_ref):
    c = _silu(c_ref[...]).astype(BF16)
    o0_ref[...] = _dot(c, w0_ref[...].astype(BF16)) + b0_ref[...]
    o1_ref[...] = _dot(c, w1_ref[...].astype(BF16)) + b1_ref[...]


def _modulation(c_all, w0, b0, w1, b1):
    n, d = c_all.shape
    d3 = w0.shape[1]
    wspec = pl.BlockSpec((d, MOD_TILE), lambda j: (0, j))
    bspec = pl.BlockSpec((1, MOD_TILE), lambda j: (0, j))
    ospec = pl.BlockSpec((n, MOD_TILE), lambda j: (0, j))
    return pl.pallas_call(
        _mod_body,
        grid=(d3 // MOD_TILE,),
        in_specs=[pl.BlockSpec((n, d), lambda j: (0, 0)), wspec, bspec, wspec, bspec],
        out_specs=[ospec, ospec],
        out_shape=[jax.ShapeDtypeStruct((n, d3), F32)] * 2,
        name="adaln_modulation",
    )(c_all, w0, b0.reshape(1, d3), w1, b1.reshape(1, d3))


def _pre_norm(x_ref, mod_ref, norm_pre_ref):
    bb, tc, d = x_ref.shape
    x = x_ref[...]
    h = _rms(x, norm_pre_ref[...]) * (1.0 + mod_ref[:, 1:2, :]) + mod_ref[:, 0:1, :]
    return h.reshape(bb * tc, d)


def _post_residual(x_ref, mod_ref, norm_post_ref, y, o_ref):
    bb, tc, d = x_ref.shape
    yn = _rms(y, norm_post_ref[...]).reshape(bb, tc, d)
    o_ref[...] = x_ref[...] + mod_ref[:, 2:3, :] * yn


def _causal_conv(ext_ref, bi, tc, w_ref):
    acc = ext_ref[bi, CONV_PAD - 3:CONV_PAD - 3 + tc, :] * w_ref[0:1, :]
    for j in range(1, CONV_W):
        lo = CONV_PAD - 3 + j
        acc = acc + ext_ref[bi, lo:lo + tc, :] * w_ref[j:j + 1, :]
    return acc


def _even_body(*refs, has_state, n_past_valid, pos0, chunk, n_tiles, k_out_rows):
    if has_state:
        (x_ref, mod_ref, npre_ref, npost_ref, win_ref, cw_ref, cb_ref, wg_ref, ba_ref, bx_ref,
         lam_ref, inv_ref, sinks_ref, wout_ref, conv0_ref, h0_ref, k0_ref, v0_ref,
         y_ref, nconv_ref, nh_ref, nk_ref, nv_ref,
         u_ref, ext_ref, a_ref, b_ref, hs_ref, knat_ref, vnat_ref, q_ref, mix_ref) = refs
    else:
        (x_ref, mod_ref, npre_ref, npost_ref, win_ref, cw_ref, cb_ref, wg_ref, ba_ref, bx_ref,
         lam_ref, inv_ref, sinks_ref, wout_ref,
         y_ref, nconv_ref, nh_ref, nk_ref, nv_ref,
         u_ref, ext_ref, a_ref, b_ref, hs_ref, knat_ref, vnat_ref, q_ref, mix_ref) = refs
    bb, tc, _ = x_ref.shape
    lw = a_ref.shape[1]
    aw = q_ref.shape[1]
    kvw = knat_ref.shape[2]
    t = pl.program_id(1)
    cpt = tc // chunk
    m = WINDOW + chunk
    o_q, o_k, o_v, o_gb = 2 * lw, 2 * lw + aw, 2 * lw + aw + kvw, 2 * lw + aw + 2 * kvw

    @pl.when(t == 0)
    def _():
        if has_state:
            ext_ref[:, CONV_PAD - 3:CONV_PAD, :] = conv0_ref[...]
            nh_ref[...] = h0_ref[...]
            knat_ref[:, 0:WINDOW, :] = k0_ref[...]
            vnat_ref[:, 0:WINDOW, :] = v0_ref[...]
        else:
            ext_ref[:, 0:CONV_PAD, :] = jnp.zeros((bb, CONV_PAD, lw), F32)
            nh_ref[...] = jnp.zeros(nh_ref.shape, F32)
            knat_ref[:, 0:WINDOW, :] = jnp.zeros((bb, WINDOW, kvw), F32)
            vnat_ref[:, 0:WINDOW, :] = jnp.zeros((bb, WINDOW, kvw), F32)

    hmod = _pre_norm(x_ref, mod_ref, npre_ref).astype(BF16)
    u_ref[...] = _dot(hmod, win_ref[...])

    pos = (pos0 + t * tc + lax.broadcasted_iota(jnp.int32, (tc, LANES), 0)).astype(F32)
    ang = pos * inv_ref[...]
    lane = lax.broadcasted_iota(jnp.int32, (tc, LANES), 1) % HEAD_DIM
    cos_t = jnp.cos(ang)
    sin_t = jnp.sin(ang)
    half = ROT_DIM // 2
    sin_a = jnp.where(lane < half, -sin_t, 0.0)
    sin_b = jnp.where(lane >= half, sin_t, 0.0)

    def rope(xcol):
        return (xcol * cos_t + pltpu.roll(xcol, LANES - half, axis=1) * sin_a
                + pltpu.roll(xcol, half, axis=1) * sin_b)

    neg_c_softplus = -LRU_C * jax.nn.softplus(-lam_ref[...])
    row8 = lax.broadcasted_iota(jnp.int32, (SUBLANES, lw), 0)

    def per_batch(bi):
        rows = _rows(bi, tc)
        ext_ref[bi, CONV_PAD:CONV_PAD + tc, :] = u_ref[rows, 0:lw]
        xc = _causal_conv(ext_ref, bi, tc, cw_ref) + cb_ref[...]
        nconv_ref[bi, :, :] = ext_ref[bi, CONV_PAD + tc - 3:CONV_PAD + tc, :]
        ext_ref[bi, CONV_PAD - 3:CONV_PAD, :] = ext_ref[bi, CONV_PAD + tc - 3:CONV_PAD + tc, :]
        xcb = xc.astype(BF16)
        halves = [_dot(xcb[:, i * MXU_TILE:(i + 1) * MXU_TILE], wg_ref[i]) for i in range(lw // MXU_TILE)]
        r = jax.nn.sigmoid(jnp.concatenate([g[:, :MXU_TILE] for g in halves], axis=1) + ba_ref[...])
        ig = jax.nn.sigmoid(jnp.concatenate([g[:, MXU_TILE:] for g in halves], axis=1) + bx_ref[...])
        log_a = r * neg_c_softplus
        a_ref[...] = jnp.exp(log_a)
        b_ref[...] = jnp.sqrt(-_expm1(2.0 * log_a)) * (ig * xc)

        def scan_block(j, hc):
            r0 = pl.multiple_of(j * SUBLANES, SUBLANES)
            a = a_ref[pl.ds(r0, SUBLANES), :]
            b = b_ref[pl.ds(r0, SUBLANES), :]
            for sft in (1, 2, 4):
                keep = row8 >= sft
                b = jnp.where(keep, a * pltpu.roll(b, sft, axis=0) + b, b)
                a = jnp.where(keep, a * pltpu.roll(a, sft, axis=0), a)
            h = a * hc + b
            hs_ref[pl.ds(r0, SUBLANES), :] = h
            return h[SUBLANES - 1:SUBLANES, :]

        nh_ref[bi, :, :] = lax.fori_loop(0, tc // SUBLANES, scan_block, nh_ref[bi, :, :])
        mix_ref[rows, 0:lw] = (hs_ref[...] * _silu(u_ref[rows, lw:2 * lw])).astype(BF16)
        for j in range(aw // LANES):
            qcol = rope(u_ref[rows, o_q + j * LANES:o_q + (j + 1) * LANES])
            q_ref[rows, j * LANES:(j + 1) * LANES] = (qcol * (HEAD_DIM ** -0.5)).astype(BF16)
        knat_ref[bi, WINDOW:WINDOW + tc, :] = rope(u_ref[rows, o_k:o_k + kvw])
        vnat_ref[bi, WINDOW:WINDOW + tc, :] = u_ref[rows, o_v:o_v + kvw]
        nk_ref[bi, :, :] = knat_ref[bi, WINDOW + tc - k_out_rows:WINDOW + tc, :]
        nv_ref[bi, :, :] = vnat_ref[bi, WINDOW + tc - k_out_rows:WINDOW + tc, :]

    _for_each(bb, per_batch)

    lo_lane = lax.broadcasted_iota(jnp.int32, (m, LANES), 1) < HEAD_DIM
    lo_q = lax.broadcasted_iota(jnp.int32, (chunk, LANES), 1) < HEAD_DIM
    group = (aw // HEAD_DIM) // ATT_KV_HEADS

    def per_segment(s):
        bi, c = _split_segment(s, bb, cpt)
        rows = _rows(s, chunk)
        krows = _rows(c, chunk, m)
        kseg = knat_ref[bi, krows, :]
        vseg = vnat_ref[bi, krows, :]
        krot = pltpu.roll(kseg, HEAD_DIM, axis=1)
        vrot = pltpu.roll(vseg, HEAD_DIM, axis=1)
        if n_past_valid < WINDOW:
            key_pos = t * tc + c * chunk - WINDOW + lax.broadcasted_iota(jnp.int32, (1, m), 1)
            valid = key_pos >= -n_past_valid
        for h in range(ATT_KV_HEADS):
            kd = (jnp.where(lo_lane, kseg, krot) if h == 0 else jnp.where(lo_lane, krot, kseg)).astype(BF16)
            v_lo = (jnp.where(lo_lane, vseg, 0.0) if h == 0 else jnp.where(lo_lane, vrot, 0.0)).astype(BF16)
            v_hi = (jnp.where(lo_lane, 0.0, vrot) if h == 0 else jnp.where(lo_lane, 0.0, vseg)).astype(BF16)
            cols = [q_ref[rows, (h * group // 2 + j) * LANES:(h * group // 2 + j + 1) * LANES]
                    for j in range(group // 2)]
            zero = jnp.zeros_like(cols[0])
            qstack = jnp.concatenate(
                [part for qc in cols for part in (jnp.where(lo_q, qc, zero), jnp.where(lo_q, zero, qc))], axis=0)
            sc = _dot_nt(qstack, kd)
            if n_past_valid < WINDOW:
                sc = jnp.where(valid, sc, NEG_INF)
            sink = jnp.concatenate(
                [jnp.full((chunk, 1), sinks_ref[h * group + g], F32) for g in range(group)], axis=0)
            mx = jnp.maximum(jnp.max(sc, axis=-1, keepdims=True), sink)
            p = jnp.exp(sc - mx)
            p = (p / (jnp.sum(p, axis=-1, keepdims=True) + jnp.exp(sink - mx))).astype(BF16)
            for j in range(group // 2):
                o = (_dot(p[(2 * j) * chunk:(2 * j + 1) * chunk], v_lo)
                     + _dot(p[(2 * j + 1) * chunk:(2 * j + 2) * chunk], v_hi))
                col = h * group // 2 + j
                gate = _silu(u_ref[rows, o_gb + col * LANES:o_gb + (col + 1) * LANES])
                mix_ref[rows, lw + col * LANES:lw + (col + 1) * LANES] = (o * gate).astype(BF16)

    _for_each(bb * cpt, per_segment)

    if n_tiles > 1:
        knat_ref[:, 0:WINDOW, :] = knat_ref[:, tc:tc + WINDOW, :]
        vnat_ref[:, 0:WINDOW, :] = vnat_ref[:, tc:tc + WINDOW, :]

    y = _dot(mix_ref[...], wout_ref[...])
    _post_residual(x_ref, mod_ref, npost_ref, y, y_ref)


def _const_spec(shape):
    zeros = (0,) * len(shape)
    return pl.BlockSpec(shape, lambda b, t: zeros)


def _even_layer(x, mod, consts, state, *, tile, chunk, batch_block, n_past_valid, pos0, k_out_rows):
    bsz, seq, d = x.shape
    (npre, npost, win, cw, cb, wg, b_a, b_x, lam, inv_row, sinks, wout) = consts
    lw = cw.shape[1]
    kvw = ATT_KV_HEADS * HEAD_DIM
    aw = (win.shape[1] - 2 * lw - 2 * kvw) // 2
    n_tiles = seq // tile
    assert seq % tile == 0 and tile % chunk == 0 and bsz % batch_block == 0
    assert n_tiles == 1 or tile >= WINDOW
    rows = batch_block * tile
    has_state = state is not None

    def bspec(shape):
        nd = len(shape)
        return pl.BlockSpec((batch_block,) + shape, lambda b, t: (b,) + (0,) * nd)

    in_specs = [
        pl.BlockSpec((batch_block, tile, d), lambda b, t: (b, t, 0)),
        bspec((3, d)),
        _const_spec(npre.shape), _const_spec(npost.shape), _const_spec(win.shape),
        _const_spec(cw.shape), _const_spec(cb.shape), _const_spec(wg.shape),
        _const_spec(b_a.shape), _const_spec(b_x.shape), _const_spec(lam.shape), _const_spec(inv_row.shape),
        pl.BlockSpec(memory_space=pltpu.SMEM),
        _const_spec(wout.shape),
    ]
    args = [x, mod, npre, npost, win, cw, cb, wg, b_a, b_x, lam, inv_row, sinks, wout]
    if has_state:
        in_specs += [bspec((CONV_W - 1, lw)), bspec((1, lw)), bspec((WINDOW, kvw)), bspec((WINDOW, kvw))]
        args += list(state)
    out_specs = [
        pl.BlockSpec((batch_block, tile, d), lambda b, t: (b, t, 0)),
        bspec((CONV_W - 1, lw)), bspec((1, lw)), bspec((k_out_rows, kvw)), bspec((k_out_rows, kvw)),
    ]
    out_shape = [
        jax.ShapeDtypeStruct((bsz, seq, d), F32),
        jax.ShapeDtypeStruct((bsz, CONV_W - 1, lw), F32),
        jax.ShapeDtypeStruct((bsz, 1, lw), F32),
        jax.ShapeDtypeStruct((bsz, k_out_rows, kvw), F32),
        jax.ShapeDtypeStruct((bsz, k_out_rows, kvw), F32),
    ]
    scratch = [
        pltpu.VMEM((rows, win.shape[1]), F32),
        pltpu.VMEM((batch_block, CONV_PAD + tile, lw), F32),
        pltpu.VMEM((tile, lw), F32), pltpu.VMEM((tile, lw), F32), pltpu.VMEM((tile, lw), F32),
        pltpu.VMEM((batch_block, WINDOW + tile, kvw), F32),
        pltpu.VMEM((batch_block, WINDOW + tile, kvw), F32),
        pltpu.VMEM((rows, aw), BF16),
        pltpu.VMEM((rows, lw + aw), BF16),
    ]
    body = functools.partial(_even_body, has_state=has_state, n_past_valid=n_past_valid, pos0=pos0,
                             chunk=chunk, n_tiles=n_tiles, k_out_rows=k_out_rows)
    return pl.pallas_call(
        body,
        grid=(bsz // batch_block, n_tiles),
        in_specs=in_specs,
        out_specs=out_specs,
        out_shape=out_shape,
        scratch_shapes=scratch,
        compiler_params=pltpu.CompilerParams(
            dimension_semantics=("arbitrary", "arbitrary"), vmem_limit_bytes=VMEM_LIMIT_BYTES),
        name="even_layer_state" if has_state else "even_layer_prompt",
    )(*args)


def _gdn_groups(chunk, n_heads):
    hpg = max(1, min(n_heads, MXU_TILE // chunk))
    assert n_heads % hpg == 0
    return hpg, n_heads // hpg


def _odd_body(*refs, has_state, chunk):
    if has_state:
        (x_ref, mod_ref, npre_ref, npost_ref, win_ref, wba_ref, cw_ref, alog_ref, dtb_ref, hn_ref, wout_ref,
         conv0_ref, s0_ref, y_ref, nconv_ref, ns_ref, *scratch) = refs
    else:
        (x_ref, mod_ref, npre_ref, npost_ref, win_ref, wba_ref, cw_ref, alog_ref, dtb_ref, hn_ref, wout_ref,
         y_ref, nconv_ref, ns_ref, *scratch) = refs
    (u_ref, ext_ref, q_ref, k_ref, kb_ref, kbe_ref, vb_ref, qe_ref, kd_ref, beta_ref, gcum_ref,
     wq_ref, uu_ref, qk_ref, o_ref, mix_ref) = scratch
    bb, tc, _ = x_ref.shape
    n_heads = ns_ref.shape[1]
    kw = n_heads * GDN_DK
    vw = n_heads * GDN_DV
    cch = 2 * kw + vw
    rows_all = bb * tc
    t = pl.program_id(1)
    cpt = tc // chunk
    n_steps = (chunk - 1).bit_length()
    assert n_steps >= 2
    hpg, n_groups = _gdn_groups(chunk, n_heads)
    lw = hpg * chunk

    @pl.when(t == 0)
    def _():
        if has_state:
            ext_ref[:, CONV_PAD - 3:CONV_PAD, :] = conv0_ref[...]
            ns_ref[...] = s0_ref[...]
        else:
            ext_ref[:, 0:CONV_PAD, :] = jnp.zeros((bb, CONV_PAD, cch), F32)
            ns_ref[...] = jnp.zeros(ns_ref.shape, F32)

    hmod = _pre_norm(x_ref, mod_ref, npre_ref).astype(BF16)
    u_ref[...] = _dot(hmod, win_ref[...])
    ba = _dot(hmod, wba_ref[...])
    beta_ref[...] = jax.nn.sigmoid(ba)
    g = -jnp.exp(alog_ref[...]) * jax.nn.softplus(ba + dtb_ref[...])
    row_in_chunk = lax.broadcasted_iota(jnp.int32, (rows_all, LANES), 0) % chunk
    sft = 1
    while sft < chunk:
        g = jnp.where(row_in_chunk >= sft, g + pltpu.roll(g, sft, axis=0), g)
        sft *= 2
    gcum_ref[...] = g

    def per_batch(bi):
        rows = _rows(bi, tc)
        ext_ref[bi, CONV_PAD:CONV_PAD + tc, :] = u_ref[rows, 0:cch]
        qkv = _silu(_causal_conv(ext_ref, bi, tc, cw_ref))
        nconv_ref[bi, :, :] = ext_ref[bi, CONV_PAD + tc - 3:CONV_PAD + tc, :]
        ext_ref[bi, CONV_PAD - 3:CONV_PAD, :] = ext_ref[bi, CONV_PAD + tc - 3:CONV_PAD + tc, :]
        beta_b = beta_ref[rows, :]
        gc_b = gcum_ref[rows, :]
        eg_b = jnp.exp(gc_b)
        gc3 = gc_b.reshape(cpt, chunk, LANES)
        dec_b = jnp.exp(gc3[:, chunk - 1:chunk, :] - gc3).reshape(tc, LANES)
        for h in range(n_heads):
            hc = slice(h * GDN_DK, (h + 1) * GDN_DK)
            qh = qkv[:, hc]
            kh = qkv[:, kw + h * GDN_DK:kw + (h + 1) * GDN_DK]
            vh = qkv[:, 2 * kw + h * GDN_DV:2 * kw + (h + 1) * GDN_DV]
            qh = qh * lax.rsqrt(jnp.sum(qh * qh, axis=-1, keepdims=True) + EPS) * (GDN_DK ** -0.5)
            kh = kh * lax.rsqrt(jnp.sum(kh * kh, axis=-1, keepdims=True) + EPS)
            beta = beta_b[:, h:h + 1]
            eg = eg_b[:, n_heads + h:n_heads + h + 1]
            kb = kh * beta
            q_ref[rows, hc] = qh.astype(BF16)
            k_ref[rows, hc] = kh.astype(BF16)
            kb_ref[rows, hc] = kb.astype(BF16)
            kbe_ref[rows, hc] = (kb * eg).astype(BF16)
            vb_ref[rows, hc] = (vh * beta).astype(BF16)
            qe_ref[rows, hc] = (qh * eg).astype(BF16)
            kd_ref[rows, hc] = (kh * dec_b[:, n_heads + h:n_heads + h + 1]).astype(BF16)

    _for_each(bb, per_batch)

    lane_w = lax.broadcasted_iota(jnp.int32, (chunk, lw), 1)
    row_w = lax.broadcasted_iota(jnp.int32, (chunk, lw), 0)
    blk_w = lane_w // chunk
    col_w = lane_w % chunk
    diag_w = row_w == col_w
    incl_w = row_w >= col_w
    strict_w = row_w > col_w
    eye_w = jnp.where(diag_w, 1.0, 0.0).astype(F32)
    feat_blk = lax.broadcasted_iota(jnp.int32, (chunk, hpg * GDN_DK), 1) // GDN_DK

    def block_diag(m_b):
        return jnp.concatenate([jnp.where(blk_w == hh, m_b, jnp.zeros_like(m_b)) for hh in range(hpg)], axis=0)

    def head_cols(h):
        return slice(h * GDN_DK, (h + 1) * GDN_DK)

    def wy_factors(bi):
        units = []
        for c in range(cpt):
            seg = bi * cpt + c
            rows = _rows(seg, chunk)
            gt = gcum_ref[rows, :]
            for gi in range(n_groups):
                gcols = slice(gi * hpg * GDN_DK, (gi + 1) * hpg * GDN_DK)
                k4 = k_ref[rows, gcols]
                k_bd = jnp.concatenate(
                    [jnp.where(feat_blk == hh, k4, jnp.zeros_like(k4)) for hh in range(hpg)], axis=0)
                kk = _dot_nt(jnp.concatenate([kb_ref[rows, gcols], q_ref[rows, gcols]], axis=0), k_bd)
                gcol = jnp.zeros((chunk, lw), F32)
                for hh in range(hpg):
                    ln = n_heads + gi * hpg + hh
                    gcol = jnp.where(blk_w == hh, gt[:, ln:ln + 1], gcol)
                grow = jnp.sum(jnp.where(diag_w, gcol, 0.0), axis=0, keepdims=True)
                decay = jnp.exp(jnp.where(incl_w, gcol - grow, 0.0))
                p_b = (-(kk[:chunk] * jnp.where(strict_w, decay, 0.0))).astype(BF16)
                qk_ref[seg * n_groups + gi] = (kk[chunk:] * jnp.where(incl_w, decay, 0.0)).astype(BF16)
                units.append(dict(seg=seg, rows=rows, gi=gi, p=p_b, t=eye_w + p_b.astype(F32)))
        for un in units:
            un["p"] = _dot(un["p"], block_diag(un["p"])).astype(BF16)
        for _ in range(n_steps - 2):
            for un in units:
                out = _dot(jnp.concatenate([un["t"].astype(BF16), un["p"]], axis=0), block_diag(un["p"]))
                un["t"] = un["t"] + out[:chunk]
                un["p"] = out[chunk:].astype(BF16)
        for un in units:
            un["t"] = un["t"] + _dot(un["t"].astype(BF16), block_diag(un["p"]))
        for un in units:
            seg, rows, gi = un["seg"], un["rows"], un["gi"]
            heads = range(gi * hpg, (gi + 1) * hpg)
            rhs = jnp.concatenate(
                [jnp.concatenate([kbe_ref[rows, head_cols(h)], vb_ref[rows, head_cols(h)]], axis=1) for h in heads],
                axis=0)
            wu = _dot(block_diag(un["t"].astype(BF16)), rhs)
            for hh, h in enumerate(heads):
                blk = wu[hh * chunk:(hh + 1) * chunk]
                wq_ref[seg * n_heads + h] = jnp.concatenate(
                    [blk[:, :GDN_DK].astype(BF16), qe_ref[rows, head_cols(h)]], axis=0)
                uu_ref[rows, head_cols(h)] = blk[:, GDN_DK:]

    _for_each(bb, wy_factors)

    zero_b = jnp.zeros((chunk, GDN_DV), BF16)

    def recurrence(bi):
        state = [ns_ref[bi, h] for h in range(n_heads)]
        for c in range(cpt):
            seg = bi * cpt + c
            rows = _rows(seg, chunk)
            ws_qs = [_dot(wq_ref[seg * n_heads + h], state[h].astype(BF16)) for h in range(n_heads)]
            v_new = [(uu_ref[rows, head_cols(h)] - ws_qs[h][:chunk]).astype(BF16) for h in range(n_heads)]
            for gi in range(n_groups):
                heads = range(gi * hpg, (gi + 1) * hpg)
                v_bd = jnp.concatenate(
                    [jnp.concatenate([v_new[h] if j == hh else zero_b for j in range(hpg)], axis=1)
                     for hh, h in enumerate(heads)], axis=0)
                o = (jnp.concatenate([ws_qs[h][chunk:] for h in heads], axis=1)
                     + _dot(qk_ref[seg * n_groups + gi], v_bd))
                o_ref[rows, gi * hpg * GDN_DV:(gi + 1) * hpg * GDN_DV] = o
            eg_last = jnp.exp(gcum_ref[pl.ds(seg * chunk + chunk - 1, 1), :])
            state = [state[h] * eg_last[:, n_heads + h:n_heads + h + 1]
                     + _dot_tn(kd_ref[rows, head_cols(h)], v_new[h]) for h in range(n_heads)]
        for h in range(n_heads):
            ns_ref[bi, h] = state[h]

    _for_each(bb, recurrence)

    for h in range(n_heads):
        cols = head_cols(h)
        gate = _silu(u_ref[:, cch + h * GDN_DV:cch + (h + 1) * GDN_DV])
        mix_ref[:, cols] = (_rms(o_ref[:, cols], hn_ref[...]) * gate).astype(BF16)
    y = _dot(mix_ref[...], wout_ref[...])
    _post_residual(x_ref, mod_ref, npost_ref, y, y_ref)


def _odd_layer(x, mod, consts, state, *, tile, chunk, batch_block):
    bsz, seq, d = x.shape
    (npre, npost, win, wba, cw, alog, dtb, hn, wout) = consts
    cch = cw.shape[1]
    vw = wout.shape[0]
    n_heads = vw // GDN_DV
    n_tiles = seq // tile
    assert seq % tile == 0 and tile % chunk == 0 and bsz % batch_block == 0
    rows = batch_block * tile
    has_state = state is not None

    def bspec(shape):
        nd = len(shape)
        return pl.BlockSpec((batch_block,) + shape, lambda b, t: (b,) + (0,) * nd)

    in_specs = [
        pl.BlockSpec((batch_block, tile, d), lambda b, t: (b, t, 0)),
        bspec((3, d)),
        _const_spec(npre.shape), _const_spec(npost.shape), _const_spec(win.shape), _const_spec(wba.shape),
        _const_spec(cw.shape), _const_spec(alog.shape), _const_spec(dtb.shape), _const_spec(hn.shape),
        _const_spec(wout.shape),
    ]
    args = [x, mod, npre, npost, win, wba, cw, alog, dtb, hn, wout]
    if has_state:
        in_specs += [bspec((CONV_W - 1, cch)), bspec((n_heads, GDN_DK, GDN_DV))]
        args += list(state)
    out_specs = [
        pl.BlockSpec((batch_block, tile, d), lambda b, t: (b, t, 0)),
        bspec((CONV_W - 1, cch)), bspec((n_heads, GDN_DK, GDN_DV)),
    ]
    out_shape = [
        jax.ShapeDtypeStruct((bsz, seq, d), F32),
        jax.ShapeDtypeStruct((bsz, CONV_W - 1, cch), F32),
        jax.ShapeDtypeStruct((bsz, n_heads, GDN_DK, GDN_DV), F32),
    ]
    hpg, n_groups = _gdn_groups(chunk, n_heads)
    n_seg = rows // chunk
    scratch = (
        [pltpu.VMEM((rows, win.shape[1]), F32),
         pltpu.VMEM((batch_block, CONV_PAD + tile, cch), F32)]
        + [pltpu.VMEM((rows, vw), BF16)] * 7
        + [pltpu.VMEM((rows, LANES), F32)] * 2
        + [pltpu.VMEM((n_seg * n_heads, 2 * chunk, GDN_DK), BF16),
           pltpu.VMEM((rows, vw), F32),
           pltpu.VMEM((n_seg * n_groups, chunk, hpg * chunk), BF16),
           pltpu.VMEM((rows, vw), F32),
           pltpu.VMEM((rows, vw), BF16)]
    )
    body = functools.partial(_odd_body, has_state=has_state, chunk=chunk)
    return pl.pallas_call(
        body,
        grid=(bsz // batch_block, n_tiles),
        in_specs=in_specs,
        out_specs=out_specs,
        out_shape=out_shape,
        scratch_shapes=scratch,
        compiler_params=pltpu.CompilerParams(
            dimension_semantics=("arbitrary", "arbitrary"), vmem_limit_bytes=VMEM_LIMIT_BYTES),
        name="odd_layer_state" if has_state else "odd_layer_prompt",
    )(*args)


def _gate_weights(w_a, w_x):
    heads, blk, _ = w_a.shape
    per_tile = MXU_TILE // blk
    tiles = []
    for i in range(heads // per_tile):
        sl = slice(i * per_tile, (i + 1) * per_tile)
        tiles.append(jnp.concatenate(
            [jax.scipy.linalg.block_diag(*w_a[sl]), jax.scipy.linalg.block_diag(*w_x[sl])], axis=1))
    return jnp.stack(tiles).astype(BF16)


def _rope_inv_row():
    half = ROT_DIM // 2
    inv = ROPE_THETA ** (-(jnp.arange(half, dtype=F32) * 2.0 / ROT_DIM))
    per_head = jnp.concatenate([inv, inv, jnp.zeros((HEAD_DIM - ROT_DIM,), F32)])
    return jnp.tile(per_head, LANES // HEAD_DIM).reshape(1, LANES)


def kernel(x_prompt, x_sample, state_lru_conv, state_lru_h, cache_swa_k, cache_swa_v, state_gdn_conv, state_gdn_s, c_prompt, c_sample, ev_mod_w, ev_mod_b, ev_norm_pre, ev_norm_post, ev_w_in, lru_conv_w, lru_conv_b, lru_w_a, lru_b_a, lru_w_x, lru_b_x, lru_lambda, swa_sinks, ev_w_out, od_mod_w, od_mod_b, od_norm_pre, od_norm_post, od_w_in, gdn_conv_w, gdn_a_log, gdn_dt_bias, gdn_head_norm, od_w_out):
    bp, seq, d = x_prompt.shape
    bs, dec_seq, _ = x_sample.shape
    lw = lru_conv_w.shape[-1]
    kvw = ATT_KV_HEADS * HEAD_DIM
    n_heads = gdn_a_log.shape[-1]
    cch = gdn_conv_w.shape[-1]
    vw = od_w_out.shape[1]

    n_c = bp + bs
    n_pad = -n_c % 16
    c_all = jnp.concatenate([c_prompt, c_sample, jnp.zeros((n_pad, d), F32)], axis=0)
    mod_ev, mod_od = _modulation(c_all, ev_mod_w[0], ev_mod_b[0], od_mod_w[0], od_mod_b[0])
    mod_ev = mod_ev.reshape(n_c + n_pad, 3, d)
    mod_od = mod_od.reshape(n_c + n_pad, 3, d)

    row = lambda a: a.reshape(1, -1)
    ev_consts = (row(ev_norm_pre[0]), row(ev_norm_post[0]), ev_w_in[0].astype(BF16), lru_conv_w[0],
                 row(lru_conv_b[0]), _gate_weights(lru_w_a[0], lru_w_x[0]), row(lru_b_a[0]), row(lru_b_x[0]),
                 row(lru_lambda[0]), _rope_inv_row(), swa_sinks[0], ev_w_out[0].astype(BF16))
    w_in_od = od_w_in[0]
    pad_lanes = lambda a: jnp.pad(a, ((0, 0), (0, LANES - a.shape[1])))
    head_row = lambda a: jnp.pad(a.reshape(1, -1), ((0, 0), (n_heads, LANES - 2 * n_heads)))
    od_consts = (row(od_norm_pre[0]), row(od_norm_post[0]), w_in_od[:, :cch + vw].astype(BF16),
                 pad_lanes(w_in_od[:, cch + vw:]).astype(BF16), gdn_conv_w[0],
                 head_row(gdn_a_log[0]), head_row(gdn_dt_bias[0]), row(gdn_head_norm[0]),
                 od_w_out[0].astype(BF16))

    tile = min(PROMPT_TILE, seq)
    xp, lru_conv_p, lru_h_p, swa_k_p, swa_v_p = _even_layer(
        x_prompt, mod_ev[:bp], ev_consts, None, tile=tile, chunk=min(CHUNK, seq), batch_block=1,
        n_past_valid=0, pos0=0, k_out_rows=WINDOW)
    ev_state = (state_lru_conv[0], state_lru_h[0].reshape(bs, 1, lw),
                cache_swa_k[0].reshape(bs, WINDOW, kvw), cache_swa_v[0].reshape(bs, WINDOW, kvw))
    xs, lru_conv_s, lru_h_s, swa_k_s, swa_v_s = _even_layer(
        x_sample, mod_ev[bp:n_c], ev_consts, ev_state, tile=dec_seq, chunk=min(CHUNK, dec_seq), batch_block=bs,
        n_past_valid=WINDOW, pos0=PAST_LEN, k_out_rows=dec_seq)

    xp, gdn_conv_p, gdn_s_p = _odd_layer(
        xp, mod_od[:bp], od_consts, None, tile=tile, chunk=min(CHUNK, seq), batch_block=1)
    xs, gdn_conv_s, gdn_s_s = _odd_layer(
        xs, mod_od[bp:n_c], od_consts, (state_gdn_conv[0], state_gdn_s[0]),
        tile=dec_seq, chunk=min(CHUNK, dec_seq), batch_block=bs)

    kv_shape = lambda a: a.reshape(1, a.shape[0], a.shape[1], ATT_KV_HEADS, HEAD_DIM)
    return (xp, xs,
            lru_conv_p[None], lru_conv_s[None],
            lru_h_p.reshape(1, bp, lw), lru_h_s.reshape(1, bs, lw),
            kv_shape(swa_k_p), kv_shape(swa_k_s), kv_shape(swa_v_p), kv_shape(swa_v_s),
            gdn_conv_p[None], gdn_conv_s[None], gdn_s_p[None], gdn_s_s[None])
```

```python
import functools

import jax
import jax.numpy as jnp
from jax import lax
from jax.experimental import pallas as pl
from jax.experimental.pallas import tpu as pltpu

F32 = jnp.float32
BF16 = jnp.bfloat16

CHUNK = 64
EPS = 1e-6
CONV_W = 4
NEG_INF = -1e30
LRU_HEADS = 8
LRU_C = 8.0
HEAD_DIM = 64
ATT_KV_HEADS = 2
WINDOW = 128
ROT_DIM = HEAD_DIM // 4
ROPE_THETA = 500000.0
GDN_DK = 128
GDN_DV = 128
PAST_LEN = 4096

LANES = 128
SUBLANES = 8
MXU_TILE = 256
VMEM_LIMIT_BYTES = 56 * 1024 * 1024

PROMPT_TILE = 256
MOD_TILE = 512
CONV_PAD = SUBLANES


def _silu(x):
    return x * jax.nn.sigmoid(x)


def _expm1(x):
    u = jnp.exp(x)
    d = u - 1.0
    return jnp.where(u == 1.0, x, jnp.where(d == -1.0, -1.0, d * x / jnp.log(u)))


def _rms(x, g):
    return x * lax.rsqrt(jnp.mean(x * x, axis=-1, keepdims=True) + EPS) * g


def _dot(a, b):
    return jnp.dot(a, b, preferred_element_type=F32)


def _dot_nt(a, b):
    return lax.dot_general(a, b, (((1,), (1,)), ((), ())), preferred_element_type=F32)


def _dot_tn(a, b):
    return lax.dot_general(a, b, (((0,), (0,)), ((), ())), preferred_element_type=F32)


def _for_each(n, body):
    if n == 1:
        body(0)
    else:
        def step(i, carry):
            body(i)
            return carry
        lax.fori_loop(0, n, step, 0)


def _rows(i, n, size=None):
    size = n if size is None else size
    if isinstance(i, int):
        return pl.ds(i * n, size)
    return pl.ds(pl.multiple_of(i * n, n), size)


def _mod_body(c_ref, w0_ref, b0_ref, w1_ref, b1_ref, o0_ref, o1_ref):
    c = _silu(c_ref[...]).astype(BF16)
    o0_ref[...] = _dot(c, w0_ref[...].astype(BF16)) + b0_ref[...]
    o1_ref[...] = _dot(c, w1_ref[...].astype(BF16)) + b1_ref[...]


def _modulation(c_all, w0, b0, w1, b1):
    n, d = c_all.shape
    d3 = w0.shape[1]
    wspec = pl.BlockSpec((d, MOD_TILE), lambda j: (0, j))
    bspec = pl.BlockSpec((1, MOD_TILE), lambda j: (0, j))
    ospec = pl.BlockSpec((n, MOD_TILE), lambda j: (0, j))
    return pl.pallas_call(
        _mod_body,
        grid=(d3 // MOD_TILE,),
        in_specs=[pl.BlockSpec((n, d), lambda j: (0, 0)), wspec, bspec, wspec, bspec],
        out_specs=[ospec, ospec],
        out_shape=[jax.ShapeDtypeStruct((n, d3), F32)] * 2,
        name="adaln_modulation",
    )(c_all, w0, b0.reshape(1, d3), w1, b1.reshape(1, d3))


def _pre_norm(x_ref, mod_ref, norm_pre_ref):
    bb, tc, d = x_ref.shape
    x = x_ref[...]
    h = _rms(x, norm_pre_ref[...]) * (1.0 + mod_ref[:, 1:2, :]) + mod_ref[:, 0:1, :]
    return h.reshape(bb * tc, d)


def _post_residual(x_ref, mod_ref, norm_post_ref, y, o_ref):
    bb, tc, d = x_ref.shape
    yn = _rms(y, norm_post_ref[...]).reshape(bb, tc, d)
    o_ref[...] = x_ref[...] + mod_ref[:, 2:3, :] * yn


def _causal_conv(ext_ref, bi, tc, w_ref):
    ext = ext_ref[bi]
    acc = pltpu.roll(ext, CONV_W - 1, axis=0)[CONV_PAD:] * w_ref[0:1, :]
    for j in range(1, CONV_W - 1):
        acc = acc + pltpu.roll(ext, CONV_W - 1 - j, axis=0)[CONV_PAD:] * w_ref[j:j + 1, :]
    return acc + ext[CONV_PAD:] * w_ref[CONV_W - 1:CONV_W, :]


def _even_body(*refs, has_state, n_past_valid, pos0, chunk, n_tiles, k_out_rows):
    if has_state:
        (x_ref, mod_ref, npre_ref, npost_ref, win_ref, cw_ref, cb_ref, wg_ref, ba_ref, bx_ref,
         lam_ref, inv_ref, sinks_ref, wout_ref, conv0_ref, h0_ref, k0_ref, v0_ref,
         y_ref, nconv_ref, nh_ref, nk_ref, nv_ref,
         u_ref, ext_ref, a_ref, b_ref, hs_ref, knat_ref, vnat_ref, q_ref, mix_ref) = refs
    else:
        (x_ref, mod_ref, npre_ref, npost_ref, win_ref, cw_ref, cb_ref, wg_ref, ba_ref, bx_ref,
         lam_ref, inv_ref, sinks_ref, wout_ref,
         y_ref, nconv_ref, nh_ref, nk_ref, nv_ref,
         u_ref, ext_ref, a_ref, b_ref, hs_ref, knat_ref, vnat_ref, q_ref, mix_ref) = refs
    bb, tc, _ = x_ref.shape
    lw = a_ref.shape[1]
    aw = q_ref.shape[1]
    kvw = knat_ref.shape[2]
    t = pl.program_id(1)
    cpt = tc // chunk
    m = WINDOW + chunk
    o_q, o_k, o_v, o_gb = 2 * lw, 2 * lw + aw, 2 * lw + aw + kvw, 2 * lw + aw + 2 * kvw

    @pl.when(t == 0)
    def _():
        ext_ref[:, 0:CONV_PAD, :] = jnp.zeros((bb, CONV_PAD, lw), F32)
        if has_state:
            ext_ref[:, CONV_PAD - 3:CONV_PAD, :] = conv0_ref[...]
            nh_ref[...] = h0_ref[...]
            knat_ref[:, 0:WINDOW, :] = k0_ref[...]
            vnat_ref[:, 0:WINDOW, :] = v0_ref[...]
        else:
            nh_ref[...] = jnp.zeros(nh_ref.shape, F32)
            knat_ref[:, 0:WINDOW, :] = jnp.zeros((bb, WINDOW, kvw), F32)
            vnat_ref[:, 0:WINDOW, :] = jnp.zeros((bb, WINDOW, kvw), F32)

    hmod = _pre_norm(x_ref, mod_ref, npre_ref).astype(BF16)
    u_ref[...] = _dot(hmod, win_ref[...])

    pos = (pos0 + t * tc + lax.broadcasted_iota(jnp.int32, (tc, LANES), 0)).astype(F32)
    ang = pos * inv_ref[...]
    lane = lax.broadcasted_iota(jnp.int32, (tc, LANES), 1) % HEAD_DIM
    cos_t = jnp.cos(ang)
    sin_t = jnp.sin(ang)
    half = ROT_DIM // 2
    sin_a = jnp.where(lane < half, -sin_t, 0.0)
    sin_b = jnp.where(lane >= half, sin_t, 0.0)

    def rope(xcol):
        return (xcol * cos_t + pltpu.roll(xcol, LANES - half, axis=1) * sin_a
                + pltpu.roll(xcol, half, axis=1) * sin_b)

    neg_c_softplus = -LRU_C * jax.nn.softplus(-lam_ref[...])
    row8 = lax.broadcasted_iota(jnp.int32, (SUBLANES, lw), 0)

    def per_batch(bi):
        rows = _rows(bi, tc)
        ext_ref[bi, CONV_PAD:CONV_PAD + tc, :] = u_ref[rows, 0:lw]
        xc = _causal_conv(ext_ref, bi, tc, cw_ref) + cb_ref[...]
        nconv_ref[bi, :, :] = ext_ref[bi, CONV_PAD + tc - 3:CONV_PAD + tc, :]
        ext_ref[bi, CONV_PAD - 3:CONV_PAD, :] = ext_ref[bi, CONV_PAD + tc - 3:CONV_PAD + tc, :]
        xcb = xc.astype(BF16)
        halves = [_dot(xcb[:, i * MXU_TILE:(i + 1) * MXU_TILE], wg_ref[i]) for i in range(lw // MXU_TILE)]
        r = jax.nn.sigmoid(jnp.concatenate([g[:, :MXU_TILE] for g in halves], axis=1) + ba_ref[...])
        ig = jax.nn.sigmoid(jnp.concatenate([g[:, MXU_TILE:] for g in halves], axis=1) + bx_ref[...])
        log_a = r * neg_c_softplus
        a_ref[...] = jnp.exp(log_a)
        b_ref[...] = jnp.sqrt(-_expm1(2.0 * log_a)) * (ig * xc)

        def scan_block(j, hc):
            r0 = pl.multiple_of(j * SUBLANES, SUBLANES)
            a = a_ref[pl.ds(r0, SUBLANES), :]
            b = b_ref[pl.ds(r0, SUBLANES), :]
            for sft in (1, 2, 4):
                keep = row8 >= sft
                b = jnp.where(keep, a * pltpu.roll(b, sft, axis=0) + b, b)
                a = jnp.where(keep, a * pltpu.roll(a, sft, axis=0), a)
            h = a * hc + b
            hs_ref[pl.ds(r0, SUBLANES), :] = h
            return h[SUBLANES - 1:SUBLANES, :]

        nh_ref[bi, :, :] = lax.fori_loop(0, tc // SUBLANES, scan_block, nh_ref[bi, :, :])
        mix_ref[rows, 0:lw] = (hs_ref[...] * _silu(u_ref[rows, lw:2 * lw])).astype(BF16)
        for j in range(aw // LANES):
            qcol = rope(u_ref[rows, o_q + j * LANES:o_q + (j + 1) * LANES])
            q_ref[rows, j * LANES:(j + 1) * LANES] = (qcol * (HEAD_DIM ** -0.5)).astype(BF16)
        knat_ref[bi, WINDOW:WINDOW + tc, :] = rope(u_ref[rows, o_k:o_k + kvw])
        vnat_ref[bi, WINDOW:WINDOW + tc, :] = u_ref[rows, o_v:o_v + kvw]
        nk_ref[bi, :, :] = knat_ref[bi, WINDOW + tc - k_out_rows:WINDOW + tc, :]
        nv_ref[bi, :, :] = vnat_ref[bi, WINDOW + tc - k_out_rows:WINDOW + tc, :]

    _for_each(bb, per_batch)

    lo_lane = lax.broadcasted_iota(jnp.int32, (m, LANES), 1) < HEAD_DIM
    lo_q = lax.broadcasted_iota(jnp.int32, (chunk, LANES), 1) < HEAD_DIM
    group = (aw // HEAD_DIM) // ATT_KV_HEADS

    sinks = [jnp.concatenate([jnp.full((chunk, 1), sinks_ref[h * group + g], F32) for g in range(group)], axis=0)
             for h in range(ATT_KV_HEADS)]

    lane_m = lax.broadcasted_iota(jnp.int32, (m, LANES), 1)

    def attend(bi):
        units = []
        for c in range(cpt):
            rows = _rows(bi * cpt + c, chunk)
            krows = _rows(c, chunk, m)
            kseg = knat_ref[bi, krows, :]
            vseg = vnat_ref[bi, krows, :]
            krot = pltpu.roll(kseg, HEAD_DIM, axis=1)
            vrot = pltpu.roll(vseg, HEAD_DIM, axis=1)
            valid = None
            if n_past_valid < WINDOW:
                key_pos = t * tc + c * chunk - WINDOW + lax.broadcasted_iota(jnp.int32, (1, m), 1)
                valid = key_pos >= -n_past_valid
            for h in range(ATT_KV_HEADS):
                kd = (jnp.where(lo_lane, kseg, krot) if h == 0 else jnp.where(lo_lane, krot, kseg)).astype(BF16)
                v_lo = jnp.where(lo_lane, vseg if h == 0 else vrot, jnp.where(lane_m == HEAD_DIM, 1.0, 0.0))
                v_hi = jnp.where(lo_lane, jnp.where(lane_m == 0, 1.0, 0.0), vrot if h == 0 else vseg)
                units.append(dict(rows=rows, h=h, valid=valid, kd=kd, v_lo=v_lo.astype(BF16),
                                  v_hi=v_hi.astype(BF16)))

        for un in units:
            rows, h = un["rows"], un["h"]
            cols = [q_ref[rows, (h * group // 2 + j) * LANES:(h * group // 2 + j + 1) * LANES]
                    for j in range(group // 2)]
            zero = jnp.zeros_like(cols[0])
            qstack = jnp.concatenate(
                [part for qc in cols for part in (jnp.where(lo_q, qc, zero), jnp.where(lo_q, zero, qc))], axis=0)
            sc = _dot_nt(qstack, un["kd"])
            un["sc"] = sc if un["valid"] is None else jnp.where(un["valid"], sc, NEG_INF)
        for un in units:
            un["mx"] = jnp.maximum(jnp.max(un["sc"], axis=-1, keepdims=True), sinks[un["h"]])
        for un in units:
            un["p"] = jnp.exp(un["sc"] - un["mx"]).astype(BF16)
            un["sink_p"] = jnp.exp(sinks[un["h"]] - un["mx"])
        for un in units:
            rows, h, p, sink_p = un["rows"], un["h"], un["p"], un["sink_p"]
            for j in range(group // 2):
                r_lo = slice((2 * j) * chunk, (2 * j + 1) * chunk)
                r_hi = slice((2 * j + 1) * chunk, (2 * j + 2) * chunk)
                o_lo = _dot(p[r_lo], un["v_lo"])
                o_hi = _dot(p[r_hi], un["v_hi"])
                den_lo = o_lo[:, HEAD_DIM:HEAD_DIM + 1] + sink_p[r_lo]
                den_hi = o_hi[:, 0:1] + sink_p[r_hi]
                o = jnp.where(lo_q, o_lo / den_lo, o_hi / den_hi)
                col = h * group // 2 + j
                gate = _silu(u_ref[rows, o_gb + col * LANES:o_gb + (col + 1) * LANES])
                mix_ref[rows, lw + col * LANES:lw + (col + 1) * LANES] = (o * gate).astype(BF16)

    _for_each(bb, attend)

    if n_tiles > 1:
        knat_ref[:, 0:WINDOW, :] = knat_ref[:, tc:tc + WINDOW, :]
        vnat_ref[:, 0:WINDOW, :] = vnat_ref[:, tc:tc + WINDOW, :]

    y = _dot(mix_ref[...], wout_ref[...])
    _post_residual(x_ref, mod_ref, npost_ref, y, y_ref)


def _const_spec(shape):
    zeros = (0,) * len(shape)
    return pl.BlockSpec(shape, lambda b, t: zeros)


def _even_layer(x, mod, consts, state, *, tile, chunk, batch_block, n_past_valid, pos0, k_out_rows):
    bsz, seq, d = x.shape
    (npre, npost, win, cw, cb, wg, b_a, b_x, lam, inv_row, sinks, wout) = consts
    lw = cw.shape[1]
    kvw = ATT_KV_HEADS * HEAD_DIM
    aw = (win.shape[1] - 2 * lw - 2 * kvw) // 2
    n_tiles = seq // tile
    assert seq % tile == 0 and tile % chunk == 0 and bsz % batch_block == 0
    assert n_tiles == 1 or tile >= WINDOW
    rows = batch_block * tile
    has_state = state is not None

    def bspec(shape):
        nd = len(shape)
        return pl.BlockSpec((batch_block,) + shape, lambda b, t: (b,) + (0,) * nd)

    in_specs = [
        pl.BlockSpec((batch_block, tile, d), lambda b, t: (b, t, 0)),
        bspec((3, d)),
        _const_spec(npre.shape), _const_spec(npost.shape), _const_spec(win.shape),
        _const_spec(cw.shape), _const_spec(cb.shape), _const_spec(wg.shape),
        _const_spec(b_a.shape), _const_spec(b_x.shape), _const_spec(lam.shape), _const_spec(inv_row.shape),
        pl.BlockSpec(memory_space=pltpu.SMEM),
        _const_spec(wout.shape),
    ]
    args = [x, mod, npre, npost, win, cw, cb, wg, b_a, b_x, lam, inv_row, sinks, wout]
    if has_state:
        in_specs += [bspec((CONV_W - 1, lw)), bspec((1, lw)), bspec((WINDOW, kvw)), bspec((WINDOW, kvw))]
        args += list(state)
    out_specs = [
        pl.BlockSpec((batch_block, tile, d), lambda b, t: (b, t, 0)),
        bspec((CONV_W - 1, lw)), bspec((1, lw)), bspec((k_out_rows, kvw)), bspec((k_out_rows, kvw)),
    ]
    out_shape = [
        jax.ShapeDtypeStruct((bsz, seq, d), F32),
        jax.ShapeDtypeStruct((bsz, CONV_W - 1, lw), F32),
        jax.ShapeDtypeStruct((bsz, 1, lw), F32),
        jax.ShapeDtypeStruct((bsz, k_out_rows, kvw), F32),
        jax.ShapeDtypeStruct((bsz, k_out_rows, kvw), F32),
    ]
    scratch = [
        pltpu.VMEM((rows, win.shape[1]), F32),
        pltpu.VMEM((batch_block, CONV_PAD + tile, lw), F32),
        pltpu.VMEM((tile, lw), F32), pltpu.VMEM((tile, lw), F32), pltpu.VMEM((tile, lw), F32),
        pltpu.VMEM((batch_block, WINDOW + tile, kvw), F32),
        pltpu.VMEM((batch_block, WINDOW + tile, kvw), F32),
        pltpu.VMEM((rows, aw), BF16),
        pltpu.VMEM((rows, lw + aw), BF16),
    ]
    body = functools.partial(_even_body, has_state=has_state, n_past_valid=n_past_valid, pos0=pos0,
                             chunk=chunk, n_tiles=n_tiles, k_out_rows=k_out_rows)
    return pl.pallas_call(
        body,
        grid=(bsz // batch_block, n_tiles),
        in_specs=in_specs,
        out_specs=out_specs,
        out_shape=out_shape,
        scratch_shapes=scratch,
        compiler_params=pltpu.CompilerParams(
            dimension_semantics=("arbitrary", "arbitrary"), vmem_limit_bytes=VMEM_LIMIT_BYTES),
        name="even_layer_state" if has_state else "even_layer_prompt",
    )(*args)


def _gdn_groups(chunk, n_heads):
    hpg = max(1, min(n_heads, MXU_TILE // chunk))
    assert n_heads % hpg == 0
    return hpg, n_heads // hpg


def _odd_body(*refs, has_state, chunk):
    if has_state:
        (x_ref, mod_ref, npre_ref, npost_ref, win_ref, wba_ref, cw_ref, alog_ref, dtb_ref, hn_ref, wout_ref,
         conv0_ref, s0_ref, y_ref, nconv_ref, ns_ref, *scratch) = refs
    else:
        (x_ref, mod_ref, npre_ref, npost_ref, win_ref, wba_ref, cw_ref, alog_ref, dtb_ref, hn_ref, wout_ref,
         y_ref, nconv_ref, ns_ref, *scratch) = refs
    (u_ref, ext_ref, q_ref, k_ref, kb_ref, kbe_ref, vb_ref, qe_ref, kd_ref, beta_ref, gcum_ref,
     wq_ref, uu_ref, qk_ref, o_ref, mix_ref) = scratch
    bb, tc, _ = x_ref.shape
    n_heads = ns_ref.shape[1]
    kw = n_heads * GDN_DK
    vw = n_heads * GDN_DV
    cch = 2 * kw + vw
    rows_all = bb * tc
    t = pl.program_id(1)
    cpt = tc // chunk
    n_steps = (chunk - 1).bit_length()
    assert n_steps >= 2
    hpg, n_groups = _gdn_groups(chunk, n_heads)
    lw = hpg * chunk

    @pl.when(t == 0)
    def _():
        ext_ref[:, 0:CONV_PAD, :] = jnp.zeros((bb, CONV_PAD, cch), F32)
        if has_state:
            ext_ref[:, CONV_PAD - 3:CONV_PAD, :] = conv0_ref[...]
            ns_ref[...] = s0_ref[...]
        else:
            ns_ref[...] = jnp.zeros(ns_ref.shape, F32)

    hmod = _pre_norm(x_ref, mod_ref, npre_ref).astype(BF16)
    u_ref[...] = _dot(hmod, win_ref[...])
    ba = _dot(hmod, wba_ref[...])
    beta_ref[...] = jax.nn.sigmoid(ba)
    g = -jnp.exp(alog_ref[...]) * jax.nn.softplus(ba + dtb_ref[...])
    row_in_chunk = lax.broadcasted_iota(jnp.int32, (rows_all, LANES), 0) % chunk
    sft = 1
    while sft < chunk:
        g = jnp.where(row_in_chunk >= sft, g + pltpu.roll(g, sft, axis=0), g)
        sft *= 2
    gcum_ref[...] = g

    def per_batch(bi):
        rows = _rows(bi, tc)
        ext_ref[bi, CONV_PAD:CONV_PAD + tc, :] = u_ref[rows, 0:cch]
        qkv = _silu(_causal_conv(ext_ref, bi, tc, cw_ref))
        nconv_ref[bi, :, :] = ext_ref[bi, CONV_PAD + tc - 3:CONV_PAD + tc, :]
        ext_ref[bi, CONV_PAD - 3:CONV_PAD, :] = ext_ref[bi, CONV_PAD + tc - 3:CONV_PAD + tc, :]
        beta_b = beta_ref[rows, :]
        gc_b = gcum_ref[rows, :]
        eg_b = jnp.exp(gc_b)
        gc3 = gc_b.reshape(cpt, chunk, LANES)
        dec_b = jnp.exp(gc3[:, chunk - 1:chunk, :] - gc3).reshape(tc, LANES)
        for h in range(n_heads):
            hc = slice(h * GDN_DK, (h + 1) * GDN_DK)
            qh = qkv[:, hc]
            kh = qkv[:, kw + h * GDN_DK:kw + (h + 1) * GDN_DK]
            vh = qkv[:, 2 * kw + h * GDN_DV:2 * kw + (h + 1) * GDN_DV]
            qh = qh * lax.rsqrt(jnp.sum(qh * qh, axis=-1, keepdims=True) + EPS) * (GDN_DK ** -0.5)
            kh = kh * lax.rsqrt(jnp.sum(kh * kh, axis=-1, keepdims=True) + EPS)
            beta = beta_b[:, h:h + 1]
            eg = eg_b[:, n_heads + h:n_heads + h + 1]
            kb = kh * beta
            q_ref[rows, hc] = qh.astype(BF16)
            k_ref[rows, hc] = kh.astype(BF16)
            kb_ref[rows, hc] = kb.astype(BF16)
            kbe_ref[rows, hc] = (kb * eg).astype(BF16)
            vb_ref[rows, hc] = (vh * beta).astype(BF16)
            qe_ref[rows, hc] = (qh * eg).astype(BF16)
            kd_ref[rows, hc] = (kh * dec_b[:, n_heads + h:n_heads + h + 1]).astype(BF16)

    _for_each(bb, per_batch)

    lane_w = lax.broadcasted_iota(jnp.int32, (chunk, lw), 1)
    row_w = lax.broadcasted_iota(jnp.int32, (chunk, lw), 0)
    blk_w = lane_w // chunk
    col_w = lane_w % chunk
    diag_w = row_w == col_w
    incl_w = row_w >= col_w
    strict_w = row_w > col_w
    eye_w = jnp.where(diag_w, 1.0, 0.0).astype(F32)
    feat_blk = lax.broadcasted_iota(jnp.int32, (chunk, hpg * GDN_DK), 1) // GDN_DK

    def block_diag(m_b):
        return jnp.concatenate([jnp.where(blk_w == hh, m_b, jnp.zeros_like(m_b)) for hh in range(hpg)], axis=0)

    def head_cols(h):
        return slice(h * GDN_DK, (h + 1) * GDN_DK)

    def wy_factors(bi):
        units = []
        for c in range(cpt):
            seg = bi * cpt + c
            rows = _rows(seg, chunk)
            gt = gcum_ref[rows, :]
            for gi in range(n_groups):
                gcols = slice(gi * hpg * GDN_DK, (gi + 1) * hpg * GDN_DK)
                k4 = k_ref[rows, gcols]
                k_bd = jnp.concatenate(
                    [jnp.where(feat_blk == hh, k4, jnp.zeros_like(k4)) for hh in range(hpg)], axis=0)
                kk = _dot_nt(jnp.concatenate([kb_ref[rows, gcols], q_ref[rows, gcols]], axis=0), k_bd)
                gcol = jnp.zeros((chunk, lw), F32)
                for hh in range(hpg):
                    ln = n_heads + gi * hpg + hh
                    gcol = jnp.where(blk_w == hh, gt[:, ln:ln + 1], gcol)
                grow = jnp.sum(jnp.where(diag_w, gcol, 0.0), axis=0, keepdims=True)
                decay = jnp.exp(jnp.where(incl_w, gcol - grow, 0.0))
                p_b = (-(kk[:chunk] * jnp.where(strict_w, decay, 0.0))).astype(BF16)
                qk_ref[seg * n_groups + gi] = (kk[chunk:] * jnp.where(incl_w, decay, 0.0)).astype(BF16)
                units.append(dict(seg=seg, rows=rows, gi=gi, p=p_b, t=eye_w + p_b.astype(F32)))
        for un in units:
            un["p"] = _dot(un["p"], block_diag(un["p"])).astype(BF16)
        for _ in range(n_steps - 2):
            for un in units:
                out = _dot(jnp.concatenate([un["t"].astype(BF16), un["p"]], axis=0), block_diag(un["p"]))
                un["t"] = un["t"] + out[:chunk]
                un["p"] = out[chunk:].astype(BF16)
        for un in units:
            un["t"] = un["t"] + _dot(un["t"].astype(BF16), block_diag(un["p"]))
        for un in units:
            seg, rows, gi = un["seg"], un["rows"], un["gi"]
            heads = range(gi * hpg, (gi + 1) * hpg)
            rhs = jnp.concatenate(
                [jnp.concatenate([kbe_ref[rows, head_cols(h)], vb_ref[rows, head_cols(h)]], axis=1) for h in heads],
                axis=0)
            wu = _dot(block_diag(un["t"].astype(BF16)), rhs)
            for hh, h in enumerate(heads):
                blk = wu[hh * chunk:(hh + 1) * chunk]
                wq_ref[seg * n_heads + h] = jnp.concatenate(
                    [blk[:, :GDN_DK].astype(BF16), qe_ref[rows, head_cols(h)]], axis=0)
                uu_ref[rows, head_cols(h)] = blk[:, GDN_DK:]

    _for_each(bb, wy_factors)

    zero_b = jnp.zeros((chunk, GDN_DV), BF16)

    def recurrence(bi):
        state = [ns_ref[bi, h] for h in range(n_heads)]
        for c in range(cpt):
            seg = bi * cpt + c
            rows = _rows(seg, chunk)
            ws_qs = [_dot(wq_ref[seg * n_heads + h], state[h].astype(BF16)) for h in range(n_heads)]
            v_new = [(uu_ref[rows, head_cols(h)] - ws_qs[h][:chunk]).astype(BF16) for h in range(n_heads)]
            for gi in range(n_groups):
                heads = range(gi * hpg, (gi + 1) * hpg)
                v_bd = jnp.concatenate(
                    [jnp.concatenate([v_new[h] if j == hh else zero_b for j in range(hpg)], axis=1)
                     for hh, h in enumerate(heads)], axis=0)
                o = (jnp.concatenate([ws_qs[h][chunk:] for h in heads], axis=1)
                     + _dot(qk_ref[seg * n_groups + gi], v_bd))
                o_ref[rows, gi * hpg * GDN_DV:(gi + 1) * hpg * GDN_DV] = o
            eg_last = jnp.exp(gcum_ref[pl.ds(seg * chunk + chunk - 1, 1), :])
            state = [state[h] * eg_last[:, n_heads + h:n_heads + h + 1]
                     + _dot_tn(kd_ref[rows, head_cols(h)], v_new[h]) for h in range(n_heads)]
        for h in range(n_heads):
            ns_ref[bi, h] = state[h]

    _for_each(bb, recurrence)

    for h in range(n_heads):
        cols = head_cols(h)
        gate = _silu(u_ref[:, cch + h * GDN_DV:cch + (h + 1) * GDN_DV])
        mix_ref[:, cols] = (_rms(o_ref[:, cols], hn_ref[...]) * gate).astype(BF16)
    y = _dot(mix_ref[...], wout_ref[...])
    _post_residual(x_ref, mod_ref, npost_ref, y, y_ref)


def _odd_layer(x, mod, consts, state, *, tile, chunk, batch_block):
    bsz, seq, d = x.shape
    (npre, npost, win, wba, cw, alog, dtb, hn, wout) = consts
    cch = cw.shape[1]
    vw = wout.shape[0]
    n_heads = vw // GDN_DV
    n_tiles = seq // tile
    assert seq % tile == 0 and tile % chunk == 0 and bsz % batch_block == 0
    rows = batch_block * tile
    has_state = state is not None

    def bspec(shape):
        nd = len(shape)
        return pl.BlockSpec((batch_block,) + shape, lambda b, t: (b,) + (0,) * nd)

    in_specs = [
        pl.BlockSpec((batch_block, tile, d), lambda b, t: (b, t, 0)),
        bspec((3, d)),
        _const_spec(npre.shape), _const_spec(npost.shape), _const_spec(win.shape), _const_spec(wba.shape),
        _const_spec(cw.shape), _const_spec(alog.shape), _const_spec(dtb.shape), _const_spec(hn.shape),
        _const_spec(wout.shape),
    ]
    args = [x, mod, npre, npost, win, wba, cw, alog, dtb, hn, wout]
    if has_state:
        in_specs += [bspec((CONV_W - 1, cch)), bspec((n_heads, GDN_DK, GDN_DV))]
        args += list(state)
    out_specs = [
        pl.BlockSpec((batch_block, tile, d), lambda b, t: (b, t, 0)),
        bspec((CONV_W - 1, cch)), bspec((n_heads, GDN_DK, GDN_DV)),
    ]
    out_shape = [
        jax.ShapeDtypeStruct((bsz, seq, d), F32),
        jax.ShapeDtypeStruct((bsz, CONV_W - 1, cch), F32),
        jax.ShapeDtypeStruct((bsz, n_heads, GDN_DK, GDN_DV), F32),
    ]
    hpg, n_groups = _gdn_groups(chunk, n_heads)
    n_seg = rows // chunk
    scratch = (
        [pltpu.VMEM((rows, win.shape[1]), F32),
         pltpu.VMEM((batch_block, CONV_PAD + tile, cch), F32)]
        + [pltpu.VMEM((rows, vw), BF16)] * 7
        + [pltpu.VMEM((rows, LANES), F32)] * 2
        + [pltpu.VMEM((n_seg * n_heads, 2 * chunk, GDN_DK), BF16),
           pltpu.VMEM((rows, vw), F32),
           pltpu.VMEM((n_seg * n_groups, chunk, hpg * chunk), BF16),
           pltpu.VMEM((rows, vw), F32),
           pltpu.VMEM((rows, vw), BF16)]
    )
    body = functools.partial(_odd_body, has_state=has_state, chunk=chunk)
    return pl.pallas_call(
        body,
        grid=(bsz // batch_block, n_tiles),
        in_specs=in_specs,
        out_specs=out_specs,
        out_shape=out_shape,
        scratch_shapes=scratch,
        compiler_params=pltpu.CompilerParams(
            dimension_semantics=("arbitrary", "arbitrary"), vmem_limit_bytes=VMEM_LIMIT_BYTES),
        name="odd_layer_state" if has_state else "odd_layer_prompt",
    )(*args)


def _gate_weights(w_a, w_x):
    heads, blk, _ = w_a.shape
    per_tile = MXU_TILE // blk
    tiles = []
    for i in range(heads // per_tile):
        sl = slice(i * per_tile, (i + 1) * per_tile)
        tiles.append(jnp.concatenate(
            [jax.scipy.linalg.block_diag(*w_a[sl]), jax.scipy.linalg.block_diag(*w_x[sl])], axis=1))
    return jnp.stack(tiles).astype(BF16)


def _rope_inv_row():
    half = ROT_DIM // 2
    inv = ROPE_THETA ** (-(jnp.arange(half, dtype=F32) * 2.0 / ROT_DIM))
    per_head = jnp.concatenate([inv, inv, jnp.zeros((HEAD_DIM - ROT_DIM,), F32)])
    return jnp.tile(per_head, LANES // HEAD_DIM).reshape(1, LANES)


def kernel(x_prompt, x_sample, state_lru_conv, state_lru_h, cache_swa_k, cache_swa_v, state_gdn_conv, state_gdn_s, c_prompt, c_sample, ev_mod_w, ev_mod_b, ev_norm_pre, ev_norm_post, ev_w_in, lru_conv_w, lru_conv_b, lru_w_a, lru_b_a, lru_w_x, lru_b_x, lru_lambda, swa_sinks, ev_w_out, od_mod_w, od_mod_b, od_norm_pre, od_norm_post, od_w_in, gdn_conv_w, gdn_a_log, gdn_dt_bias, gdn_head_norm, od_w_out):
    bp, seq, d = x_prompt.shape
    bs, dec_seq, _ = x_sample.shape
    lw = lru_conv_w.shape[-1]
    kvw = ATT_KV_HEADS * HEAD_DIM
    n_heads = gdn_a_log.shape[-1]
    cch = gdn_conv_w.shape[-1]
    vw = od_w_out.shape[1]

    n_c = bp + bs
    n_pad = -n_c % 16
    c_all = jnp.concatenate([c_prompt, c_sample, jnp.zeros((n_pad, d), F32)], axis=0)
    mod_ev, mod_od = _modulation(c_all, ev_mod_w[0], ev_mod_b[0], od_mod_w[0], od_mod_b[0])
    mod_ev = mod_ev.reshape(n_c + n_pad, 3, d)
    mod_od = mod_od.reshape(n_c + n_pad, 3, d)

    row = lambda a: a.reshape(1, -1)
    ev_consts = (row(ev_norm_pre[0]), row(ev_norm_post[0]), ev_w_in[0].astype(BF16), lru_conv_w[0],
                 row(lru_conv_b[0]), _gate_weights(lru_w_a[0], lru_w_x[0]), row(lru_b_a[0]), row(lru_b_x[0]),
                 row(lru_lambda[0]), _rope_inv_row(), swa_sinks[0], ev_w_out[0].astype(BF16))
    w_in_od = od_w_in[0]
    pad_lanes = lambda a: jnp.pad(a, ((0, 0), (0, LANES - a.shape[1])))
    head_row = lambda a: jnp.pad(a.reshape(1, -1), ((0, 0), (n_heads, LANES - 2 * n_heads)))
    od_consts = (row(od_norm_pre[0]), row(od_norm_post[0]), w_in_od[:, :cch + vw].astype(BF16),
                 pad_lanes(w_in_od[:, cch + vw:]).astype(BF16), gdn_conv_w[0],
                 head_row(gdn_a_log[0]), head_row(gdn_dt_bias[0]), row(gdn_head_norm[0]),
                 od_w_out[0].astype(BF16))

    tile = min(PROMPT_TILE, seq)
    xp, lru_conv_p, lru_h_p, swa_k_p, swa_v_p = _even_layer(
        x_prompt, mod_ev[:bp], ev_consts, None, tile=tile, chunk=min(CHUNK, seq), batch_block=1,
        n_past_valid=0, pos0=0, k_out_rows=WINDOW)
    ev_state = (state_lru_conv[0], state_lru_h[0].reshape(bs, 1, lw),
                cache_swa_k[0].reshape(bs, WINDOW, kvw), cache_swa_v[0].reshape(bs, WINDOW, kvw))
    xs, lru_conv_s, lru_h_s, swa_k_s, swa_v_s = _even_layer(
        x_sample, mod_ev[bp:n_c], ev_consts, ev_state, tile=dec_seq, chunk=min(CHUNK, dec_seq), batch_block=bs,
        n_past_valid=WINDOW, pos0=PAST_LEN, k_out_rows=dec_seq)

    xp, gdn_conv_p, gdn_s_p = _odd_layer(
        xp, mod_od[:bp], od_consts, None, tile=tile, chunk=min(CHUNK, seq), batch_block=1)
    xs, gdn_conv_s, gdn_s_s = _odd_layer(
        xs, mod_od[bp:n_c], od_consts, (state_gdn_conv[0], state_gdn_s[0]),
        tile=dec_seq, chunk=min(CHUNK, dec_seq), batch_block=bs)

    kv_shape = lambda a: a.reshape(1, a.shape[0], a.shape[1], ATT_KV_HEADS, HEAD_DIM)
    return (xp, xs,
            lru_conv_p[None], lru_conv_s[None],
            lru_h_p.reshape(1, bp, lw), lru_h_s.reshape(1, bs, lw),
            kv_shape(swa_k_p), kv_shape(swa_k_s), kv_shape(swa_v_p), kv_shape(swa_v_s),
            gdn_conv_p[None], gdn_conv_s[None], gdn_s_p[None], gdn_s_s[None])
```

```python
import functools

import jax
import jax.numpy as jnp
from jax import lax
from jax.experimental import pallas as pl
from jax.experimental.pallas import tpu as pltpu

F32 = jnp.float32
BF16 = jnp.bfloat16

CHUNK = 64
EPS = 1e-6
CONV_W = 4
NEG_INF = -1e30
LRU_HEADS = 8
LRU_C = 8.0
HEAD_DIM = 64
ATT_KV_HEADS = 2
WINDOW = 128
ROT_DIM = HEAD_DIM // 4
ROPE_THETA = 500000.0
GDN_DK = 128
GDN_DV = 128
PAST_LEN = 4096

LANES = 128
SUBLANES = 8
MXU_TILE = 256
VMEM_LIMIT_BYTES = 56 * 1024 * 1024

PROMPT_TILE = 256
MOD_TILE = 512
CONV_PAD = SUBLANES


def _silu(x):
    return x * jax.nn.sigmoid(x)


def _expm1(x):
    u = jnp.exp(x)
    d = u - 1.0
    return jnp.where(u == 1.0, x, jnp.where(d == -1.0, -1.0, d * x / jnp.log(u)))


def _rms(x, g):
    return x * lax.rsqrt(jnp.mean(x * x, axis=-1, keepdims=True) + EPS) * g


def _dot(a, b):
    return jnp.dot(a, b, preferred_element_type=F32)


def _dot_nt(a, b):
    return lax.dot_general(a, b, (((1,), (1,)), ((), ())), preferred_element_type=F32)


def _dot_tn(a, b):
    return lax.dot_general(a, b, (((0,), (0,)), ((), ())), preferred_element_type=F32)


def _for_each(n, body):
    if n == 1:
        body(0)
    else:
        def step(i, carry):
            body(i)
            return carry
        lax.fori_loop(0, n, step, 0)


def _rows(i, n, size=None):
    size = n if size is None else size
    if isinstance(i, int):
        return pl.ds(i * n, size)
    return pl.ds(pl.multiple_of(i * n, n), size)


def _mod_body(c_ref, w0_ref, b0_ref, w1_ref, b1_ref, o0_ref, o1_ref):
    c = _silu(c_ref[...]).astype(BF16)
    o0_ref[...] = _dot(c, w0_ref[...].astype(BF16)) + b0_ref[...]
    o1_ref[...] = _dot(c, w1_ref[...].astype(BF16)) + b1_ref[...]


def _modulation(c_all, w0, b0, w1, b1):
    n, d = c_all.shape
    d3 = w0.shape[1]
    wspec = pl.BlockSpec((d, MOD_TILE), lambda j: (0, j))
    bspec = pl.BlockSpec((1, MOD_TILE), lambda j: (0, j))
    ospec = pl.BlockSpec((n, MOD_TILE), lambda j: (0, j))
    return pl.pallas_call(
        _mod_body,
        grid=(d3 // MOD_TILE,),
        in_specs=[pl.BlockSpec((n, d), lambda j: (0, 0)), wspec, bspec, wspec, bspec],
        out_specs=[ospec, ospec],
        out_shape=[jax.ShapeDtypeStruct((n, d3), F32)] * 2,
        name="adaln_modulation",
    )(c_all, w0, b0.reshape(1, d3), w1, b1.reshape(1, d3))


def _pre_norm(x_ref, mod_ref, norm_pre_ref):
    bb, tc, d = x_ref.shape
    x = x_ref[...]
    h = _rms(x, norm_pre_ref[...]) * (1.0 + mod_ref[:, 1:2, :]) + mod_ref[:, 0:1, :]
    return h.reshape(bb * tc, d)


def _post_residual(x_ref, mod_ref, norm_post_ref, y, o_ref):
    bb, tc, d = x_ref.shape
    yn = _rms(y, norm_post_ref[...]).reshape(bb, tc, d)
    o_ref[...] = x_ref[...] + mod_ref[:, 2:3, :] * yn


def _causal_conv(ext_ref, bi, tc, w_ref):
    ext = ext_ref[bi]
    acc = pltpu.roll(ext, CONV_W - 1, axis=0)[CONV_PAD:] * w_ref[0:1, :]
    for j in range(1, CONV_W - 1):
        acc = acc + pltpu.roll(ext, CONV_W - 1 - j, axis=0)[CONV_PAD:] * w_ref[j:j + 1, :]
    return acc + ext[CONV_PAD:] * w_ref[CONV_W - 1:CONV_W, :]


def _even_body(*refs, has_state, n_past_valid, pos0, chunk, n_tiles, k_out_rows):
    refs = list(refs)
    x_ref = refs.pop(0)
    (mod_ref, npre_ref, npost_ref, win_ref, cw_ref, cb_ref, wg_ref, ba_ref, bx_ref,
     lam_ref, inv_ref, sinks_ref, wout_ref) = refs[:13]
    refs = refs[13:]
    if has_state:
        conv0_ref, h0_ref, k0_ref, v0_ref = refs[:4]
        refs = refs[4:]
    (y_ref, nconv_ref, nh_ref, nk_ref, nv_ref,
     u_ref, ext_ref, a_ref, b_ref, hs_ref, knat_ref, vnat_ref, q_ref, mix_ref) = refs
    bb, tc, _ = x_ref.shape
    lw = a_ref.shape[1]
    aw = q_ref.shape[1]
    kvw = knat_ref.shape[2]
    t = pl.program_id(1)
    cpt = tc // chunk
    m = WINDOW + chunk
    o_q, o_k, o_v, o_gb = 2 * lw, 2 * lw + aw, 2 * lw + aw + kvw, 2 * lw + aw + 2 * kvw

    @pl.when(t == 0)
    def _():
        ext_ref[:, 0:CONV_PAD, :] = jnp.zeros((bb, CONV_PAD, lw), F32)
        if has_state:
            ext_ref[:, CONV_PAD - 3:CONV_PAD, :] = conv0_ref[...]
            nh_ref[...] = h0_ref[...]
            knat_ref[:, 0:WINDOW, :] = k0_ref[...]
            vnat_ref[:, 0:WINDOW, :] = v0_ref[...]
        else:
            nh_ref[...] = jnp.zeros(nh_ref.shape, F32)
            knat_ref[:, 0:WINDOW, :] = jnp.zeros((bb, WINDOW, kvw), F32)
            vnat_ref[:, 0:WINDOW, :] = jnp.zeros((bb, WINDOW, kvw), F32)

    u_ref[...] = _dot(_pre_norm(x_ref, mod_ref, npre_ref).astype(BF16), win_ref[...])

    pos = (pos0 + t * tc + lax.broadcasted_iota(jnp.int32, (tc, LANES), 0)).astype(F32)
    ang = pos * inv_ref[...]
    lane = lax.broadcasted_iota(jnp.int32, (tc, LANES), 1) % HEAD_DIM
    cos_t = jnp.cos(ang)
    sin_t = jnp.sin(ang)
    half = ROT_DIM // 2
    sin_a = jnp.where(lane < half, -sin_t, 0.0)
    sin_b = jnp.where(lane >= half, sin_t, 0.0)

    def rope(xcol):
        return (xcol * cos_t + pltpu.roll(xcol, LANES - half, axis=1) * sin_a
                + pltpu.roll(xcol, half, axis=1) * sin_b)

    neg_c_softplus = -LRU_C * jax.nn.softplus(-lam_ref[...])
    row8 = lax.broadcasted_iota(jnp.int32, (SUBLANES, lw), 0)

    def per_batch(bi):
        rows = _rows(bi, tc)
        ext_ref[bi, CONV_PAD:CONV_PAD + tc, :] = u_ref[rows, 0:lw]
        xc = _causal_conv(ext_ref, bi, tc, cw_ref) + cb_ref[...]
        nconv_ref[bi, :, :] = ext_ref[bi, CONV_PAD + tc - 3:CONV_PAD + tc, :]
        ext_ref[bi, CONV_PAD - 3:CONV_PAD, :] = ext_ref[bi, CONV_PAD + tc - 3:CONV_PAD + tc, :]
        xcb = xc.astype(BF16)
        halves = [_dot(xcb[:, i * MXU_TILE:(i + 1) * MXU_TILE], wg_ref[i]) for i in range(lw // MXU_TILE)]
        r = jax.nn.sigmoid(jnp.concatenate([g[:, :MXU_TILE] for g in halves], axis=1) + ba_ref[...])
        ig = jax.nn.sigmoid(jnp.concatenate([g[:, MXU_TILE:] for g in halves], axis=1) + bx_ref[...])
        log_a = r * neg_c_softplus
        a_ref[...] = jnp.exp(log_a)
        b_ref[...] = jnp.sqrt(-_expm1(2.0 * log_a)) * (ig * xc)

        def scan_block(j, hc):
            r0 = pl.multiple_of(j * SUBLANES, SUBLANES)
            a = a_ref[pl.ds(r0, SUBLANES), :]
            b = b_ref[pl.ds(r0, SUBLANES), :]
            for sft in (1, 2, 4):
                keep = row8 >= sft
                b = jnp.where(keep, a * pltpu.roll(b, sft, axis=0) + b, b)
                a = jnp.where(keep, a * pltpu.roll(a, sft, axis=0), a)
            h = a * hc + b
            hs_ref[pl.ds(r0, SUBLANES), :] = h
            return h[SUBLANES - 1:SUBLANES, :]

        nh_ref[bi, :, :] = lax.fori_loop(0, tc // SUBLANES, scan_block, nh_ref[bi, :, :])
        mix_ref[rows, 0:lw] = (hs_ref[...] * _silu(u_ref[rows, lw:2 * lw])).astype(BF16)
        for j in range(aw // LANES):
            qcol = rope(u_ref[rows, o_q + j * LANES:o_q + (j + 1) * LANES])
            q_ref[rows, j * LANES:(j + 1) * LANES] = (qcol * (HEAD_DIM ** -0.5)).astype(BF16)
        knat_ref[bi, WINDOW:WINDOW + tc, :] = rope(u_ref[rows, o_k:o_k + kvw])
        vnat_ref[bi, WINDOW:WINDOW + tc, :] = u_ref[rows, o_v:o_v + kvw]
        nk_ref[bi, :, :] = knat_ref[bi, WINDOW + tc - k_out_rows:WINDOW + tc, :]
        nv_ref[bi, :, :] = vnat_ref[bi, WINDOW + tc - k_out_rows:WINDOW + tc, :]

    _for_each(bb, per_batch)

    lo_lane = lax.broadcasted_iota(jnp.int32, (m, LANES), 1) < HEAD_DIM
    lo_q = lax.broadcasted_iota(jnp.int32, (chunk, LANES), 1) < HEAD_DIM
    group = (aw // HEAD_DIM) // ATT_KV_HEADS

    sinks = [jnp.concatenate([jnp.full((chunk, 1), sinks_ref[h * group + g], F32) for g in range(group)], axis=0)
             for h in range(ATT_KV_HEADS)]

    lane_m = lax.broadcasted_iota(jnp.int32, (m, LANES), 1)

    def attend():
        units = []
        for bi, c in [(bi, c) for bi in range(bb) for c in range(cpt)]:
            rows = _rows(bi * cpt + c, chunk)
            krows = _rows(c, chunk, m)
            kseg = knat_ref[bi, krows, :]
            vseg = vnat_ref[bi, krows, :]
            krot = pltpu.roll(kseg, HEAD_DIM, axis=1)
            vrot = pltpu.roll(vseg, HEAD_DIM, axis=1)
            valid = None
            if n_past_valid < WINDOW:
                key_pos = t * tc + c * chunk - WINDOW + lax.broadcasted_iota(jnp.int32, (1, m), 1)
                valid = key_pos >= -n_past_valid
            for h in range(ATT_KV_HEADS):
                kd = (jnp.where(lo_lane, kseg, krot) if h == 0 else jnp.where(lo_lane, krot, kseg)).astype(BF16)
                v_lo = jnp.where(lo_lane, vseg if h == 0 else vrot, jnp.where(lane_m == HEAD_DIM, 1.0, 0.0))
                v_hi = jnp.where(lo_lane, jnp.where(lane_m == 0, 1.0, 0.0), vrot if h == 0 else vseg)
                units.append(dict(rows=rows, h=h, valid=valid, kd=kd, v_lo=v_lo.astype(BF16),
                                  v_hi=v_hi.astype(BF16)))

        for un in units:
            rows, h = un["rows"], un["h"]
            cols = [q_ref[rows, (h * group // 2 + j) * LANES:(h * group // 2 + j + 1) * LANES]
                    for j in range(group // 2)]
            zero = jnp.zeros_like(cols[0])
            qstack = jnp.concatenate(
                [part for qc in cols for part in (jnp.where(lo_q, qc, zero), jnp.where(lo_q, zero, qc))], axis=0)
            sc = _dot_nt(qstack, un["kd"])
            un["sc"] = sc if un["valid"] is None else jnp.where(un["valid"], sc, NEG_INF)
        for un in units:
            un["mx"] = jnp.maximum(jnp.max(un["sc"], axis=-1, keepdims=True), sinks[un["h"]])
        for un in units:
            un["p"] = jnp.exp(un["sc"] - un["mx"]).astype(BF16)
            un["sink_p"] = jnp.exp(sinks[un["h"]] - un["mx"])
        for un in units:
            rows, h, p, sink_p = un["rows"], un["h"], un["p"], un["sink_p"]
            for j in range(group // 2):
                r_lo = slice((2 * j) * chunk, (2 * j + 1) * chunk)
                r_hi = slice((2 * j + 1) * chunk, (2 * j + 2) * chunk)
                o_lo = _dot(p[r_lo], un["v_lo"])
                o_hi = _dot(p[r_hi], un["v_hi"])
                den_lo = o_lo[:, HEAD_DIM:HEAD_DIM + 1] + sink_p[r_lo]
                den_hi = o_hi[:, 0:1] + sink_p[r_hi]
                o = jnp.where(lo_q, o_lo / den_lo, o_hi / den_hi)
                col = h * group // 2 + j
                gate = _silu(u_ref[rows, o_gb + col * LANES:o_gb + (col + 1) * LANES])
                mix_ref[rows, lw + col * LANES:lw + (col + 1) * LANES] = (o * gate).astype(BF16)

    attend()

    if n_tiles > 1:
        knat_ref[:, 0:WINDOW, :] = knat_ref[:, tc:tc + WINDOW, :]
        vnat_ref[:, 0:WINDOW, :] = vnat_ref[:, tc:tc + WINDOW, :]

    y = _dot(mix_ref[...], wout_ref[...])
    _post_residual(x_ref, mod_ref, npost_ref, y, y_ref)


def _const_spec(shape):
    zeros = (0,) * len(shape)
    return pl.BlockSpec(shape, lambda b, t: zeros)


def _even_layer(x, mod, consts, state, *, tile, chunk, batch_block, n_past_valid, pos0, k_out_rows):
    bsz, seq, d = x.shape
    (npre, npost, win, cw, cb, wg, b_a, b_x, lam, inv_row, sinks, wout) = consts
    lw = cw.shape[1]
    kvw = ATT_KV_HEADS * HEAD_DIM
    aw = (win.shape[1] - 2 * lw - 2 * kvw) // 2
    n_tiles = seq // tile
    assert seq % tile == 0 and tile % chunk == 0 and bsz % batch_block == 0
    assert n_tiles == 1 or tile >= WINDOW
    rows = batch_block * tile
    has_state = state is not None

    def bspec(shape):
        nd = len(shape)
        return pl.BlockSpec((batch_block,) + shape, lambda b, t: (b,) + (0,) * nd)

    in_specs = [
        pl.BlockSpec((batch_block, tile, d), lambda b, t: (b, t, 0)),
        bspec((3, d)),
        _const_spec(npre.shape), _const_spec(npost.shape), _const_spec(win.shape),
        _const_spec(cw.shape), _const_spec(cb.shape), _const_spec(wg.shape),
        _const_spec(b_a.shape), _const_spec(b_x.shape), _const_spec(lam.shape), _const_spec(inv_row.shape),
        pl.BlockSpec(memory_space=pltpu.SMEM),
        _const_spec(wout.shape),
    ]
    args = [x, mod, npre, npost, win, cw, cb, wg, b_a, b_x, lam, inv_row, sinks, wout]
    if has_state:
        in_specs += [bspec((CONV_W - 1, lw)), bspec((1, lw)), bspec((WINDOW, kvw)), bspec((WINDOW, kvw))]
        args += list(state)
    out_specs = [
        pl.BlockSpec((batch_block, tile, d), lambda b, t: (b, t, 0)),
        bspec((CONV_W - 1, lw)), bspec((1, lw)), bspec((k_out_rows, kvw)), bspec((k_out_rows, kvw)),
    ]
    out_shape = [
        jax.ShapeDtypeStruct((bsz, seq, d), F32),
        jax.ShapeDtypeStruct((bsz, CONV_W - 1, lw), F32),
        jax.ShapeDtypeStruct((bsz, 1, lw), F32),
        jax.ShapeDtypeStruct((bsz, k_out_rows, kvw), F32),
        jax.ShapeDtypeStruct((bsz, k_out_rows, kvw), F32),
    ]
    scratch = [
        pltpu.VMEM((rows, win.shape[1]), F32),
        pltpu.VMEM((batch_block, CONV_PAD + tile, lw), F32),
        pltpu.VMEM((tile, lw), F32), pltpu.VMEM((tile, lw), F32), pltpu.VMEM((tile, lw), F32),
        pltpu.VMEM((batch_block, WINDOW + tile, kvw), F32),
        pltpu.VMEM((batch_block, WINDOW + tile, kvw), F32),
        pltpu.VMEM((rows, aw), BF16),
        pltpu.VMEM((rows, lw + aw), BF16),
    ]
    body = functools.partial(_even_body, has_state=has_state, n_past_valid=n_past_valid, pos0=pos0,
                             chunk=chunk, n_tiles=n_tiles, k_out_rows=k_out_rows)
    return pl.pallas_call(
        body,
        grid=(bsz // batch_block, n_tiles),
        in_specs=in_specs,
        out_specs=out_specs,
        out_shape=out_shape,
        scratch_shapes=scratch,
        compiler_params=pltpu.CompilerParams(
            dimension_semantics=("arbitrary", "arbitrary"), vmem_limit_bytes=VMEM_LIMIT_BYTES),
        name="even_layer_state" if has_state else "even_layer_prompt",
    )(*args)


def _gdn_groups(chunk, n_heads):
    hpg = max(1, min(n_heads, MXU_TILE // chunk))
    assert n_heads % hpg == 0
    return hpg, n_heads // hpg


def _odd_body(*refs, has_state, chunk):
    refs = list(refs)
    x_ref = refs.pop(0)
    (mod_ref, npre_ref, npost_ref, win_ref, wba_ref, cw_ref, alog_ref, dtb_ref, hn_ref, wout_ref) = refs[:10]
    refs = refs[10:]
    if has_state:
        conv0_ref, s0_ref = refs[:2]
        refs = refs[2:]
    (y_ref, nconv_ref, ns_ref,
     u_ref, ext_ref, q_ref, k_ref, kb_ref, kbe_ref, vb_ref, qe_ref, kd_ref, beta_ref, gcum_ref,
     wq_ref, uu_ref, qk_ref, mix_ref) = refs
    bb, tc, _ = x_ref.shape
    n_heads = ns_ref.shape[1]
    kw = n_heads * GDN_DK
    vw = n_heads * GDN_DV
    cch = 2 * kw + vw
    rows_all = bb * tc
    t = pl.program_id(1)
    cpt = tc // chunk
    n_steps = (chunk - 1).bit_length()
    assert n_steps >= 2
    hpg, n_groups = _gdn_groups(chunk, n_heads)
    lw = hpg * chunk

    @pl.when(t == 0)
    def _():
        ext_ref[:, 0:CONV_PAD, :] = jnp.zeros((bb, CONV_PAD, cch), F32)
        if has_state:
            ext_ref[:, CONV_PAD - 3:CONV_PAD, :] = conv0_ref[...]
            ns_ref[...] = s0_ref[...]
        else:
            ns_ref[...] = jnp.zeros(ns_ref.shape, F32)

    hmod = _pre_norm(x_ref, mod_ref, npre_ref).astype(BF16)
    u_ref[...] = _dot(hmod, win_ref[...])
    ba = _dot(hmod, wba_ref[...])
    beta_ref[...] = jax.nn.sigmoid(ba)
    g = -jnp.exp(alog_ref[...]) * jax.nn.softplus(ba + dtb_ref[...])
    row_in_chunk = lax.broadcasted_iota(jnp.int32, (rows_all, LANES), 0) % chunk
    sft = 1
    while sft < chunk:
        g = jnp.where(row_in_chunk >= sft, g + pltpu.roll(g, sft, axis=0), g)
        sft *= 2
    gcum_ref[...] = g

    def per_batch(bi):
        rows = _rows(bi, tc)
        ext_ref[bi, CONV_PAD:CONV_PAD + tc, :] = u_ref[rows, 0:cch]
        qkv = _silu(_causal_conv(ext_ref, bi, tc, cw_ref))
        nconv_ref[bi, :, :] = ext_ref[bi, CONV_PAD + tc - 3:CONV_PAD + tc, :]
        ext_ref[bi, CONV_PAD - 3:CONV_PAD, :] = ext_ref[bi, CONV_PAD + tc - 3:CONV_PAD + tc, :]
        beta_b = beta_ref[rows, :]
        gc_b = gcum_ref[rows, :]
        eg_b = jnp.exp(gc_b)
        gc3 = gc_b.reshape(cpt, chunk, LANES)
        dec_b = jnp.exp(gc3[:, chunk - 1:chunk, :] - gc3).reshape(tc, LANES)
        for h in range(n_heads):
            hc = slice(h * GDN_DK, (h + 1) * GDN_DK)
            qh = qkv[:, hc]
            kh = qkv[:, kw + h * GDN_DK:kw + (h + 1) * GDN_DK]
            vh = qkv[:, 2 * kw + h * GDN_DV:2 * kw + (h + 1) * GDN_DV]
            qh = qh * lax.rsqrt(jnp.sum(qh * qh, axis=-1, keepdims=True) + EPS) * (GDN_DK ** -0.5)
            kh = kh * lax.rsqrt(jnp.sum(kh * kh, axis=-1, keepdims=True) + EPS)
            beta = beta_b[:, h:h + 1]
            eg = eg_b[:, n_heads + h:n_heads + h + 1]
            kb = kh * beta
            q_ref[rows, hc] = qh.astype(BF16)
            k_ref[rows, hc] = kh.astype(BF16)
            kb_ref[rows, hc] = kb.astype(BF16)
            kbe_ref[rows, hc] = (kb * eg).astype(BF16)
            vb_ref[rows, hc] = (vh * beta).astype(BF16)
            qe_ref[rows, hc] = (qh * eg).astype(BF16)
            kd_ref[rows, hc] = (kh * dec_b[:, n_heads + h:n_heads + h + 1]).astype(BF16)

    _for_each(bb, per_batch)

    lane_w = lax.broadcasted_iota(jnp.int32, (chunk, lw), 1)
    row_w = lax.broadcasted_iota(jnp.int32, (chunk, lw), 0)
    blk_w = lane_w // chunk
    col_w = lane_w % chunk
    diag_w = row_w == col_w
    incl_w = row_w >= col_w
    strict_w = row_w > col_w
    eye_w = jnp.where(diag_w, 1.0, 0.0).astype(F32)
    feat_blk = lax.broadcasted_iota(jnp.int32, (chunk, hpg * GDN_DK), 1) // GDN_DK

    def block_diag(m_b):
        return jnp.concatenate([jnp.where(blk_w == hh, m_b, jnp.zeros_like(m_b)) for hh in range(hpg)], axis=0)

    def head_cols(h):
        return slice(h * GDN_DK, (h + 1) * GDN_DK)

    def wy_factors():
        units = []
        for seg in range(bb * cpt):
            rows = _rows(seg, chunk)
            gt = gcum_ref[rows, :]
            for gi in range(n_groups):
                gcols = slice(gi * hpg * GDN_DK, (gi + 1) * hpg * GDN_DK)
                k4 = k_ref[rows, gcols]
                k_bd = jnp.concatenate(
                    [jnp.where(feat_blk == hh, k4, jnp.zeros_like(k4)) for hh in range(hpg)], axis=0)
                kk = _dot_nt(jnp.concatenate([kb_ref[rows, gcols], q_ref[rows, gcols]], axis=0), k_bd)
                gcol = jnp.zeros((chunk, lw), F32)
                for hh in range(hpg):
                    ln = n_heads + gi * hpg + hh
                    gcol = jnp.where(blk_w == hh, gt[:, ln:ln + 1], gcol)
                grow = jnp.sum(jnp.where(diag_w, gcol, 0.0), axis=0, keepdims=True)
                decay = jnp.exp(jnp.where(incl_w, gcol - grow, 0.0))
                p_b = (-(kk[:chunk] * jnp.where(strict_w, decay, 0.0))).astype(BF16)
                qk_ref[seg * n_groups + gi] = (kk[chunk:] * jnp.where(incl_w, decay, 0.0)).astype(BF16)
                units.append(dict(seg=seg, rows=rows, gi=gi, p=p_b, t=eye_w + p_b.astype(F32)))
        for un in units:
            un["p"] = _dot(un["p"], block_diag(un["p"])).astype(BF16)
        for _ in range(n_steps - 2):
            for un in units:
                out = _dot(jnp.concatenate([un["t"].astype(BF16), un["p"]], axis=0), block_diag(un["p"]))
                un["t"] = un["t"] + out[:chunk]
                un["p"] = out[chunk:].astype(BF16)
        for un in units:
            un["t"] = un["t"] + _dot(un["t"].astype(BF16), block_diag(un["p"]))
        for un in units:
            seg, rows, gi = un["seg"], un["rows"], un["gi"]
            heads = range(gi * hpg, (gi + 1) * hpg)
            rhs = jnp.concatenate(
                [jnp.concatenate([kbe_ref[rows, head_cols(h)], vb_ref[rows, head_cols(h)]], axis=1) for h in heads],
                axis=0)
            wu = _dot(block_diag(un["t"].astype(BF16)), rhs)
            for hh, h in enumerate(heads):
                blk = wu[hh * chunk:(hh + 1) * chunk]
                wq_ref[seg * n_heads + h] = jnp.concatenate(
                    [blk[:, :GDN_DK].astype(BF16), qe_ref[rows, head_cols(h)]], axis=0)
                uu_ref[rows, head_cols(h)] = blk[:, GDN_DK:]

    wy_factors()

    zero_b = jnp.zeros((chunk, GDN_DV), BF16)

    pairs = [(bi, h) for bi in range(bb) for h in range(n_heads)]
    state = {bh: ns_ref[bh[0], bh[1]] for bh in pairs}
    for c in range(cpt):
        seg_of = {bi: bi * cpt + c for bi in range(bb)}
        rows_of = {bi: _rows(seg_of[bi], chunk) for bi in range(bb)}
        ws_qs = {(bi, h): _dot(wq_ref[seg_of[bi] * n_heads + h], state[bi, h].astype(BF16)) for bi, h in pairs}
        v_new = {(bi, h): (uu_ref[rows_of[bi], head_cols(h)] - ws_qs[bi, h][:chunk]).astype(BF16)
                 for bi, h in pairs}
        eg_last = {bi: jnp.exp(gcum_ref[pl.ds(seg_of[bi] * chunk + chunk - 1, 1), :]) for bi in range(bb)}
        state = {(bi, h): state[bi, h] * eg_last[bi][:, n_heads + h:n_heads + h + 1]
                 + _dot_tn(kd_ref[rows_of[bi], head_cols(h)], v_new[bi, h]) for bi, h in pairs}
        for bi in range(bb):
            rows = rows_of[bi]
            for gi in range(n_groups):
                heads = range(gi * hpg, (gi + 1) * hpg)
                v_bd = jnp.concatenate(
                    [jnp.concatenate([v_new[bi, h] if j == hh else zero_b for j in range(hpg)], axis=1)
                     for hh, h in enumerate(heads)], axis=0)
                o = (jnp.concatenate([ws_qs[bi, h][chunk:] for h in heads], axis=1)
                     + _dot(qk_ref[seg_of[bi] * n_groups + gi], v_bd))
                for hh, h in enumerate(heads):
                    gate = _silu(u_ref[rows, cch + h * GDN_DV:cch + (h + 1) * GDN_DV])
                    o_h = _rms(o[:, hh * GDN_DV:(hh + 1) * GDN_DV], hn_ref[...])
                    mix_ref[rows, head_cols(h)] = (o_h * gate).astype(BF16)
    for bi, h in pairs:
        ns_ref[bi, h] = state[bi, h]

    y = _dot(mix_ref[...], wout_ref[...])
    _post_residual(x_ref, mod_ref, npost_ref, y, y_ref)


def _odd_layer(x, mod, consts, state, *, tile, chunk, batch_block):
    bsz, seq, d = x.shape
    (npre, npost, win, wba, cw, alog, dtb, hn, wout) = consts
    cch = cw.shape[1]
    vw = wout.shape[0]
    n_heads = vw // GDN_DV
    n_tiles = seq // tile
    assert seq % tile == 0 and tile % chunk == 0 and bsz % batch_block == 0
    rows = batch_block * tile
    has_state = state is not None

    def bspec(shape):
        nd = len(shape)
        return pl.BlockSpec((batch_block,) + shape, lambda b, t: (b,) + (0,) * nd)

    in_specs = [
        pl.BlockSpec((batch_block, tile, d), lambda b, t: (b, t, 0)),
        bspec((3, d)),
        _const_spec(npre.shape), _const_spec(npost.shape), _const_spec(win.shape), _const_spec(wba.shape),
        _const_spec(cw.shape), _const_spec(alog.shape), _const_spec(dtb.shape), _const_spec(hn.shape),
        _const_spec(wout.shape),
    ]
    args = [x, mod, npre, npost, win, wba, cw, alog, dtb, hn, wout]
    if has_state:
        in_specs += [bspec((CONV_W - 1, cch)), bspec((n_heads, GDN_DK, GDN_DV))]
        args += list(state)
    out_specs = [
        pl.BlockSpec((batch_block, tile, d), lambda b, t: (b, t, 0)),
        bspec((CONV_W - 1, cch)), bspec((n_heads, GDN_DK, GDN_DV)),
    ]
    out_shape = [
        jax.ShapeDtypeStruct((bsz, seq, d), F32),
        jax.ShapeDtypeStruct((bsz, CONV_W - 1, cch), F32),
        jax.ShapeDtypeStruct((bsz, n_heads, GDN_DK, GDN_DV), F32),
    ]
    hpg, n_groups = _gdn_groups(chunk, n_heads)
    n_seg = rows // chunk
    scratch = (
        [pltpu.VMEM((rows, win.shape[1]), F32),
         pltpu.VMEM((batch_block, CONV_PAD + tile, cch), F32)]
        + [pltpu.VMEM((rows, vw), BF16)] * 7
        + [pltpu.VMEM((rows, LANES), F32)] * 2
        + [pltpu.VMEM((n_seg * n_heads, 2 * chunk, GDN_DK), BF16),
           pltpu.VMEM((rows, vw), F32),
           pltpu.VMEM((n_seg * n_groups, chunk, hpg * chunk), BF16),
           pltpu.VMEM((rows, vw), BF16)]
    )
    body = functools.partial(_odd_body, has_state=has_state, chunk=chunk)
    return pl.pallas_call(
        body,
        grid=(bsz // batch_block, n_tiles),
        in_specs=in_specs,
        out_specs=out_specs,
        out_shape=out_shape,
        scratch_shapes=scratch,
        compiler_params=pltpu.CompilerParams(
            dimension_semantics=("arbitrary", "arbitrary"), vmem_limit_bytes=VMEM_LIMIT_BYTES),
        name="odd_layer_state" if has_state else "odd_layer_prompt",
    )(*args)


def _gate_weights(w_a, w_x):
    heads, blk, _ = w_a.shape
    per_tile = MXU_TILE // blk
    tiles = []
    for i in range(heads // per_tile):
        sl = slice(i * per_tile, (i + 1) * per_tile)
        tiles.append(jnp.concatenate(
            [jax.scipy.linalg.block_diag(*w_a[sl]), jax.scipy.linalg.block_diag(*w_x[sl])], axis=1))
    return jnp.stack(tiles).astype(BF16)


def _rope_inv_row():
    half = ROT_DIM // 2
    inv = ROPE_THETA ** (-(jnp.arange(half, dtype=F32) * 2.0 / ROT_DIM))
    per_head = jnp.concatenate([inv, inv, jnp.zeros((HEAD_DIM - ROT_DIM,), F32)])
    return jnp.tile(per_head, LANES // HEAD_DIM).reshape(1, LANES)


def kernel(x_prompt, x_sample, state_lru_conv, state_lru_h, cache_swa_k, cache_swa_v, state_gdn_conv, state_gdn_s, c_prompt, c_sample, ev_mod_w, ev_mod_b, ev_norm_pre, ev_norm_post, ev_w_in, lru_conv_w, lru_conv_b, lru_w_a, lru_b_a, lru_w_x, lru_b_x, lru_lambda, swa_sinks, ev_w_out, od_mod_w, od_mod_b, od_norm_pre, od_norm_post, od_w_in, gdn_conv_w, gdn_a_log, gdn_dt_bias, gdn_head_norm, od_w_out):
    bp, seq, d = x_prompt.shape
    bs, dec_seq, _ = x_sample.shape
    lw = lru_conv_w.shape[-1]
    kvw = ATT_KV_HEADS * HEAD_DIM
    n_heads = gdn_a_log.shape[-1]
    cch = gdn_conv_w.shape[-1]
    vw = od_w_out.shape[1]

    n_c = bp + bs
    n_pad = -n_c % 16
    c_all = jnp.concatenate([c_prompt, c_sample, jnp.zeros((n_pad, d), F32)], axis=0)
    mod_ev, mod_od = _modulation(c_all, ev_mod_w[0], ev_mod_b[0], od_mod_w[0], od_mod_b[0])
    mod_ev = mod_ev.reshape(n_c + n_pad, 3, d)
    mod_od = mod_od.reshape(n_c + n_pad, 3, d)

    row = lambda a: a.reshape(1, -1)
    ev_consts = (row(ev_norm_pre[0]), row(ev_norm_post[0]), ev_w_in[0].astype(BF16), lru_conv_w[0],
                 row(lru_conv_b[0]), _gate_weights(lru_w_a[0], lru_w_x[0]), row(lru_b_a[0]), row(lru_b_x[0]),
                 row(lru_lambda[0]), _rope_inv_row(), swa_sinks[0], ev_w_out[0].astype(BF16))
    w_in_od = od_w_in[0]
    pad_lanes = lambda a: jnp.pad(a, ((0, 0), (0, LANES - a.shape[1])))
    head_row = lambda a: jnp.pad(a.reshape(1, -1), ((0, 0), (n_heads, LANES - 2 * n_heads)))
    od_consts = (row(od_norm_pre[0]), row(od_norm_post[0]), w_in_od[:, :cch + vw].astype(BF16),
                 pad_lanes(w_in_od[:, cch + vw:]).astype(BF16), gdn_conv_w[0],
                 head_row(gdn_a_log[0]), head_row(gdn_dt_bias[0]), row(gdn_head_norm[0]),
                 od_w_out[0].astype(BF16))

    tile = min(PROMPT_TILE, seq)
    xp, lru_conv_p, lru_h_p, swa_k_p, swa_v_p = _even_layer(
        x_prompt, mod_ev[:bp], ev_consts, None, tile=tile, chunk=min(CHUNK, seq), batch_block=1,
        n_past_valid=0, pos0=0, k_out_rows=WINDOW)
    ev_state = (state_lru_conv[0], state_lru_h[0].reshape(bs, 1, lw),
                cache_swa_k[0].reshape(bs, WINDOW, kvw), cache_swa_v[0].reshape(bs, WINDOW, kvw))
    xs, lru_conv_s, lru_h_s, swa_k_s, swa_v_s = _even_layer(
        x_sample, mod_ev[bp:n_c], ev_consts, ev_state, tile=dec_seq, chunk=min(CHUNK, dec_seq), batch_block=bs,
        n_past_valid=WINDOW, pos0=PAST_LEN, k_out_rows=dec_seq)

    xp, gdn_conv_p, gdn_s_p = _odd_layer(
        xp, mod_od[:bp], od_consts, None, tile=tile, chunk=min(CHUNK, seq), batch_block=1)
    xs, gdn_conv_s, gdn_s_s = _odd_layer(
        xs, mod_od[bp:n_c], od_consts, (state_gdn_conv[0], state_gdn_s[0]),
        tile=dec_seq, chunk=min(CHUNK, dec_seq), batch_block=bs)

    kv_shape = lambda a: a.reshape(1, a.shape[0], a.shape[1], ATT_KV_HEADS, HEAD_DIM)
    return (xp, xs,
            lru_conv_p[None], lru_conv_s[None],
            lru_h_p.reshape(1, bp, lw), lru_h_s.reshape(1, bs, lw),
            kv_shape(swa_k_p), kv_shape(swa_k_s), kv_shape(swa_v_p), kv_shape(swa_v_s),
            gdn_conv_p[None], gdn_conv_s[None], gdn_s_p[None], gdn_s_s[None])
```

```python
import functools

import jax
import jax.numpy as jnp
from jax import lax
from jax.experimental import pallas as pl
from jax.experimental.pallas import tpu as pltpu

F32 = jnp.float32
BF16 = jnp.bfloat16

CHUNK = 64
EPS = 1e-6
CONV_W = 4
NEG_INF = -1e30
LRU_HEADS = 8
LRU_C = 8.0
HEAD_DIM = 64
ATT_KV_HEADS = 2
WINDOW = 128
ROT_DIM = HEAD_DIM // 4
ROPE_THETA = 500000.0
GDN_DK = 128
GDN_DV = 128
PAST_LEN = 4096

LANES = 128
SUBLANES = 8
MXU_TILE = 256
VMEM_LIMIT_BYTES = 56 * 1024 * 1024

PROMPT_TILE = 256
MOD_TILE = 512
CONV_PAD = SUBLANES


def _silu(x):
    return x * jax.nn.sigmoid(x)


def _expm1(x):
    u = jnp.exp(x)
    d = u - 1.0
    return jnp.where(u == 1.0, x, jnp.where(d == -1.0, -1.0, d * x / jnp.log(u)))


def _rms(x, g):
    return x * lax.rsqrt(jnp.mean(x * x, axis=-1, keepdims=True) + EPS) * g


def _dot(a, b):
    return jnp.dot(a, b, preferred_element_type=F32)


def _dot_nt(a, b):
    return lax.dot_general(a, b, (((1,), (1,)), ((), ())), preferred_element_type=F32)


def _dot_tn(a, b):
    return lax.dot_general(a, b, (((0,), (0,)), ((), ())), preferred_element_type=F32)


def _for_each(n, body):
    if n == 1:
        body(0)
    else:
        def step(i, carry):
            body(i)
            return carry
        lax.fori_loop(0, n, step, 0)


def _rows(i, n, size=None):
    size = n if size is None else size
    if isinstance(i, int):
        return pl.ds(i * n, size)
    return pl.ds(pl.multiple_of(i * n, n), size)


def _mod_body(c_ref, w0_ref, b0_ref, w1_ref, b1_ref, o0_ref, o1_ref):
    c = _silu(c_ref[...]).astype(BF16)
    o0_ref[...] = _dot(c, w0_ref[...].astype(BF16)) + b0_ref[...]
    o1_ref[...] = _dot(c, w1_ref[...].astype(BF16)) + b1_ref[...]


def _modulation(c_all, w0, b0, w1, b1):
    n, d = c_all.shape
    d3 = w0.shape[1]
    wspec = pl.BlockSpec((d, MOD_TILE), lambda j: (0, j))
    bspec = pl.BlockSpec((1, MOD_TILE), lambda j: (0, j))
    ospec = pl.BlockSpec((n, MOD_TILE), lambda j: (0, j))
    return pl.pallas_call(
        _mod_body,
        grid=(d3 // MOD_TILE,),
        in_specs=[pl.BlockSpec((n, d), lambda j: (0, 0)), wspec, bspec, wspec, bspec],
        out_specs=[ospec, ospec],
        out_shape=[jax.ShapeDtypeStruct((n, d3), F32)] * 2,
        name="adaln_modulation",
    )(c_all, w0, b0.reshape(1, d3), w1, b1.reshape(1, d3))


def _pre_norm(x_ref, mod_ref, norm_pre_ref):
    bb, tc, d = x_ref.shape
    x = x_ref[...]
    h = _rms(x, norm_pre_ref[...]) * (1.0 + mod_ref[:, 1:2, :]) + mod_ref[:, 0:1, :]
    return h.reshape(bb * tc, d)


def _post_residual(x_ref, mod_ref, norm_post_ref, y, o_ref):
    bb, tc, d = x_ref.shape
    yn = _rms(y, norm_post_ref[...]).reshape(bb, tc, d)
    o_ref[...] = x_ref[...] + mod_ref[:, 2:3, :] * yn


def _causal_conv(ext_ref, bi, tc, w_ref):
    ext = ext_ref[bi]
    acc = pltpu.roll(ext, CONV_W - 1, axis=0)[CONV_PAD:] * w_ref[0:1, :]
    for j in range(1, CONV_W - 1):
        acc = acc + pltpu.roll(ext, CONV_W - 1 - j, axis=0)[CONV_PAD:] * w_ref[j:j + 1, :]
    return acc + ext[CONV_PAD:] * w_ref[CONV_W - 1:CONV_W, :]


def _even_body(*refs, has_state, n_past_valid, pos0, chunk, n_tiles, k_out_rows):
    refs = list(refs)
    x_ref = refs.pop(0)
    (mod_ref, npre_ref, npost_ref, win_ref, cw_ref, cb_ref, wg_ref, ba_ref, bx_ref,
     lam_ref, inv_ref, sinks_ref, wout_ref) = refs[:13]
    refs = refs[13:]
    if has_state:
        conv0_ref, h0_ref, k0_ref, v0_ref = refs[:4]
        refs = refs[4:]
    (y_ref, nconv_ref, nh_ref, nk_ref, nv_ref,
     u_ref, ext_ref, a_ref, b_ref, hs_ref, knat_ref, vnat_ref, q_ref, mix_ref) = refs
    bb, tc, _ = x_ref.shape
    lw = a_ref.shape[1]
    aw = q_ref.shape[1]
    kvw = knat_ref.shape[2]
    t = pl.program_id(1)
    cpt = tc // chunk
    m = WINDOW + chunk
    o_q, o_k, o_v, o_gb = 2 * lw, 2 * lw + aw, 2 * lw + aw + kvw, 2 * lw + aw + 2 * kvw

    @pl.when(t == 0)
    def _():
        ext_ref[:, 0:CONV_PAD, :] = jnp.zeros((bb, CONV_PAD, lw), F32)
        if has_state:
            ext_ref[:, CONV_PAD - 3:CONV_PAD, :] = conv0_ref[...]
            nh_ref[...] = h0_ref[...]
            knat_ref[:, 0:WINDOW, :] = k0_ref[...]
            vnat_ref[:, 0:WINDOW, :] = v0_ref[...]
        else:
            nh_ref[...] = jnp.zeros(nh_ref.shape, F32)
            knat_ref[:, 0:WINDOW, :] = jnp.zeros((bb, WINDOW, kvw), F32)
            vnat_ref[:, 0:WINDOW, :] = jnp.zeros((bb, WINDOW, kvw), F32)

    u_ref[...] = _dot(_pre_norm(x_ref, mod_ref, npre_ref).astype(BF16), win_ref[...])

    pos = (pos0 + t * tc + lax.broadcasted_iota(jnp.int32, (tc, LANES), 0)).astype(F32)
    ang = pos * inv_ref[...]
    lane = lax.broadcasted_iota(jnp.int32, (tc, LANES), 1) % HEAD_DIM
    cos_t = jnp.cos(ang)
    sin_t = jnp.sin(ang)
    half = ROT_DIM // 2
    sin_a = jnp.where(lane < half, -sin_t, 0.0)
    sin_b = jnp.where(lane >= half, sin_t, 0.0)

    def rope(xcol):
        return (xcol * cos_t + pltpu.roll(xcol, LANES - half, axis=1) * sin_a
                + pltpu.roll(xcol, half, axis=1) * sin_b)

    neg_c_softplus = -LRU_C * jax.nn.softplus(-lam_ref[...])
    row8 = lax.broadcasted_iota(jnp.int32, (SUBLANES, lw), 0)

    def per_batch(bi):
        rows = _rows(bi, tc)
        ext_ref[bi, CONV_PAD:CONV_PAD + tc, :] = u_ref[rows, 0:lw]
        xc = _causal_conv(ext_ref, bi, tc, cw_ref) + cb_ref[...]
        nconv_ref[bi, :, :] = ext_ref[bi, CONV_PAD + tc - 3:CONV_PAD + tc, :]
        ext_ref[bi, CONV_PAD - 3:CONV_PAD, :] = ext_ref[bi, CONV_PAD + tc - 3:CONV_PAD + tc, :]
        xcb = xc.astype(BF16)
        halves = [_dot(xcb[:, i * MXU_TILE:(i + 1) * MXU_TILE], wg_ref[i]) for i in range(lw // MXU_TILE)]
        r = jax.nn.sigmoid(jnp.concatenate([g[:, :MXU_TILE] for g in halves], axis=1) + ba_ref[...])
        ig = jax.nn.sigmoid(jnp.concatenate([g[:, MXU_TILE:] for g in halves], axis=1) + bx_ref[...])
        log_a = r * neg_c_softplus
        a_ref[...] = jnp.exp(log_a)
        b_ref[...] = jnp.sqrt(-_expm1(2.0 * log_a)) * (ig * xc)

        def scan_block(j, hc):
            r0 = pl.multiple_of(j * SUBLANES, SUBLANES)
            a = a_ref[pl.ds(r0, SUBLANES), :]
            b = b_ref[pl.ds(r0, SUBLANES), :]
            for sft in (1, 2, 4):
                keep = row8 >= sft
                b = jnp.where(keep, a * pltpu.roll(b, sft, axis=0) + b, b)
                a = jnp.where(keep, a * pltpu.roll(a, sft, axis=0), a)
            h = a * hc + b
            hs_ref[pl.ds(r0, SUBLANES), :] = h
            return h[SUBLANES - 1:SUBLANES, :]

        nh_ref[bi, :, :] = lax.fori_loop(0, tc // SUBLANES, scan_block, nh_ref[bi, :, :])
        mix_ref[rows, 0:lw] = (hs_ref[...] * _silu(u_ref[rows, lw:2 * lw])).astype(BF16)
        for j in range(aw // LANES):
            qcol = rope(u_ref[rows, o_q + j * LANES:o_q + (j + 1) * LANES])
            q_ref[rows, j * LANES:(j + 1) * LANES] = (qcol * (HEAD_DIM ** -0.5)).astype(BF16)
        knat_ref[bi, WINDOW:WINDOW + tc, :] = rope(u_ref[rows, o_k:o_k + kvw])
        vnat_ref[bi, WINDOW:WINDOW + tc, :] = u_ref[rows, o_v:o_v + kvw]
        nk_ref[bi, :, :] = knat_ref[bi, WINDOW + tc - k_out_rows:WINDOW + tc, :]
        nv_ref[bi, :, :] = vnat_ref[bi, WINDOW + tc - k_out_rows:WINDOW + tc, :]

    _for_each(bb, per_batch)

    lo_lane = lax.broadcasted_iota(jnp.int32, (m, LANES), 1) < HEAD_DIM
    lo_q = lax.broadcasted_iota(jnp.int32, (chunk, LANES), 1) < HEAD_DIM
    group = (aw // HEAD_DIM) // ATT_KV_HEADS

    sinks = [jnp.concatenate([jnp.full((chunk, 1), sinks_ref[h * group + g], F32) for g in range(group)], axis=0)
             for h in range(ATT_KV_HEADS)]

    lane_m = lax.broadcasted_iota(jnp.int32, (m, LANES), 1)

    def attend():
        units = []
        for bi, c in [(bi, c) for bi in range(bb) for c in range(cpt)]:
            rows = _rows(bi * cpt + c, chunk)
            krows = _rows(c, chunk, m)
            kseg = knat_ref[bi, krows, :]
            vseg = vnat_ref[bi, krows, :]
            krot = pltpu.roll(kseg, HEAD_DIM, axis=1)
            vrot = pltpu.roll(vseg, HEAD_DIM, axis=1)
            valid = None
            if n_past_valid < WINDOW:
                key_pos = t * tc + c * chunk - WINDOW + lax.broadcasted_iota(jnp.int32, (1, m), 1)
                valid = key_pos >= -n_past_valid
            for h in range(ATT_KV_HEADS):
                kd = (jnp.where(lo_lane, kseg, krot) if h == 0 else jnp.where(lo_lane, krot, kseg)).astype(BF16)
                v_lo = jnp.where(lo_lane, vseg if h == 0 else vrot, jnp.where(lane_m == HEAD_DIM, 1.0, 0.0))
                v_hi = jnp.where(lo_lane, jnp.where(lane_m == 0, 1.0, 0.0), vrot if h == 0 else vseg)
                units.append(dict(rows=rows, h=h, valid=valid, kd=kd, v_lo=v_lo.astype(BF16),
                                  v_hi=v_hi.astype(BF16)))

        for un in units:
            rows, h = un["rows"], un["h"]
            cols = [q_ref[rows, (h * group // 2 + j) * LANES:(h * group // 2 + j + 1) * LANES]
                    for j in range(group // 2)]
            zero = jnp.zeros_like(cols[0])
            qstack = jnp.concatenate(
                [part for qc in cols for part in (jnp.where(lo_q, qc, zero), jnp.where(lo_q, zero, qc))], axis=0)
            sc = _dot_nt(qstack, un["kd"])
            un["sc"] = sc if un["valid"] is None else jnp.where(un["valid"], sc, NEG_INF)
        for un in units:
            un["mx"] = jnp.maximum(jnp.max(un["sc"], axis=-1, keepdims=True), sinks[un["h"]])
        for un in units:
            un["p"] = jnp.exp(un["sc"] - un["mx"]).astype(BF16)
            un["sink_p"] = jnp.exp(sinks[un["h"]] - un["mx"])
        for un in units:
            rows, h, p, sink_p = un["rows"], un["h"], un["p"], un["sink_p"]
            for j in range(group // 2):
                r_lo = slice((2 * j) * chunk, (2 * j + 1) * chunk)
                r_hi = slice((2 * j + 1) * chunk, (2 * j + 2) * chunk)
                o_lo = _dot(p[r_lo], un["v_lo"])
                o_hi = _dot(p[r_hi], un["v_hi"])
                den_lo = o_lo[:, HEAD_DIM:HEAD_DIM + 1] + sink_p[r_lo]
                den_hi = o_hi[:, 0:1] + sink_p[r_hi]
                o = jnp.where(lo_q, o_lo / den_lo, o_hi / den_hi)
                col = h * group // 2 + j
                gate = _silu(u_ref[rows, o_gb + col * LANES:o_gb + (col + 1) * LANES])
                mix_ref[rows, lw + col * LANES:lw + (col + 1) * LANES] = (o * gate).astype(BF16)

    attend()

    if n_tiles > 1:
        knat_ref[:, 0:WINDOW, :] = knat_ref[:, tc:tc + WINDOW, :]
        vnat_ref[:, 0:WINDOW, :] = vnat_ref[:, tc:tc + WINDOW, :]

    y = _dot(mix_ref[...], wout_ref[...])
    _post_residual(x_ref, mod_ref, npost_ref, y, y_ref)


def _const_spec(shape):
    zeros = (0,) * len(shape)
    return pl.BlockSpec(shape, lambda b, t: zeros)


def _even_layer(x, mod, consts, state, *, tile, chunk, batch_block, n_past_valid, pos0, k_out_rows):
    bsz, seq, d = x.shape
    (npre, npost, win, cw, cb, wg, b_a, b_x, lam, inv_row, sinks, wout) = consts
    lw = cw.shape[1]
    kvw = ATT_KV_HEADS * HEAD_DIM
    aw = (win.shape[1] - 2 * lw - 2 * kvw) // 2
    n_tiles = seq // tile
    assert seq % tile == 0 and tile % chunk == 0 and bsz % batch_block == 0
    assert n_tiles == 1 or tile >= WINDOW
    rows = batch_block * tile
    has_state = state is not None

    def bspec(shape):
        nd = len(shape)
        return pl.BlockSpec((batch_block,) + shape, lambda b, t: (b,) + (0,) * nd)

    in_specs = [
        pl.BlockSpec((batch_block, tile, d), lambda b, t: (b, t, 0)),
        bspec((3, d)),
        _const_spec(npre.shape), _const_spec(npost.shape), _const_spec(win.shape),
        _const_spec(cw.shape), _const_spec(cb.shape), _const_spec(wg.shape),
        _const_spec(b_a.shape), _const_spec(b_x.shape), _const_spec(lam.shape), _const_spec(inv_row.shape),
        pl.BlockSpec(memory_space=pltpu.SMEM),
        _const_spec(wout.shape),
    ]
    args = [x, mod, npre, npost, win, cw, cb, wg, b_a, b_x, lam, inv_row, sinks, wout]
    if has_state:
        in_specs += [bspec((CONV_W - 1, lw)), bspec((1, lw)), bspec((WINDOW, kvw)), bspec((WINDOW, kvw))]
        args += list(state)
    out_specs = [
        pl.BlockSpec((batch_block, tile, d), lambda b, t: (b, t, 0)),
        bspec((CONV_W - 1, lw)), bspec((1, lw)), bspec((k_out_rows, kvw)), bspec((k_out_rows, kvw)),
    ]
    out_shape = [
        jax.ShapeDtypeStruct((bsz, seq, d), F32),
        jax.ShapeDtypeStruct((bsz, CONV_W - 1, lw), F32),
        jax.ShapeDtypeStruct((bsz, 1, lw), F32),
        jax.ShapeDtypeStruct((bsz, k_out_rows, kvw), F32),
        jax.ShapeDtypeStruct((bsz, k_out_rows, kvw), F32),
    ]
    scratch = [
        pltpu.VMEM((rows, win.shape[1]), F32),
        pltpu.VMEM((batch_block, CONV_PAD + tile, lw), F32),
        pltpu.VMEM((tile, lw), F32), pltpu.VMEM((tile, lw), F32), pltpu.VMEM((tile, lw), F32),
        pltpu.VMEM((batch_block, WINDOW + tile, kvw), F32),
        pltpu.VMEM((batch_block, WINDOW + tile, kvw), F32),
        pltpu.VMEM((rows, aw), BF16),
        pltpu.VMEM((rows, lw + aw), BF16),
    ]
    body = functools.partial(_even_body, has_state=has_state, n_past_valid=n_past_valid, pos0=pos0,
                             chunk=chunk, n_tiles=n_tiles, k_out_rows=k_out_rows)
    return pl.pallas_call(
        body,
        grid=(bsz // batch_block, n_tiles),
        in_specs=in_specs,
        out_specs=out_specs,
        out_shape=out_shape,
        scratch_shapes=scratch,
        compiler_params=pltpu.CompilerParams(
            dimension_semantics=("arbitrary", "arbitrary"), vmem_limit_bytes=VMEM_LIMIT_BYTES),
        name="even_layer_state" if has_state else "even_layer_prompt",
    )(*args)


def _gdn_groups(chunk, n_heads):
    hpg = max(1, min(n_heads, MXU_TILE // chunk))
    assert n_heads % hpg == 0
    return hpg, n_heads // hpg


def _time_of_row(i):
    return (i % SUBLANES) * SUBLANES + i // SUBLANES


def _phase_major_matrix(n_rows, chunk):
    i = jnp.arange(n_rows)
    src = (i // chunk) * chunk + _time_of_row(i % chunk)
    return (src[:, None] == i[None, :]).astype(BF16)


def _conv_phase_major(u_ref, hist_ref, nconv_ref, w_ref, tc, chunk, cch):
    cpt = tc // chunk
    blocks = [u_ref[SUBLANES * b:SUBLANES * (b + 1), 0:cch] for b in range(tc // SUBLANES)]
    taps_w = [w_ref[j:j + 1, :] for j in range(CONV_W)]
    top_row = lax.broadcasted_iota(jnp.int32, (SUBLANES, cch), 0) == 0
    first_late = SUBLANES - (CONV_W - 1)
    prev = {r: pltpu.roll(hist_ref[r - first_late], 1, axis=0) for r in range(first_late, SUBLANES)}
    out = []
    for c in range(cpt):
        late = {}
        for r in range(first_late, SUBLANES):
            moved = pltpu.roll(blocks[SUBLANES * c + r], 1, axis=0)
            late[r] = jnp.where(top_row, prev[r], moved)
            prev[r] = moved
        for r in range(SUBLANES):
            acc = None
            for j in range(CONV_W):
                s = CONV_W - 1 - j
                tap = blocks[SUBLANES * c + r - s] if r >= s else late[r - s + SUBLANES]
                acc = tap * taps_w[j] if acc is None else acc + tap * taps_w[j]
            out.append(acc)
    for i in range(CONV_W - 1):
        last = blocks[SUBLANES * (cpt - 1) + first_late + i]
        hist_ref[i] = last
        nconv_ref[0, i:i + 1, :] = last[SUBLANES - 1:SUBLANES, :]
    return jnp.concatenate(out, axis=0)


def _odd_body(*refs, has_state, chunk, phase_major):
    refs = list(refs)
    x_ref = refs.pop(0)
    perm_ref = refs.pop(0) if phase_major else None
    (mod_ref, npre_ref, npost_ref, win_ref, wba_ref, cw_ref, alog_ref, dtb_ref, hn_ref, wout_ref) = refs[:10]
    refs = refs[10:]
    if has_state:
        conv0_ref, s0_ref = refs[:2]
        refs = refs[2:]
    (y_ref, nconv_ref, ns_ref,
     u_ref, ext_ref, q_ref, k_ref, kb_ref, kbe_ref, vb_ref, qe_ref, kd_ref, beta_ref, gcum_ref,
     wq_ref, uu_ref, qk_ref, mix_ref) = refs
    bb, tc, _ = x_ref.shape
    n_heads = ns_ref.shape[1]
    kw = n_heads * GDN_DK
    vw = n_heads * GDN_DV
    cch = 2 * kw + vw
    rows_all = bb * tc
    t = pl.program_id(1)
    cpt = tc // chunk
    n_steps = (chunk - 1).bit_length()
    assert n_steps >= 2
    hpg, n_groups = _gdn_groups(chunk, n_heads)
    lw = hpg * chunk

    @pl.when(t == 0)
    def _():
        ext_ref[...] = jnp.zeros(ext_ref.shape, F32)
        if has_state:
            ext_ref[:, CONV_PAD - 3:CONV_PAD, :] = conv0_ref[...]
            ns_ref[...] = s0_ref[...]
        else:
            ns_ref[...] = jnp.zeros(ns_ref.shape, F32)

    hmod = _pre_norm(x_ref, mod_ref, npre_ref).astype(BF16)
    if phase_major:
        hmod = _dot(perm_ref[...], hmod).astype(BF16)
    u_ref[...] = _dot(hmod, win_ref[...])
    ba = _dot(hmod, wba_ref[...])
    beta_ref[...] = jax.nn.sigmoid(ba)
    g = -jnp.exp(alog_ref[...]) * jax.nn.softplus(ba + dtb_ref[...])
    if phase_major:
        row8 = lax.broadcasted_iota(jnp.int32, (SUBLANES, LANES), 0)
        parts = []
        for c in range(rows_all // chunk):
            run = []
            for r in range(SUBLANES):
                blk = g[c * chunk + r * SUBLANES:c * chunk + (r + 1) * SUBLANES]
                run.append(blk if r == 0 else run[-1] + blk)
            total = run[-1]
            incl = total
            for sft in (1, 2, 4):
                incl = jnp.where(row8 >= sft, incl + pltpu.roll(incl, sft, axis=0), incl)
            parts += [blk + (incl - total) for blk in run]
        g = jnp.concatenate(parts, axis=0)
    else:
        row_in_chunk = lax.broadcasted_iota(jnp.int32, (rows_all, LANES), 0) % chunk
        sft = 1
        while sft < chunk:
            g = jnp.where(row_in_chunk >= sft, g + pltpu.roll(g, sft, axis=0), g)
            sft *= 2
    gcum_ref[...] = g

    def per_batch(bi):
        rows = _rows(bi, tc)
        if phase_major:
            qkv = _silu(_conv_phase_major(u_ref, ext_ref, nconv_ref, cw_ref, tc, chunk, cch))
        else:
            ext_ref[bi, CONV_PAD:CONV_PAD + tc, :] = u_ref[rows, 0:cch]
            qkv = _silu(_causal_conv(ext_ref, bi, tc, cw_ref))
            nconv_ref[bi, :, :] = ext_ref[bi, CONV_PAD + tc - 3:CONV_PAD + tc, :]
            ext_ref[bi, CONV_PAD - 3:CONV_PAD, :] = ext_ref[bi, CONV_PAD + tc - 3:CONV_PAD + tc, :]
        beta_b = beta_ref[rows, :]
        gc_b = gcum_ref[rows, :]
        eg_b = jnp.exp(gc_b)
        gc3 = gc_b.reshape(cpt, chunk, LANES)
        dec_b = jnp.exp(gc3[:, chunk - 1:chunk, :] - gc3).reshape(tc, LANES)
        for h in range(n_heads):
            hc = slice(h * GDN_DK, (h + 1) * GDN_DK)
            qh = qkv[:, hc]
            kh = qkv[:, kw + h * GDN_DK:kw + (h + 1) * GDN_DK]
            vh = qkv[:, 2 * kw + h * GDN_DV:2 * kw + (h + 1) * GDN_DV]
            qh = qh * lax.rsqrt(jnp.sum(qh * qh, axis=-1, keepdims=True) + EPS) * (GDN_DK ** -0.5)
            kh = kh * lax.rsqrt(jnp.sum(kh * kh, axis=-1, keepdims=True) + EPS)
            beta = beta_b[:, h:h + 1]
            eg = eg_b[:, n_heads + h:n_heads + h + 1]
            kb = kh * beta
            q_ref[rows, hc] = qh.astype(BF16)
            k_ref[rows, hc] = kh.astype(BF16)
            kb_ref[rows, hc] = kb.astype(BF16)
            kbe_ref[rows, hc] = (kb * eg).astype(BF16)
            vb_ref[rows, hc] = (vh * beta).astype(BF16)
            qe_ref[rows, hc] = (qh * eg).astype(BF16)
            kd_ref[rows, hc] = (kh * dec_b[:, n_heads + h:n_heads + h + 1]).astype(BF16)

    _for_each(bb, per_batch)

    lane_w = lax.broadcasted_iota(jnp.int32, (chunk, lw), 1)
    row_w = lax.broadcasted_iota(jnp.int32, (chunk, lw), 0)
    blk_w = lane_w // chunk
    col_w = lane_w % chunk
    diag_w = row_w == col_w
    if phase_major:
        row_w, col_w = _time_of_row(row_w), _time_of_row(col_w)
    incl_w = row_w >= col_w
    strict_w = row_w > col_w
    eye_w = jnp.where(diag_w, 1.0, 0.0).astype(F32)
    feat_blk = lax.broadcasted_iota(jnp.int32, (chunk, hpg * GDN_DK), 1) // GDN_DK

    def block_diag(m_b):
        return jnp.concatenate([jnp.where(blk_w == hh, m_b, jnp.zeros_like(m_b)) for hh in range(hpg)], axis=0)

    def head_cols(h):
        return slice(h * GDN_DK, (h + 1) * GDN_DK)

    def wy_factors():
        units = []
        for seg in range(bb * cpt):
            rows = _rows(seg, chunk)
            gt = gcum_ref[rows, :]
            for gi in range(n_groups):
                gcols = slice(gi * hpg * GDN_DK, (gi + 1) * hpg * GDN_DK)
                k4 = k_ref[rows, gcols]
                k_bd = jnp.concatenate(
                    [jnp.where(feat_blk == hh, k4, jnp.zeros_like(k4)) for hh in range(hpg)], axis=0)
                kk = _dot_nt(jnp.concatenate([kb_ref[rows, gcols], q_ref[rows, gcols]], axis=0), k_bd)
                gcol = jnp.zeros((chunk, lw), F32)
                for hh in range(hpg):
                    ln = n_heads + gi * hpg + hh
                    gcol = jnp.where(blk_w == hh, gt[:, ln:ln + 1], gcol)
                grow = jnp.sum(jnp.where(diag_w, gcol, 0.0), axis=0, keepdims=True)
                decay = jnp.exp(jnp.where(incl_w, gcol - grow, 0.0))
                p_b = (-(kk[:chunk] * jnp.where(strict_w, decay, 0.0))).astype(BF16)
                qk_ref[seg * n_groups + gi] = (kk[chunk:] * jnp.where(incl_w, decay, 0.0)).astype(BF16)
                units.append(dict(seg=seg, rows=rows, gi=gi, p=p_b, t=eye_w + p_b.astype(F32)))
        for un in units:
            un["p"] = _dot(un["p"], block_diag(un["p"])).astype(BF16)
        for _ in range(n_steps - 2):
            for un in units:
                out = _dot(jnp.concatenate([un["t"].astype(BF16), un["p"]], axis=0), block_diag(un["p"]))
                un["t"] = un["t"] + out[:chunk]
                un["p"] = out[chunk:].astype(BF16)
        for un in units:
            un["t"] = un["t"] + _dot(un["t"].astype(BF16), block_diag(un["p"]))
        for un in units:
            seg, rows, gi = un["seg"], un["rows"], un["gi"]
            heads = range(gi * hpg, (gi + 1) * hpg)
            rhs = jnp.concatenate(
                [jnp.concatenate([kbe_ref[rows, head_cols(h)], vb_ref[rows, head_cols(h)]], axis=1) for h in heads],
                axis=0)
            wu = _dot(block_diag(un["t"].astype(BF16)), rhs)
            for hh, h in enumerate(heads):
                blk = wu[hh * chunk:(hh + 1) * chunk]
                wq_ref[seg * n_heads + h] = jnp.concatenate(
                    [blk[:, :GDN_DK].astype(BF16), qe_ref[rows, head_cols(h)]], axis=0)
                uu_ref[rows, head_cols(h)] = blk[:, GDN_DK:]

    wy_factors()

    zero_b = jnp.zeros((chunk, GDN_DV), BF16)

    pairs = [(bi, h) for bi in range(bb) for h in range(n_heads)]
    state = {bh: ns_ref[bh[0], bh[1]] for bh in pairs}
    for c in range(cpt):
        seg_of = {bi: bi * cpt + c for bi in range(bb)}
        rows_of = {bi: _rows(seg_of[bi], chunk) for bi in range(bb)}
        ws_qs = {(bi, h): _dot(wq_ref[seg_of[bi] * n_heads + h], state[bi, h].astype(BF16)) for bi, h in pairs}
        v_new = {(bi, h): (uu_ref[rows_of[bi], head_cols(h)] - ws_qs[bi, h][:chunk]).astype(BF16)
                 for bi, h in pairs}
        eg_last = {bi: jnp.exp(gcum_ref[pl.ds(seg_of[bi] * chunk + chunk - 1, 1), :]) for bi in range(bb)}
        state = {(bi, h): state[bi, h] * eg_last[bi][:, n_heads + h:n_heads + h + 1]
                 + _dot_tn(kd_ref[rows_of[bi], head_cols(h)], v_new[bi, h]) for bi, h in pairs}
        for bi in range(bb):
            rows = rows_of[bi]
            for gi in range(n_groups):
                heads = range(gi * hpg, (gi + 1) * hpg)
                v_bd = jnp.concatenate(
                    [jnp.concatenate([v_new[bi, h] if j == hh else zero_b for j in range(hpg)], axis=1)
                     for hh, h in enumerate(heads)], axis=0)
                o = (jnp.concatenate([ws_qs[bi, h][chunk:] for h in heads], axis=1)
                     + _dot(qk_ref[seg_of[bi] * n_groups + gi], v_bd))
                for hh, h in enumerate(heads):
                    gate = _silu(u_ref[rows, cch + h * GDN_DV:cch + (h + 1) * GDN_DV])
                    o_h = _rms(o[:, hh * GDN_DV:(hh + 1) * GDN_DV], hn_ref[...])
                    mix_ref[rows, head_cols(h)] = (o_h * gate).astype(BF16)
    for bi, h in pairs:
        ns_ref[bi, h] = state[bi, h]

    mixed = mix_ref[...]
    if phase_major:
        mixed = _dot(perm_ref[...], mixed).astype(BF16)
    y = _dot(mixed, wout_ref[...])
    _post_residual(x_ref, mod_ref, npost_ref, y, y_ref)


def _odd_layer(x, mod, consts, state, *, tile, chunk, batch_block):
    bsz, seq, d = x.shape
    (npre, npost, win, wba, cw, alog, dtb, hn, wout) = consts
    cch = cw.shape[1]
    vw = wout.shape[0]
    n_heads = vw // GDN_DV
    n_tiles = seq // tile
    assert seq % tile == 0 and tile % chunk == 0 and bsz % batch_block == 0
    rows = batch_block * tile
    has_state = state is not None
    phase_major = chunk == SUBLANES * SUBLANES and batch_block == 1 and not has_state

    def bspec(shape):
        nd = len(shape)
        return pl.BlockSpec((batch_block,) + shape, lambda b, t: (b,) + (0,) * nd)

    in_specs = [pl.BlockSpec((batch_block, tile, d), lambda b, t: (b, t, 0))]
    args = [x]
    if phase_major:
        in_specs.append(_const_spec((rows, rows)))
        args.append(_phase_major_matrix(rows, chunk))
    in_specs += [
        bspec((3, d)),
        _const_spec(npre.shape), _const_spec(npost.shape), _const_spec(win.shape), _const_spec(wba.shape),
        _const_spec(cw.shape), _const_spec(alog.shape), _const_spec(dtb.shape), _const_spec(hn.shape),
        _const_spec(wout.shape),
    ]
    args += [mod, npre, npost, win, wba, cw, alog, dtb, hn, wout]
    if has_state:
        in_specs += [bspec((CONV_W - 1, cch)), bspec((n_heads, GDN_DK, GDN_DV))]
        args += list(state)
    out_specs = [
        pl.BlockSpec((batch_block, tile, d), lambda b, t: (b, t, 0)),
        bspec((CONV_W - 1, cch)), bspec((n_heads, GDN_DK, GDN_DV)),
    ]
    out_shape = [
        jax.ShapeDtypeStruct((bsz, seq, d), F32),
        jax.ShapeDtypeStruct((bsz, CONV_W - 1, cch), F32),
        jax.ShapeDtypeStruct((bsz, n_heads, GDN_DK, GDN_DV), F32),
    ]
    hpg, n_groups = _gdn_groups(chunk, n_heads)
    n_seg = rows // chunk
    conv_hist = (CONV_W - 1, SUBLANES, cch) if phase_major else (batch_block, CONV_PAD + tile, cch)
    scratch = (
        [pltpu.VMEM((rows, win.shape[1]), F32),
         pltpu.VMEM(conv_hist, F32)]
        + [pltpu.VMEM((rows, vw), BF16)] * 7
        + [pltpu.VMEM((rows, LANES), F32)] * 2
        + [pltpu.VMEM((n_seg * n_heads, 2 * chunk, GDN_DK), BF16),
           pltpu.VMEM((rows, vw), F32),
           pltpu.VMEM((n_seg * n_groups, chunk, hpg * chunk), BF16),
           pltpu.VMEM((rows, vw), BF16)]
    )
    body = functools.partial(_odd_body, has_state=has_state, chunk=chunk, phase_major=phase_major)
    return pl.pallas_call(
        body,
        grid=(bsz // batch_block, n_tiles),
        in_specs=in_specs,
        out_specs=out_specs,
        out_shape=out_shape,
        scratch_shapes=scratch,
        compiler_params=pltpu.CompilerParams(
            dimension_semantics=("arbitrary", "arbitrary"), vmem_limit_bytes=VMEM_LIMIT_BYTES),
        name="odd_layer_state" if has_state else "odd_layer_prompt",
    )(*args)


def _gate_weights(w_a, w_x):
    heads, blk, _ = w_a.shape
    per_tile = MXU_TILE // blk
    tiles = []
    for i in range(heads // per_tile):
        sl = slice(i * per_tile, (i + 1) * per_tile)
        tiles.append(jnp.concatenate(
            [jax.scipy.linalg.block_diag(*w_a[sl]), jax.scipy.linalg.block_diag(*w_x[sl])], axis=1))
    return jnp.stack(tiles).astype(BF16)


def _rope_inv_row():
    half = ROT_DIM // 2
    inv = ROPE_THETA ** (-(jnp.arange(half, dtype=F32) * 2.0 / ROT_DIM))
    per_head = jnp.concatenate([inv, inv, jnp.zeros((HEAD_DIM - ROT_DIM,), F32)])
    return jnp.tile(per_head, LANES // HEAD_DIM).reshape(1, LANES)


def kernel(x_prompt, x_sample, state_lru_conv, state_lru_h, cache_swa_k, cache_swa_v, state_gdn_conv, state_gdn_s, c_prompt, c_sample, ev_mod_w, ev_mod_b, ev_norm_pre, ev_norm_post, ev_w_in, lru_conv_w, lru_conv_b, lru_w_a, lru_b_a, lru_w_x, lru_b_x, lru_lambda, swa_sinks, ev_w_out, od_mod_w, od_mod_b, od_norm_pre, od_norm_post, od_w_in, gdn_conv_w, gdn_a_log, gdn_dt_bias, gdn_head_norm, od_w_out):
    bp, seq, d = x_prompt.shape
    bs, dec_seq, _ = x_sample.shape
    lw = lru_conv_w.shape[-1]
    kvw = ATT_KV_HEADS * HEAD_DIM
    n_heads = gdn_a_log.shape[-1]
    cch = gdn_conv_w.shape[-1]
    vw = od_w_out.shape[1]

    n_c = bp + bs
    n_pad = -n_c % 16
    c_all = jnp.concatenate([c_prompt, c_sample, jnp.zeros((n_pad, d), F32)], axis=0)
    mod_ev, mod_od = _modulation(c_all, ev_mod_w[0], ev_mod_b[0], od_mod_w[0], od_mod_b[0])
    mod_ev = mod_ev.reshape(n_c + n_pad, 3, d)
    mod_od = mod_od.reshape(n_c + n_pad, 3, d)

    row = lambda a: a.reshape(1, -1)
    ev_consts = (row(ev_norm_pre[0]), row(ev_norm_post[0]), ev_w_in[0].astype(BF16), lru_conv_w[0],
                 row(lru_conv_b[0]), _gate_weights(lru_w_a[0], lru_w_x[0]), row(lru_b_a[0]), row(lru_b_x[0]),
                 row(lru_lambda[0]), _rope_inv_row(), swa_sinks[0], ev_w_out[0].astype(BF16))
    w_in_od = od_w_in[0]
    pad_lanes = lambda a: jnp.pad(a, ((0, 0), (0, LANES - a.shape[1])))
    head_row = lambda a: jnp.pad(a.reshape(1, -1), ((0, 0), (n_heads, LANES - 2 * n_heads)))
    od_consts = (row(od_norm_pre[0]), row(od_norm_post[0]), w_in_od[:, :cch + vw].astype(BF16),
                 pad_lanes(w_in_od[:, cch + vw:]).astype(BF16), gdn_conv_w[0],
                 head_row(gdn_a_log[0]), head_row(gdn_dt_bias[0]), row(gdn_head_norm[0]),
                 od_w_out[0].astype(BF16))

    tile = min(PROMPT_TILE, seq)
    xp, lru_conv_p, lru_h_p, swa_k_p, swa_v_p = _even_layer(
        x_prompt, mod_ev[:bp], ev_consts, None, tile=tile, chunk=min(CHUNK, seq), batch_block=1,
        n_past_valid=0, pos0=0, k_out_rows=WINDOW)
    ev_state = (state_lru_conv[0], state_lru_h[0].reshape(bs, 1, lw),
                cache_swa_k[0].reshape(bs, WINDOW, kvw), cache_swa_v[0].reshape(bs, WINDOW, kvw))
    xs, lru_conv_s, lru_h_s, swa_k_s, swa_v_s = _even_layer(
        x_sample, mod_ev[bp:n_c], ev_consts, ev_state, tile=dec_seq, chunk=min(CHUNK, dec_seq), batch_block=bs,
        n_past_valid=WINDOW, pos0=PAST_LEN, k_out_rows=dec_seq)

    xp, gdn_conv_p, gdn_s_p = _odd_layer(
        xp, mod_od[:bp], od_consts, None, tile=tile, chunk=min(CHUNK, seq), batch_block=1)
    xs, gdn_conv_s, gdn_s_s = _odd_layer(
        xs, mod_od[bp:n_c], od_consts, (state_gdn_conv[0], state_gdn_s[0]),
        tile=dec_seq, chunk=min(CHUNK, dec_seq), batch_block=bs)

    kv_shape = lambda a: a.reshape(1, a.shape[0], a.shape[1], ATT_KV_HEADS, HEAD_DIM)
    return (xp, xs,
            lru_conv_p[None], lru_conv_s[None],
            lru_h_p.reshape(1, bp, lw), lru_h_s.reshape(1, bs, lw),
            kv_shape(swa_k_p), kv_shape(swa_k_s), kv_shape(swa_v_p), kv_shape(swa_v_s),
            gdn_conv_p[None], gdn_conv_s[None], gdn_s_p[None], gdn_s_s[None])
```

```python
import functools

import jax
import jax.numpy as jnp
from jax import lax
from jax.experimental import pallas as pl
from jax.experimental.pallas import tpu as pltpu

F32 = jnp.float32
BF16 = jnp.bfloat16

CHUNK = 64
EPS = 1e-6
CONV_W = 4
NEG_INF = -1e30
LRU_HEADS = 8
LRU_C = 8.0
HEAD_DIM = 64
ATT_KV_HEADS = 2
WINDOW = 128
ROT_DIM = HEAD_DIM // 4
ROPE_THETA = 500000.0
GDN_DK = 128
GDN_DV = 128
PAST_LEN = 4096

LANES = 128
SUBLANES = 8
MXU_TILE = 256
VMEM_LIMIT_BYTES = 56 * 1024 * 1024

PROMPT_TILE = 512
MOD_TILE = 512
CONV_PAD = SUBLANES


def _silu(x):
    return x * jax.nn.sigmoid(x)


def _expm1(x):
    u = jnp.exp(x)
    d = u - 1.0
    return jnp.where(u == 1.0, x, jnp.where(d == -1.0, -1.0, d * x / jnp.log(u)))


def _rms(x, g):
    return x * lax.rsqrt(jnp.mean(x * x, axis=-1, keepdims=True) + EPS) * g


def _dot(a, b):
    return jnp.dot(a, b, preferred_element_type=F32)


def _dot_nt(a, b):
    return lax.dot_general(a, b, (((1,), (1,)), ((), ())), preferred_element_type=F32)


def _dot_tn(a, b):
    return lax.dot_general(a, b, (((0,), (0,)), ((), ())), preferred_element_type=F32)


def _for_each(n, body):
    if n == 1:
        body(0)
    else:
        def step(i, carry):
            body(i)
            return carry
        lax.fori_loop(0, n, step, 0)


def _rows(i, n, size=None):
    size = n if size is None else size
    if isinstance(i, int):
        return pl.ds(i * n, size)
    return pl.ds(pl.multiple_of(i * n, n), size)


def _mod_body(c_ref, w0_ref, b0_ref, w1_ref, b1_ref, o0_ref, o1_ref):
    c = _silu(c_ref[...]).astype(BF16)
    o0_ref[...] = _dot(c, w0_ref[...].astype(BF16)) + b0_ref[...]
    o1_ref[...] = _dot(c, w1_ref[...].astype(BF16)) + b1_ref[...]


def _modulation(c_all, w0, b0, w1, b1):
    n, d = c_all.shape
    d3 = w0.shape[1]
    wspec = pl.BlockSpec((d, MOD_TILE), lambda j: (0, j))
    bspec = pl.BlockSpec((1, MOD_TILE), lambda j: (0, j))
    ospec = pl.BlockSpec((n, MOD_TILE), lambda j: (0, j))
    return pl.pallas_call(
        _mod_body,
        grid=(d3 // MOD_TILE,),
        in_specs=[pl.BlockSpec((n, d), lambda j: (0, 0)), wspec, bspec, wspec, bspec],
        out_specs=[ospec, ospec],
        out_shape=[jax.ShapeDtypeStruct((n, d3), F32)] * 2,
        name="adaln_modulation",
    )(c_all, w0, b0.reshape(1, d3), w1, b1.reshape(1, d3))


def _pre_norm(x_ref, mod_ref, norm_pre_ref):
    bb, tc, d = x_ref.shape
    x = x_ref[...]
    h = _rms(x, norm_pre_ref[...]) * (1.0 + mod_ref[:, 1:2, :]) + mod_ref[:, 0:1, :]
    return h.reshape(bb * tc, d)


def _post_residual(x_ref, mod_ref, norm_post_ref, y, o_ref):
    bb, tc, d = x_ref.shape
    yn = _rms(y, norm_post_ref[...]).reshape(bb, tc, d)
    o_ref[...] = x_ref[...] + mod_ref[:, 2:3, :] * yn


def _causal_conv(ext_ref, bi, tc, w_ref):
    ext = ext_ref[bi]
    acc = pltpu.roll(ext, CONV_W - 1, axis=0)[CONV_PAD:] * w_ref[0:1, :]
    for j in range(1, CONV_W - 1):
        acc = acc + pltpu.roll(ext, CONV_W - 1 - j, axis=0)[CONV_PAD:] * w_ref[j:j + 1, :]
    return acc + ext[CONV_PAD:] * w_ref[CONV_W - 1:CONV_W, :]


def _even_body(*refs, has_state, n_past_valid, pos0, chunk, n_tiles, k_out_rows):
    refs = list(refs)
    x_ref = refs.pop(0)
    (mod_ref, npre_ref, npost_ref, win_ref, cw_ref, cb_ref, wg_ref, ba_ref, bx_ref,
     lam_ref, inv_ref, sinks_ref, wout_ref) = refs[:13]
    refs = refs[13:]
    if has_state:
        conv0_ref, h0_ref, k0_ref, v0_ref = refs[:4]
        refs = refs[4:]
    (y_ref, nconv_ref, nh_ref, nk_ref, nv_ref,
     u_ref, ext_ref, a_ref, b_ref, hs_ref, knat_ref, vnat_ref, q_ref, mix_ref) = refs
    bb, tc, _ = x_ref.shape
    lw = a_ref.shape[1]
    aw = q_ref.shape[1]
    kvw = knat_ref.shape[2]
    t = pl.program_id(1)
    cpt = tc // chunk
    m = WINDOW + chunk
    o_q, o_k, o_v, o_gb = 2 * lw, 2 * lw + aw, 2 * lw + aw + kvw, 2 * lw + aw + 2 * kvw

    @pl.when(t == 0)
    def _():
        ext_ref[:, 0:CONV_PAD, :] = jnp.zeros((bb, CONV_PAD, lw), F32)
        if has_state:
            ext_ref[:, CONV_PAD - 3:CONV_PAD, :] = conv0_ref[...]
            nh_ref[...] = h0_ref[...]
            knat_ref[:, 0:WINDOW, :] = k0_ref[...]
            vnat_ref[:, 0:WINDOW, :] = v0_ref[...]
        else:
            nh_ref[...] = jnp.zeros(nh_ref.shape, F32)
            knat_ref[:, 0:WINDOW, :] = jnp.zeros((bb, WINDOW, kvw), F32)
            vnat_ref[:, 0:WINDOW, :] = jnp.zeros((bb, WINDOW, kvw), F32)

    u_ref[...] = _dot(_pre_norm(x_ref, mod_ref, npre_ref).astype(BF16), win_ref[...])

    pos = (pos0 + t * tc + lax.broadcasted_iota(jnp.int32, (tc, LANES), 0)).astype(F32)
    ang = pos * inv_ref[...]
    lane = lax.broadcasted_iota(jnp.int32, (tc, LANES), 1) % HEAD_DIM
    cos_t = jnp.cos(ang)
    sin_t = jnp.sin(ang)
    half = ROT_DIM // 2
    sin_a = jnp.where(lane < half, -sin_t, 0.0)
    sin_b = jnp.where(lane >= half, sin_t, 0.0)

    def rope(xcol):
        return (xcol * cos_t + pltpu.roll(xcol, LANES - half, axis=1) * sin_a
                + pltpu.roll(xcol, half, axis=1) * sin_b)

    neg_c_softplus = -LRU_C * jax.nn.softplus(-lam_ref[...])
    row8 = lax.broadcasted_iota(jnp.int32, (SUBLANES, lw), 0)

    def per_batch(bi):
        rows = _rows(bi, tc)
        ext_ref[bi, CONV_PAD:CONV_PAD + tc, :] = u_ref[rows, 0:lw]
        xc = _causal_conv(ext_ref, bi, tc, cw_ref) + cb_ref[...]
        nconv_ref[bi, :, :] = ext_ref[bi, CONV_PAD + tc - 3:CONV_PAD + tc, :]
        ext_ref[bi, CONV_PAD - 3:CONV_PAD, :] = ext_ref[bi, CONV_PAD + tc - 3:CONV_PAD + tc, :]
        xcb = xc.astype(BF16)
        halves = [_dot(xcb[:, i * MXU_TILE:(i + 1) * MXU_TILE], wg_ref[i]) for i in range(lw // MXU_TILE)]
        r = jax.nn.sigmoid(jnp.concatenate([g[:, :MXU_TILE] for g in halves], axis=1) + ba_ref[...])
        ig = jax.nn.sigmoid(jnp.concatenate([g[:, MXU_TILE:] for g in halves], axis=1) + bx_ref[...])
        log_a = r * neg_c_softplus
        a_ref[...] = jnp.exp(log_a)
        b_ref[...] = jnp.sqrt(-_expm1(2.0 * log_a)) * (ig * xc)

        def scan_block(j, hc):
            r0 = pl.multiple_of(j * SUBLANES, SUBLANES)
            a = a_ref[pl.ds(r0, SUBLANES), :]
            b = b_ref[pl.ds(r0, SUBLANES), :]
            for sft in (1, 2, 4):
                keep = row8 >= sft
                b = jnp.where(keep, a * pltpu.roll(b, sft, axis=0) + b, b)
                a = jnp.where(keep, a * pltpu.roll(a, sft, axis=0), a)
            h = a * hc + b
            hs_ref[pl.ds(r0, SUBLANES), :] = h
            return h[SUBLANES - 1:SUBLANES, :]

        nh_ref[bi, :, :] = lax.fori_loop(0, tc // SUBLANES, scan_block, nh_ref[bi, :, :])
        mix_ref[rows, 0:lw] = (hs_ref[...] * _silu(u_ref[rows, lw:2 * lw])).astype(BF16)
        for j in range(aw // LANES):
            qcol = rope(u_ref[rows, o_q + j * LANES:o_q + (j + 1) * LANES])
            q_ref[rows, j * LANES:(j + 1) * LANES] = (qcol * (HEAD_DIM ** -0.5)).astype(BF16)
        knat_ref[bi, WINDOW:WINDOW + tc, :] = rope(u_ref[rows, o_k:o_k + kvw])
        vnat_ref[bi, WINDOW:WINDOW + tc, :] = u_ref[rows, o_v:o_v + kvw]
        nk_ref[bi, :, :] = knat_ref[bi, WINDOW + tc - k_out_rows:WINDOW + tc, :]
        nv_ref[bi, :, :] = vnat_ref[bi, WINDOW + tc - k_out_rows:WINDOW + tc, :]

    _for_each(bb, per_batch)

    lo_lane = lax.broadcasted_iota(jnp.int32, (m, LANES), 1) < HEAD_DIM
    lo_q = lax.broadcasted_iota(jnp.int32, (chunk, LANES), 1) < HEAD_DIM
    group = (aw // HEAD_DIM) // ATT_KV_HEADS

    sinks = [jnp.concatenate([jnp.full((chunk, 1), sinks_ref[h * group + g], F32) for g in range(group)], axis=0)
             for h in range(ATT_KV_HEADS)]

    lane_m = lax.broadcasted_iota(jnp.int32, (m, LANES), 1)

    def attend():
        units = []
        for bi, c in [(bi, c) for bi in range(bb) for c in range(cpt)]:
            rows = _rows(bi * cpt + c, chunk)
            krows = _rows(c, chunk, m)
            kseg = knat_ref[bi, krows, :]
            vseg = vnat_ref[bi, krows, :]
            krot = pltpu.roll(kseg, HEAD_DIM, axis=1)
            vrot = pltpu.roll(vseg, HEAD_DIM, axis=1)
            valid = None
            if n_past_valid < WINDOW:
                key_pos = t * tc + c * chunk - WINDOW + lax.broadcasted_iota(jnp.int32, (1, m), 1)
                valid = key_pos >= -n_past_valid
            for h in range(ATT_KV_HEADS):
                kd = (jnp.where(lo_lane, kseg, krot) if h == 0 else jnp.where(lo_lane, krot, kseg)).astype(BF16)
                v_lo = jnp.where(lo_lane, vseg if h == 0 else vrot, jnp.where(lane_m == HEAD_DIM, 1.0, 0.0))
                v_hi = jnp.where(lo_lane, jnp.where(lane_m == 0, 1.0, 0.0), vrot if h == 0 else vseg)
                units.append(dict(rows=rows, h=h, valid=valid, kd=kd, v_lo=v_lo.astype(BF16),
                                  v_hi=v_hi.astype(BF16)))

        for un in units:
            rows, h = un["rows"], un["h"]
            cols = [q_ref[rows, (h * group // 2 + j) * LANES:(h * group // 2 + j + 1) * LANES]
                    for j in range(group // 2)]
            zero = jnp.zeros_like(cols[0])
            qstack = jnp.concatenate(
                [part for qc in cols for part in (jnp.where(lo_q, qc, zero), jnp.where(lo_q, zero, qc))], axis=0)
            sc = _dot_nt(qstack, un["kd"])
            un["sc"] = sc if un["valid"] is None else jnp.where(un["valid"], sc, NEG_INF)
        for un in units:
            un["mx"] = jnp.maximum(jnp.max(un["sc"], axis=-1, keepdims=True), sinks[un["h"]])
        for un in units:
            un["p"] = jnp.exp(un["sc"] - un["mx"]).astype(BF16)
            un["sink_p"] = jnp.exp(sinks[un["h"]] - un["mx"])
        for un in units:
            rows, h, p, sink_p = un["rows"], un["h"], un["p"], un["sink_p"]
            for j in range(group // 2):
                r_lo = slice((2 * j) * chunk, (2 * j + 1) * chunk)
                r_hi = slice((2 * j + 1) * chunk, (2 * j + 2) * chunk)
                o_lo = _dot(p[r_lo], un["v_lo"])
                o_hi = _dot(p[r_hi], un["v_hi"])
                den_lo = o_lo[:, HEAD_DIM:HEAD_DIM + 1] + sink_p[r_lo]
                den_hi = o_hi[:, 0:1] + sink_p[r_hi]
                o = jnp.where(lo_q, o_lo / den_lo, o_hi / den_hi)
                col = h * group // 2 + j
                gate = _silu(u_ref[rows, o_gb + col * LANES:o_gb + (col + 1) * LANES])
                mix_ref[rows, lw + col * LANES:lw + (col + 1) * LANES] = (o * gate).astype(BF16)

    attend()

    if n_tiles > 1:
        knat_ref[:, 0:WINDOW, :] = knat_ref[:, tc:tc + WINDOW, :]
        vnat_ref[:, 0:WINDOW, :] = vnat_ref[:, tc:tc + WINDOW, :]

    y = _dot(mix_ref[...], wout_ref[...])
    _post_residual(x_ref, mod_ref, npost_ref, y, y_ref)


def _const_spec(shape):
    zeros = (0,) * len(shape)
    return pl.BlockSpec(shape, lambda b, t: zeros)


def _even_layer(x, mod, consts, state, *, tile, chunk, batch_block, n_past_valid, pos0, k_out_rows):
    bsz, seq, d = x.shape
    (npre, npost, win, cw, cb, wg, b_a, b_x, lam, inv_row, sinks, wout) = consts
    lw = cw.shape[1]
    kvw = ATT_KV_HEADS * HEAD_DIM
    aw = (win.shape[1] - 2 * lw - 2 * kvw) // 2
    n_tiles = seq // tile
    assert seq % tile == 0 and tile % chunk == 0 and bsz % batch_block == 0
    assert n_tiles == 1 or tile >= WINDOW
    rows = batch_block * tile
    has_state = state is not None

    def bspec(shape):
        nd = len(shape)
        return pl.BlockSpec((batch_block,) + shape, lambda b, t: (b,) + (0,) * nd)

    in_specs = [
        pl.BlockSpec((batch_block, tile, d), lambda b, t: (b, t, 0)),
        bspec((3, d)),
        _const_spec(npre.shape), _const_spec(npost.shape), _const_spec(win.shape),
        _const_spec(cw.shape), _const_spec(cb.shape), _const_spec(wg.shape),
        _const_spec(b_a.shape), _const_spec(b_x.shape), _const_spec(lam.shape), _const_spec(inv_row.shape),
        pl.BlockSpec(memory_space=pltpu.SMEM),
        _const_spec(wout.shape),
    ]
    args = [x, mod, npre, npost, win, cw, cb, wg, b_a, b_x, lam, inv_row, sinks, wout]
    if has_state:
        in_specs += [bspec((CONV_W - 1, lw)), bspec((1, lw)), bspec((WINDOW, kvw)), bspec((WINDOW, kvw))]
        args += list(state)
    out_specs = [
        pl.BlockSpec((batch_block, tile, d), lambda b, t: (b, t, 0)),
        bspec((CONV_W - 1, lw)), bspec((1, lw)), bspec((k_out_rows, kvw)), bspec((k_out_rows, kvw)),
    ]
    out_shape = [
        jax.ShapeDtypeStruct((bsz, seq, d), F32),
        jax.ShapeDtypeStruct((bsz, CONV_W - 1, lw), F32),
        jax.ShapeDtypeStruct((bsz, 1, lw), F32),
        jax.ShapeDtypeStruct((bsz, k_out_rows, kvw), F32),
        jax.ShapeDtypeStruct((bsz, k_out_rows, kvw), F32),
    ]
    scratch = [
        pltpu.VMEM((rows, win.shape[1]), F32),
        pltpu.VMEM((batch_block, CONV_PAD + tile, lw), F32),
        pltpu.VMEM((tile, lw), F32), pltpu.VMEM((tile, lw), F32), pltpu.VMEM((tile, lw), F32),
        pltpu.VMEM((batch_block, WINDOW + tile, kvw), F32),
        pltpu.VMEM((batch_block, WINDOW + tile, kvw), F32),
        pltpu.VMEM((rows, aw), BF16),
        pltpu.VMEM((rows, lw + aw), BF16),
    ]
    body = functools.partial(_even_body, has_state=has_state, n_past_valid=n_past_valid, pos0=pos0,
                             chunk=chunk, n_tiles=n_tiles, k_out_rows=k_out_rows)
    return pl.pallas_call(
        body,
        grid=(bsz // batch_block, n_tiles),
        in_specs=in_specs,
        out_specs=out_specs,
        out_shape=out_shape,
        scratch_shapes=scratch,
        compiler_params=pltpu.CompilerParams(
            dimension_semantics=("arbitrary", "arbitrary"), vmem_limit_bytes=VMEM_LIMIT_BYTES),
        name="even_layer_state" if has_state else "even_layer_prompt",
    )(*args)


def _gdn_groups(chunk, n_heads):
    hpg = max(1, min(n_heads, MXU_TILE // chunk))
    assert n_heads % hpg == 0
    return hpg, n_heads // hpg


def _time_of_row(i):
    return (i % SUBLANES) * SUBLANES + i // SUBLANES


def _phase_major_matrix(n_rows, chunk):
    i = jnp.arange(n_rows)
    src = (i // chunk) * chunk + _time_of_row(i % chunk)
    return (src[:, None] == i[None, :]).astype(BF16)


def _conv_phase_major(u_ref, hist_ref, nconv_ref, w_ref, tc, chunk, cch):
    cpt = tc // chunk
    blocks = [u_ref[SUBLANES * b:SUBLANES * (b + 1), 0:cch] for b in range(tc // SUBLANES)]
    taps_w = [w_ref[j:j + 1, :] for j in range(CONV_W)]
    top_row = lax.broadcasted_iota(jnp.int32, (SUBLANES, cch), 0) == 0
    first_late = SUBLANES - (CONV_W - 1)
    prev = {r: pltpu.roll(hist_ref[r - first_late], 1, axis=0) for r in range(first_late, SUBLANES)}
    out = []
    for c in range(cpt):
        late = {}
        for r in range(first_late, SUBLANES):
            moved = pltpu.roll(blocks[SUBLANES * c + r], 1, axis=0)
            late[r] = jnp.where(top_row, prev[r], moved)
            prev[r] = moved
        for r in range(SUBLANES):
            acc = None
            for j in range(CONV_W):
                s = CONV_W - 1 - j
                tap = blocks[SUBLANES * c + r - s] if r >= s else late[r - s + SUBLANES]
                acc = tap * taps_w[j] if acc is None else acc + tap * taps_w[j]
            out.append(acc)
    for i in range(CONV_W - 1):
        last = blocks[SUBLANES * (cpt - 1) + first_late + i]
        hist_ref[i] = last
        nconv_ref[0, i:i + 1, :] = last[SUBLANES - 1:SUBLANES, :]
    return jnp.concatenate(out, axis=0)


def _odd_body(*refs, has_state, chunk, phase_major):
    refs = list(refs)
    x_ref = refs.pop(0)
    perm_ref = refs.pop(0) if phase_major else None
    (mod_ref, npre_ref, npost_ref, win_ref, wba_ref, cw_ref, alog_ref, dtb_ref, hn_ref, wout_ref) = refs[:10]
    refs = refs[10:]
    if has_state:
        conv0_ref, s0_ref = refs[:2]
        refs = refs[2:]
    (y_ref, nconv_ref, ns_ref,
     u_ref, ext_ref, q_ref, k_ref, kb_ref, kbe_ref, vb_ref, qe_ref, kd_ref, beta_ref, gcum_ref,
     wq_ref, uu_ref, qk_ref, mix_ref) = refs
    bb, tc, _ = x_ref.shape
    n_heads = ns_ref.shape[1]
    kw = n_heads * GDN_DK
    vw = n_heads * GDN_DV
    cch = 2 * kw + vw
    rows_all = bb * tc
    t = pl.program_id(1)
    cpt = tc // chunk
    n_steps = (chunk - 1).bit_length()
    assert n_steps >= 2
    hpg, n_groups = _gdn_groups(chunk, n_heads)
    lw = hpg * chunk

    @pl.when(t == 0)
    def _():
        ext_ref[...] = jnp.zeros(ext_ref.shape, F32)
        if has_state:
            ext_ref[:, CONV_PAD - 3:CONV_PAD, :] = conv0_ref[...]
            ns_ref[...] = s0_ref[...]
        else:
            ns_ref[...] = jnp.zeros(ns_ref.shape, F32)

    def permute_rows(v):
        n = perm_ref.shape[0]
        return jnp.concatenate(
            [_dot(perm_ref[...], v[i * n:(i + 1) * n]).astype(BF16) for i in range(rows_all // n)], axis=0)

    hmod = _pre_norm(x_ref, mod_ref, npre_ref).astype(BF16)
    if phase_major:
        hmod = permute_rows(hmod)
    u_ref[...] = _dot(hmod, win_ref[...])
    ba = _dot(hmod, wba_ref[...])
    beta_ref[...] = jax.nn.sigmoid(ba)
    g = -jnp.exp(alog_ref[...]) * jax.nn.softplus(ba + dtb_ref[...])
    if phase_major:
        row8 = lax.broadcasted_iota(jnp.int32, (SUBLANES, LANES), 0)
        parts = []
        for c in range(rows_all // chunk):
            run = []
            for r in range(SUBLANES):
                blk = g[c * chunk + r * SUBLANES:c * chunk + (r + 1) * SUBLANES]
                run.append(blk if r == 0 else run[-1] + blk)
            total = run[-1]
            incl = total
            for sft in (1, 2, 4):
                incl = jnp.where(row8 >= sft, incl + pltpu.roll(incl, sft, axis=0), incl)
            parts += [blk + (incl - total) for blk in run]
        g = jnp.concatenate(parts, axis=0)
    else:
        row_in_chunk = lax.broadcasted_iota(jnp.int32, (rows_all, LANES), 0) % chunk
        sft = 1
        while sft < chunk:
            g = jnp.where(row_in_chunk >= sft, g + pltpu.roll(g, sft, axis=0), g)
            sft *= 2
    gcum_ref[...] = g

    def per_batch(bi):
        rows = _rows(bi, tc)
        if phase_major:
            qkv = _silu(_conv_phase_major(u_ref, ext_ref, nconv_ref, cw_ref, tc, chunk, cch))
        else:
            ext_ref[bi, CONV_PAD:CONV_PAD + tc, :] = u_ref[rows, 0:cch]
            qkv = _silu(_causal_conv(ext_ref, bi, tc, cw_ref))
            nconv_ref[bi, :, :] = ext_ref[bi, CONV_PAD + tc - 3:CONV_PAD + tc, :]
            ext_ref[bi, CONV_PAD - 3:CONV_PAD, :] = ext_ref[bi, CONV_PAD + tc - 3:CONV_PAD + tc, :]
        beta_b = beta_ref[rows, :]
        gc_b = gcum_ref[rows, :]
        eg_b = jnp.exp(gc_b)
        gc3 = gc_b.reshape(cpt, chunk, LANES)
        dec_b = jnp.exp(gc3[:, chunk - 1:chunk, :] - gc3).reshape(tc, LANES)
        for h in range(n_heads):
            hc = slice(h * GDN_DK, (h + 1) * GDN_DK)
            qh = qkv[:, hc]
            kh = qkv[:, kw + h * GDN_DK:kw + (h + 1) * GDN_DK]
            vh = qkv[:, 2 * kw + h * GDN_DV:2 * kw + (h + 1) * GDN_DV]
            qh = qh * lax.rsqrt(jnp.sum(qh * qh, axis=-1, keepdims=True) + EPS) * (GDN_DK ** -0.5)
            kh = kh * lax.rsqrt(jnp.sum(kh * kh, axis=-1, keepdims=True) + EPS)
            beta = beta_b[:, h:h + 1]
            eg = eg_b[:, n_heads + h:n_heads + h + 1]
            kb = kh * beta
            q_ref[rows, hc] = qh.astype(BF16)
            k_ref[rows, hc] = kh.astype(BF16)
            kb_ref[rows, hc] = kb.astype(BF16)
            kbe_ref[rows, hc] = (kb * eg).astype(BF16)
            vb_ref[rows, hc] = (vh * beta).astype(BF16)
            qe_ref[rows, hc] = (qh * eg).astype(BF16)
            kd_ref[rows, hc] = (kh * dec_b[:, n_heads + h:n_heads + h + 1]).astype(BF16)

    _for_each(bb, per_batch)

    lane_w = lax.broadcasted_iota(jnp.int32, (chunk, lw), 1)
    row_w = lax.broadcasted_iota(jnp.int32, (chunk, lw), 0)
    blk_w = lane_w // chunk
    col_w = lane_w % chunk
    diag_w = row_w == col_w
    if phase_major:
        row_w, col_w = _time_of_row(row_w), _time_of_row(col_w)
    incl_w = row_w >= col_w
    strict_w = row_w > col_w
    eye_w = jnp.where(diag_w, 1.0, 0.0).astype(F32)
    feat_blk = lax.broadcasted_iota(jnp.int32, (chunk, hpg * GDN_DK), 1) // GDN_DK

    def block_diag(m_b):
        return jnp.concatenate([jnp.where(blk_w == hh, m_b, jnp.zeros_like(m_b)) for hh in range(hpg)], axis=0)

    def head_cols(h):
        return slice(h * GDN_DK, (h + 1) * GDN_DK)

    def wy_factors():
        units = []
        for seg in range(bb * cpt):
            rows = _rows(seg, chunk)
            gt = gcum_ref[rows, :]
            for gi in range(n_groups):
                gcols = slice(gi * hpg * GDN_DK, (gi + 1) * hpg * GDN_DK)
                k4 = k_ref[rows, gcols]
                k_bd = jnp.concatenate(
                    [jnp.where(feat_blk == hh, k4, jnp.zeros_like(k4)) for hh in range(hpg)], axis=0)
                kk = _dot_nt(jnp.concatenate([kb_ref[rows, gcols], q_ref[rows, gcols]], axis=0), k_bd)
                gcol = jnp.zeros((chunk, lw), F32)
                for hh in range(hpg):
                    ln = n_heads + gi * hpg + hh
                    gcol = jnp.where(blk_w == hh, gt[:, ln:ln + 1], gcol)
                grow = jnp.sum(jnp.where(diag_w, gcol, 0.0), axis=0, keepdims=True)
                decay = jnp.exp(jnp.where(incl_w, gcol - grow, 0.0))
                p_b = (-(kk[:chunk] * jnp.where(strict_w, decay, 0.0))).astype(BF16)
                qk_ref[seg * n_groups + gi] = (kk[chunk:] * jnp.where(incl_w, decay, 0.0)).astype(BF16)
                units.append(dict(seg=seg, rows=rows, gi=gi, p=p_b, t=eye_w + p_b.astype(F32)))
        for un in units:
            un["p"] = _dot(un["p"], block_diag(un["p"])).astype(BF16)
        for _ in range(n_steps - 2):
            for un in units:
                out = _dot(jnp.concatenate([un["t"].astype(BF16), un["p"]], axis=0), block_diag(un["p"]))
                un["t"] = un["t"] + out[:chunk]
                un["p"] = out[chunk:].astype(BF16)
        for un in units:
            un["t"] = un["t"] + _dot(un["t"].astype(BF16), block_diag(un["p"]))
        for un in units:
            seg, rows, gi = un["seg"], un["rows"], un["gi"]
            heads = range(gi * hpg, (gi + 1) * hpg)
            rhs = jnp.concatenate(
                [jnp.concatenate([kbe_ref[rows, head_cols(h)], vb_ref[rows, head_cols(h)]], axis=1) for h in heads],
                axis=0)
            wu = _dot(block_diag(un["t"].astype(BF16)), rhs)
            for hh, h in enumerate(heads):
                blk = wu[hh * chunk:(hh + 1) * chunk]
                wq_ref[seg * n_heads + h] = jnp.concatenate(
                    [blk[:, :GDN_DK].astype(BF16), qe_ref[rows, head_cols(h)]], axis=0)
                uu_ref[rows, head_cols(h)] = blk[:, GDN_DK:]

    wy_factors()

    zero_b = jnp.zeros((chunk, GDN_DV), BF16)

    pairs = [(bi, h) for bi in range(bb) for h in range(n_heads)]
    state = {bh: ns_ref[bh[0], bh[1]] for bh in pairs}
    for c in range(cpt):
        seg_of = {bi: bi * cpt + c for bi in range(bb)}
        rows_of = {bi: _rows(seg_of[bi], chunk) for bi in range(bb)}
        ws_qs = {(bi, h): _dot(wq_ref[seg_of[bi] * n_heads + h], state[bi, h].astype(BF16)) for bi, h in pairs}
        v_new = {(bi, h): (uu_ref[rows_of[bi], head_cols(h)] - ws_qs[bi, h][:chunk]).astype(BF16)
                 for bi, h in pairs}
        eg_last = {bi: jnp.exp(gcum_ref[pl.ds(seg_of[bi] * chunk + chunk - 1, 1), :]) for bi in range(bb)}
        state = {(bi, h): state[bi, h] * eg_last[bi][:, n_heads + h:n_heads + h + 1]
                 + _dot_tn(kd_ref[rows_of[bi], head_cols(h)], v_new[bi, h]) for bi, h in pairs}
        for bi in range(bb):
            rows = rows_of[bi]
            for gi in range(n_groups):
                heads = range(gi * hpg, (gi + 1) * hpg)
                v_bd = jnp.concatenate(
                    [jnp.concatenate([v_new[bi, h] if j == hh else zero_b for j in range(hpg)], axis=1)
                     for hh, h in enumerate(heads)], axis=0)
                o = (jnp.concatenate([ws_qs[bi, h][chunk:] for h in heads], axis=1)
                     + _dot(qk_ref[seg_of[bi] * n_groups + gi], v_bd))
                for hh, h in enumerate(heads):
                    gate = _silu(u_ref[rows, cch + h * GDN_DV:cch + (h + 1) * GDN_DV])
                    o_h = _rms(o[:, hh * GDN_DV:(hh + 1) * GDN_DV], hn_ref[...])
                    mix_ref[rows, head_cols(h)] = (o_h * gate).astype(BF16)
    for bi, h in pairs:
        ns_ref[bi, h] = state[bi, h]

    mixed = mix_ref[...]
    if phase_major:
        mixed = permute_rows(mixed)
    y = _dot(mixed, wout_ref[...])
    _post_residual(x_ref, mod_ref, npost_ref, y, y_ref)


def _odd_layer(x, mod, consts, state, *, tile, chunk, batch_block):
    bsz, seq, d = x.shape
    (npre, npost, win, wba, cw, alog, dtb, hn, wout) = consts
    cch = cw.shape[1]
    vw = wout.shape[0]
    n_heads = vw // GDN_DV
    n_tiles = seq // tile
    assert seq % tile == 0 and tile % chunk == 0 and bsz % batch_block == 0
    rows = batch_block * tile
    has_state = state is not None
    phase_major = chunk == SUBLANES * SUBLANES and batch_block == 1 and not has_state

    def bspec(shape):
        nd = len(shape)
        return pl.BlockSpec((batch_block,) + shape, lambda b, t: (b,) + (0,) * nd)

    in_specs = [pl.BlockSpec((batch_block, tile, d), lambda b, t: (b, t, 0))]
    args = [x]
    if phase_major:
        perm_rows = min(rows, MXU_TILE)
        assert rows % perm_rows == 0 and perm_rows % chunk == 0
        in_specs.append(_const_spec((perm_rows, perm_rows)))
        args.append(_phase_major_matrix(perm_rows, chunk))
    in_specs += [
        bspec((3, d)),
        _const_spec(npre.shape), _const_spec(npost.shape), _const_spec(win.shape), _const_spec(wba.shape),
        _const_spec(cw.shape), _const_spec(alog.shape), _const_spec(dtb.shape), _const_spec(hn.shape),
        _const_spec(wout.shape),
    ]
    args += [mod, npre, npost, win, wba, cw, alog, dtb, hn, wout]
    if has_state:
        in_specs += [bspec((CONV_W - 1, cch)), bspec((n_heads, GDN_DK, GDN_DV))]
        args += list(state)
    out_specs = [
        pl.BlockSpec((batch_block, tile, d), lambda b, t: (b, t, 0)),
        bspec((CONV_W - 1, cch)), bspec((n_heads, GDN_DK, GDN_DV)),
    ]
    out_shape = [
        jax.ShapeDtypeStruct((bsz, seq, d), F32),
        jax.ShapeDtypeStruct((bsz, CONV_W - 1, cch), F32),
        jax.ShapeDtypeStruct((bsz, n_heads, GDN_DK, GDN_DV), F32),
    ]
    hpg, n_groups = _gdn_groups(chunk, n_heads)
    n_seg = rows // chunk
    conv_hist = (CONV_W - 1, SUBLANES, cch) if phase_major else (batch_block, CONV_PAD + tile, cch)
    scratch = (
        [pltpu.VMEM((rows, win.shape[1]), F32),
         pltpu.VMEM(conv_hist, F32)]
        + [pltpu.VMEM((rows, vw), BF16)] * 7
        + [pltpu.VMEM((rows, LANES), F32)] * 2
        + [pltpu.VMEM((n_seg * n_heads, 2 * chunk, GDN_DK), BF16),
           pltpu.VMEM((rows, vw), F32),
           pltpu.VMEM((n_seg * n_groups, chunk, hpg * chunk), BF16),
           pltpu.VMEM((rows, vw), BF16)]
    )
    body = functools.partial(_odd_body, has_state=has_state, chunk=chunk, phase_major=phase_major)
    return pl.pallas_call(
        body,
        grid=(bsz // batch_block, n_tiles),
        in_specs=in_specs,
        out_specs=out_specs,
        out_shape=out_shape,
        scratch_shapes=scratch,
        compiler_params=pltpu.CompilerParams(
            dimension_semantics=("arbitrary", "arbitrary"), vmem_limit_bytes=VMEM_LIMIT_BYTES),
        name="odd_layer_state" if has_state else "odd_layer_prompt",
    )(*args)


def _gate_weights(w_a, w_x):
    heads, blk, _ = w_a.shape
    per_tile = MXU_TILE // blk
    tiles = []
    for i in range(heads // per_tile):
        sl = slice(i * per_tile, (i + 1) * per_tile)
        tiles.append(jnp.concatenate(
            [jax.scipy.linalg.block_diag(*w_a[sl]), jax.scipy.linalg.block_diag(*w_x[sl])], axis=1))
    return jnp.stack(tiles).astype(BF16)


def _rope_inv_row():
    half = ROT_DIM // 2
    inv = ROPE_THETA ** (-(jnp.arange(half, dtype=F32) * 2.0 / ROT_DIM))
    per_head = jnp.concatenate([inv, inv, jnp.zeros((HEAD_DIM - ROT_DIM,), F32)])
    return jnp.tile(per_head, LANES // HEAD_DIM).reshape(1, LANES)


def kernel(x_prompt, x_sample, state_lru_conv, state_lru_h, cache_swa_k, cache_swa_v, state_gdn_conv, state_gdn_s, c_prompt, c_sample, ev_mod_w, ev_mod_b, ev_norm_pre, ev_norm_post, ev_w_in, lru_conv_w, lru_conv_b, lru_w_a, lru_b_a, lru_w_x, lru_b_x, lru_lambda, swa_sinks, ev_w_out, od_mod_w, od_mod_b, od_norm_pre, od_norm_post, od_w_in, gdn_conv_w, gdn_a_log, gdn_dt_bias, gdn_head_norm, od_w_out):
    bp, seq, d = x_prompt.shape
    bs, dec_seq, _ = x_sample.shape
    lw = lru_conv_w.shape[-1]
    kvw = ATT_KV_HEADS * HEAD_DIM
    n_heads = gdn_a_log.shape[-1]
    cch = gdn_conv_w.shape[-1]
    vw = od_w_out.shape[1]

    n_c = bp + bs
    n_pad = -n_c % 16
    c_all = jnp.concatenate([c_prompt, c_sample, jnp.zeros((n_pad, d), F32)], axis=0)
    mod_ev, mod_od = _modulation(c_all, ev_mod_w[0], ev_mod_b[0], od_mod_w[0], od_mod_b[0])
    mod_ev = mod_ev.reshape(n_c + n_pad, 3, d)
    mod_od = mod_od.reshape(n_c + n_pad, 3, d)

    row = lambda a: a.reshape(1, -1)
    ev_consts = (row(ev_norm_pre[0]), row(ev_norm_post[0]), ev_w_in[0].astype(BF16), lru_conv_w[0],
                 row(lru_conv_b[0]), _gate_weights(lru_w_a[0], lru_w_x[0]), row(lru_b_a[0]), row(lru_b_x[0]),
                 row(lru_lambda[0]), _rope_inv_row(), swa_sinks[0], ev_w_out[0].astype(BF16))
    w_in_od = od_w_in[0]
    pad_lanes = lambda a: jnp.pad(a, ((0, 0), (0, LANES - a.shape[1])))
    head_row = lambda a: jnp.pad(a.reshape(1, -1), ((0, 0), (n_heads, LANES - 2 * n_heads)))
    od_consts = (row(od_norm_pre[0]), row(od_norm_post[0]), w_in_od[:, :cch + vw].astype(BF16),
                 pad_lanes(w_in_od[:, cch + vw:]).astype(BF16), gdn_conv_w[0],
                 head_row(gdn_a_log[0]), head_row(gdn_dt_bias[0]), row(gdn_head_norm[0]),
                 od_w_out[0].astype(BF16))

    tile = min(PROMPT_TILE, seq)
    xp, lru_conv_p, lru_h_p, swa_k_p, swa_v_p = _even_layer(
        x_prompt, mod_ev[:bp], ev_consts, None, tile=tile, chunk=min(CHUNK, seq), batch_block=1,
        n_past_valid=0, pos0=0, k_out_rows=WINDOW)
    ev_state = (state_lru_conv[0], state_lru_h[0].reshape(bs, 1, lw),
                cache_swa_k[0].reshape(bs, WINDOW, kvw), cache_swa_v[0].reshape(bs, WINDOW, kvw))
    xs, lru_conv_s, lru_h_s, swa_k_s, swa_v_s = _even_layer(
        x_sample, mod_ev[bp:n_c], ev_consts, ev_state, tile=dec_seq, chunk=min(CHUNK, dec_seq), batch_block=bs,
        n_past_valid=WINDOW, pos0=PAST_LEN, k_out_rows=dec_seq)

    xp, gdn_conv_p, gdn_s_p = _odd_layer(
        xp, mod_od[:bp], od_consts, None, tile=tile, chunk=min(CHUNK, seq), batch_block=1)
    xs, gdn_conv_s, gdn_s_s = _odd_layer(
        xs, mod_od[bp:n_c], od_consts, (state_gdn_conv[0], state_gdn_s[0]),
        tile=dec_seq, chunk=min(CHUNK, dec_seq), batch_block=bs)

    kv_shape = lambda a: a.reshape(1, a.shape[0], a.shape[1], ATT_KV_HEADS, HEAD_DIM)
    return (xp, xs,
            lru_conv_p[None], lru_conv_s[None],
            lru_h_p.reshape(1, bp, lw), lru_h_s.reshape(1, bs, lw),
            kv_shape(swa_k_p), kv_shape(swa_k_s), kv_shape(swa_v_p), kv_shape(swa_v_s),
            gdn_conv_p[None], gdn_conv_s[None], gdn_s_p[None], gdn_s_s[None])
```

```python
import functools

import jax
import jax.numpy as jnp
from jax import lax
from jax.experimental import pallas as pl
from jax.experimental.pallas import tpu as pltpu

F32 = jnp.float32
BF16 = jnp.bfloat16

CHUNK = 64
EPS = 1e-6
CONV_W = 4
NEG_INF = -1e30
LRU_HEADS = 8
LRU_C = 8.0
HEAD_DIM = 64
ATT_KV_HEADS = 2
WINDOW = 128
ROT_DIM = HEAD_DIM // 4
ROPE_THETA = 500000.0
GDN_DK = 128
GDN_DV = 128
PAST_LEN = 4096

LANES = 128
SUBLANES = 8
MXU_TILE = 256
VMEM_LIMIT_BYTES = 56 * 1024 * 1024

PROMPT_TILE = 512
MOD_TILE = 512
CONV_PAD = SUBLANES


def _silu(x):
    return x * jax.nn.sigmoid(x)


def _expm1(x):
    u = jnp.exp(x)
    d = u - 1.0
    return jnp.where(u == 1.0, x, jnp.where(d == -1.0, -1.0, d * x / jnp.log(u)))


def _rms(x, g):
    return x * lax.rsqrt(jnp.mean(x * x, axis=-1, keepdims=True) + EPS) * g


def _dot(a, b):
    return jnp.dot(a, b, preferred_element_type=F32)


def _dot_nt(a, b):
    return lax.dot_general(a, b, (((1,), (1,)), ((), ())), preferred_element_type=F32)


def _dot_tn(a, b):
    return lax.dot_general(a, b, (((0,), (0,)), ((), ())), preferred_element_type=F32)


def _for_each(n, body):
    if n == 1:
        body(0)
    else:
        def step(i, carry):
            body(i)
            return carry
        lax.fori_loop(0, n, step, 0)


def _rows(i, n, size=None):
    size = n if size is None else size
    if isinstance(i, int):
        return pl.ds(i * n, size)
    return pl.ds(pl.multiple_of(i * n, n), size)


def _mod_body(c_ref, w0_ref, b0_ref, w1_ref, b1_ref, o0_ref, o1_ref):
    c = _silu(c_ref[...]).astype(BF16)
    o0_ref[...] = _dot(c, w0_ref[...].astype(BF16)) + b0_ref[...]
    o1_ref[...] = _dot(c, w1_ref[...].astype(BF16)) + b1_ref[...]


def _modulation(c_all, w0, b0, w1, b1):
    n, d = c_all.shape
    d3 = w0.shape[1]
    wspec = pl.BlockSpec((d, MOD_TILE), lambda j: (0, j))
    bspec = pl.BlockSpec((1, MOD_TILE), lambda j: (0, j))
    ospec = pl.BlockSpec((n, MOD_TILE), lambda j: (0, j))
    return pl.pallas_call(
        _mod_body,
        grid=(d3 // MOD_TILE,),
        in_specs=[pl.BlockSpec((n, d), lambda j: (0, 0)), wspec, bspec, wspec, bspec],
        out_specs=[ospec, ospec],
        out_shape=[jax.ShapeDtypeStruct((n, d3), F32)] * 2,
        name="adaln_modulation",
    )(c_all, w0, b0.reshape(1, d3), w1, b1.reshape(1, d3))


def _pre_norm(x_ref, mod_ref, norm_pre_ref):
    bb, tc, d = x_ref.shape
    x = x_ref[...]
    h = _rms(x, norm_pre_ref[...]) * (1.0 + mod_ref[:, 1:2, :]) + mod_ref[:, 0:1, :]
    return h.reshape(bb * tc, d)


def _post_residual(x_ref, mod_ref, norm_post_ref, y, o_ref):
    bb, tc, d = x_ref.shape
    yn = _rms(y, norm_post_ref[...]).reshape(bb, tc, d)
    o_ref[...] = x_ref[...] + mod_ref[:, 2:3, :] * yn


def _causal_conv(ext_ref, bi, tc, w_ref):
    ext = ext_ref[bi]
    acc = pltpu.roll(ext, CONV_W - 1, axis=0)[CONV_PAD:] * w_ref[0:1, :]
    for j in range(1, CONV_W - 1):
        acc = acc + pltpu.roll(ext, CONV_W - 1 - j, axis=0)[CONV_PAD:] * w_ref[j:j + 1, :]
    return acc + ext[CONV_PAD:] * w_ref[CONV_W - 1:CONV_W, :]


def _scan_phase_major(a, b, h0, chunk):
    tc, width = a.shape
    row8 = lax.broadcasted_iota(jnp.int32, (SUBLANES, width), 0)
    carry = h0
    out = []
    for c in range(tc // chunk):
        acc_a, acc_b = [], []
        for r in range(SUBLANES):
            blk = slice(c * chunk + r * SUBLANES, c * chunk + (r + 1) * SUBLANES)
            if r == 0:
                acc_a.append(a[blk])
                acc_b.append(b[blk])
            else:
                acc_b.append(a[blk] * acc_b[-1] + b[blk])
                acc_a.append(a[blk] * acc_a[-1])
        tot_a, tot_b = acc_a[-1], acc_b[-1]
        for sft in (1, 2, 4):
            keep = row8 >= sft
            tot_b = jnp.where(keep, tot_a * pltpu.roll(tot_b, sft, axis=0) + tot_b, tot_b)
            tot_a = jnp.where(keep, tot_a * pltpu.roll(tot_a, sft, axis=0), tot_a)
        h_out = tot_b + tot_a * carry
        h_in = jnp.where(row8 == 0, carry, pltpu.roll(h_out, 1, axis=0))
        out += [acc_b[r] + acc_a[r] * h_in for r in range(SUBLANES)]
        carry = h_out[SUBLANES - 1:SUBLANES, :]
    return jnp.concatenate(out, axis=0), carry


def _even_body(*refs, has_state, n_past_valid, pos0, chunk, n_tiles, k_out_rows, phase_major):
    refs = list(refs)
    x_ref = refs.pop(0)
    perm_ref = refs.pop(0) if phase_major else None
    (mod_ref, npre_ref, npost_ref, win_ref, cw_ref, cb_ref, wg_ref, ba_ref, bx_ref,
     lam_ref, inv_ref, sinks_ref, wout_ref) = refs[:13]
    refs = refs[13:]
    if has_state:
        conv0_ref, h0_ref, k0_ref, v0_ref = refs[:4]
        refs = refs[4:]
    (y_ref, nconv_ref, nh_ref, nk_ref, nv_ref,
     u_ref, ext_ref, a_ref, b_ref, hs_ref, knat_ref, vnat_ref, q_ref, mix_ref) = refs
    bb, tc, _ = x_ref.shape
    lw = a_ref.shape[1]
    aw = q_ref.shape[1]
    kvw = knat_ref.shape[2]
    t = pl.program_id(1)
    cpt = tc // chunk
    m = WINDOW + chunk
    o_q, o_k, o_v, o_gb = 2 * lw, 2 * lw + aw, 2 * lw + aw + kvw, 2 * lw + aw + 2 * kvw

    @pl.when(t == 0)
    def _():
        ext_ref[...] = jnp.zeros(ext_ref.shape, F32)
        if has_state:
            ext_ref[:, CONV_PAD - 3:CONV_PAD, :] = conv0_ref[...]
            nh_ref[...] = h0_ref[...]
            knat_ref[:, 0:WINDOW, :] = k0_ref[...]
            vnat_ref[:, 0:WINDOW, :] = v0_ref[...]
        else:
            nh_ref[...] = jnp.zeros(nh_ref.shape, F32)
            knat_ref[:, 0:WINDOW, :] = jnp.zeros((bb, WINDOW, kvw), F32)
            vnat_ref[:, 0:WINDOW, :] = jnp.zeros((bb, WINDOW, kvw), F32)

    def permute_rows(v):
        n = perm_ref.shape[0]
        return jnp.concatenate(
            [_dot(perm_ref[...], v[i * n:(i + 1) * n]).astype(BF16) for i in range(bb * tc // n)], axis=0)

    hmod_time = _pre_norm(x_ref, mod_ref, npre_ref).astype(BF16)
    hmod = permute_rows(hmod_time) if phase_major else hmod_time
    u_ref[...] = _dot(hmod, win_ref[...])

    def make_rope(time_in_tile):
        ang = (pos0 + t * tc + time_in_tile).astype(F32) * inv_ref[...]
        lane = lax.broadcasted_iota(jnp.int32, ang.shape, 1) % HEAD_DIM
        cos_t = jnp.cos(ang)
        sin_t = jnp.sin(ang)
        half = ROT_DIM // 2
        sin_a = jnp.where(lane < half, -sin_t, 0.0)
        sin_b = jnp.where(lane >= half, sin_t, 0.0)

        def rope(xcol):
            return (xcol * cos_t + pltpu.roll(xcol, LANES - half, axis=1) * sin_a
                    + pltpu.roll(xcol, half, axis=1) * sin_b)
        return rope

    row_t = lax.broadcasted_iota(jnp.int32, (tc, LANES), 0)
    if phase_major:
        row_t = (row_t // chunk) * chunk + _time_of_row(row_t % chunk)
    rope = make_rope(row_t)

    neg_c_softplus = -LRU_C * jax.nn.softplus(-lam_ref[...])
    row8 = lax.broadcasted_iota(jnp.int32, (SUBLANES, lw), 0)

    def per_batch(bi):
        rows = _rows(bi, tc)
        if phase_major:
            xc = _conv_phase_major(u_ref, ext_ref, nconv_ref, cw_ref, tc, chunk, lw) + cb_ref[...]
        else:
            ext_ref[bi, CONV_PAD:CONV_PAD + tc, :] = u_ref[rows, 0:lw]
            xc = _causal_conv(ext_ref, bi, tc, cw_ref) + cb_ref[...]
            nconv_ref[bi, :, :] = ext_ref[bi, CONV_PAD + tc - 3:CONV_PAD + tc, :]
            ext_ref[bi, CONV_PAD - 3:CONV_PAD, :] = ext_ref[bi, CONV_PAD + tc - 3:CONV_PAD + tc, :]
        xcb = xc.astype(BF16)
        halves = [_dot(xcb[:, i * MXU_TILE:(i + 1) * MXU_TILE], wg_ref[i]) for i in range(lw // MXU_TILE)]
        r = jax.nn.sigmoid(jnp.concatenate([g[:, :MXU_TILE] for g in halves], axis=1) + ba_ref[...])
        ig = jax.nn.sigmoid(jnp.concatenate([g[:, MXU_TILE:] for g in halves], axis=1) + bx_ref[...])
        log_a = r * neg_c_softplus
        a_all = jnp.exp(log_a)
        b_all = jnp.sqrt(-_expm1(2.0 * log_a)) * (ig * xc)
        if phase_major:
            h_all, nh_ref[bi, :, :] = _scan_phase_major(a_all, b_all, nh_ref[bi, :, :], chunk)
            hs_ref[...] = h_all
        else:
            a_ref[...] = a_all
            b_ref[...] = b_all

        def scan_block(j, hc):
            r0 = pl.multiple_of(j * SUBLANES, SUBLANES)
            a = a_ref[pl.ds(r0, SUBLANES), :]
            b = b_ref[pl.ds(r0, SUBLANES), :]
            for sft in (1, 2, 4):
                keep = row8 >= sft
                b = jnp.where(keep, a * pltpu.roll(b, sft, axis=0) + b, b)
                a = jnp.where(keep, a * pltpu.roll(a, sft, axis=0), a)
            h = a * hc + b
            hs_ref[pl.ds(r0, SUBLANES), :] = h
            return h[SUBLANES - 1:SUBLANES, :]

        if not phase_major:
            nh_ref[bi, :, :] = lax.fori_loop(0, tc // SUBLANES, scan_block, nh_ref[bi, :, :])
        mix_ref[rows, 0:lw] = (hs_ref[...] * _silu(u_ref[rows, lw:2 * lw])).astype(BF16)
        for j in range(aw // LANES):
            qcol = rope(u_ref[rows, o_q + j * LANES:o_q + (j + 1) * LANES])
            q_ref[rows, j * LANES:(j + 1) * LANES] = (qcol * (HEAD_DIM ** -0.5)).astype(BF16)
        knat_ref[bi, WINDOW:WINDOW + tc, :] = rope(u_ref[rows, o_k:o_k + kvw])
        vnat_ref[bi, WINDOW:WINDOW + tc, :] = u_ref[rows, o_v:o_v + kvw]
        if not phase_major:
            nk_ref[bi, :, :] = knat_ref[bi, WINDOW + tc - k_out_rows:WINDOW + tc, :]
            nv_ref[bi, :, :] = vnat_ref[bi, WINDOW + tc - k_out_rows:WINDOW + tc, :]

    _for_each(bb, per_batch)

    if phase_major:
        @pl.when(t == n_tiles - 1)
        def _():
            first = tc - k_out_rows
            kv = _dot(hmod_time[first:], win_ref[:, o_k:o_k + 2 * kvw])
            rope_last = make_rope(first + lax.broadcasted_iota(jnp.int32, (k_out_rows, LANES), 0))
            nk_ref[0] = rope_last(kv[:, 0:kvw])
            nv_ref[0] = kv[:, kvw:2 * kvw]

    lo_lane = lax.broadcasted_iota(jnp.int32, (m, LANES), 1) < HEAD_DIM
    lo_q = lax.broadcasted_iota(jnp.int32, (chunk, LANES), 1) < HEAD_DIM
    group = (aw // HEAD_DIM) // ATT_KV_HEADS

    sinks = [jnp.concatenate([jnp.full((chunk, 1), sinks_ref[h * group + g], F32) for g in range(group)], axis=0)
             for h in range(ATT_KV_HEADS)]

    lane_m = lax.broadcasted_iota(jnp.int32, (m, LANES), 1)

    def attend():
        units = []
        for bi, c in [(bi, c) for bi in range(bb) for c in range(cpt)]:
            rows = _rows(bi * cpt + c, chunk)
            krows = _rows(c, chunk, m)
            kseg = knat_ref[bi, krows, :]
            vseg = vnat_ref[bi, krows, :]
            krot = pltpu.roll(kseg, HEAD_DIM, axis=1)
            vrot = pltpu.roll(vseg, HEAD_DIM, axis=1)
            valid = None
            if n_past_valid < WINDOW:
                key_t = lax.broadcasted_iota(jnp.int32, (1, m), 1)
                if phase_major:
                    key_t = (key_t // chunk) * chunk + _time_of_row(key_t % chunk)
                valid = t * tc + c * chunk - WINDOW + key_t >= -n_past_valid
            for h in range(ATT_KV_HEADS):
                kd = (jnp.where(lo_lane, kseg, krot) if h == 0 else jnp.where(lo_lane, krot, kseg)).astype(BF16)
                v_lo = jnp.where(lo_lane, vseg if h == 0 else vrot, jnp.where(lane_m == HEAD_DIM, 1.0, 0.0))
                v_hi = jnp.where(lo_lane, jnp.where(lane_m == 0, 1.0, 0.0), vrot if h == 0 else vseg)
                units.append(dict(rows=rows, h=h, valid=valid, kd=kd, v_lo=v_lo.astype(BF16),
                                  v_hi=v_hi.astype(BF16)))

        for un in units:
            rows, h = un["rows"], un["h"]
            cols = [q_ref[rows, (h * group // 2 + j) * LANES:(h * group // 2 + j + 1) * LANES]
                    for j in range(group // 2)]
            zero = jnp.zeros_like(cols[0])
            qstack = jnp.concatenate(
                [part for qc in cols for part in (jnp.where(lo_q, qc, zero), jnp.where(lo_q, zero, qc))], axis=0)
            sc = _dot_nt(qstack, un["kd"])
            un["sc"] = sc if un["valid"] is None else jnp.where(un["valid"], sc, NEG_INF)
        for un in units:
            un["mx"] = jnp.maximum(jnp.max(un["sc"], axis=-1, keepdims=True), sinks[un["h"]])
        for un in units:
            un["p"] = jnp.exp(un["sc"] - un["mx"]).astype(BF16)
            un["sink_p"] = jnp.exp(sinks[un["h"]] - un["mx"])
        for un in units:
            rows, h, p, sink_p = un["rows"], un["h"], un["p"], un["sink_p"]
            for j in range(group // 2):
                r_lo = slice((2 * j) * chunk, (2 * j + 1) * chunk)
                r_hi = slice((2 * j + 1) * chunk, (2 * j + 2) * chunk)
                o_lo = _dot(p[r_lo], un["v_lo"])
                o_hi = _dot(p[r_hi], un["v_hi"])
                den_lo = o_lo[:, HEAD_DIM:HEAD_DIM + 1] + sink_p[r_lo]
                den_hi = o_hi[:, 0:1] + sink_p[r_hi]
                o = jnp.where(lo_q, o_lo / den_lo, o_hi / den_hi)
                col = h * group // 2 + j
                gate = _silu(u_ref[rows, o_gb + col * LANES:o_gb + (col + 1) * LANES])
                mix_ref[rows, lw + col * LANES:lw + (col + 1) * LANES] = (o * gate).astype(BF16)

    attend()

    if n_tiles > 1:
        knat_ref[:, 0:WINDOW, :] = knat_ref[:, tc:tc + WINDOW, :]
        vnat_ref[:, 0:WINDOW, :] = vnat_ref[:, tc:tc + WINDOW, :]

    mixed = mix_ref[...]
    if phase_major:
        mixed = permute_rows(mixed)
    y = _dot(mixed, wout_ref[...])
    _post_residual(x_ref, mod_ref, npost_ref, y, y_ref)


def _runs_phase_major(chunk, batch_block, has_state):
    return chunk == SUBLANES * SUBLANES and batch_block == 1 and not has_state


def _conv_history_shape(phase_major, batch_block, tile, channels):
    if phase_major:
        return (CONV_W - 1, SUBLANES, channels)
    return (batch_block, CONV_PAD + tile, channels)


def _const_spec(shape):
    zeros = (0,) * len(shape)
    return pl.BlockSpec(shape, lambda b, t: zeros)


def _even_layer(x, mod, consts, state, *, tile, chunk, batch_block, n_past_valid, pos0, k_out_rows):
    bsz, seq, d = x.shape
    (npre, npost, win, cw, cb, wg, b_a, b_x, lam, inv_row, sinks, wout) = consts
    lw = cw.shape[1]
    kvw = ATT_KV_HEADS * HEAD_DIM
    aw = (win.shape[1] - 2 * lw - 2 * kvw) // 2
    n_tiles = seq // tile
    assert seq % tile == 0 and tile % chunk == 0 and bsz % batch_block == 0
    assert n_tiles == 1 or tile >= WINDOW
    rows = batch_block * tile
    has_state = state is not None
    phase_major = _runs_phase_major(chunk, batch_block, has_state)
    assert not phase_major or tile >= k_out_rows

    def bspec(shape):
        nd = len(shape)
        return pl.BlockSpec((batch_block,) + shape, lambda b, t: (b,) + (0,) * nd)

    in_specs = [pl.BlockSpec((batch_block, tile, d), lambda b, t: (b, t, 0))]
    args = [x]
    if phase_major:
        perm = _phase_major_matrix(min(rows, MXU_TILE), chunk)
        assert rows % perm.shape[0] == 0
        in_specs.append(_const_spec(perm.shape))
        args.append(perm)
    in_specs += [
        bspec((3, d)),
        _const_spec(npre.shape), _const_spec(npost.shape), _const_spec(win.shape),
        _const_spec(cw.shape), _const_spec(cb.shape), _const_spec(wg.shape),
        _const_spec(b_a.shape), _const_spec(b_x.shape), _const_spec(lam.shape), _const_spec(inv_row.shape),
        pl.BlockSpec(memory_space=pltpu.SMEM),
        _const_spec(wout.shape),
    ]
    args += [mod, npre, npost, win, cw, cb, wg, b_a, b_x, lam, inv_row, sinks, wout]
    if has_state:
        in_specs += [bspec((CONV_W - 1, lw)), bspec((1, lw)), bspec((WINDOW, kvw)), bspec((WINDOW, kvw))]
        args += list(state)
    out_specs = [
        pl.BlockSpec((batch_block, tile, d), lambda b, t: (b, t, 0)),
        bspec((CONV_W - 1, lw)), bspec((1, lw)), bspec((k_out_rows, kvw)), bspec((k_out_rows, kvw)),
    ]
    out_shape = [
        jax.ShapeDtypeStruct((bsz, seq, d), F32),
        jax.ShapeDtypeStruct((bsz, CONV_W - 1, lw), F32),
        jax.ShapeDtypeStruct((bsz, 1, lw), F32),
        jax.ShapeDtypeStruct((bsz, k_out_rows, kvw), F32),
        jax.ShapeDtypeStruct((bsz, k_out_rows, kvw), F32),
    ]
    scratch = [
        pltpu.VMEM((rows, win.shape[1]), F32),
        pltpu.VMEM(_conv_history_shape(phase_major, batch_block, tile, lw), F32),
        pltpu.VMEM((tile, lw), F32), pltpu.VMEM((tile, lw), F32), pltpu.VMEM((tile, lw), F32),
        pltpu.VMEM((batch_block, WINDOW + tile, kvw), F32),
        pltpu.VMEM((batch_block, WINDOW + tile, kvw), F32),
        pltpu.VMEM((rows, aw), BF16),
        pltpu.VMEM((rows, lw + aw), BF16),
    ]
    body = functools.partial(_even_body, has_state=has_state, n_past_valid=n_past_valid, pos0=pos0,
                             chunk=chunk, n_tiles=n_tiles, k_out_rows=k_out_rows, phase_major=phase_major)
    return pl.pallas_call(
        body,
        grid=(bsz // batch_block, n_tiles),
        in_specs=in_specs,
        out_specs=out_specs,
        out_shape=out_shape,
        scratch_shapes=scratch,
        compiler_params=pltpu.CompilerParams(
            dimension_semantics=("arbitrary", "arbitrary"), vmem_limit_bytes=VMEM_LIMIT_BYTES),
        name="even_layer_state" if has_state else "even_layer_prompt",
    )(*args)


def _gdn_groups(chunk, n_heads):
    hpg = max(1, min(n_heads, MXU_TILE // chunk))
    assert n_heads % hpg == 0
    return hpg, n_heads // hpg


def _time_of_row(i):
    return (i % SUBLANES) * SUBLANES + i // SUBLANES


def _phase_major_matrix(n_rows, chunk):
    i = jnp.arange(n_rows)
    src = (i // chunk) * chunk + _time_of_row(i % chunk)
    return (src[:, None] == i[None, :]).astype(BF16)


def _conv_phase_major(u_ref, hist_ref, nconv_ref, w_ref, tc, chunk, cch):
    cpt = tc // chunk
    blocks = [u_ref[SUBLANES * b:SUBLANES * (b + 1), 0:cch] for b in range(tc // SUBLANES)]
    taps_w = [w_ref[j:j + 1, :] for j in range(CONV_W)]
    top_row = lax.broadcasted_iota(jnp.int32, (SUBLANES, cch), 0) == 0
    first_late = SUBLANES - (CONV_W - 1)
    prev = {r: pltpu.roll(hist_ref[r - first_late], 1, axis=0) for r in range(first_late, SUBLANES)}
    out = []
    for c in range(cpt):
        late = {}
        for r in range(first_late, SUBLANES):
            moved = pltpu.roll(blocks[SUBLANES * c + r], 1, axis=0)
            late[r] = jnp.where(top_row, prev[r], moved)
            prev[r] = moved
        for r in range(SUBLANES):
            acc = None
            for j in range(CONV_W):
                s = CONV_W - 1 - j
                tap = blocks[SUBLANES * c + r - s] if r >= s else late[r - s + SUBLANES]
                acc = tap * taps_w[j] if acc is None else acc + tap * taps_w[j]
            out.append(acc)
    for i in range(CONV_W - 1):
        last = blocks[SUBLANES * (cpt - 1) + first_late + i]
        hist_ref[i] = last
        nconv_ref[0, i:i + 1, :] = last[SUBLANES - 1:SUBLANES, :]
    return jnp.concatenate(out, axis=0)


def _odd_body(*refs, has_state, chunk, phase_major):
    refs = list(refs)
    x_ref = refs.pop(0)
    perm_ref = refs.pop(0) if phase_major else None
    (mod_ref, npre_ref, npost_ref, win_ref, wba_ref, cw_ref, alog_ref, dtb_ref, hn_ref, wout_ref) = refs[:10]
    refs = refs[10:]
    if has_state:
        conv0_ref, s0_ref = refs[:2]
        refs = refs[2:]
    (y_ref, nconv_ref, ns_ref,
     u_ref, ext_ref, q_ref, k_ref, kb_ref, kbe_ref, vb_ref, qe_ref, kd_ref, beta_ref, gcum_ref,
     wq_ref, uu_ref, qk_ref, mix_ref) = refs
    bb, tc, _ = x_ref.shape
    n_heads = ns_ref.shape[1]
    kw = n_heads * GDN_DK
    vw = n_heads * GDN_DV
    cch = 2 * kw + vw
    rows_all = bb * tc
    t = pl.program_id(1)
    cpt = tc // chunk
    n_steps = (chunk - 1).bit_length()
    assert n_steps >= 2
    hpg, n_groups = _gdn_groups(chunk, n_heads)
    lw = hpg * chunk

    @pl.when(t == 0)
    def _():
        ext_ref[...] = jnp.zeros(ext_ref.shape, F32)
        if has_state:
            ext_ref[:, CONV_PAD - 3:CONV_PAD, :] = conv0_ref[...]
            ns_ref[...] = s0_ref[...]
        else:
            ns_ref[...] = jnp.zeros(ns_ref.shape, F32)

    def permute_rows(v):
        n = perm_ref.shape[0]
        return jnp.concatenate(
            [_dot(perm_ref[...], v[i * n:(i + 1) * n]).astype(BF16) for i in range(rows_all // n)], axis=0)

    hmod = _pre_norm(x_ref, mod_ref, npre_ref).astype(BF16)
    if phase_major:
        hmod = permute_rows(hmod)
    u_ref[...] = _dot(hmod, win_ref[...])
    ba = _dot(hmod, wba_ref[...])
    beta_ref[...] = jax.nn.sigmoid(ba)
    g = -jnp.exp(alog_ref[...]) * jax.nn.softplus(ba + dtb_ref[...])
    if phase_major:
        row8 = lax.broadcasted_iota(jnp.int32, (SUBLANES, LANES), 0)
        parts = []
        for c in range(rows_all // chunk):
            run = []
            for r in range(SUBLANES):
                blk = g[c * chunk + r * SUBLANES:c * chunk + (r + 1) * SUBLANES]
                run.append(blk if r == 0 else run[-1] + blk)
            total = run[-1]
            incl = total
            for sft in (1, 2, 4):
                incl = jnp.where(row8 >= sft, incl + pltpu.roll(incl, sft, axis=0), incl)
            parts += [blk + (incl - total) for blk in run]
        g = jnp.concatenate(parts, axis=0)
    else:
        row_in_chunk = lax.broadcasted_iota(jnp.int32, (rows_all, LANES), 0) % chunk
        sft = 1
        while sft < chunk:
            g = jnp.where(row_in_chunk >= sft, g + pltpu.roll(g, sft, axis=0), g)
            sft *= 2
    gcum_ref[...] = g

    def per_batch(bi):
        rows = _rows(bi, tc)
        if phase_major:
            qkv = _silu(_conv_phase_major(u_ref, ext_ref, nconv_ref, cw_ref, tc, chunk, cch))
        else:
            ext_ref[bi, CONV_PAD:CONV_PAD + tc, :] = u_ref[rows, 0:cch]
            qkv = _silu(_causal_conv(ext_ref, bi, tc, cw_ref))
            nconv_ref[bi, :, :] = ext_ref[bi, CONV_PAD + tc - 3:CONV_PAD + tc, :]
            ext_ref[bi, CONV_PAD - 3:CONV_PAD, :] = ext_ref[bi, CONV_PAD + tc - 3:CONV_PAD + tc, :]
        beta_b = beta_ref[rows, :]
        gc_b = gcum_ref[rows, :]
        eg_b = jnp.exp(gc_b)
        gc3 = gc_b.reshape(cpt, chunk, LANES)
        dec_b = jnp.exp(gc3[:, chunk - 1:chunk, :] - gc3).reshape(tc, LANES)
        for h in range(n_heads):
            hc = slice(h * GDN_DK, (h + 1) * GDN_DK)
            qh = qkv[:, hc]
            kh = qkv[:, kw + h * GDN_DK:kw + (h + 1) * GDN_DK]
            vh = qkv[:, 2 * kw + h * GDN_DV:2 * kw + (h + 1) * GDN_DV]
            qh = qh * lax.rsqrt(jnp.sum(qh * qh, axis=-1, keepdims=True) + EPS) * (GDN_DK ** -0.5)
            kh = kh * lax.rsqrt(jnp.sum(kh * kh, axis=-1, keepdims=True) + EPS)
            beta = beta_b[:, h:h + 1]
            eg = eg_b[:, n_heads + h:n_heads + h + 1]
            kb = kh * beta
            q_ref[rows, hc] = qh.astype(BF16)
            k_ref[rows, hc] = kh.astype(BF16)
            kb_ref[rows, hc] = kb.astype(BF16)
            kbe_ref[rows, hc] = (kb * eg).astype(BF16)
            vb_ref[rows, hc] = (vh * beta).astype(BF16)
            qe_ref[rows, hc] = (qh * eg).astype(BF16)
            kd_ref[rows, hc] = (kh * dec_b[:, n_heads + h:n_heads + h + 1]).astype(BF16)

    _for_each(bb, per_batch)

    lane_w = lax.broadcasted_iota(jnp.int32, (chunk, lw), 1)
    row_w = lax.broadcasted_iota(jnp.int32, (chunk, lw), 0)
    blk_w = lane_w // chunk
    col_w = lane_w % chunk
    diag_w = row_w == col_w
    if phase_major:
        row_w, col_w = _time_of_row(row_w), _time_of_row(col_w)
    incl_w = row_w >= col_w
    strict_w = row_w > col_w
    eye_w = jnp.where(diag_w, 1.0, 0.0).astype(F32)
    feat_blk = lax.broadcasted_iota(jnp.int32, (chunk, hpg * GDN_DK), 1) // GDN_DK

    def block_diag(m_b):
        return jnp.concatenate([jnp.where(blk_w == hh, m_b, jnp.zeros_like(m_b)) for hh in range(hpg)], axis=0)

    def head_cols(h):
        return slice(h * GDN_DK, (h + 1) * GDN_DK)

    def wy_factors():
        units = []
        for seg in range(bb * cpt):
            rows = _rows(seg, chunk)
            gt = gcum_ref[rows, :]
            for gi in range(n_groups):
                gcols = slice(gi * hpg * GDN_DK, (gi + 1) * hpg * GDN_DK)
                k4 = k_ref[rows, gcols]
                k_bd = jnp.concatenate(
                    [jnp.where(feat_blk == hh, k4, jnp.zeros_like(k4)) for hh in range(hpg)], axis=0)
                kk = _dot_nt(jnp.concatenate([kb_ref[rows, gcols], q_ref[rows, gcols]], axis=0), k_bd)
                gcol = jnp.zeros((chunk, lw), F32)
                for hh in range(hpg):
                    ln = n_heads + gi * hpg + hh
                    gcol = jnp.where(blk_w == hh, gt[:, ln:ln + 1], gcol)
                grow = jnp.sum(jnp.where(diag_w, gcol, 0.0), axis=0, keepdims=True)
                decay = jnp.exp(jnp.where(incl_w, gcol - grow, 0.0))
                p_b = (-(kk[:chunk] * jnp.where(strict_w, decay, 0.0))).astype(BF16)
                qk_ref[seg * n_groups + gi] = (kk[chunk:] * jnp.where(incl_w, decay, 0.0)).astype(BF16)
                units.append(dict(seg=seg, rows=rows, gi=gi, p=p_b, t=eye_w + p_b.astype(F32)))
        for un in units:
            un["p"] = _dot(un["p"], block_diag(un["p"])).astype(BF16)
        for _ in range(n_steps - 2):
            for un in units:
                out = _dot(jnp.concatenate([un["t"].astype(BF16), un["p"]], axis=0), block_diag(un["p"]))
                un["t"] = un["t"] + out[:chunk]
                un["p"] = out[chunk:].astype(BF16)
        for un in units:
            un["t"] = un["t"] + _dot(un["t"].astype(BF16), block_diag(un["p"]))
        for un in units:
            seg, rows, gi = un["seg"], un["rows"], un["gi"]
            heads = range(gi * hpg, (gi + 1) * hpg)
            rhs = jnp.concatenate(
                [jnp.concatenate([kbe_ref[rows, head_cols(h)], vb_ref[rows, head_cols(h)]], axis=1) for h in heads],
                axis=0)
            wu = _dot(block_diag(un["t"].astype(BF16)), rhs)
            for hh, h in enumerate(heads):
                blk = wu[hh * chunk:(hh + 1) * chunk]
                wq_ref[seg * n_heads + h] = jnp.concatenate(
                    [blk[:, :GDN_DK].astype(BF16), qe_ref[rows, head_cols(h)]], axis=0)
                uu_ref[rows, head_cols(h)] = blk[:, GDN_DK:]

    wy_factors()

    zero_b = jnp.zeros((chunk, GDN_DV), BF16)

    pairs = [(bi, h) for bi in range(bb) for h in range(n_heads)]
    state = {bh: ns_ref[bh[0], bh[1]] for bh in pairs}
    for c in range(cpt):
        seg_of = {bi: bi * cpt + c for bi in range(bb)}
        rows_of = {bi: _rows(seg_of[bi], chunk) for bi in range(bb)}
        ws_qs = {(bi, h): _dot(wq_ref[seg_of[bi] * n_heads + h], state[bi, h].astype(BF16)) for bi, h in pairs}
        v_new = {(bi, h): (uu_ref[rows_of[bi], head_cols(h)] - ws_qs[bi, h][:chunk]).astype(BF16)
                 for bi, h in pairs}
        eg_last = {bi: jnp.exp(gcum_ref[pl.ds(seg_of[bi] * chunk + chunk - 1, 1), :]) for bi in range(bb)}
        state = {(bi, h): state[bi, h] * eg_last[bi][:, n_heads + h:n_heads + h + 1]
                 + _dot_tn(kd_ref[rows_of[bi], head_cols(h)], v_new[bi, h]) for bi, h in pairs}
        for bi in range(bb):
            rows = rows_of[bi]
            for gi in range(n_groups):
                heads = range(gi * hpg, (gi + 1) * hpg)
                v_bd = jnp.concatenate(
                    [jnp.concatenate([v_new[bi, h] if j == hh else zero_b for j in range(hpg)], axis=1)
                     for hh, h in enumerate(heads)], axis=0)
                o = (jnp.concatenate([ws_qs[bi, h][chunk:] for h in heads], axis=1)
                     + _dot(qk_ref[seg_of[bi] * n_groups + gi], v_bd))
                for hh, h in enumerate(heads):
                    gate = _silu(u_ref[rows, cch + h * GDN_DV:cch + (h + 1) * GDN_DV])
                    o_h = _rms(o[:, hh * GDN_DV:(hh + 1) * GDN_DV], hn_ref[...])
                    mix_ref[rows, head_cols(h)] = (o_h * gate).astype(BF16)
    for bi, h in pairs:
        ns_ref[bi, h] = state[bi, h]

    mixed = mix_ref[...]
    if phase_major:
        mixed = permute_rows(mixed)
    y = _dot(mixed, wout_ref[...])
    _post_residual(x_ref, mod_ref, npost_ref, y, y_ref)


def _odd_layer(x, mod, consts, state, *, tile, chunk, batch_block):
    bsz, seq, d = x.shape
    (npre, npost, win, wba, cw, alog, dtb, hn, wout) = consts
    cch = cw.shape[1]
    vw = wout.shape[0]
    n_heads = vw // GDN_DV
    n_tiles = seq // tile
    assert seq % tile == 0 and tile % chunk == 0 and bsz % batch_block == 0
    rows = batch_block * tile
    has_state = state is not None
    phase_major = _runs_phase_major(chunk, batch_block, has_state)

    def bspec(shape):
        nd = len(shape)
        return pl.BlockSpec((batch_block,) + shape, lambda b, t: (b,) + (0,) * nd)

    in_specs = [pl.BlockSpec((batch_block, tile, d), lambda b, t: (b, t, 0))]
    args = [x]
    if phase_major:
        perm_rows = min(rows, MXU_TILE)
        assert rows % perm_rows == 0 and perm_rows % chunk == 0
        in_specs.append(_const_spec((perm_rows, perm_rows)))
        args.append(_phase_major_matrix(perm_rows, chunk))
    in_specs += [
        bspec((3, d)),
        _const_spec(npre.shape), _const_spec(npost.shape), _const_spec(win.shape), _const_spec(wba.shape),
        _const_spec(cw.shape), _const_spec(alog.shape), _const_spec(dtb.shape), _const_spec(hn.shape),
        _const_spec(wout.shape),
    ]
    args += [mod, npre, npost, win, wba, cw, alog, dtb, hn, wout]
    if has_state:
        in_specs += [bspec((CONV_W - 1, cch)), bspec((n_heads, GDN_DK, GDN_DV))]
        args += list(state)
    out_specs = [
        pl.BlockSpec((batch_block, tile, d), lambda b, t: (b, t, 0)),
        bspec((CONV_W - 1, cch)), bspec((n_heads, GDN_DK, GDN_DV)),
    ]
    out_shape = [
        jax.ShapeDtypeStruct((bsz, seq, d), F32),
        jax.ShapeDtypeStruct((bsz, CONV_W - 1, cch), F32),
        jax.ShapeDtypeStruct((bsz, n_heads, GDN_DK, GDN_DV), F32),
    ]
    hpg, n_groups = _gdn_groups(chunk, n_heads)
    n_seg = rows // chunk
    scratch = (
        [pltpu.VMEM((rows, win.shape[1]), F32),
         pltpu.VMEM(_conv_history_shape(phase_major, batch_block, tile, cch), F32)]
        + [pltpu.VMEM((rows, vw), BF16)] * 7
        + [pltpu.VMEM((rows, LANES), F32)] * 2
        + [pltpu.VMEM((n_seg * n_heads, 2 * chunk, GDN_DK), BF16),
           pltpu.VMEM((rows, vw), F32),
           pltpu.VMEM((n_seg * n_groups, chunk, hpg * chunk), BF16),
           pltpu.VMEM((rows, vw), BF16)]
    )
    body = functools.partial(_odd_body, has_state=has_state, chunk=chunk, phase_major=phase_major)
    return pl.pallas_call(
        body,
        grid=(bsz // batch_block, n_tiles),
        in_specs=in_specs,
        out_specs=out_specs,
        out_shape=out_shape,
        scratch_shapes=scratch,
        compiler_params=pltpu.CompilerParams(
            dimension_semantics=("arbitrary", "arbitrary"), vmem_limit_bytes=VMEM_LIMIT_BYTES),
        name="odd_layer_state" if has_state else "odd_layer_prompt",
    )(*args)


def _gate_weights(w_a, w_x):
    heads, blk, _ = w_a.shape
    per_tile = MXU_TILE // blk
    tiles = []
    for i in range(heads // per_tile):
        sl = slice(i * per_tile, (i + 1) * per_tile)
        tiles.append(jnp.concatenate(
            [jax.scipy.linalg.block_diag(*w_a[sl]), jax.scipy.linalg.block_diag(*w_x[sl])], axis=1))
    return jnp.stack(tiles).astype(BF16)


def _rope_inv_row():
    half = ROT_DIM // 2
    inv = ROPE_THETA ** (-(jnp.arange(half, dtype=F32) * 2.0 / ROT_DIM))
    per_head = jnp.concatenate([inv, inv, jnp.zeros((HEAD_DIM - ROT_DIM,), F32)])
    return jnp.tile(per_head, LANES // HEAD_DIM).reshape(1, LANES)


def kernel(x_prompt, x_sample, state_lru_conv, state_lru_h, cache_swa_k, cache_swa_v, state_gdn_conv, state_gdn_s, c_prompt, c_sample, ev_mod_w, ev_mod_b, ev_norm_pre, ev_norm_post, ev_w_in, lru_conv_w, lru_conv_b, lru_w_a, lru_b_a, lru_w_x, lru_b_x, lru_lambda, swa_sinks, ev_w_out, od_mod_w, od_mod_b, od_norm_pre, od_norm_post, od_w_in, gdn_conv_w, gdn_a_log, gdn_dt_bias, gdn_head_norm, od_w_out):
    bp, seq, d = x_prompt.shape
    bs, dec_seq, _ = x_sample.shape
    lw = lru_conv_w.shape[-1]
    kvw = ATT_KV_HEADS * HEAD_DIM
    n_heads = gdn_a_log.shape[-1]
    cch = gdn_conv_w.shape[-1]
    vw = od_w_out.shape[1]

    n_c = bp + bs
    n_pad = -n_c % 16
    c_all = jnp.concatenate([c_prompt, c_sample, jnp.zeros((n_pad, d), F32)], axis=0)
    mod_ev, mod_od = _modulation(c_all, ev_mod_w[0], ev_mod_b[0], od_mod_w[0], od_mod_b[0])
    mod_ev = mod_ev.reshape(n_c + n_pad, 3, d)
    mod_od = mod_od.reshape(n_c + n_pad, 3, d)

    row = lambda a: a.reshape(1, -1)
    ev_consts = (row(ev_norm_pre[0]), row(ev_norm_post[0]), ev_w_in[0].astype(BF16), lru_conv_w[0],
                 row(lru_conv_b[0]), _gate_weights(lru_w_a[0], lru_w_x[0]), row(lru_b_a[0]), row(lru_b_x[0]),
                 row(lru_lambda[0]), _rope_inv_row(), swa_sinks[0], ev_w_out[0].astype(BF16))
    w_in_od = od_w_in[0]
    pad_lanes = lambda a: jnp.pad(a, ((0, 0), (0, LANES - a.shape[1])))
    head_row = lambda a: jnp.pad(a.reshape(1, -1), ((0, 0), (n_heads, LANES - 2 * n_heads)))
    od_consts = (row(od_norm_pre[0]), row(od_norm_post[0]), w_in_od[:, :cch + vw].astype(BF16),
                 pad_lanes(w_in_od[:, cch + vw:]).astype(BF16), gdn_conv_w[0],
                 head_row(gdn_a_log[0]), head_row(gdn_dt_bias[0]), row(gdn_head_norm[0]),
                 od_w_out[0].astype(BF16))

    tile = min(PROMPT_TILE, seq)
    xp, lru_conv_p, lru_h_p, swa_k_p, swa_v_p = _even_layer(
        x_prompt, mod_ev[:bp], ev_consts, None, tile=tile, chunk=min(CHUNK, seq), batch_block=1,
        n_past_valid=0, pos0=0, k_out_rows=WINDOW)
    ev_state = (state_lru_conv[0], state_lru_h[0].reshape(bs, 1, lw),
                cache_swa_k[0].reshape(bs, WINDOW, kvw), cache_swa_v[0].reshape(bs, WINDOW, kvw))
    xs, lru_conv_s, lru_h_s, swa_k_s, swa_v_s = _even_layer(
        x_sample, mod_ev[bp:n_c], ev_consts, ev_state, tile=dec_seq, chunk=min(CHUNK, dec_seq), batch_block=bs,
        n_past_valid=WINDOW, pos0=PAST_LEN, k_out_rows=dec_seq)

    xp, gdn_conv_p, gdn_s_p = _odd_layer(
        xp, mod_od[:bp], od_consts, None, tile=tile, chunk=min(CHUNK, seq), batch_block=1)
    xs, gdn_conv_s, gdn_s_s = _odd_layer(
        xs, mod_od[bp:n_c], od_consts, (state_gdn_conv[0], state_gdn_s[0]),
        tile=dec_seq, chunk=min(CHUNK, dec_seq), batch_block=bs)

    kv_shape = lambda a: a.reshape(1, a.shape[0], a.shape[1], ATT_KV_HEADS, HEAD_DIM)
    return (xp, xs,
            lru_conv_p[None], lru_conv_s[None],
            lru_h_p.reshape(1, bp, lw), lru_h_s.reshape(1, bs, lw),
            kv_shape(swa_k_p), kv_shape(swa_k_s), kv_shape(swa_v_p), kv_shape(swa_v_s),
            gdn_conv_p[None], gdn_conv_s[None], gdn_s_p[None], gdn_s_s[None])
```

```python
import functools

import jax
import jax.numpy as jnp
from jax import lax
from jax.experimental import pallas as pl
from jax.experimental.pallas import tpu as pltpu

F32 = jnp.float32
BF16 = jnp.bfloat16

CHUNK = 64
EPS = 1e-6
CONV_W = 4
NEG_INF = -1e30
LRU_HEADS = 8
LRU_C = 8.0
HEAD_DIM = 64
ATT_KV_HEADS = 2
WINDOW = 128
ROT_DIM = HEAD_DIM // 4
ROPE_THETA = 500000.0
GDN_DK = 128
GDN_DV = 128
PAST_LEN = 4096

LANES = 128
SUBLANES = 8
MXU_TILE = 256
VMEM_LIMIT_BYTES = 56 * 1024 * 1024

PROMPT_TILE = 512
MOD_TILE = 512
CONV_PAD = SUBLANES


def _silu(x):
    return x * jax.nn.sigmoid(x)


def _expm1(x):
    u = jnp.exp(x)
    d = u - 1.0
    return jnp.where(u == 1.0, x, jnp.where(d == -1.0, -1.0, d * x / jnp.log(u)))


def _rms(x, g):
    return x * lax.rsqrt(jnp.mean(x * x, axis=-1, keepdims=True) + EPS) * g


def _dot(a, b):
    return jnp.dot(a, b, preferred_element_type=F32)


def _dot_nt(a, b):
    return lax.dot_general(a, b, (((1,), (1,)), ((), ())), preferred_element_type=F32)


def _dot_tn(a, b):
    return lax.dot_general(a, b, (((0,), (0,)), ((), ())), preferred_element_type=F32)


def _for_each(n, body):
    if n == 1:
        body(0)
    else:
        def step(i, carry):
            body(i)
            return carry
        lax.fori_loop(0, n, step, 0)


def _rows(i, n, size=None):
    size = n if size is None else size
    if isinstance(i, int):
        return pl.ds(i * n, size)
    return pl.ds(pl.multiple_of(i * n, n), size)


def _mod_body(c_ref, w0_ref, b0_ref, w1_ref, b1_ref, o0_ref, o1_ref):
    c = _silu(c_ref[...]).astype(BF16)
    o0_ref[...] = _dot(c, w0_ref[...].astype(BF16)) + b0_ref[...]
    o1_ref[...] = _dot(c, w1_ref[...].astype(BF16)) + b1_ref[...]


def _modulation(c_all, w0, b0, w1, b1):
    n, d = c_all.shape
    d3 = w0.shape[1]
    wspec = pl.BlockSpec((d, MOD_TILE), lambda j: (0, j))
    bspec = pl.BlockSpec((1, MOD_TILE), lambda j: (0, j))
    ospec = pl.BlockSpec((n, MOD_TILE), lambda j: (0, j))
    return pl.pallas_call(
        _mod_body,
        grid=(d3 // MOD_TILE,),
        in_specs=[pl.BlockSpec((n, d), lambda j: (0, 0)), wspec, bspec, wspec, bspec],
        out_specs=[ospec, ospec],
        out_shape=[jax.ShapeDtypeStruct((n, d3), F32)] * 2,
        name="adaln_modulation",
    )(c_all, w0, b0.reshape(1, d3), w1, b1.reshape(1, d3))


def _pre_norm(x_ref, mod_ref, norm_pre_ref):
    bb, tc, d = x_ref.shape
    x = x_ref[...]
    h = _rms(x, norm_pre_ref[...]) * (1.0 + mod_ref[:, 1:2, :]) + mod_ref[:, 0:1, :]
    return h.reshape(bb * tc, d)


def _post_residual(x_ref, mod_ref, norm_post_ref, y, o_ref, rows=None):
    bb, tc, d = x_ref.shape
    if rows is None:
        yn = _rms(y, norm_post_ref[...]).reshape(bb, tc, d)
        o_ref[...] = x_ref[...] + mod_ref[:, 2:3, :] * yn
    else:
        assert bb == 1
        o_ref[0, rows, :] = x_ref[0, rows, :] + mod_ref[0, 2:3, :] * _rms(y, norm_post_ref[...])


def _run_interleaved(stage_lists):
    tagged = [((i + 0.5) / len(stages), k, i, stage)
              for k, stages in enumerate(stage_lists) for i, stage in enumerate(stages)]
    for _, _, _, stage in sorted(tagged, key=lambda entry: entry[:3]):
        stage()


def _causal_conv(ext_ref, bi, tc, w_ref):
    ext = ext_ref[bi]
    acc = pltpu.roll(ext, CONV_W - 1, axis=0)[CONV_PAD:] * w_ref[0:1, :]
    for j in range(1, CONV_W - 1):
        acc = acc + pltpu.roll(ext, CONV_W - 1 - j, axis=0)[CONV_PAD:] * w_ref[j:j + 1, :]
    return acc + ext[CONV_PAD:] * w_ref[CONV_W - 1:CONV_W, :]


def _scan_phase_major(a, b, h0, chunk):
    tc, width = a.shape
    row8 = lax.broadcasted_iota(jnp.int32, (SUBLANES, width), 0)
    carry = h0
    out = []
    for c in range(tc // chunk):
        acc_a, acc_b = [], []
        for r in range(SUBLANES):
            blk = slice(c * chunk + r * SUBLANES, c * chunk + (r + 1) * SUBLANES)
            if r == 0:
                acc_a.append(a[blk])
                acc_b.append(b[blk])
            else:
                acc_b.append(a[blk] * acc_b[-1] + b[blk])
                acc_a.append(a[blk] * acc_a[-1])
        tot_a, tot_b = acc_a[-1], acc_b[-1]
        for sft in (1, 2, 4):
            keep = row8 >= sft
            tot_b = jnp.where(keep, tot_a * pltpu.roll(tot_b, sft, axis=0) + tot_b, tot_b)
            tot_a = jnp.where(keep, tot_a * pltpu.roll(tot_a, sft, axis=0), tot_a)
        h_out = tot_b + tot_a * carry
        h_in = jnp.where(row8 == 0, carry, pltpu.roll(h_out, 1, axis=0))
        out += [acc_b[r] + acc_a[r] * h_in for r in range(SUBLANES)]
        carry = h_out[SUBLANES - 1:SUBLANES, :]
    return jnp.concatenate(out, axis=0), carry


def _even_body(*refs, has_state, n_past_valid, pos0, chunk, n_tiles, k_out_rows, phase_major):
    refs = list(refs)
    x_ref = refs.pop(0)
    perm_ref = refs.pop(0) if phase_major else None
    (mod_ref, npre_ref, npost_ref, win_ref, cw_ref, cb_ref, wg_ref, ba_ref, bx_ref,
     lam_ref, inv_ref, sinks_ref, wout_ref) = refs[:13]
    refs = refs[13:]
    if has_state:
        conv0_ref, h0_ref, k0_ref, v0_ref = refs[:4]
        refs = refs[4:]
    (y_ref, nconv_ref, nh_ref, nk_ref, nv_ref,
     u_ref, ext_ref, a_ref, b_ref, hs_ref, knat_ref, vnat_ref, q_ref, mix_ref) = refs
    bb, tc, _ = x_ref.shape
    lw = a_ref.shape[1]
    aw = q_ref.shape[1]
    kvw = knat_ref.shape[2]
    t = pl.program_id(1)
    cpt = tc // chunk
    m = WINDOW + chunk
    o_q, o_k, o_v, o_gb = 2 * lw, 2 * lw + aw, 2 * lw + aw + kvw, 2 * lw + aw + 2 * kvw

    @pl.when(t == 0)
    def _():
        ext_ref[...] = jnp.zeros(ext_ref.shape, F32)
        if has_state:
            ext_ref[:, CONV_PAD - 3:CONV_PAD, :] = conv0_ref[...]
            nh_ref[...] = h0_ref[...]
            knat_ref[:, 0:WINDOW, :] = k0_ref[...]
            vnat_ref[:, 0:WINDOW, :] = v0_ref[...]
        else:
            nh_ref[...] = jnp.zeros(nh_ref.shape, F32)
            knat_ref[:, 0:WINDOW, :] = jnp.zeros((bb, WINDOW, kvw), F32)
            vnat_ref[:, 0:WINDOW, :] = jnp.zeros((bb, WINDOW, kvw), F32)

    def permute_rows(v):
        n = perm_ref.shape[0]
        return jnp.concatenate(
            [_dot(perm_ref[...], v[i * n:(i + 1) * n]).astype(BF16) for i in range(v.shape[0] // n)], axis=0)

    hmod_time = _pre_norm(x_ref, mod_ref, npre_ref).astype(BF16)
    hmod = permute_rows(hmod_time) if phase_major else hmod_time
    u_ref[...] = _dot(hmod, win_ref[...])

    def make_rope(time_in_tile):
        ang = (pos0 + t * tc + time_in_tile).astype(F32) * inv_ref[...]
        lane = lax.broadcasted_iota(jnp.int32, ang.shape, 1) % HEAD_DIM
        cos_t = jnp.cos(ang)
        sin_t = jnp.sin(ang)
        half = ROT_DIM // 2
        sin_a = jnp.where(lane < half, -sin_t, 0.0)
        sin_b = jnp.where(lane >= half, sin_t, 0.0)

        def rope(xcol):
            return (xcol * cos_t + pltpu.roll(xcol, LANES - half, axis=1) * sin_a
                    + pltpu.roll(xcol, half, axis=1) * sin_b)
        return rope

    row_t = lax.broadcasted_iota(jnp.int32, (tc, LANES), 0)
    if phase_major:
        row_t = (row_t // chunk) * chunk + _time_of_row(row_t % chunk)
    rope = make_rope(row_t)

    neg_c_softplus = -LRU_C * jax.nn.softplus(-lam_ref[...])
    row8 = lax.broadcasted_iota(jnp.int32, (SUBLANES, lw), 0)

    def per_batch(bi):
        rows = _rows(bi, tc)
        if phase_major:
            xc = _conv_phase_major(u_ref, ext_ref, nconv_ref, cw_ref, tc, chunk, lw) + cb_ref[...]
        else:
            ext_ref[bi, CONV_PAD:CONV_PAD + tc, :] = u_ref[rows, 0:lw]
            xc = _causal_conv(ext_ref, bi, tc, cw_ref) + cb_ref[...]
            nconv_ref[bi, :, :] = ext_ref[bi, CONV_PAD + tc - 3:CONV_PAD + tc, :]
            ext_ref[bi, CONV_PAD - 3:CONV_PAD, :] = ext_ref[bi, CONV_PAD + tc - 3:CONV_PAD + tc, :]
        xcb = xc.astype(BF16)
        halves = [_dot(xcb[:, i * MXU_TILE:(i + 1) * MXU_TILE], wg_ref[i]) for i in range(lw // MXU_TILE)]
        r = jax.nn.sigmoid(jnp.concatenate([g[:, :MXU_TILE] for g in halves], axis=1) + ba_ref[...])
        ig = jax.nn.sigmoid(jnp.concatenate([g[:, MXU_TILE:] for g in halves], axis=1) + bx_ref[...])
        log_a = r * neg_c_softplus
        a_all = jnp.exp(log_a)
        b_all = jnp.sqrt(-_expm1(2.0 * log_a)) * (ig * xc)
        if phase_major:
            h_all, nh_ref[bi, :, :] = _scan_phase_major(a_all, b_all, nh_ref[bi, :, :], chunk)
            hs_ref[...] = h_all
        else:
            a_ref[...] = a_all
            b_ref[...] = b_all

        def scan_block(j, hc):
            r0 = pl.multiple_of(j * SUBLANES, SUBLANES)
            a = a_ref[pl.ds(r0, SUBLANES), :]
            b = b_ref[pl.ds(r0, SUBLANES), :]
            for sft in (1, 2, 4):
                keep = row8 >= sft
                b = jnp.where(keep, a * pltpu.roll(b, sft, axis=0) + b, b)
                a = jnp.where(keep, a * pltpu.roll(a, sft, axis=0), a)
            h = a * hc + b
            hs_ref[pl.ds(r0, SUBLANES), :] = h
            return h[SUBLANES - 1:SUBLANES, :]

        if not phase_major:
            nh_ref[bi, :, :] = lax.fori_loop(0, tc // SUBLANES, scan_block, nh_ref[bi, :, :])
        mix_ref[rows, 0:lw] = (hs_ref[...] * _silu(u_ref[rows, lw:2 * lw])).astype(BF16)
        for j in range(aw // LANES):
            qcol = rope(u_ref[rows, o_q + j * LANES:o_q + (j + 1) * LANES])
            q_ref[rows, j * LANES:(j + 1) * LANES] = (qcol * (HEAD_DIM ** -0.5)).astype(BF16)
        knat_ref[bi, WINDOW:WINDOW + tc, :] = rope(u_ref[rows, o_k:o_k + kvw])
        vnat_ref[bi, WINDOW:WINDOW + tc, :] = u_ref[rows, o_v:o_v + kvw]
        if not phase_major:
            nk_ref[bi, :, :] = knat_ref[bi, WINDOW + tc - k_out_rows:WINDOW + tc, :]
            nv_ref[bi, :, :] = vnat_ref[bi, WINDOW + tc - k_out_rows:WINDOW + tc, :]

    _for_each(bb, per_batch)

    if phase_major:
        @pl.when(t == n_tiles - 1)
        def _():
            first = tc - k_out_rows
            kv = _dot(hmod_time[first:], win_ref[:, o_k:o_k + 2 * kvw])
            rope_last = make_rope(first + lax.broadcasted_iota(jnp.int32, (k_out_rows, LANES), 0))
            nk_ref[0] = rope_last(kv[:, 0:kvw])
            nv_ref[0] = kv[:, kvw:2 * kvw]

    lo_lane = lax.broadcasted_iota(jnp.int32, (m, LANES), 1) < HEAD_DIM
    lo_q = lax.broadcasted_iota(jnp.int32, (chunk, LANES), 1) < HEAD_DIM
    group = (aw // HEAD_DIM) // ATT_KV_HEADS

    sinks = [jnp.concatenate([jnp.full((chunk, 1), sinks_ref[h * group + g], F32) for g in range(group)], axis=0)
             for h in range(ATT_KV_HEADS)]

    lane_m = lax.broadcasted_iota(jnp.int32, (m, LANES), 1)

    def attend():
        units = []
        for bi, c in [(bi, c) for bi in range(bb) for c in range(cpt)]:
            rows = _rows(bi * cpt + c, chunk)
            krows = _rows(c, chunk, m)
            kseg = knat_ref[bi, krows, :]
            vseg = vnat_ref[bi, krows, :]
            krot = pltpu.roll(kseg, HEAD_DIM, axis=1)
            vrot = pltpu.roll(vseg, HEAD_DIM, axis=1)
            valid = None
            if n_past_valid < WINDOW:
                key_t = lax.broadcasted_iota(jnp.int32, (1, m), 1)
                if phase_major:
                    key_t = (key_t // chunk) * chunk + _time_of_row(key_t % chunk)
                valid = t * tc + c * chunk - WINDOW + key_t >= -n_past_valid
            for h in range(ATT_KV_HEADS):
                kd = (jnp.where(lo_lane, kseg, krot) if h == 0 else jnp.where(lo_lane, krot, kseg)).astype(BF16)
                v_lo = jnp.where(lo_lane, vseg if h == 0 else vrot, jnp.where(lane_m == HEAD_DIM, 1.0, 0.0))
                v_hi = jnp.where(lo_lane, jnp.where(lane_m == 0, 1.0, 0.0), vrot if h == 0 else vseg)
                units.append(dict(rows=rows, h=h, valid=valid, kd=kd, v_lo=v_lo.astype(BF16),
                                  v_hi=v_hi.astype(BF16)))

        for un in units:
            rows, h = un["rows"], un["h"]
            cols = [q_ref[rows, (h * group // 2 + j) * LANES:(h * group // 2 + j + 1) * LANES]
                    for j in range(group // 2)]
            zero = jnp.zeros_like(cols[0])
            qstack = jnp.concatenate(
                [part for qc in cols for part in (jnp.where(lo_q, qc, zero), jnp.where(lo_q, zero, qc))], axis=0)
            sc = _dot_nt(qstack, un["kd"])
            un["sc"] = sc if un["valid"] is None else jnp.where(un["valid"], sc, NEG_INF)
        for un in units:
            un["mx"] = jnp.maximum(jnp.max(un["sc"], axis=-1, keepdims=True), sinks[un["h"]])
        for un in units:
            un["p"] = jnp.exp(un["sc"] - un["mx"]).astype(BF16)
            un["sink_p"] = jnp.exp(sinks[un["h"]] - un["mx"])
        for un in units:
            rows, h, p, sink_p = un["rows"], un["h"], un["p"], un["sink_p"]
            for j in range(group // 2):
                r_lo = slice((2 * j) * chunk, (2 * j + 1) * chunk)
                r_hi = slice((2 * j + 1) * chunk, (2 * j + 2) * chunk)
                o_lo = _dot(p[r_lo], un["v_lo"])
                o_hi = _dot(p[r_hi], un["v_hi"])
                den_lo = o_lo[:, HEAD_DIM:HEAD_DIM + 1] + sink_p[r_lo]
                den_hi = o_hi[:, 0:1] + sink_p[r_hi]
                o = jnp.where(lo_q, o_lo / den_lo, o_hi / den_hi)
                col = h * group // 2 + j
                gate = _silu(u_ref[rows, o_gb + col * LANES:o_gb + (col + 1) * LANES])
                mix_ref[rows, lw + col * LANES:lw + (col + 1) * LANES] = (o * gate).astype(BF16)

    attend()

    if n_tiles > 1:
        knat_ref[:, 0:WINDOW, :] = knat_ref[:, tc:tc + WINDOW, :]
        vnat_ref[:, 0:WINDOW, :] = vnat_ref[:, tc:tc + WINDOW, :]

    mixed = mix_ref[...]
    if phase_major:
        mixed = permute_rows(mixed)
    y = _dot(mixed, wout_ref[...])
    _post_residual(x_ref, mod_ref, npost_ref, y, y_ref)


def _runs_phase_major(chunk, batch_block, has_state):
    return chunk == SUBLANES * SUBLANES and batch_block == 1 and not has_state


def _conv_history_shape(phase_major, batch_block, tile, channels):
    if phase_major:
        return (CONV_W - 1, SUBLANES, channels)
    return (batch_block, CONV_PAD + tile, channels)


def _const_spec(shape):
    zeros = (0,) * len(shape)
    return pl.BlockSpec(shape, lambda b, t: zeros)


def _even_layer(x, mod, consts, state, *, tile, chunk, batch_block, n_past_valid, pos0, k_out_rows):
    bsz, seq, d = x.shape
    (npre, npost, win, cw, cb, wg, b_a, b_x, lam, inv_row, sinks, wout) = consts
    lw = cw.shape[1]
    kvw = ATT_KV_HEADS * HEAD_DIM
    aw = (win.shape[1] - 2 * lw - 2 * kvw) // 2
    n_tiles = seq // tile
    assert seq % tile == 0 and tile % chunk == 0 and bsz % batch_block == 0
    assert n_tiles == 1 or tile >= WINDOW
    rows = batch_block * tile
    has_state = state is not None
    phase_major = _runs_phase_major(chunk, batch_block, has_state)
    assert not phase_major or tile >= k_out_rows

    def bspec(shape):
        nd = len(shape)
        return pl.BlockSpec((batch_block,) + shape, lambda b, t: (b,) + (0,) * nd)

    in_specs = [pl.BlockSpec((batch_block, tile, d), lambda b, t: (b, t, 0))]
    args = [x]
    if phase_major:
        perm = _phase_major_matrix(min(rows, MXU_TILE), chunk)
        assert rows % perm.shape[0] == 0
        in_specs.append(_const_spec(perm.shape))
        args.append(perm)
    in_specs += [
        bspec((3, d)),
        _const_spec(npre.shape), _const_spec(npost.shape), _const_spec(win.shape),
        _const_spec(cw.shape), _const_spec(cb.shape), _const_spec(wg.shape),
        _const_spec(b_a.shape), _const_spec(b_x.shape), _const_spec(lam.shape), _const_spec(inv_row.shape),
        pl.BlockSpec(memory_space=pltpu.SMEM),
        _const_spec(wout.shape),
    ]
    args += [mod, npre, npost, win, cw, cb, wg, b_a, b_x, lam, inv_row, sinks, wout]
    if has_state:
        in_specs += [bspec((CONV_W - 1, lw)), bspec((1, lw)), bspec((WINDOW, kvw)), bspec((WINDOW, kvw))]
        args += list(state)
    out_specs = [
        pl.BlockSpec((batch_block, tile, d), lambda b, t: (b, t, 0)),
        bspec((CONV_W - 1, lw)), bspec((1, lw)), bspec((k_out_rows, kvw)), bspec((k_out_rows, kvw)),
    ]
    out_shape = [
        jax.ShapeDtypeStruct((bsz, seq, d), F32),
        jax.ShapeDtypeStruct((bsz, CONV_W - 1, lw), F32),
        jax.ShapeDtypeStruct((bsz, 1, lw), F32),
        jax.ShapeDtypeStruct((bsz, k_out_rows, kvw), F32),
        jax.ShapeDtypeStruct((bsz, k_out_rows, kvw), F32),
    ]
    scratch = [
        pltpu.VMEM((rows, win.shape[1]), F32),
        pltpu.VMEM(_conv_history_shape(phase_major, batch_block, tile, lw), F32),
        pltpu.VMEM((tile, lw), F32), pltpu.VMEM((tile, lw), F32), pltpu.VMEM((tile, lw), F32),
        pltpu.VMEM((batch_block, WINDOW + tile, kvw), F32),
        pltpu.VMEM((batch_block, WINDOW + tile, kvw), F32),
        pltpu.VMEM((rows, aw), BF16),
        pltpu.VMEM((rows, lw + aw), BF16),
    ]
    body = functools.partial(_even_body, has_state=has_state, n_past_valid=n_past_valid, pos0=pos0,
                             chunk=chunk, n_tiles=n_tiles, k_out_rows=k_out_rows, phase_major=phase_major)
    return pl.pallas_call(
        body,
        grid=(bsz // batch_block, n_tiles),
        in_specs=in_specs,
        out_specs=out_specs,
        out_shape=out_shape,
        scratch_shapes=scratch,
        compiler_params=pltpu.CompilerParams(
            dimension_semantics=("arbitrary", "arbitrary"), vmem_limit_bytes=VMEM_LIMIT_BYTES),
        name="even_layer_state" if has_state else "even_layer_prompt",
    )(*args)


def _gdn_groups(chunk, n_heads):
    hpg = max(1, min(n_heads, MXU_TILE // chunk))
    assert n_heads % hpg == 0
    return hpg, n_heads // hpg


def _time_of_row(i):
    return (i % SUBLANES) * SUBLANES + i // SUBLANES


def _phase_major_matrix(n_rows, chunk):
    i = jnp.arange(n_rows)
    src = (i // chunk) * chunk + _time_of_row(i % chunk)
    return (src[:, None] == i[None, :]).astype(BF16)


def _conv_phase_major(u_ref, hist_ref, nconv_ref, w_ref, tc, chunk, cch):
    cpt = tc // chunk
    blocks = [u_ref[SUBLANES * b:SUBLANES * (b + 1), 0:cch] for b in range(tc // SUBLANES)]
    taps_w = [w_ref[j:j + 1, :] for j in range(CONV_W)]
    top_row = lax.broadcasted_iota(jnp.int32, (SUBLANES, cch), 0) == 0
    first_late = SUBLANES - (CONV_W - 1)
    prev = {r: pltpu.roll(hist_ref[r - first_late], 1, axis=0) for r in range(first_late, SUBLANES)}
    out = []
    for c in range(cpt):
        late = {}
        for r in range(first_late, SUBLANES):
            moved = pltpu.roll(blocks[SUBLANES * c + r], 1, axis=0)
            late[r] = jnp.where(top_row, prev[r], moved)
            prev[r] = moved
        for r in range(SUBLANES):
            acc = None
            for j in range(CONV_W):
                s = CONV_W - 1 - j
                tap = blocks[SUBLANES * c + r - s] if r >= s else late[r - s + SUBLANES]
                acc = tap * taps_w[j] if acc is None else acc + tap * taps_w[j]
            out.append(acc)
    for i in range(CONV_W - 1):
        last = blocks[SUBLANES * (cpt - 1) + first_late + i]
        hist_ref[i] = last
        nconv_ref[0, i:i + 1, :] = last[SUBLANES - 1:SUBLANES, :]
    return jnp.concatenate(out, axis=0)


def _odd_body(*refs, has_state, chunk, phase_major):
    refs = list(refs)
    x_ref = refs.pop(0)
    perm_ref = refs.pop(0) if phase_major else None
    (mod_ref, npre_ref, npost_ref, win_ref, wba_ref, cw_ref, alog_ref, dtb_ref, hn_ref, wout_ref) = refs[:10]
    refs = refs[10:]
    if has_state:
        conv0_ref, s0_ref = refs[:2]
        refs = refs[2:]
    (y_ref, nconv_ref, ns_ref,
     u_ref, ext_ref, q_ref, k_ref, kb_ref, kbe_ref, vb_ref, qe_ref, kd_ref, beta_ref, gcum_ref,
     wq_ref, uu_ref, qk_ref, mix_ref) = refs
    bb, tc, _ = x_ref.shape
    n_heads = ns_ref.shape[1]
    kw = n_heads * GDN_DK
    vw = n_heads * GDN_DV
    cch = 2 * kw + vw
    rows_all = bb * tc
    t = pl.program_id(1)
    cpt = tc // chunk
    n_steps = (chunk - 1).bit_length()
    assert n_steps >= 2
    hpg, n_groups = _gdn_groups(chunk, n_heads)
    lw = hpg * chunk

    @pl.when(t == 0)
    def _():
        ext_ref[...] = jnp.zeros(ext_ref.shape, F32)
        if has_state:
            ext_ref[:, CONV_PAD - 3:CONV_PAD, :] = conv0_ref[...]
            ns_ref[...] = s0_ref[...]
        else:
            ns_ref[...] = jnp.zeros(ns_ref.shape, F32)

    def permute_rows(v):
        n = perm_ref.shape[0]
        return jnp.concatenate(
            [_dot(perm_ref[...], v[i * n:(i + 1) * n]).astype(BF16) for i in range(v.shape[0] // n)], axis=0)

    hmod = _pre_norm(x_ref, mod_ref, npre_ref).astype(BF16)
    if phase_major:
        hmod = permute_rows(hmod)
    u_ref[...] = _dot(hmod, win_ref[...])
    ba = _dot(hmod, wba_ref[...])
    beta_ref[...] = jax.nn.sigmoid(ba)
    g = -jnp.exp(alog_ref[...]) * jax.nn.softplus(ba + dtb_ref[...])
    if phase_major:
        row8 = lax.broadcasted_iota(jnp.int32, (SUBLANES, LANES), 0)
        parts = []
        for c in range(rows_all // chunk):
            run = []
            for r in range(SUBLANES):
                blk = g[c * chunk + r * SUBLANES:c * chunk + (r + 1) * SUBLANES]
                run.append(blk if r == 0 else run[-1] + blk)
            total = run[-1]
            incl = total
            for sft in (1, 2, 4):
                incl = jnp.where(row8 >= sft, incl + pltpu.roll(incl, sft, axis=0), incl)
            parts += [blk + (incl - total) for blk in run]
        g = jnp.concatenate(parts, axis=0)
    else:
        row_in_chunk = lax.broadcasted_iota(jnp.int32, (rows_all, LANES), 0) % chunk
        sft = 1
        while sft < chunk:
            g = jnp.where(row_in_chunk >= sft, g + pltpu.roll(g, sft, axis=0), g)
            sft *= 2
    gcum_ref[...] = g

    def per_batch(bi):
        rows = _rows(bi, tc)
        if phase_major:
            qkv = _silu(_conv_phase_major(u_ref, ext_ref, nconv_ref, cw_ref, tc, chunk, cch))
        else:
            ext_ref[bi, CONV_PAD:CONV_PAD + tc, :] = u_ref[rows, 0:cch]
            qkv = _silu(_causal_conv(ext_ref, bi, tc, cw_ref))
            nconv_ref[bi, :, :] = ext_ref[bi, CONV_PAD + tc - 3:CONV_PAD + tc, :]
            ext_ref[bi, CONV_PAD - 3:CONV_PAD, :] = ext_ref[bi, CONV_PAD + tc - 3:CONV_PAD + tc, :]
        beta_b = beta_ref[rows, :]
        gc_b = gcum_ref[rows, :]
        eg_b = jnp.exp(gc_b)
        gc3 = gc_b.reshape(cpt, chunk, LANES)
        dec_b = jnp.exp(gc3[:, chunk - 1:chunk, :] - gc3).reshape(tc, LANES)
        for h in range(n_heads):
            hc = slice(h * GDN_DK, (h + 1) * GDN_DK)
            qh = qkv[:, hc]
            kh = qkv[:, kw + h * GDN_DK:kw + (h + 1) * GDN_DK]
            vh = qkv[:, 2 * kw + h * GDN_DV:2 * kw + (h + 1) * GDN_DV]
            qh = qh * lax.rsqrt(jnp.sum(qh * qh, axis=-1, keepdims=True) + EPS) * (GDN_DK ** -0.5)
            kh = kh * lax.rsqrt(jnp.sum(kh * kh, axis=-1, keepdims=True) + EPS)
            beta = beta_b[:, h:h + 1]
            eg = eg_b[:, n_heads + h:n_heads + h + 1]
            kb = kh * beta
            q_ref[rows, hc] = qh.astype(BF16)
            k_ref[rows, hc] = kh.astype(BF16)
            kb_ref[rows, hc] = kb.astype(BF16)
            kbe_ref[rows, hc] = (kb * eg).astype(BF16)
            vb_ref[rows, hc] = (vh * beta).astype(BF16)
            qe_ref[rows, hc] = (qh * eg).astype(BF16)
            kd_ref[rows, hc] = (kh * dec_b[:, n_heads + h:n_heads + h + 1]).astype(BF16)

    _for_each(bb, per_batch)

    lane_w = lax.broadcasted_iota(jnp.int32, (chunk, lw), 1)
    row_w = lax.broadcasted_iota(jnp.int32, (chunk, lw), 0)
    blk_w = lane_w // chunk
    col_w = lane_w % chunk
    diag_w = row_w == col_w
    if phase_major:
        row_w, col_w = _time_of_row(row_w), _time_of_row(col_w)
    incl_w = row_w >= col_w
    strict_w = row_w > col_w
    eye_w = jnp.where(diag_w, 1.0, 0.0).astype(F32)
    feat_blk = lax.broadcasted_iota(jnp.int32, (chunk, hpg * GDN_DK), 1) // GDN_DK

    def block_diag(m_b):
        return jnp.concatenate([jnp.where(blk_w == hh, m_b, jnp.zeros_like(m_b)) for hh in range(hpg)], axis=0)

    def head_cols(h):
        return slice(h * GDN_DK, (h + 1) * GDN_DK)

    def wy_stages(segs):
        units = []

        def build():
            for seg in segs:
                rows = _rows(seg, chunk)
                gt = gcum_ref[rows, :]
                for gi in range(n_groups):
                    gcols = slice(gi * hpg * GDN_DK, (gi + 1) * hpg * GDN_DK)
                    k4 = k_ref[rows, gcols]
                    k_bd = jnp.concatenate(
                        [jnp.where(feat_blk == hh, k4, jnp.zeros_like(k4)) for hh in range(hpg)], axis=0)
                    kk = _dot_nt(jnp.concatenate([kb_ref[rows, gcols], q_ref[rows, gcols]], axis=0), k_bd)
                    gcol = jnp.zeros((chunk, lw), F32)
                    for hh in range(hpg):
                        ln = n_heads + gi * hpg + hh
                        gcol = jnp.where(blk_w == hh, gt[:, ln:ln + 1], gcol)
                    grow = jnp.sum(jnp.where(diag_w, gcol, 0.0), axis=0, keepdims=True)
                    decay = jnp.exp(jnp.where(incl_w, gcol - grow, 0.0))
                    p_b = (-(kk[:chunk] * jnp.where(strict_w, decay, 0.0))).astype(BF16)
                    qk_ref[seg * n_groups + gi] = (kk[chunk:] * jnp.where(incl_w, decay, 0.0)).astype(BF16)
                    units.append(dict(seg=seg, rows=rows, gi=gi, p=p_b, t=eye_w + p_b.astype(F32)))

        def square():
            for un in units:
                un["p"] = _dot(un["p"], block_diag(un["p"])).astype(BF16)

        def extend_and_square():
            for un in units:
                out = _dot(jnp.concatenate([un["t"].astype(BF16), un["p"]], axis=0), block_diag(un["p"]))
                un["t"] = un["t"] + out[:chunk]
                un["p"] = out[chunk:].astype(BF16)

        def extend():
            for un in units:
                un["t"] = un["t"] + _dot(un["t"].astype(BF16), block_diag(un["p"]))

        def apply():
            for un in units:
                seg, rows, gi = un["seg"], un["rows"], un["gi"]
                heads = range(gi * hpg, (gi + 1) * hpg)
                rhs = jnp.concatenate(
                    [jnp.concatenate([kbe_ref[rows, head_cols(h)], vb_ref[rows, head_cols(h)]], axis=1)
                     for h in heads], axis=0)
                wu = _dot(block_diag(un["t"].astype(BF16)), rhs)
                for hh, h in enumerate(heads):
                    blk = wu[hh * chunk:(hh + 1) * chunk]
                    wq_ref[seg * n_heads + h] = jnp.concatenate(
                        [blk[:, :GDN_DK].astype(BF16), qe_ref[rows, head_cols(h)]], axis=0)
                    uu_ref[rows, head_cols(h)] = blk[:, GDN_DK:]

        return [build, square] + [extend_and_square] * (n_steps - 2) + [extend, apply]

    zero_b = jnp.zeros((chunk, GDN_DV), BF16)
    state = {(bi, h): ns_ref[bi, h] for bi in range(bb) for h in range(n_heads)}

    def recurrence_stages(segs):
        stages = []
        for c in sorted({seg % cpt for seg in segs}):
            wave = [seg for seg in segs if seg % cpt == c]
            pairs = [(seg, h) for seg in wave for h in range(n_heads)]
            held = {}

            def correct(wave=wave, pairs=pairs, held=held):
                for seg, h in pairs:
                    held["ws_qs", seg, h] = _dot(wq_ref[seg * n_heads + h], state[seg // cpt, h].astype(BF16))
                for seg, h in pairs:
                    held["v", seg, h] = (uu_ref[_rows(seg, chunk), head_cols(h)]
                                         - held["ws_qs", seg, h][:chunk]).astype(BF16)

            def advance(wave=wave, pairs=pairs, held=held):
                for seg in wave:
                    held["eg", seg] = jnp.exp(gcum_ref[pl.ds(seg * chunk + chunk - 1, 1), :])
                for seg, h in pairs:
                    state[seg // cpt, h] = (state[seg // cpt, h] * held["eg", seg][:, n_heads + h:n_heads + h + 1]
                                            + _dot_tn(kd_ref[_rows(seg, chunk), head_cols(h)], held["v", seg, h]))
                for seg in wave:
                    rows = _rows(seg, chunk)
                    for gi in range(n_groups):
                        heads = range(gi * hpg, (gi + 1) * hpg)
                        v_bd = jnp.concatenate(
                            [jnp.concatenate([held["v", seg, h] if j == hh else zero_b for j in range(hpg)], axis=1)
                             for hh, h in enumerate(heads)], axis=0)
                        o = (jnp.concatenate([held["ws_qs", seg, h][chunk:] for h in heads], axis=1)
                             + _dot(qk_ref[seg * n_groups + gi], v_bd))
                        for hh, h in enumerate(heads):
                            gate = _silu(u_ref[rows, cch + h * GDN_DV:cch + (h + 1) * GDN_DV])
                            o_h = _rms(o[:, hh * GDN_DV:(hh + 1) * GDN_DV], hn_ref[...])
                            mix_ref[rows, head_cols(h)] = (o_h * gate).astype(BF16)

            stages += [correct, advance]
        return stages

    def output_stages(rows):
        held = {}

        def gather():
            mixed = mix_ref[rows, :]
            held["mixed"] = permute_rows(mixed) if phase_major else mixed

        def project():
            held["y"] = _dot(held["mixed"], wout_ref[...])

        def finish():
            _post_residual(x_ref, mod_ref, npost_ref, held["y"], y_ref, rows if n_row_groups > 1 else None)

        return [gather, project, finish]

    group_rows = min(rows_all, MXU_TILE)
    n_row_groups = rows_all // group_rows
    assert rows_all % group_rows == 0 and group_rows % chunk == 0 and (bb == 1 or n_row_groups == 1)
    segs_of = [range(g * group_rows // chunk, (g + 1) * group_rows // chunk) for g in range(n_row_groups)]
    for step in range(n_row_groups + 2):
        lists = []
        if step < n_row_groups:
            lists.append(wy_stages(segs_of[step]))
        if 0 <= step - 1 < n_row_groups:
            lists.append(recurrence_stages(segs_of[step - 1]))
        if 0 <= step - 2 < n_row_groups:
            lists.append(output_stages(pl.ds((step - 2) * group_rows, group_rows)))
        _run_interleaved(lists)
    for bh, value in state.items():
        ns_ref[bh[0], bh[1]] = value


def _odd_layer(x, mod, consts, state, *, tile, chunk, batch_block):
    bsz, seq, d = x.shape
    (npre, npost, win, wba, cw, alog, dtb, hn, wout) = consts
    cch = cw.shape[1]
    vw = wout.shape[0]
    n_heads = vw // GDN_DV
    n_tiles = seq // tile
    assert seq % tile == 0 and tile % chunk == 0 and bsz % batch_block == 0
    rows = batch_block * tile
    has_state = state is not None
    phase_major = _runs_phase_major(chunk, batch_block, has_state)

    def bspec(shape):
        nd = len(shape)
        return pl.BlockSpec((batch_block,) + shape, lambda b, t: (b,) + (0,) * nd)

    in_specs = [pl.BlockSpec((batch_block, tile, d), lambda b, t: (b, t, 0))]
    args = [x]
    if phase_major:
        perm_rows = min(rows, MXU_TILE)
        assert rows % perm_rows == 0 and perm_rows % chunk == 0
        in_specs.append(_const_spec((perm_rows, perm_rows)))
        args.append(_phase_major_matrix(perm_rows, chunk))
    in_specs += [
        bspec((3, d)),
        _const_spec(npre.shape), _const_spec(npost.shape), _const_spec(win.shape), _const_spec(wba.shape),
        _const_spec(cw.shape), _const_spec(alog.shape), _const_spec(dtb.shape), _const_spec(hn.shape),
        _const_spec(wout.shape),
    ]
    args += [mod, npre, npost, win, wba, cw, alog, dtb, hn, wout]
    if has_state:
        in_specs += [bspec((CONV_W - 1, cch)), bspec((n_heads, GDN_DK, GDN_DV))]
        args += list(state)
    out_specs = [
        pl.BlockSpec((batch_block, tile, d), lambda b, t: (b, t, 0)),
        bspec((CONV_W - 1, cch)), bspec((n_heads, GDN_DK, GDN_DV)),
    ]
    out_shape = [
        jax.ShapeDtypeStruct((bsz, seq, d), F32),
        jax.ShapeDtypeStruct((bsz, CONV_W - 1, cch), F32),
        jax.ShapeDtypeStruct((bsz, n_heads, GDN_DK, GDN_DV), F32),
    ]
    hpg, n_groups = _gdn_groups(chunk, n_heads)
    n_seg = rows // chunk
    scratch = (
        [pltpu.VMEM((rows, win.shape[1]), F32),
         pltpu.VMEM(_conv_history_shape(phase_major, batch_block, tile, cch), F32)]
        + [pltpu.VMEM((rows, vw), BF16)] * 7
        + [pltpu.VMEM((rows, LANES), F32)] * 2
        + [pltpu.VMEM((n_seg * n_heads, 2 * chunk, GDN_DK), BF16),
           pltpu.VMEM((rows, vw), F32),
           pltpu.VMEM((n_seg * n_groups, chunk, hpg * chunk), BF16),
           pltpu.VMEM((rows, vw), BF16)]
    )
    body = functools.partial(_odd_body, has_state=has_state, chunk=chunk, phase_major=phase_major)
    return pl.pallas_call(
        body,
        grid=(bsz // batch_block, n_tiles),
        in_specs=in_specs,
        out_specs=out_specs,
        out_shape=out_shape,
        scratch_shapes=scratch,
        compiler_params=pltpu.CompilerParams(
            dimension_semantics=("arbitrary", "arbitrary"), vmem_limit_bytes=VMEM_LIMIT_BYTES),
        name="odd_layer_state" if has_state else "odd_layer_prompt",
    )(*args)


def _gate_weights(w_a, w_x):
    heads, blk, _ = w_a.shape
    per_tile = MXU_TILE // blk
    tiles = []
    for i in range(heads // per_tile):
        sl = slice(i * per_tile, (i + 1) * per_tile)
        tiles.append(jnp.concatenate(
            [jax.scipy.linalg.block_diag(*w_a[sl]), jax.scipy.linalg.block_diag(*w_x[sl])], axis=1))
    return jnp.stack(tiles).astype(BF16)


def _rope_inv_row():
    half = ROT_DIM // 2
    inv = ROPE_THETA ** (-(jnp.arange(half, dtype=F32) * 2.0 / ROT_DIM))
    per_head = jnp.concatenate([inv, inv, jnp.zeros((HEAD_DIM - ROT_DIM,), F32)])
    return jnp.tile(per_head, LANES // HEAD_DIM).reshape(1, LANES)


def kernel(x_prompt, x_sample, state_lru_conv, state_lru_h, cache_swa_k, cache_swa_v, state_gdn_conv, state_gdn_s, c_prompt, c_sample, ev_mod_w, ev_mod_b, ev_norm_pre, ev_norm_post, ev_w_in, lru_conv_w, lru_conv_b, lru_w_a, lru_b_a, lru_w_x, lru_b_x, lru_lambda, swa_sinks, ev_w_out, od_mod_w, od_mod_b, od_norm_pre, od_norm_post, od_w_in, gdn_conv_w, gdn_a_log, gdn_dt_bias, gdn_head_norm, od_w_out):
    bp, seq, d = x_prompt.shape
    bs, dec_seq, _ = x_sample.shape
    lw = lru_conv_w.shape[-1]
    kvw = ATT_KV_HEADS * HEAD_DIM
    n_heads = gdn_a_log.shape[-1]
    cch = gdn_conv_w.shape[-1]
    vw = od_w_out.shape[1]

    n_c = bp + bs
    n_pad = -n_c % 16
    c_all = jnp.concatenate([c_prompt, c_sample, jnp.zeros((n_pad, d), F32)], axis=0)
    mod_ev, mod_od = _modulation(c_all, ev_mod_w[0], ev_mod_b[0], od_mod_w[0], od_mod_b[0])
    mod_ev = mod_ev.reshape(n_c + n_pad, 3, d)
    mod_od = mod_od.reshape(n_c + n_pad, 3, d)

    row = lambda a: a.reshape(1, -1)
    ev_consts = (row(ev_norm_pre[0]), row(ev_norm_post[0]), ev_w_in[0].astype(BF16), lru_conv_w[0],
                 row(lru_conv_b[0]), _gate_weights(lru_w_a[0], lru_w_x[0]), row(lru_b_a[0]), row(lru_b_x[0]),
                 row(lru_lambda[0]), _rope_inv_row(), swa_sinks[0], ev_w_out[0].astype(BF16))
    w_in_od = od_w_in[0]
    pad_lanes = lambda a: jnp.pad(a, ((0, 0), (0, LANES - a.shape[1])))
    head_row = lambda a: jnp.pad(a.reshape(1, -1), ((0, 0), (n_heads, LANES - 2 * n_heads)))
    od_consts = (row(od_norm_pre[0]), row(od_norm_post[0]), w_in_od[:, :cch + vw].astype(BF16),
                 pad_lanes(w_in_od[:, cch + vw:]).astype(BF16), gdn_conv_w[0],
                 head_row(gdn_a_log[0]), head_row(gdn_dt_bias[0]), row(gdn_head_norm[0]),
                 od_w_out[0].astype(BF16))

    tile = min(PROMPT_TILE, seq)
    xp, lru_conv_p, lru_h_p, swa_k_p, swa_v_p = _even_layer(
        x_prompt, mod_ev[:bp], ev_consts, None, tile=tile, chunk=min(CHUNK, seq), batch_block=1,
        n_past_valid=0, pos0=0, k_out_rows=WINDOW)
    ev_state = (state_lru_conv[0], state_lru_h[0].reshape(bs, 1, lw),
                cache_swa_k[0].reshape(bs, WINDOW, kvw), cache_swa_v[0].reshape(bs, WINDOW, kvw))
    xs, lru_conv_s, lru_h_s, swa_k_s, swa_v_s = _even_layer(
        x_sample, mod_ev[bp:n_c], ev_consts, ev_state, tile=dec_seq, chunk=min(CHUNK, dec_seq), batch_block=bs,
        n_past_valid=WINDOW, pos0=PAST_LEN, k_out_rows=dec_seq)

    xp, gdn_conv_p, gdn_s_p = _odd_layer(
        xp, mod_od[:bp], od_consts, None, tile=tile, chunk=min(CHUNK, seq), batch_block=1)
    xs, gdn_conv_s, gdn_s_s = _odd_layer(
        xs, mod_od[bp:n_c], od_consts, (state_gdn_conv[0], state_gdn_s[0]),
        tile=dec_seq, chunk=min(CHUNK, dec_seq), batch_block=bs)

    kv_shape = lambda a: a.reshape(1, a.shape[0], a.shape[1], ATT_KV_HEADS, HEAD_DIM)
    return (xp, xs,
            lru_conv_p[None], lru_conv_s[None],
            lru_h_p.reshape(1, bp, lw), lru_h_s.reshape(1, bs, lw),
            kv_shape(swa_k_p), kv_shape(swa_k_s), kv_shape(swa_v_p), kv_shape(swa_v_s),
            gdn_conv_p[None], gdn_conv_s[None], gdn_s_p[None], gdn_s_s[None])
```

```python
import functools

import jax
import jax.numpy as jnp
from jax import lax
from jax.experimental import pallas as pl
from jax.experimental.pallas import tpu as pltpu

F32 = jnp.float32
BF16 = jnp.bfloat16

CHUNK = 64
EPS = 1e-6
CONV_W = 4
NEG_INF = -1e30
LRU_HEADS = 8
LRU_C = 8.0
HEAD_DIM = 64
ATT_KV_HEADS = 2
WINDOW = 128
ROT_DIM = HEAD_DIM // 4
ROPE_THETA = 500000.0
GDN_DK = 128
GDN_DV = 128
PAST_LEN = 4096

LANES = 128
SUBLANES = 8
MXU_TILE = 256
VMEM_LIMIT_BYTES = 56 * 1024 * 1024

PROMPT_TILE = 512
MOD_TILE = 512
CONV_PAD = SUBLANES


def _silu(x):
    return x * jax.nn.sigmoid(x)


def _expm1(x):
    u = jnp.exp(x)
    d = u - 1.0
    return jnp.where(u == 1.0, x, jnp.where(d == -1.0, -1.0, d * x / jnp.log(u)))


def _rms(x, g):
    return x * lax.rsqrt(jnp.mean(x * x, axis=-1, keepdims=True) + EPS) * g


def _dot(a, b):
    return jnp.dot(a, b, preferred_element_type=F32)


def _dot_nt(a, b):
    return lax.dot_general(a, b, (((1,), (1,)), ((), ())), preferred_element_type=F32)


def _dot_tn(a, b):
    return lax.dot_general(a, b, (((0,), (0,)), ((), ())), preferred_element_type=F32)


def _for_each(n, body):
    if n == 1:
        body(0)
    else:
        def step(i, carry):
            body(i)
            return carry
        lax.fori_loop(0, n, step, 0)


def _rows(i, n, size=None):
    size = n if size is None else size
    if isinstance(i, int):
        return pl.ds(i * n, size)
    return pl.ds(pl.multiple_of(i * n, n), size)


def _mod_body(c_ref, w0_ref, b0_ref, w1_ref, b1_ref, o0_ref, o1_ref):
    c = _silu(c_ref[...]).astype(BF16)
    o0_ref[...] = _dot(c, w0_ref[...].astype(BF16)) + b0_ref[...]
    o1_ref[...] = _dot(c, w1_ref[...].astype(BF16)) + b1_ref[...]


def _modulation(c_all, w0, b0, w1, b1):
    n, d = c_all.shape
    d3 = w0.shape[1]
    wspec = pl.BlockSpec((d, MOD_TILE), lambda j: (0, j))
    bspec = pl.BlockSpec((1, MOD_TILE), lambda j: (0, j))
    ospec = pl.BlockSpec((n, MOD_TILE), lambda j: (0, j))
    return pl.pallas_call(
        _mod_body,
        grid=(d3 // MOD_TILE,),
        in_specs=[pl.BlockSpec((n, d), lambda j: (0, 0)), wspec, bspec, wspec, bspec],
        out_specs=[ospec, ospec],
        out_shape=[jax.ShapeDtypeStruct((n, d3), F32)] * 2,
        name="adaln_modulation",
    )(c_all, w0, b0.reshape(1, d3), w1, b1.reshape(1, d3))


def _pre_norm(x_ref, mod_ref, norm_pre_ref):
    bb, tc, d = x_ref.shape
    x = x_ref[...]
    h = _rms(x, norm_pre_ref[...]) * (1.0 + mod_ref[:, 1:2, :]) + mod_ref[:, 0:1, :]
    return h.reshape(bb * tc, d)


def _post_residual(x_ref, mod_ref, norm_post_ref, y, o_ref, rows=None):
    bb, tc, d = x_ref.shape
    if rows is None:
        yn = _rms(y, norm_post_ref[...]).reshape(bb, tc, d)
        o_ref[...] = x_ref[...] + mod_ref[:, 2:3, :] * yn
    else:
        assert bb == 1
        o_ref[0, rows, :] = x_ref[0, rows, :] + mod_ref[0, 2:3, :] * _rms(y, norm_post_ref[...])


def _run_interleaved(stage_lists):
    tagged = [((i + 0.5) / len(stages), k, i, stage)
              for k, stages in enumerate(stage_lists) for i, stage in enumerate(stages)]
    for _, _, _, stage in sorted(tagged, key=lambda entry: entry[:3]):
        stage()


def _causal_conv(ext_ref, bi, tc, w_ref):
    ext = ext_ref[bi]
    acc = pltpu.roll(ext, CONV_W - 1, axis=0)[CONV_PAD:] * w_ref[0:1, :]
    for j in range(1, CONV_W - 1):
        acc = acc + pltpu.roll(ext, CONV_W - 1 - j, axis=0)[CONV_PAD:] * w_ref[j:j + 1, :]
    return acc + ext[CONV_PAD:] * w_ref[CONV_W - 1:CONV_W, :]


def _scan_phase_major(a, b, h0, chunk):
    tc, width = a.shape
    row8 = lax.broadcasted_iota(jnp.int32, (SUBLANES, width), 0)
    carry = h0
    out = []
    for c in range(tc // chunk):
        acc_a, acc_b = [], []
        for r in range(SUBLANES):
            blk = slice(c * chunk + r * SUBLANES, c * chunk + (r + 1) * SUBLANES)
            if r == 0:
                acc_a.append(a[blk])
                acc_b.append(b[blk])
            else:
                acc_b.append(a[blk] * acc_b[-1] + b[blk])
                acc_a.append(a[blk] * acc_a[-1])
        tot_a, tot_b = acc_a[-1], acc_b[-1]
        for sft in (1, 2, 4):
            keep = row8 >= sft
            tot_b = jnp.where(keep, tot_a * pltpu.roll(tot_b, sft, axis=0) + tot_b, tot_b)
            tot_a = jnp.where(keep, tot_a * pltpu.roll(tot_a, sft, axis=0), tot_a)
        h_out = tot_b + tot_a * carry
        h_in = jnp.where(row8 == 0, carry, pltpu.roll(h_out, 1, axis=0))
        out += [acc_b[r] + acc_a[r] * h_in for r in range(SUBLANES)]
        carry = h_out[SUBLANES - 1:SUBLANES, :]
    return jnp.concatenate(out, axis=0), carry


def _even_body(*refs, has_state, n_past_valid, pos0, chunk, n_tiles, k_out_rows, phase_major):
    refs = list(refs)
    x_ref = refs.pop(0)
    perm_ref = refs.pop(0) if phase_major else None
    (mod_ref, npre_ref, npost_ref, win_ref, cw_ref, cb_ref, wg_ref, ba_ref, bx_ref,
     lam_ref, inv_ref, sinks_ref, wout_ref) = refs[:13]
    refs = refs[13:]
    if has_state:
        conv0_ref, h0_ref, k0_ref, v0_ref = refs[:4]
        refs = refs[4:]
    (y_ref, nconv_ref, nh_ref, nk_ref, nv_ref,
     u_ref, ext_ref, a_ref, b_ref, hs_ref, knat_ref, vnat_ref, q_ref, mix_ref) = refs
    bb, tc, _ = x_ref.shape
    lw = a_ref.shape[1]
    aw = q_ref.shape[1]
    kvw = knat_ref.shape[2]
    t = pl.program_id(1)
    cpt = tc // chunk
    m = WINDOW + chunk
    o_q, o_k, o_v, o_gb = 2 * lw, 2 * lw + aw, 2 * lw + aw + kvw, 2 * lw + aw + 2 * kvw

    @pl.when(t == 0)
    def _():
        ext_ref[...] = jnp.zeros(ext_ref.shape, F32)
        if has_state:
            ext_ref[:, CONV_PAD - 3:CONV_PAD, :] = conv0_ref[...]
            nh_ref[...] = h0_ref[...]
            knat_ref[:, 0:WINDOW, :] = k0_ref[...]
            vnat_ref[:, 0:WINDOW, :] = v0_ref[...]
        else:
            nh_ref[...] = jnp.zeros(nh_ref.shape, F32)
            knat_ref[:, 0:WINDOW, :] = jnp.zeros((bb, WINDOW, kvw), F32)
            vnat_ref[:, 0:WINDOW, :] = jnp.zeros((bb, WINDOW, kvw), F32)

    def permute_rows(v):
        n = perm_ref.shape[0]
        return jnp.concatenate(
            [_dot(perm_ref[...], v[i * n:(i + 1) * n]).astype(BF16) for i in range(v.shape[0] // n)], axis=0)

    hmod_time = _pre_norm(x_ref, mod_ref, npre_ref).astype(BF16)
    hmod = permute_rows(hmod_time) if phase_major else hmod_time
    u_ref[...] = _dot(hmod, win_ref[...])

    def make_rope(time_in_tile):
        ang = (pos0 + t * tc + time_in_tile).astype(F32) * inv_ref[...]
        lane = lax.broadcasted_iota(jnp.int32, ang.shape, 1) % HEAD_DIM
        cos_t = jnp.cos(ang)
        sin_t = jnp.sin(ang)
        half = ROT_DIM // 2
        sin_a = jnp.where(lane < half, -sin_t, 0.0)
        sin_b = jnp.where(lane >= half, sin_t, 0.0)

        def rope(xcol):
            return (xcol * cos_t + pltpu.roll(xcol, LANES - half, axis=1) * sin_a
                    + pltpu.roll(xcol, half, axis=1) * sin_b)
        return rope

    row_t = lax.broadcasted_iota(jnp.int32, (tc, LANES), 0)
    if phase_major:
        row_t = (row_t // chunk) * chunk + _time_of_row(row_t % chunk)
    rope = make_rope(row_t)

    neg_c_softplus = -LRU_C * jax.nn.softplus(-lam_ref[...])
    row8 = lax.broadcasted_iota(jnp.int32, (SUBLANES, lw), 0)

    def per_batch(bi):
        rows = _rows(bi, tc)
        if phase_major:
            xc = _conv_phase_major(u_ref, ext_ref, nconv_ref, cw_ref, tc, chunk, lw) + cb_ref[...]
        else:
            ext_ref[bi, CONV_PAD:CONV_PAD + tc, :] = u_ref[rows, 0:lw]
            xc = _causal_conv(ext_ref, bi, tc, cw_ref) + cb_ref[...]
            nconv_ref[bi, :, :] = ext_ref[bi, CONV_PAD + tc - 3:CONV_PAD + tc, :]
            ext_ref[bi, CONV_PAD - 3:CONV_PAD, :] = ext_ref[bi, CONV_PAD + tc - 3:CONV_PAD + tc, :]
        xcb = xc.astype(BF16)
        halves = [_dot(xcb[:, i * MXU_TILE:(i + 1) * MXU_TILE], wg_ref[i]) for i in range(lw // MXU_TILE)]
        r = jax.nn.sigmoid(jnp.concatenate([g[:, :MXU_TILE] for g in halves], axis=1) + ba_ref[...])
        ig = jax.nn.sigmoid(jnp.concatenate([g[:, MXU_TILE:] for g in halves], axis=1) + bx_ref[...])
        log_a = r * neg_c_softplus
        a_all = jnp.exp(log_a)
        b_all = jnp.sqrt(-_expm1(2.0 * log_a)) * (ig * xc)
        if phase_major:
            h_all, nh_ref[bi, :, :] = _scan_phase_major(a_all, b_all, nh_ref[bi, :, :], chunk)
            hs_ref[...] = h_all
        else:
            a_ref[...] = a_all
            b_ref[...] = b_all

        def scan_block(j, hc):
            r0 = pl.multiple_of(j * SUBLANES, SUBLANES)
            a = a_ref[pl.ds(r0, SUBLANES), :]
            b = b_ref[pl.ds(r0, SUBLANES), :]
            for sft in (1, 2, 4):
                keep = row8 >= sft
                b = jnp.where(keep, a * pltpu.roll(b, sft, axis=0) + b, b)
                a = jnp.where(keep, a * pltpu.roll(a, sft, axis=0), a)
            h = a * hc + b
            hs_ref[pl.ds(r0, SUBLANES), :] = h
            return h[SUBLANES - 1:SUBLANES, :]

        if not phase_major:
            nh_ref[bi, :, :] = lax.fori_loop(0, tc // SUBLANES, scan_block, nh_ref[bi, :, :])
        mix_ref[rows, 0:lw] = (hs_ref[...] * _silu(u_ref[rows, lw:2 * lw])).astype(BF16)
        for j in range(aw // LANES):
            qcol = rope(u_ref[rows, o_q + j * LANES:o_q + (j + 1) * LANES])
            q_ref[rows, j * LANES:(j + 1) * LANES] = (qcol * (HEAD_DIM ** -0.5)).astype(BF16)
        knat_ref[bi, WINDOW:WINDOW + tc, :] = rope(u_ref[rows, o_k:o_k + kvw])
        vnat_ref[bi, WINDOW:WINDOW + tc, :] = u_ref[rows, o_v:o_v + kvw]
        if not phase_major:
            nk_ref[bi, :, :] = knat_ref[bi, WINDOW + tc - k_out_rows:WINDOW + tc, :]
            nv_ref[bi, :, :] = vnat_ref[bi, WINDOW + tc - k_out_rows:WINDOW + tc, :]

    _for_each(bb, per_batch)

    if phase_major:
        @pl.when(t == n_tiles - 1)
        def _():
            first = tc - k_out_rows
            kv = _dot(hmod_time[first:], win_ref[:, o_k:o_k + 2 * kvw])
            rope_last = make_rope(first + lax.broadcasted_iota(jnp.int32, (k_out_rows, LANES), 0))
            nk_ref[0] = rope_last(kv[:, 0:kvw])
            nv_ref[0] = kv[:, kvw:2 * kvw]

    lo_lane = lax.broadcasted_iota(jnp.int32, (m, LANES), 1) < HEAD_DIM
    lo_q = lax.broadcasted_iota(jnp.int32, (chunk, LANES), 1) < HEAD_DIM
    group = (aw // HEAD_DIM) // ATT_KV_HEADS

    sinks = [jnp.concatenate([jnp.full((chunk, 1), sinks_ref[h * group + g], F32) for g in range(group)], axis=0)
             for h in range(ATT_KV_HEADS)]

    lane_m = lax.broadcasted_iota(jnp.int32, (m, LANES), 1)

    def attend():
        units = []
        for bi, c in [(bi, c) for bi in range(bb) for c in range(cpt)]:
            rows = _rows(bi * cpt + c, chunk)
            krows = _rows(c, chunk, m)
            kseg = knat_ref[bi, krows, :]
            vseg = vnat_ref[bi, krows, :]
            krot = pltpu.roll(kseg, HEAD_DIM, axis=1)
            vrot = pltpu.roll(vseg, HEAD_DIM, axis=1)
            valid = None
            if n_past_valid < WINDOW:
                key_t = lax.broadcasted_iota(jnp.int32, (1, m), 1)
                if phase_major:
                    key_t = (key_t // chunk) * chunk + _time_of_row(key_t % chunk)
                valid = t * tc + c * chunk - WINDOW + key_t >= -n_past_valid
            for h in range(ATT_KV_HEADS):
                kd = (jnp.where(lo_lane, kseg, krot) if h == 0 else jnp.where(lo_lane, krot, kseg)).astype(BF16)
                v_lo = jnp.where(lo_lane, vseg if h == 0 else vrot, jnp.where(lane_m == HEAD_DIM, 1.0, 0.0))
                v_hi = jnp.where(lo_lane, jnp.where(lane_m == 0, 1.0, 0.0), vrot if h == 0 else vseg)
                units.append(dict(rows=rows, h=h, valid=valid, kd=kd, v_lo=v_lo.astype(BF16),
                                  v_hi=v_hi.astype(BF16)))

        for un in units:
            rows, h = un["rows"], un["h"]
            cols = [q_ref[rows, (h * group // 2 + j) * LANES:(h * group // 2 + j + 1) * LANES]
                    for j in range(group // 2)]
            zero = jnp.zeros_like(cols[0])
            qstack = jnp.concatenate(
                [part for qc in cols for part in (jnp.where(lo_q, qc, zero), jnp.where(lo_q, zero, qc))], axis=0)
            sc = _dot_nt(qstack, un["kd"])
            un["sc"] = sc if un["valid"] is None else jnp.where(un["valid"], sc, NEG_INF)
        for un in units:
            un["mx"] = jnp.maximum(jnp.max(un["sc"], axis=-1, keepdims=True), sinks[un["h"]])
        for un in units:
            un["p"] = jnp.exp(un["sc"] - un["mx"]).astype(BF16)
            un["sink_p"] = jnp.exp(sinks[un["h"]] - un["mx"])
        for un in units:
            rows, h, p, sink_p = un["rows"], un["h"], un["p"], un["sink_p"]
            for j in range(group // 2):
                r_lo = slice((2 * j) * chunk, (2 * j + 1) * chunk)
                r_hi = slice((2 * j + 1) * chunk, (2 * j + 2) * chunk)
                o_lo = _dot(p[r_lo], un["v_lo"])
                o_hi = _dot(p[r_hi], un["v_hi"])
                den_lo = o_lo[:, HEAD_DIM:HEAD_DIM + 1] + sink_p[r_lo]
                den_hi = o_hi[:, 0:1] + sink_p[r_hi]
                o = jnp.where(lo_q, o_lo / den_lo, o_hi / den_hi)
                col = h * group // 2 + j
                gate = _silu(u_ref[rows, o_gb + col * LANES:o_gb + (col + 1) * LANES])
                mix_ref[rows, lw + col * LANES:lw + (col + 1) * LANES] = (o * gate).astype(BF16)

    attend()

    if n_tiles > 1:
        knat_ref[:, 0:WINDOW, :] = knat_ref[:, tc:tc + WINDOW, :]
        vnat_ref[:, 0:WINDOW, :] = vnat_ref[:, tc:tc + WINDOW, :]

    mixed = mix_ref[...]
    if phase_major:
        mixed = permute_rows(mixed)
    y = _dot(mixed, wout_ref[...])
    _post_residual(x_ref, mod_ref, npost_ref, y, y_ref)


def _runs_phase_major(chunk, batch_block, has_state):
    return chunk == SUBLANES * SUBLANES and batch_block == 1 and not has_state


def _conv_history_shape(phase_major, batch_block, tile, channels):
    if phase_major:
        return (CONV_W - 1, SUBLANES, channels)
    return (batch_block, CONV_PAD + tile, channels)


def _const_spec(shape):
    zeros = (0,) * len(shape)
    return pl.BlockSpec(shape, lambda b, t: zeros)


def _even_layer(x, mod, consts, state, *, tile, chunk, batch_block, n_past_valid, pos0, k_out_rows):
    bsz, seq, d = x.shape
    (npre, npost, win, cw, cb, wg, b_a, b_x, lam, inv_row, sinks, wout) = consts
    lw = cw.shape[1]
    kvw = ATT_KV_HEADS * HEAD_DIM
    aw = (win.shape[1] - 2 * lw - 2 * kvw) // 2
    n_tiles = seq // tile
    assert seq % tile == 0 and tile % chunk == 0 and bsz % batch_block == 0
    assert n_tiles == 1 or tile >= WINDOW
    rows = batch_block * tile
    has_state = state is not None
    phase_major = _runs_phase_major(chunk, batch_block, has_state)
    assert not phase_major or tile >= k_out_rows

    def bspec(shape):
        nd = len(shape)
        return pl.BlockSpec((batch_block,) + shape, lambda b, t: (b,) + (0,) * nd)

    in_specs = [pl.BlockSpec((batch_block, tile, d), lambda b, t: (b, t, 0))]
    args = [x]
    if phase_major:
        perm = _phase_major_matrix(min(rows, MXU_TILE), chunk)
        assert rows % perm.shape[0] == 0
        in_specs.append(_const_spec(perm.shape))
        args.append(perm)
    in_specs += [
        bspec((3, d)),
        _const_spec(npre.shape), _const_spec(npost.shape), _const_spec(win.shape),
        _const_spec(cw.shape), _const_spec(cb.shape), _const_spec(wg.shape),
        _const_spec(b_a.shape), _const_spec(b_x.shape), _const_spec(lam.shape), _const_spec(inv_row.shape),
        pl.BlockSpec(memory_space=pltpu.SMEM),
        _const_spec(wout.shape),
    ]
    args += [mod, npre, npost, win, cw, cb, wg, b_a, b_x, lam, inv_row, sinks, wout]
    if has_state:
        in_specs += [bspec((CONV_W - 1, lw)), bspec((1, lw)), bspec((WINDOW, kvw)), bspec((WINDOW, kvw))]
        args += list(state)
    out_specs = [
        pl.BlockSpec((batch_block, tile, d), lambda b, t: (b, t, 0)),
        bspec((CONV_W - 1, lw)), bspec((1, lw)), bspec((k_out_rows, kvw)), bspec((k_out_rows, kvw)),
    ]
    out_shape = [
        jax.ShapeDtypeStruct((bsz, seq, d), F32),
        jax.ShapeDtypeStruct((bsz, CONV_W - 1, lw), F32),
        jax.ShapeDtypeStruct((bsz, 1, lw), F32),
        jax.ShapeDtypeStruct((bsz, k_out_rows, kvw), F32),
        jax.ShapeDtypeStruct((bsz, k_out_rows, kvw), F32),
    ]
    scratch = [
        pltpu.VMEM((rows, win.shape[1]), F32),
        pltpu.VMEM(_conv_history_shape(phase_major, batch_block, tile, lw), F32),
        pltpu.VMEM((tile, lw), F32), pltpu.VMEM((tile, lw), F32), pltpu.VMEM((tile, lw), F32),
        pltpu.VMEM((batch_block, WINDOW + tile, kvw), F32),
        pltpu.VMEM((batch_block, WINDOW + tile, kvw), F32),
        pltpu.VMEM((rows, aw), BF16),
        pltpu.VMEM((rows, lw + aw), BF16),
    ]
    body = functools.partial(_even_body, has_state=has_state, n_past_valid=n_past_valid, pos0=pos0,
                             chunk=chunk, n_tiles=n_tiles, k_out_rows=k_out_rows, phase_major=phase_major)
    return pl.pallas_call(
        body,
        grid=(bsz // batch_block, n_tiles),
        in_specs=in_specs,
        out_specs=out_specs,
        out_shape=out_shape,
        scratch_shapes=scratch,
        compiler_params=pltpu.CompilerParams(
            dimension_semantics=("arbitrary", "arbitrary"), vmem_limit_bytes=VMEM_LIMIT_BYTES),
        name="even_layer_state" if has_state else "even_layer_prompt",
    )(*args)


def _gdn_groups(chunk, n_heads):
    hpg = max(1, min(n_heads, MXU_TILE // chunk))
    assert n_heads % hpg == 0
    return hpg, n_heads // hpg


def _time_of_row(i):
    return (i % SUBLANES) * SUBLANES + i // SUBLANES


def _phase_major_matrix(n_rows, chunk):
    i = jnp.arange(n_rows)
    src = (i // chunk) * chunk + _time_of_row(i % chunk)
    return (src[:, None] == i[None, :]).astype(BF16)


def _conv_phase_major(u_ref, hist_ref, nconv_ref, w_ref, tc, chunk, cch):
    cpt = tc // chunk
    blocks = [u_ref[SUBLANES * b:SUBLANES * (b + 1), 0:cch] for b in range(tc // SUBLANES)]
    taps_w = [w_ref[j:j + 1, :] for j in range(CONV_W)]
    top_row = lax.broadcasted_iota(jnp.int32, (SUBLANES, cch), 0) == 0
    first_late = SUBLANES - (CONV_W - 1)
    prev = {r: pltpu.roll(hist_ref[r - first_late], 1, axis=0) for r in range(first_late, SUBLANES)}
    out = []
    for c in range(cpt):
        late = {}
        for r in range(first_late, SUBLANES):
            moved = pltpu.roll(blocks[SUBLANES * c + r], 1, axis=0)
            late[r] = jnp.where(top_row, prev[r], moved)
            prev[r] = moved
        for r in range(SUBLANES):
            acc = None
            for j in range(CONV_W):
                s = CONV_W - 1 - j
                tap = blocks[SUBLANES * c + r - s] if r >= s else late[r - s + SUBLANES]
                acc = tap * taps_w[j] if acc is None else acc + tap * taps_w[j]
            out.append(acc)
    for i in range(CONV_W - 1):
        last = blocks[SUBLANES * (cpt - 1) + first_late + i]
        hist_ref[i] = last
        nconv_ref[0, i:i + 1, :] = last[SUBLANES - 1:SUBLANES, :]
    return jnp.concatenate(out, axis=0)


def _odd_body(*refs, has_state, chunk, phase_major):
    refs = list(refs)
    x_ref = refs.pop(0)
    perm_ref = refs.pop(0) if phase_major else None
    (mod_ref, npre_ref, npost_ref, win_ref, wba_ref, cw_ref, alog_ref, dtb_ref, hn_ref, wout_ref) = refs[:10]
    refs = refs[10:]
    if has_state:
        conv0_ref, s0_ref = refs[:2]
        refs = refs[2:]
    (y_ref, nconv_ref, ns_ref,
     u_ref, ext_ref, q_ref, k_ref, kb_ref, kbe_ref, vb_ref, qe_ref, kd_ref, beta_ref, gcum_ref,
     wq_ref, uu_ref, qk_ref, mix_ref) = refs
    bb, tc, _ = x_ref.shape
    n_heads = ns_ref.shape[1]
    kw = n_heads * GDN_DK
    vw = n_heads * GDN_DV
    cch = 2 * kw + vw
    rows_all = bb * tc
    t = pl.program_id(1)
    cpt = tc // chunk
    n_steps = (chunk - 1).bit_length()
    assert n_steps >= 2
    hpg, n_groups = _gdn_groups(chunk, n_heads)
    lw = hpg * chunk

    @pl.when(t == 0)
    def _():
        ext_ref[...] = jnp.zeros(ext_ref.shape, F32)
        if has_state:
            ext_ref[:, CONV_PAD - 3:CONV_PAD, :] = conv0_ref[...]
            ns_ref[...] = s0_ref[...]
        else:
            ns_ref[...] = jnp.zeros(ns_ref.shape, F32)

    def permute_rows(v):
        n = perm_ref.shape[0]
        return jnp.concatenate(
            [_dot(perm_ref[...], v[i * n:(i + 1) * n]).astype(BF16) for i in range(v.shape[0] // n)], axis=0)

    hmod = _pre_norm(x_ref, mod_ref, npre_ref).astype(BF16)
    if phase_major:
        hmod = permute_rows(hmod)
    u_ref[...] = _dot(hmod, win_ref[...])
    ba = _dot(hmod, wba_ref[...])
    beta_ref[...] = jax.nn.sigmoid(ba)
    g = -jnp.exp(alog_ref[...]) * jax.nn.softplus(ba + dtb_ref[...])
    if phase_major:
        row8 = lax.broadcasted_iota(jnp.int32, (SUBLANES, LANES), 0)
        parts = []
        for c in range(rows_all // chunk):
            run = []
            for r in range(SUBLANES):
                blk = g[c * chunk + r * SUBLANES:c * chunk + (r + 1) * SUBLANES]
                run.append(blk if r == 0 else run[-1] + blk)
            total = run[-1]
            incl = total
            for sft in (1, 2, 4):
                incl = jnp.where(row8 >= sft, incl + pltpu.roll(incl, sft, axis=0), incl)
            parts += [blk + (incl - total) for blk in run]
        g = jnp.concatenate(parts, axis=0)
    else:
        row_in_chunk = lax.broadcasted_iota(jnp.int32, (rows_all, LANES), 0) % chunk
        sft = 1
        while sft < chunk:
            g = jnp.where(row_in_chunk >= sft, g + pltpu.roll(g, sft, axis=0), g)
            sft *= 2
    gcum_ref[...] = g

    def per_batch(bi):
        rows = _rows(bi, tc)
        if phase_major:
            qkv = _silu(_conv_phase_major(u_ref, ext_ref, nconv_ref, cw_ref, tc, chunk, cch))
        else:
            ext_ref[bi, CONV_PAD:CONV_PAD + tc, :] = u_ref[rows, 0:cch]
            qkv = _silu(_causal_conv(ext_ref, bi, tc, cw_ref))
            nconv_ref[bi, :, :] = ext_ref[bi, CONV_PAD + tc - 3:CONV_PAD + tc, :]
            ext_ref[bi, CONV_PAD - 3:CONV_PAD, :] = ext_ref[bi, CONV_PAD + tc - 3:CONV_PAD + tc, :]
        beta_b = beta_ref[rows, :]
        gc_b = gcum_ref[rows, :]
        eg_b = jnp.exp(gc_b)
        gc3 = gc_b.reshape(cpt, chunk, LANES)
        dec_b = jnp.exp(gc3[:, chunk - 1:chunk, :] - gc3).reshape(tc, LANES)
        for h in range(n_heads):
            hc = slice(h * GDN_DK, (h + 1) * GDN_DK)
            qh = qkv[:, hc]
            kh = qkv[:, kw + h * GDN_DK:kw + (h + 1) * GDN_DK]
            vh = qkv[:, 2 * kw + h * GDN_DV:2 * kw + (h + 1) * GDN_DV]
            qh = qh * lax.rsqrt(jnp.sum(qh * qh, axis=-1, keepdims=True) + EPS) * (GDN_DK ** -0.5)
            kh = kh * lax.rsqrt(jnp.sum(kh * kh, axis=-1, keepdims=True) + EPS)
            beta = beta_b[:, h:h + 1]
            eg = eg_b[:, n_heads + h:n_heads + h + 1]
            kb = kh * beta
            q_ref[rows, hc] = qh.astype(BF16)
            k_ref[rows, hc] = kh.astype(BF16)
            kb_ref[rows, hc] = kb.astype(BF16)
            kbe_ref[rows, hc] = (kb * eg).astype(BF16)
            vb_ref[rows, hc] = (vh * beta).astype(BF16)
            qe_ref[rows, hc] = (qh * eg).astype(BF16)
            kd_ref[rows, hc] = (kh * dec_b[:, n_heads + h:n_heads + h + 1]).astype(BF16)

    _for_each(bb, per_batch)

    lane_w = lax.broadcasted_iota(jnp.int32, (chunk, lw), 1)
    row_w = lax.broadcasted_iota(jnp.int32, (chunk, lw), 0)
    blk_w = lane_w // chunk
    col_w = lane_w % chunk
    diag_w = row_w == col_w
    if phase_major:
        row_w, col_w = _time_of_row(row_w), _time_of_row(col_w)
    incl_w = row_w >= col_w
    strict_w = row_w > col_w
    eye_w = jnp.where(diag_w, 1.0, 0.0).astype(F32)
    feat_blk = lax.broadcasted_iota(jnp.int32, (chunk, hpg * GDN_DK), 1) // GDN_DK

    def block_diag(m_b):
        return jnp.concatenate([jnp.where(blk_w == hh, m_b, jnp.zeros_like(m_b)) for hh in range(hpg)], axis=0)

    def head_cols(h):
        return slice(h * GDN_DK, (h + 1) * GDN_DK)

    def wy_stages(segs):
        units = []

        def build():
            for seg in segs:
                rows = _rows(seg, chunk)
                gt = gcum_ref[rows, :]
                for gi in range(n_groups):
                    gcols = slice(gi * hpg * GDN_DK, (gi + 1) * hpg * GDN_DK)
                    k4 = k_ref[rows, gcols]
                    k_bd = jnp.concatenate(
                        [jnp.where(feat_blk == hh, k4, jnp.zeros_like(k4)) for hh in range(hpg)], axis=0)
                    kk = _dot_nt(jnp.concatenate([kb_ref[rows, gcols], q_ref[rows, gcols]], axis=0), k_bd)
                    gcol = jnp.zeros((chunk, lw), F32)
                    for hh in range(hpg):
                        ln = n_heads + gi * hpg + hh
                        gcol = jnp.where(blk_w == hh, gt[:, ln:ln + 1], gcol)
                    grow = jnp.sum(jnp.where(diag_w, gcol, 0.0), axis=0, keepdims=True)
                    decay = jnp.exp(jnp.where(incl_w, gcol - grow, 0.0))
                    p_b = (-(kk[:chunk] * jnp.where(strict_w, decay, 0.0))).astype(BF16)
                    qk_ref[seg * n_groups + gi] = (kk[chunk:] * jnp.where(incl_w, decay, 0.0)).astype(BF16)
                    units.append(dict(seg=seg, rows=rows, gi=gi, p=p_b, t=eye_w + p_b.astype(F32)))

        def square():
            for un in units:
                un["p"] = _dot(un["p"], block_diag(un["p"])).astype(BF16)

        def extend_and_square():
            for un in units:
                out = _dot(jnp.concatenate([un["t"].astype(BF16), un["p"]], axis=0), block_diag(un["p"]))
                un["t"] = un["t"] + out[:chunk]
                un["p"] = out[chunk:].astype(BF16)

        def extend():
            for un in units:
                un["t"] = un["t"] + _dot(un["t"].astype(BF16), block_diag(un["p"]))

        def apply():
            for un in units:
                seg, rows, gi = un["seg"], un["rows"], un["gi"]
                heads = range(gi * hpg, (gi + 1) * hpg)
                rhs = jnp.concatenate(
                    [jnp.concatenate([kbe_ref[rows, head_cols(h)], vb_ref[rows, head_cols(h)]], axis=1)
                     for h in heads], axis=0)
                wu = _dot(block_diag(un["t"].astype(BF16)), rhs)
                for hh, h in enumerate(heads):
                    blk = wu[hh * chunk:(hh + 1) * chunk]
                    wq_ref[seg * n_heads + h] = jnp.concatenate(
                        [blk[:, :GDN_DK].astype(BF16), qe_ref[rows, head_cols(h)]], axis=0)
                    uu_ref[rows, head_cols(h)] = blk[:, GDN_DK:]

        return [build, square] + [extend_and_square] * (n_steps - 2) + [extend, apply]

    zero_b = jnp.zeros((chunk, GDN_DV), BF16)
    state = {(bi, h): ns_ref[bi, h] for bi in range(bb) for h in range(n_heads)}

    def recurrence_stages(segs):
        stages = []
        for c in sorted({seg % cpt for seg in segs}):
            wave = [seg for seg in segs if seg % cpt == c]
            pairs = [(seg, h) for seg in wave for h in range(n_heads)]
            held = {}

            def correct(wave=wave, pairs=pairs, held=held):
                for seg, h in pairs:
                    held["ws_qs", seg, h] = _dot(wq_ref[seg * n_heads + h], state[seg // cpt, h].astype(BF16))
                for seg, h in pairs:
                    held["v", seg, h] = (uu_ref[_rows(seg, chunk), head_cols(h)]
                                         - held["ws_qs", seg, h][:chunk]).astype(BF16)

            def advance(wave=wave, pairs=pairs, held=held):
                for seg in wave:
                    held["eg", seg] = jnp.exp(gcum_ref[pl.ds(seg * chunk + chunk - 1, 1), :])
                for seg, h in pairs:
                    state[seg // cpt, h] = (state[seg // cpt, h] * held["eg", seg][:, n_heads + h:n_heads + h + 1]
                                            + _dot_tn(kd_ref[_rows(seg, chunk), head_cols(h)], held["v", seg, h]))
                for seg in wave:
                    rows = _rows(seg, chunk)
                    for gi in range(n_groups):
                        heads = range(gi * hpg, (gi + 1) * hpg)
                        v_bd = jnp.concatenate(
                            [jnp.concatenate([held["v", seg, h] if j == hh else zero_b for j in range(hpg)], axis=1)
                             for hh, h in enumerate(heads)], axis=0)
                        o = (jnp.concatenate([held["ws_qs", seg, h][chunk:] for h in heads], axis=1)
                             + _dot(qk_ref[seg * n_groups + gi], v_bd))
                        for hh, h in enumerate(heads):
                            gate = _silu(u_ref[rows, cch + h * GDN_DV:cch + (h + 1) * GDN_DV])
                            o_h = _rms(o[:, hh * GDN_DV:(hh + 1) * GDN_DV], hn_ref[...])
                            mix_ref[rows, head_cols(h)] = (o_h * gate).astype(BF16)

            stages += [correct, advance]
        return stages

    def output_stages(rows):
        held = {}

        def gather():
            mixed = mix_ref[rows, :]
            held["mixed"] = permute_rows(mixed) if phase_major else mixed

        def project():
            held["y"] = _dot(held["mixed"], wout_ref[...])

        def finish():
            _post_residual(x_ref, mod_ref, npost_ref, held["y"], y_ref, rows if n_row_groups > 1 else None)

        return [gather, project, finish]

    group_rows = min(rows_all, MXU_TILE)
    n_row_groups = rows_all // group_rows
    assert rows_all % group_rows == 0 and group_rows % chunk == 0 and (bb == 1 or n_row_groups == 1)
    segs_of = [range(g * group_rows // chunk, (g + 1) * group_rows // chunk) for g in range(n_row_groups)]
    for phase in (wy_stages, recurrence_stages):
        for segs in segs_of:
            for stage in phase(segs):
                stage()
    for g in range(n_row_groups):
        for stage in output_stages(pl.ds(g * group_rows, group_rows)):
            stage()
    for bh, value in state.items():
        ns_ref[bh[0], bh[1]] = value


def _odd_layer(x, mod, consts, state, *, tile, chunk, batch_block):
    bsz, seq, d = x.shape
    (npre, npost, win, wba, cw, alog, dtb, hn, wout) = consts
    cch = cw.shape[1]
    vw = wout.shape[0]
    n_heads = vw // GDN_DV
    n_tiles = seq // tile
    assert seq % tile == 0 and tile % chunk == 0 and bsz % batch_block == 0
    rows = batch_block * tile
    has_state = state is not None
    phase_major = _runs_phase_major(chunk, batch_block, has_state)

    def bspec(shape):
        nd = len(shape)
        return pl.BlockSpec((batch_block,) + shape, lambda b, t: (b,) + (0,) * nd)

    in_specs = [pl.BlockSpec((batch_block, tile, d), lambda b, t: (b, t, 0))]
    args = [x]
    if phase_major:
        perm_rows = min(rows, MXU_TILE)
        assert rows % perm_rows == 0 and perm_rows % chunk == 0
        in_specs.append(_const_spec((perm_rows, perm_rows)))
        args.append(_phase_major_matrix(perm_rows, chunk))
    in_specs += [
        bspec((3, d)),
        _const_spec(npre.shape), _const_spec(npost.shape), _const_spec(win.shape), _const_spec(wba.shape),
        _const_spec(cw.shape), _const_spec(alog.shape), _const_spec(dtb.shape), _const_spec(hn.shape),
        _const_spec(wout.shape),
    ]
    args += [mod, npre, npost, win, wba, cw, alog, dtb, hn, wout]
    if has_state:
        in_specs += [bspec((CONV_W - 1, cch)), bspec((n_heads, GDN_DK, GDN_DV))]
        args += list(state)
    out_specs = [
        pl.BlockSpec((batch_block, tile, d), lambda b, t: (b, t, 0)),
        bspec((CONV_W - 1, cch)), bspec((n_heads, GDN_DK, GDN_DV)),
    ]
    out_shape = [
        jax.ShapeDtypeStruct((bsz, seq, d), F32),
        jax.ShapeDtypeStruct((bsz, CONV_W - 1, cch), F32),
        jax.ShapeDtypeStruct((bsz, n_heads, GDN_DK, GDN_DV), F32),
    ]
    hpg, n_groups = _gdn_groups(chunk, n_heads)
    n_seg = rows // chunk
    scratch = (
        [pltpu.VMEM((rows, win.shape[1]), F32),
         pltpu.VMEM(_conv_history_shape(phase_major, batch_block, tile, cch), F32)]
        + [pltpu.VMEM((rows, vw), BF16)] * 7
        + [pltpu.VMEM((rows, LANES), F32)] * 2
        + [pltpu.VMEM((n_seg * n_heads, 2 * chunk, GDN_DK), BF16),
           pltpu.VMEM((rows, vw), F32),
           pltpu.VMEM((n_seg * n_groups, chunk, hpg * chunk), BF16),
           pltpu.VMEM((rows, vw), BF16)]
    )
    body = functools.partial(_odd_body, has_state=has_state, chunk=chunk, phase_major=phase_major)
    return pl.pallas_call(
        body,
        grid=(bsz // batch_block, n_tiles),
        in_specs=in_specs,
        out_specs=out_specs,
        out_shape=out_shape,
        scratch_shapes=scratch,
        compiler_params=pltpu.CompilerParams(
            dimension_semantics=("arbitrary", "arbitrary"), vmem_limit_bytes=VMEM_LIMIT_BYTES),
        name="odd_layer_state" if has_state else "odd_layer_prompt",
    )(*args)


def _gate_weights(w_a, w_x):
    heads, blk, _ = w_a.shape
    per_tile = MXU_TILE // blk
    tiles = []
    for i in range(heads // per_tile):
        sl = slice(i * per_tile, (i + 1) * per_tile)
        tiles.append(jnp.concatenate(
            [jax.scipy.linalg.block_diag(*w_a[sl]), jax.scipy.linalg.block_diag(*w_x[sl])], axis=1))
    return jnp.stack(tiles).astype(BF16)


def _rope_inv_row():
    half = ROT_DIM // 2
    inv = ROPE_THETA ** (-(jnp.arange(half, dtype=F32) * 2.0 / ROT_DIM))
    per_head = jnp.concatenate([inv, inv, jnp.zeros((HEAD_DIM - ROT_DIM,), F32)])
    return jnp.tile(per_head, LANES // HEAD_DIM).reshape(1, LANES)


def kernel(x_prompt, x_sample, state_lru_conv, state_lru_h, cache_swa_k, cache_swa_v, state_gdn_conv, state_gdn_s, c_prompt, c_sample, ev_mod_w, ev_mod_b, ev_norm_pre, ev_norm_post, ev_w_in, lru_conv_w, lru_conv_b, lru_w_a, lru_b_a, lru_w_x, lru_b_x, lru_lambda, swa_sinks, ev_w_out, od_mod_w, od_mod_b, od_norm_pre, od_norm_post, od_w_in, gdn_conv_w, gdn_a_log, gdn_dt_bias, gdn_head_norm, od_w_out):
    bp, seq, d = x_prompt.shape
    bs, dec_seq, _ = x_sample.shape
    lw = lru_conv_w.shape[-1]
    kvw = ATT_KV_HEADS * HEAD_DIM
    n_heads = gdn_a_log.shape[-1]
    cch = gdn_conv_w.shape[-1]
    vw = od_w_out.shape[1]

    n_c = bp + bs
    n_pad = -n_c % 16
    c_all = jnp.concatenate([c_prompt, c_sample, jnp.zeros((n_pad, d), F32)], axis=0)
    mod_ev, mod_od = _modulation(c_all, ev_mod_w[0], ev_mod_b[0], od_mod_w[0], od_mod_b[0])
    mod_ev = mod_ev.reshape(n_c + n_pad, 3, d)
    mod_od = mod_od.reshape(n_c + n_pad, 3, d)

    row = lambda a: a.reshape(1, -1)
    ev_consts = (row(ev_norm_pre[0]), row(ev_norm_post[0]), ev_w_in[0].astype(BF16), lru_conv_w[0],
                 row(lru_conv_b[0]), _gate_weights(lru_w_a[0], lru_w_x[0]), row(lru_b_a[0]), row(lru_b_x[0]),
                 row(lru_lambda[0]), _rope_inv_row(), swa_sinks[0], ev_w_out[0].astype(BF16))
    w_in_od = od_w_in[0]
    pad_lanes = lambda a: jnp.pad(a, ((0, 0), (0, LANES - a.shape[1])))
    head_row = lambda a: jnp.pad(a.reshape(1, -1), ((0, 0), (n_heads, LANES - 2 * n_heads)))
    od_consts = (row(od_norm_pre[0]), row(od_norm_post[0]), w_in_od[:, :cch + vw].astype(BF16),
                 pad_lanes(w_in_od[:, cch + vw:]).astype(BF16), gdn_conv_w[0],
                 head_row(gdn_a_log[0]), head_row(gdn_dt_bias[0]), row(gdn_head_norm[0]),
                 od_w_out[0].astype(BF16))

    tile = min(PROMPT_TILE, seq)
    xp, lru_conv_p, lru_h_p, swa_k_p, swa_v_p = _even_layer(
        x_prompt, mod_ev[:bp], ev_consts, None, tile=tile, chunk=min(CHUNK, seq), batch_block=1,
        n_past_valid=0, pos0=0, k_out_rows=WINDOW)
    ev_state = (state_lru_conv[0], state_lru_h[0].reshape(bs, 1, lw),
                cache_swa_k[0].reshape(bs, WINDOW, kvw), cache_swa_v[0].reshape(bs, WINDOW, kvw))
    xs, lru_conv_s, lru_h_s, swa_k_s, swa_v_s = _even_layer(
        x_sample, mod_ev[bp:n_c], ev_consts, ev_state, tile=dec_seq, chunk=min(CHUNK, dec_seq), batch_block=bs,
        n_past_valid=WINDOW, pos0=PAST_LEN, k_out_rows=dec_seq)

    xp, gdn_conv_p, gdn_s_p = _odd_layer(
        xp, mod_od[:bp], od_consts, None, tile=tile, chunk=min(CHUNK, seq), batch_block=1)
    xs, gdn_conv_s, gdn_s_s = _odd_layer(
        xs, mod_od[bp:n_c], od_consts, (state_gdn_conv[0], state_gdn_s[0]),
        tile=dec_seq, chunk=min(CHUNK, dec_seq), batch_block=bs)

    kv_shape = lambda a: a.reshape(1, a.shape[0], a.shape[1], ATT_KV_HEADS, HEAD_DIM)
    return (xp, xs,
            lru_conv_p[None], lru_conv_s[None],
            lru_h_p.reshape(1, bp, lw), lru_h_s.reshape(1, bs, lw),
            kv_shape(swa_k_p), kv_shape(swa_k_s), kv_shape(swa_v_p), kv_shape(swa_v_s),
            gdn_conv_p[None], gdn_conv_s[None], gdn_s_p[None], gdn_s_s[None])
```

```python
import functools

import jax
import jax.numpy as jnp
from jax import lax
from jax.experimental import pallas as pl
from jax.experimental.pallas import tpu as pltpu

F32 = jnp.float32
BF16 = jnp.bfloat16

CHUNK = 64
EPS = 1e-6
CONV_W = 4
NEG_INF = -1e30
LRU_HEADS = 8
LRU_C = 8.0
HEAD_DIM = 64
ATT_KV_HEADS = 2
WINDOW = 128
ROT_DIM = HEAD_DIM // 4
ROPE_THETA = 500000.0
GDN_DK = 128
GDN_DV = 128
PAST_LEN = 4096

LANES = 128
SUBLANES = 8
MXU_TILE = 256
VMEM_LIMIT_BYTES = 56 * 1024 * 1024

PROMPT_TILE = 512
MOD_TILE = 512
CONV_PAD = SUBLANES


def _silu(x):
    return x * jax.nn.sigmoid(x)


def _expm1(x):
    u = jnp.exp(x)
    d = u - 1.0
    return jnp.where(u == 1.0, x, jnp.where(d == -1.0, -1.0, d * x / jnp.log(u)))


def _rms(x, g):
    return x * lax.rsqrt(jnp.mean(x * x, axis=-1, keepdims=True) + EPS) * g


def _dot(a, b):
    return jnp.dot(a, b, preferred_element_type=F32)


def _dot_nt(a, b):
    return lax.dot_general(a, b, (((1,), (1,)), ((), ())), preferred_element_type=F32)


def _dot_tn(a, b):
    return lax.dot_general(a, b, (((0,), (0,)), ((), ())), preferred_element_type=F32)


def _for_each(n, body):
    if n == 1:
        body(0)
    else:
        def step(i, carry):
            body(i)
            return carry
        lax.fori_loop(0, n, step, 0)


def _rows(i, n, size=None):
    size = n if size is None else size
    if isinstance(i, int):
        return pl.ds(i * n, size)
    return pl.ds(pl.multiple_of(i * n, n), size)


def _mod_body(c_ref, w0_ref, b0_ref, w1_ref, b1_ref, o0_ref, o1_ref):
    c = _silu(c_ref[...]).astype(BF16)
    o0_ref[...] = _dot(c, w0_ref[...].astype(BF16)) + b0_ref[...]
    o1_ref[...] = _dot(c, w1_ref[...].astype(BF16)) + b1_ref[...]


def _modulation(c_all, w0, b0, w1, b1):
    n, d = c_all.shape
    d3 = w0.shape[1]
    wspec = pl.BlockSpec((d, MOD_TILE), lambda j: (0, j))
    bspec = pl.BlockSpec((1, MOD_TILE), lambda j: (0, j))
    ospec = pl.BlockSpec((n, MOD_TILE), lambda j: (0, j))
    return pl.pallas_call(
        _mod_body,
        grid=(d3 // MOD_TILE,),
        in_specs=[pl.BlockSpec((n, d), lambda j: (0, 0)), wspec, bspec, wspec, bspec],
        out_specs=[ospec, ospec],
        out_shape=[jax.ShapeDtypeStruct((n, d3), F32)] * 2,
        name="adaln_modulation",
    )(c_all, w0, b0.reshape(1, d3), w1, b1.reshape(1, d3))


def _pre_norm(x_ref, mod_ref, norm_pre_ref):
    bb, tc, d = x_ref.shape
    x = x_ref[...]
    h = _rms(x, norm_pre_ref[...]) * (1.0 + mod_ref[:, 1:2, :]) + mod_ref[:, 0:1, :]
    return h.reshape(bb * tc, d)


def _post_residual(x_ref, mod_ref, norm_post_ref, y, o_ref, rows=None):
    bb, tc, d = x_ref.shape
    if rows is None:
        yn = _rms(y, norm_post_ref[...]).reshape(bb, tc, d)
        o_ref[...] = x_ref[...] + mod_ref[:, 2:3, :] * yn
    else:
        assert bb == 1
        o_ref[0, rows, :] = x_ref[0, rows, :] + mod_ref[0, 2:3, :] * _rms(y, norm_post_ref[...])


def _run_interleaved(stage_lists):
    tagged = [((i + 0.5) / len(stages), k, i, stage)
              for k, stages in enumerate(stage_lists) for i, stage in enumerate(stages)]
    for _, _, _, stage in sorted(tagged, key=lambda entry: entry[:3]):
        stage()


def _causal_conv(ext_ref, bi, tc, w_ref):
    ext = ext_ref[bi]
    acc = pltpu.roll(ext, CONV_W - 1, axis=0)[CONV_PAD:] * w_ref[0:1, :]
    for j in range(1, CONV_W - 1):
        acc = acc + pltpu.roll(ext, CONV_W - 1 - j, axis=0)[CONV_PAD:] * w_ref[j:j + 1, :]
    return acc + ext[CONV_PAD:] * w_ref[CONV_W - 1:CONV_W, :]


def _scan_phase_major(a, b, h0, chunk):
    tc, width = a.shape
    row8 = lax.broadcasted_iota(jnp.int32, (SUBLANES, width), 0)
    carry = h0
    out = []
    for c in range(tc // chunk):
        acc_a, acc_b = [], []
        for r in range(SUBLANES):
            blk = slice(c * chunk + r * SUBLANES, c * chunk + (r + 1) * SUBLANES)
            if r == 0:
                acc_a.append(a[blk])
                acc_b.append(b[blk])
            else:
                acc_b.append(a[blk] * acc_b[-1] + b[blk])
                acc_a.append(a[blk] * acc_a[-1])
        tot_a, tot_b = acc_a[-1], acc_b[-1]
        for sft in (1, 2, 4):
            keep = row8 >= sft
            tot_b = jnp.where(keep, tot_a * pltpu.roll(tot_b, sft, axis=0) + tot_b, tot_b)
            tot_a = jnp.where(keep, tot_a * pltpu.roll(tot_a, sft, axis=0), tot_a)
        h_out = tot_b + tot_a * carry
        h_in = jnp.where(row8 == 0, carry, pltpu.roll(h_out, 1, axis=0))
        out += [acc_b[r] + acc_a[r] * h_in for r in range(SUBLANES)]
        carry = h_out[SUBLANES - 1:SUBLANES, :]
    return jnp.concatenate(out, axis=0), carry


def _even_body(*refs, has_state, n_past_valid, pos0, chunk, n_tiles, k_out_rows, phase_major):
    refs = list(refs)
    x_ref = refs.pop(0)
    perm_ref = refs.pop(0) if phase_major else None
    (mod_ref, npre_ref, npost_ref, win_ref, cw_ref, cb_ref, wg_ref, ba_ref, bx_ref,
     lam_ref, inv_ref, sinks_ref, wout_ref) = refs[:13]
    refs = refs[13:]
    if has_state:
        conv0_ref, h0_ref, k0_ref, v0_ref = refs[:4]
        refs = refs[4:]
    (y_ref, nconv_ref, nh_ref, nk_ref, nv_ref,
     u_ref, ext_ref, a_ref, b_ref, hs_ref, knat_ref, vnat_ref, q_ref, mix_ref) = refs
    bb, tc, _ = x_ref.shape
    lw = a_ref.shape[1]
    aw = q_ref.shape[1]
    kvw = knat_ref.shape[2]
    t = pl.program_id(1)
    cpt = tc // chunk
    m = WINDOW + chunk
    o_q, o_k, o_v, o_gb = 2 * lw, 2 * lw + aw, 2 * lw + aw + kvw, 2 * lw + aw + 2 * kvw

    @pl.when(t == 0)
    def _():
        ext_ref[...] = jnp.zeros(ext_ref.shape, F32)
        if has_state:
            ext_ref[:, CONV_PAD - 3:CONV_PAD, :] = conv0_ref[...]
            nh_ref[...] = h0_ref[...]
            knat_ref[:, 0:WINDOW, :] = k0_ref[...]
            vnat_ref[:, 0:WINDOW, :] = v0_ref[...]
        else:
            nh_ref[...] = jnp.zeros(nh_ref.shape, F32)
            knat_ref[:, 0:WINDOW, :] = jnp.zeros((bb, WINDOW, kvw), F32)
            vnat_ref[:, 0:WINDOW, :] = jnp.zeros((bb, WINDOW, kvw), F32)

    def permute_rows(v):
        n = perm_ref.shape[0]
        return jnp.concatenate(
            [_dot(perm_ref[...], v[i * n:(i + 1) * n]).astype(BF16) for i in range(v.shape[0] // n)], axis=0)

    hmod_time = _pre_norm(x_ref, mod_ref, npre_ref).astype(BF16)
    hmod = permute_rows(hmod_time) if phase_major else hmod_time
    u_ref[...] = _dot(hmod, win_ref[...])

    def make_rope(time_in_tile):
        ang = (pos0 + t * tc + time_in_tile).astype(F32) * inv_ref[...]
        lane = lax.broadcasted_iota(jnp.int32, ang.shape, 1) % HEAD_DIM
        cos_t = jnp.cos(ang)
        sin_t = jnp.sin(ang)
        half = ROT_DIM // 2
        sin_a = jnp.where(lane < half, -sin_t, 0.0)
        sin_b = jnp.where(lane >= half, sin_t, 0.0)

        def rope(xcol):
            return (xcol * cos_t + pltpu.roll(xcol, LANES - half, axis=1) * sin_a
                    + pltpu.roll(xcol, half, axis=1) * sin_b)
        return rope

    row_t = lax.broadcasted_iota(jnp.int32, (tc, LANES), 0)
    if phase_major:
        row_t = (row_t // chunk) * chunk + _time_of_row(row_t % chunk)
    rope = make_rope(row_t)

    neg_c_softplus = -LRU_C * jax.nn.softplus(-lam_ref[...])
    row8 = lax.broadcasted_iota(jnp.int32, (SUBLANES, lw), 0)

    def per_batch(bi):
        rows = _rows(bi, tc)
        if phase_major:
            xc = _conv_phase_major(u_ref, ext_ref, nconv_ref, cw_ref, tc, chunk, slice(0, lw)) + cb_ref[...]
        else:
            ext_ref[bi, CONV_PAD:CONV_PAD + tc, :] = u_ref[rows, 0:lw]
            xc = _causal_conv(ext_ref, bi, tc, cw_ref) + cb_ref[...]
            nconv_ref[bi, :, :] = ext_ref[bi, CONV_PAD + tc - 3:CONV_PAD + tc, :]
            ext_ref[bi, CONV_PAD - 3:CONV_PAD, :] = ext_ref[bi, CONV_PAD + tc - 3:CONV_PAD + tc, :]
        xcb = xc.astype(BF16)
        halves = [_dot(xcb[:, i * MXU_TILE:(i + 1) * MXU_TILE], wg_ref[i]) for i in range(lw // MXU_TILE)]
        r = jax.nn.sigmoid(jnp.concatenate([g[:, :MXU_TILE] for g in halves], axis=1) + ba_ref[...])
        ig = jax.nn.sigmoid(jnp.concatenate([g[:, MXU_TILE:] for g in halves], axis=1) + bx_ref[...])
        log_a = r * neg_c_softplus
        a_all = jnp.exp(log_a)
        b_all = jnp.sqrt(-_expm1(2.0 * log_a)) * (ig * xc)
        if phase_major:
            h_all, nh_ref[bi, :, :] = _scan_phase_major(a_all, b_all, nh_ref[bi, :, :], chunk)
            hs_ref[...] = h_all
        else:
            a_ref[...] = a_all
            b_ref[...] = b_all

        def scan_block(j, hc):
            r0 = pl.multiple_of(j * SUBLANES, SUBLANES)
            a = a_ref[pl.ds(r0, SUBLANES), :]
            b = b_ref[pl.ds(r0, SUBLANES), :]
            for sft in (1, 2, 4):
                keep = row8 >= sft
                b = jnp.where(keep, a * pltpu.roll(b, sft, axis=0) + b, b)
                a = jnp.where(keep, a * pltpu.roll(a, sft, axis=0), a)
            h = a * hc + b
            hs_ref[pl.ds(r0, SUBLANES), :] = h
            return h[SUBLANES - 1:SUBLANES, :]

        if not phase_major:
            nh_ref[bi, :, :] = lax.fori_loop(0, tc // SUBLANES, scan_block, nh_ref[bi, :, :])
        mix_ref[rows, 0:lw] = (hs_ref[...] * _silu(u_ref[rows, lw:2 * lw])).astype(BF16)
        for j in range(aw // LANES):
            qcol = rope(u_ref[rows, o_q + j * LANES:o_q + (j + 1) * LANES])
            q_ref[rows, j * LANES:(j + 1) * LANES] = (qcol * (HEAD_DIM ** -0.5)).astype(BF16)
        knat_ref[bi, WINDOW:WINDOW + tc, :] = rope(u_ref[rows, o_k:o_k + kvw])
        vnat_ref[bi, WINDOW:WINDOW + tc, :] = u_ref[rows, o_v:o_v + kvw]
        if not phase_major:
            nk_ref[bi, :, :] = knat_ref[bi, WINDOW + tc - k_out_rows:WINDOW + tc, :]
            nv_ref[bi, :, :] = vnat_ref[bi, WINDOW + tc - k_out_rows:WINDOW + tc, :]

    _for_each(bb, per_batch)

    if phase_major:
        @pl.when(t == n_tiles - 1)
        def _():
            first = tc - k_out_rows
            kv = _dot(hmod_time[first:], win_ref[:, o_k:o_k + 2 * kvw])
            rope_last = make_rope(first + lax.broadcasted_iota(jnp.int32, (k_out_rows, LANES), 0))
            nk_ref[0] = rope_last(kv[:, 0:kvw])
            nv_ref[0] = kv[:, kvw:2 * kvw]

    lo_lane = lax.broadcasted_iota(jnp.int32, (m, LANES), 1) < HEAD_DIM
    lo_q = lax.broadcasted_iota(jnp.int32, (chunk, LANES), 1) < HEAD_DIM
    group = (aw // HEAD_DIM) // ATT_KV_HEADS

    sinks = [jnp.concatenate([jnp.full((chunk, 1), sinks_ref[h * group + g], F32) for g in range(group)], axis=0)
             for h in range(ATT_KV_HEADS)]

    lane_m = lax.broadcasted_iota(jnp.int32, (m, LANES), 1)

    def attend():
        units = []
        for bi, c in [(bi, c) for bi in range(bb) for c in range(cpt)]:
            rows = _rows(bi * cpt + c, chunk)
            krows = _rows(c, chunk, m)
            kseg = knat_ref[bi, krows, :]
            vseg = vnat_ref[bi, krows, :]
            krot = pltpu.roll(kseg, HEAD_DIM, axis=1)
            vrot = pltpu.roll(vseg, HEAD_DIM, axis=1)
            valid = None
            if n_past_valid < WINDOW:
                key_t = lax.broadcasted_iota(jnp.int32, (1, m), 1)
                if phase_major:
                    key_t = (key_t // chunk) * chunk + _time_of_row(key_t % chunk)
                valid = t * tc + c * chunk - WINDOW + key_t >= -n_past_valid
            for h in range(ATT_KV_HEADS):
                kd = (jnp.where(lo_lane, kseg, krot) if h == 0 else jnp.where(lo_lane, krot, kseg)).astype(BF16)
                v_lo = jnp.where(lo_lane, vseg if h == 0 else vrot, jnp.where(lane_m == HEAD_DIM, 1.0, 0.0))
                v_hi = jnp.where(lo_lane, jnp.where(lane_m == 0, 1.0, 0.0), vrot if h == 0 else vseg)
                units.append(dict(rows=rows, h=h, valid=valid, kd=kd, v_lo=v_lo.astype(BF16),
                                  v_hi=v_hi.astype(BF16)))

        for un in units:
            rows, h = un["rows"], un["h"]
            cols = [q_ref[rows, (h * group // 2 + j) * LANES:(h * group // 2 + j + 1) * LANES]
                    for j in range(group // 2)]
            zero = jnp.zeros_like(cols[0])
            qstack = jnp.concatenate(
                [part for qc in cols for part in (jnp.where(lo_q, qc, zero), jnp.where(lo_q, zero, qc))], axis=0)
            sc = _dot_nt(qstack, un["kd"])
            un["sc"] = sc if un["valid"] is None else jnp.where(un["valid"], sc, NEG_INF)
        for un in units:
            un["mx"] = jnp.maximum(jnp.max(un["sc"], axis=-1, keepdims=True), sinks[un["h"]])
        for un in units:
            un["p"] = jnp.exp(un["sc"] - un["mx"]).astype(BF16)
            un["sink_p"] = jnp.exp(sinks[un["h"]] - un["mx"])
        for un in units:
            rows, h, p, sink_p = un["rows"], un["h"], un["p"], un["sink_p"]
            for j in range(group // 2):
                r_lo = slice((2 * j) * chunk, (2 * j + 1) * chunk)
                r_hi = slice((2 * j + 1) * chunk, (2 * j + 2) * chunk)
                o_lo = _dot(p[r_lo], un["v_lo"])
                o_hi = _dot(p[r_hi], un["v_hi"])
                den_lo = o_lo[:, HEAD_DIM:HEAD_DIM + 1] + sink_p[r_lo]
                den_hi = o_hi[:, 0:1] + sink_p[r_hi]
                o = jnp.where(lo_q, o_lo / den_lo, o_hi / den_hi)
                col = h * group // 2 + j
                gate = _silu(u_ref[rows, o_gb + col * LANES:o_gb + (col + 1) * LANES])
                mix_ref[rows, lw + col * LANES:lw + (col + 1) * LANES] = (o * gate).astype(BF16)

    attend()

    if n_tiles > 1:
        knat_ref[:, 0:WINDOW, :] = knat_ref[:, tc:tc + WINDOW, :]
        vnat_ref[:, 0:WINDOW, :] = vnat_ref[:, tc:tc + WINDOW, :]

    mixed = mix_ref[...]
    if phase_major:
        mixed = permute_rows(mixed)
    y = _dot(mixed, wout_ref[...])
    _post_residual(x_ref, mod_ref, npost_ref, y, y_ref)


def _runs_phase_major(chunk, batch_block, has_state):
    return chunk == SUBLANES * SUBLANES and batch_block == 1 and not has_state


def _conv_history_shape(phase_major, batch_block, tile, channels):
    if phase_major:
        return (CONV_W - 1, SUBLANES, channels)
    return (batch_block, CONV_PAD + tile, channels)


def _const_spec(shape):
    zeros = (0,) * len(shape)
    return pl.BlockSpec(shape, lambda b, t: zeros)


def _even_layer(x, mod, consts, state, *, tile, chunk, batch_block, n_past_valid, pos0, k_out_rows):
    bsz, seq, d = x.shape
    (npre, npost, win, cw, cb, wg, b_a, b_x, lam, inv_row, sinks, wout) = consts
    lw = cw.shape[1]
    kvw = ATT_KV_HEADS * HEAD_DIM
    aw = (win.shape[1] - 2 * lw - 2 * kvw) // 2
    n_tiles = seq // tile
    assert seq % tile == 0 and tile % chunk == 0 and bsz % batch_block == 0
    assert n_tiles == 1 or tile >= WINDOW
    rows = batch_block * tile
    has_state = state is not None
    phase_major = _runs_phase_major(chunk, batch_block, has_state)
    assert not phase_major or tile >= k_out_rows

    def bspec(shape):
        nd = len(shape)
        return pl.BlockSpec((batch_block,) + shape, lambda b, t: (b,) + (0,) * nd)

    in_specs = [pl.BlockSpec((batch_block, tile, d), lambda b, t: (b, t, 0))]
    args = [x]
    if phase_major:
        perm = _phase_major_matrix(min(rows, MXU_TILE), chunk)
        assert rows % perm.shape[0] == 0
        in_specs.append(_const_spec(perm.shape))
        args.append(perm)
    in_specs += [
        bspec((3, d)),
        _const_spec(npre.shape), _const_spec(npost.shape), _const_spec(win.shape),
        _const_spec(cw.shape), _const_spec(cb.shape), _const_spec(wg.shape),
        _const_spec(b_a.shape), _const_spec(b_x.shape), _const_spec(lam.shape), _const_spec(inv_row.shape),
        pl.BlockSpec(memory_space=pltpu.SMEM),
        _const_spec(wout.shape),
    ]
    args += [mod, npre, npost, win, cw, cb, wg, b_a, b_x, lam, inv_row, sinks, wout]
    if has_state:
        in_specs += [bspec((CONV_W - 1, lw)), bspec((1, lw)), bspec((WINDOW, kvw)), bspec((WINDOW, kvw))]
        args += list(state)
    out_specs = [
        pl.BlockSpec((batch_block, tile, d), lambda b, t: (b, t, 0)),
        bspec((CONV_W - 1, lw)), bspec((1, lw)), bspec((k_out_rows, kvw)), bspec((k_out_rows, kvw)),
    ]
    out_shape = [
        jax.ShapeDtypeStruct((bsz, seq, d), F32),
        jax.ShapeDtypeStruct((bsz, CONV_W - 1, lw), F32),
        jax.ShapeDtypeStruct((bsz, 1, lw), F32),
        jax.ShapeDtypeStruct((bsz, k_out_rows, kvw), F32),
        jax.ShapeDtypeStruct((bsz, k_out_rows, kvw), F32),
    ]
    scratch = [
        pltpu.VMEM((rows, win.shape[1]), F32),
        pltpu.VMEM(_conv_history_shape(phase_major, batch_block, tile, lw), F32),
        pltpu.VMEM((tile, lw), F32), pltpu.VMEM((tile, lw), F32), pltpu.VMEM((tile, lw), F32),
        pltpu.VMEM((batch_block, WINDOW + tile, kvw), F32),
        pltpu.VMEM((batch_block, WINDOW + tile, kvw), F32),
        pltpu.VMEM((rows, aw), BF16),
        pltpu.VMEM((rows, lw + aw), BF16),
    ]
    body = functools.partial(_even_body, has_state=has_state, n_past_valid=n_past_valid, pos0=pos0,
                             chunk=chunk, n_tiles=n_tiles, k_out_rows=k_out_rows, phase_major=phase_major)
    return pl.pallas_call(
        body,
        grid=(bsz // batch_block, n_tiles),
        in_specs=in_specs,
        out_specs=out_specs,
        out_shape=out_shape,
        scratch_shapes=scratch,
        compiler_params=pltpu.CompilerParams(
            dimension_semantics=("arbitrary", "arbitrary"), vmem_limit_bytes=VMEM_LIMIT_BYTES),
        name="even_layer_state" if has_state else "even_layer_prompt",
    )(*args)


def _gdn_groups(chunk, n_heads):
    hpg = max(1, min(n_heads, MXU_TILE // chunk))
    assert n_heads % hpg == 0
    return hpg, n_heads // hpg


def _time_of_row(i):
    return (i % SUBLANES) * SUBLANES + i // SUBLANES


def _phase_major_matrix(n_rows, chunk):
    i = jnp.arange(n_rows)
    src = (i // chunk) * chunk + _time_of_row(i % chunk)
    return (src[:, None] == i[None, :]).astype(BF16)


def _conv_phase_major(u_ref, hist_ref, nconv_ref, w_ref, tc, chunk, cols):
    cpt = tc // chunk
    blocks = [u_ref[SUBLANES * b:SUBLANES * (b + 1), cols] for b in range(tc // SUBLANES)]
    taps_w = [w_ref[j:j + 1, cols] for j in range(CONV_W)]
    top_row = lax.broadcasted_iota(jnp.int32, blocks[0].shape, 0) == 0
    first_late = SUBLANES - (CONV_W - 1)
    prev = {r: pltpu.roll(hist_ref[r - first_late, :, cols], 1, axis=0) for r in range(first_late, SUBLANES)}
    out = []
    for c in range(cpt):
        late = {}
        for r in range(first_late, SUBLANES):
            moved = pltpu.roll(blocks[SUBLANES * c + r], 1, axis=0)
            late[r] = jnp.where(top_row, prev[r], moved)
            prev[r] = moved
        for r in range(SUBLANES):
            acc = None
            for j in range(CONV_W):
                s = CONV_W - 1 - j
                tap = blocks[SUBLANES * c + r - s] if r >= s else late[r - s + SUBLANES]
                acc = tap * taps_w[j] if acc is None else acc + tap * taps_w[j]
            out.append(acc)
    for i in range(CONV_W - 1):
        last = blocks[SUBLANES * (cpt - 1) + first_late + i]
        hist_ref[i, :, cols] = last
        nconv_ref[0, i:i + 1, cols] = last[SUBLANES - 1:SUBLANES, :]
    return jnp.concatenate(out, axis=0)


def _odd_body(*refs, has_state, chunk, phase_major):
    refs = list(refs)
    x_ref = refs.pop(0)
    perm_ref = refs.pop(0) if phase_major else None
    (mod_ref, npre_ref, npost_ref, win_ref, wba_ref, cw_ref, alog_ref, dtb_ref, hn_ref, wout_ref) = refs[:10]
    refs = refs[10:]
    if has_state:
        conv0_ref, s0_ref = refs[:2]
        refs = refs[2:]
    (y_ref, nconv_ref, ns_ref,
     hm_ref, u_ref, ext_ref, q_ref, k_ref, kb_ref, kbe_ref, vb_ref, qe_ref, kd_ref,
     beta_ref, gcum_ref, eg_ref, dec_ref, wq_ref, uu_ref, qk_ref, mix_ref) = refs
    bb, tc, _ = x_ref.shape
    n_heads = ns_ref.shape[1]
    kw = n_heads * GDN_DK
    vw = n_heads * GDN_DV
    cch = 2 * kw + vw
    rows_all = bb * tc
    t = pl.program_id(1)
    cpt = tc // chunk
    n_steps = (chunk - 1).bit_length()
    assert n_steps >= 2
    hpg, n_groups = _gdn_groups(chunk, n_heads)
    lw = hpg * chunk

    @pl.when(t == 0)
    def _():
        ext_ref[...] = jnp.zeros(ext_ref.shape, F32)
        if has_state:
            ext_ref[:, CONV_PAD - 3:CONV_PAD, :] = conv0_ref[...]
            ns_ref[...] = s0_ref[...]
        else:
            ns_ref[...] = jnp.zeros(ns_ref.shape, F32)

    def permute_rows(v):
        n = perm_ref.shape[0]
        return jnp.concatenate(
            [_dot(perm_ref[...], v[i * n:(i + 1) * n]).astype(BF16) for i in range(v.shape[0] // n)], axis=0)

    hmod = _pre_norm(x_ref, mod_ref, npre_ref).astype(BF16)
    if phase_major:
        hmod = permute_rows(hmod)
    hm_ref[...] = hmod
    ba = _dot(hmod, wba_ref[...])
    beta_ref[...] = jax.nn.sigmoid(ba)
    g = -jnp.exp(alog_ref[...]) * jax.nn.softplus(ba + dtb_ref[...])
    if phase_major:
        row8 = lax.broadcasted_iota(jnp.int32, (SUBLANES, LANES), 0)
        parts = []
        for c in range(rows_all // chunk):
            run = []
            for r in range(SUBLANES):
                blk = g[c * chunk + r * SUBLANES:c * chunk + (r + 1) * SUBLANES]
                run.append(blk if r == 0 else run[-1] + blk)
            total = run[-1]
            incl = total
            for sft in (1, 2, 4):
                incl = jnp.where(row8 >= sft, incl + pltpu.roll(incl, sft, axis=0), incl)
            parts += [blk + (incl - total) for blk in run]
        g = jnp.concatenate(parts, axis=0)
    else:
        row_in_chunk = lax.broadcasted_iota(jnp.int32, (rows_all, LANES), 0) % chunk
        sft = 1
        while sft < chunk:
            g = jnp.where(row_in_chunk >= sft, g + pltpu.roll(g, sft, axis=0), g)
            sft *= 2
    gcum_ref[...] = g
    eg_ref[...] = jnp.exp(g)
    g3 = g.reshape(rows_all // chunk, chunk, LANES)
    dec_ref[...] = jnp.exp(g3[:, chunk - 1:chunk, :] - g3).reshape(rows_all, LANES)

    def project(cols):
        u_ref[:, cols] = _dot(hm_ref[...], win_ref[:, cols])

    def head_operands(rows, h, qh, kh, vh):
        hc = slice(h * GDN_DK, (h + 1) * GDN_DK)
        qh = qh * lax.rsqrt(jnp.sum(qh * qh, axis=-1, keepdims=True) + EPS) * (GDN_DK ** -0.5)
        kh = kh * lax.rsqrt(jnp.sum(kh * kh, axis=-1, keepdims=True) + EPS)
        beta = beta_ref[rows, :][:, h:h + 1]
        eg = eg_ref[rows, :][:, n_heads + h:n_heads + h + 1]
        kb = kh * beta
        q_ref[rows, hc] = qh.astype(BF16)
        k_ref[rows, hc] = kh.astype(BF16)
        kb_ref[rows, hc] = kb.astype(BF16)
        kbe_ref[rows, hc] = (kb * eg).astype(BF16)
        vb_ref[rows, hc] = (vh * beta).astype(BF16)
        qe_ref[rows, hc] = (qh * eg).astype(BF16)
        kd_ref[rows, hc] = (kh * dec_ref[rows, :][:, n_heads + h:n_heads + h + 1]).astype(BF16)

    def per_batch(bi):
        rows = _rows(bi, tc)
        ext_ref[bi, CONV_PAD:CONV_PAD + tc, :] = u_ref[rows, 0:cch]
        qkv = _silu(_causal_conv(ext_ref, bi, tc, cw_ref))
        nconv_ref[bi, :, :] = ext_ref[bi, CONV_PAD + tc - 3:CONV_PAD + tc, :]
        ext_ref[bi, CONV_PAD - 3:CONV_PAD, :] = ext_ref[bi, CONV_PAD + tc - 3:CONV_PAD + tc, :]
        for h in range(n_heads):
            head_operands(rows, h, qkv[:, h * GDN_DK:(h + 1) * GDN_DK],
                          qkv[:, kw + h * GDN_DK:kw + (h + 1) * GDN_DK],
                          qkv[:, 2 * kw + h * GDN_DV:2 * kw + (h + 1) * GDN_DV])

    pair = MXU_TILE // GDN_DK

    def pair_stages(j):
        col_sets = [slice(base + j * MXU_TILE, base + (j + 1) * MXU_TILE) for base in (0, kw, 2 * kw)]

        def operands():
            rows = pl.ds(0, tc)
            q2, k2, v2 = [_silu(_conv_phase_major(u_ref, ext_ref, nconv_ref, cw_ref, tc, chunk, cols))
                          for cols in col_sets]
            for i in range(pair):
                part = slice(i * GDN_DK, (i + 1) * GDN_DK)
                head_operands(rows, pair * j + i, q2[:, part], k2[:, part], v2[:, part])

        return [functools.partial(project, cols) for cols in col_sets] + [operands]

    if not phase_major:
        project(slice(0, cch + vw))
        _for_each(bb, per_batch)

    lane_w = lax.broadcasted_iota(jnp.int32, (chunk, lw), 1)
    row_w = lax.broadcasted_iota(jnp.int32, (chunk, lw), 0)
    blk_w = lane_w // chunk
    col_w = lane_w % chunk
    diag_w = row_w == col_w
    if phase_major:
        row_w, col_w = _time_of_row(row_w), _time_of_row(col_w)
    incl_w = row_w >= col_w
    strict_w = row_w > col_w
    eye_w = jnp.where(diag_w, 1.0, 0.0).astype(F32)
    feat_blk = lax.broadcasted_iota(jnp.int32, (chunk, hpg * GDN_DK), 1) // GDN_DK

    def block_diag(m_b):
        return jnp.concatenate([jnp.where(blk_w == hh, m_b, jnp.zeros_like(m_b)) for hh in range(hpg)], axis=0)

    def head_cols(h):
        return slice(h * GDN_DK, (h + 1) * GDN_DK)

    def wy_stages(segs, head_groups):
        units = []

        def build():
            for seg in segs:
                rows = _rows(seg, chunk)
                gt = gcum_ref[rows, :]
                for gi in head_groups:
                    gcols = slice(gi * hpg * GDN_DK, (gi + 1) * hpg * GDN_DK)
                    k4 = k_ref[rows, gcols]
                    k_bd = jnp.concatenate(
                        [jnp.where(feat_blk == hh, k4, jnp.zeros_like(k4)) for hh in range(hpg)], axis=0)
                    kk = _dot_nt(jnp.concatenate([kb_ref[rows, gcols], q_ref[rows, gcols]], axis=0), k_bd)
                    gcol = jnp.zeros((chunk, lw), F32)
                    for hh in range(hpg):
                        ln = n_heads + gi * hpg + hh
                        gcol = jnp.where(blk_w == hh, gt[:, ln:ln + 1], gcol)
                    grow = jnp.sum(jnp.where(diag_w, gcol, 0.0), axis=0, keepdims=True)
                    decay = jnp.exp(jnp.where(incl_w, gcol - grow, 0.0))
                    p_b = (-(kk[:chunk] * jnp.where(strict_w, decay, 0.0))).astype(BF16)
                    qk_ref[seg * n_groups + gi] = (kk[chunk:] * jnp.where(incl_w, decay, 0.0)).astype(BF16)
                    units.append(dict(seg=seg, rows=rows, gi=gi, p=p_b, t=eye_w + p_b.astype(F32)))

        def square():
            for un in units:
                un["p"] = _dot(un["p"], block_diag(un["p"])).astype(BF16)

        def extend_and_square():
            for un in units:
                out = _dot(jnp.concatenate([un["t"].astype(BF16), un["p"]], axis=0), block_diag(un["p"]))
                un["t"] = un["t"] + out[:chunk]
                un["p"] = out[chunk:].astype(BF16)

        def extend():
            for un in units:
                un["t"] = un["t"] + _dot(un["t"].astype(BF16), block_diag(un["p"]))

        def apply():
            for un in units:
                seg, rows, gi = un["seg"], un["rows"], un["gi"]
                heads = range(gi * hpg, (gi + 1) * hpg)
                rhs = jnp.concatenate(
                    [jnp.concatenate([kbe_ref[rows, head_cols(h)], vb_ref[rows, head_cols(h)]], axis=1)
                     for h in heads], axis=0)
                wu = _dot(block_diag(un["t"].astype(BF16)), rhs)
                for hh, h in enumerate(heads):
                    blk = wu[hh * chunk:(hh + 1) * chunk]
                    wq_ref[seg * n_heads + h] = jnp.concatenate(
                        [blk[:, :GDN_DK].astype(BF16), qe_ref[rows, head_cols(h)]], axis=0)
                    uu_ref[rows, head_cols(h)] = blk[:, GDN_DK:]

        return [build, square] + [extend_and_square] * (n_steps - 2) + [extend, apply]

    zero_b = jnp.zeros((chunk, GDN_DV), BF16)
    state = {(bi, h): ns_ref[bi, h] for bi in range(bb) for h in range(n_heads)}

    def recurrence_stages(segs):
        stages = []
        for c in sorted({seg % cpt for seg in segs}):
            wave = [seg for seg in segs if seg % cpt == c]
            pairs = [(seg, h) for seg in wave for h in range(n_heads)]
            held = {}

            def correct(wave=wave, pairs=pairs, held=held):
                for seg, h in pairs:
                    held["ws_qs", seg, h] = _dot(wq_ref[seg * n_heads + h], state[seg // cpt, h].astype(BF16))
                for seg, h in pairs:
                    held["v", seg, h] = (uu_ref[_rows(seg, chunk), head_cols(h)]
                                         - held["ws_qs", seg, h][:chunk]).astype(BF16)

            def advance(wave=wave, pairs=pairs, held=held):
                for seg in wave:
                    held["eg", seg] = jnp.exp(gcum_ref[pl.ds(seg * chunk + chunk - 1, 1), :])
                for seg, h in pairs:
                    state[seg // cpt, h] = (state[seg // cpt, h] * held["eg", seg][:, n_heads + h:n_heads + h + 1]
                                            + _dot_tn(kd_ref[_rows(seg, chunk), head_cols(h)], held["v", seg, h]))
                for seg in wave:
                    rows = _rows(seg, chunk)
                    for gi in range(n_groups):
                        heads = range(gi * hpg, (gi + 1) * hpg)
                        v_bd = jnp.concatenate(
                            [jnp.concatenate([held["v", seg, h] if j == hh else zero_b for j in range(hpg)], axis=1)
                             for hh, h in enumerate(heads)], axis=0)
                        o = (jnp.concatenate([held["ws_qs", seg, h][chunk:] for h in heads], axis=1)
                             + _dot(qk_ref[seg * n_groups + gi], v_bd))
                        for hh, h in enumerate(heads):
                            gate = _silu(u_ref[rows, cch + h * GDN_DV:cch + (h + 1) * GDN_DV])
                            o_h = _rms(o[:, hh * GDN_DV:(hh + 1) * GDN_DV], hn_ref[...])
                            mix_ref[rows, head_cols(h)] = (o_h * gate).astype(BF16)

            stages += [correct, advance]
        return stages

    def output_stages(rows):
        held = {}

        def gather():
            mixed = mix_ref[rows, :]
            held["mixed"] = permute_rows(mixed) if phase_major else mixed

        def project():
            held["y"] = _dot(held["mixed"], wout_ref[...])

        def finish():
            _post_residual(x_ref, mod_ref, npost_ref, held["y"], y_ref, rows if n_row_groups > 1 else None)

        return [gather, project, finish]

    group_rows = min(rows_all, MXU_TILE)
    n_row_groups = rows_all // group_rows
    assert rows_all % group_rows == 0 and group_rows % chunk == 0 and (bb == 1 or n_row_groups == 1)
    segs_of = [range(g * group_rows // chunk, (g + 1) * group_rows // chunk) for g in range(n_row_groups)]
    if phase_major:
        pairs_per_group = hpg // pair
        assert hpg % pair == 0
        work = [(segs, gi) for gi in range(n_groups) for segs in segs_of]
        ahead = [[stage for j in range(gi * pairs_per_group, (gi + 1) * pairs_per_group) for stage in pair_stages(j)]
                 for gi in range(n_groups)]
        ahead.append([functools.partial(project, slice(cch + i * MXU_TILE, cch + (i + 1) * MXU_TILE))
                      for i in range(vw // MXU_TILE)])
        for stage in ahead[0]:
            stage()
        for gi in range(n_groups):
            factors = [stage for segs, g in work if g == gi for stage in wy_stages(segs, [gi])]
            _run_interleaved([ahead[gi + 1], factors])
    else:
        for segs in segs_of:
            for stage in wy_stages(segs, range(n_groups)):
                stage()
    for segs in segs_of:
        for stage in recurrence_stages(segs):
            stage()
    for g in range(n_row_groups):
        for stage in output_stages(pl.ds(g * group_rows, group_rows)):
            stage()
    for bh, value in state.items():
        ns_ref[bh[0], bh[1]] = value


def _odd_layer(x, mod, consts, state, *, tile, chunk, batch_block):
    bsz, seq, d = x.shape
    (npre, npost, win, wba, cw, alog, dtb, hn, wout) = consts
    cch = cw.shape[1]
    vw = wout.shape[0]
    n_heads = vw // GDN_DV
    n_tiles = seq // tile
    assert seq % tile == 0 and tile % chunk == 0 and bsz % batch_block == 0
    rows = batch_block * tile
    has_state = state is not None
    phase_major = _runs_phase_major(chunk, batch_block, has_state)

    def bspec(shape):
        nd = len(shape)
        return pl.BlockSpec((batch_block,) + shape, lambda b, t: (b,) + (0,) * nd)

    in_specs = [pl.BlockSpec((batch_block, tile, d), lambda b, t: (b, t, 0))]
    args = [x]
    if phase_major:
        perm_rows = min(rows, MXU_TILE)
        assert rows % perm_rows == 0 and perm_rows % chunk == 0
        in_specs.append(_const_spec((perm_rows, perm_rows)))
        args.append(_phase_major_matrix(perm_rows, chunk))
    in_specs += [
        bspec((3, d)),
        _const_spec(npre.shape), _const_spec(npost.shape), _const_spec(win.shape), _const_spec(wba.shape),
        _const_spec(cw.shape), _const_spec(alog.shape), _const_spec(dtb.shape), _const_spec(hn.shape),
        _const_spec(wout.shape),
    ]
    args += [mod, npre, npost, win, wba, cw, alog, dtb, hn, wout]
    if has_state:
        in_specs += [bspec((CONV_W - 1, cch)), bspec((n_heads, GDN_DK, GDN_DV))]
        args += list(state)
    out_specs = [
        pl.BlockSpec((batch_block, tile, d), lambda b, t: (b, t, 0)),
        bspec((CONV_W - 1, cch)), bspec((n_heads, GDN_DK, GDN_DV)),
    ]
    out_shape = [
        jax.ShapeDtypeStruct((bsz, seq, d), F32),
        jax.ShapeDtypeStruct((bsz, CONV_W - 1, cch), F32),
        jax.ShapeDtypeStruct((bsz, n_heads, GDN_DK, GDN_DV), F32),
    ]
    hpg, n_groups = _gdn_groups(chunk, n_heads)
    n_seg = rows // chunk
    scratch = (
        [pltpu.VMEM((rows, d), BF16),
         pltpu.VMEM((rows, win.shape[1]), F32),
         pltpu.VMEM(_conv_history_shape(phase_major, batch_block, tile, cch), F32)]
        + [pltpu.VMEM((rows, vw), BF16)] * 7
        + [pltpu.VMEM((rows, LANES), F32)] * 4
        + [pltpu.VMEM((n_seg * n_heads, 2 * chunk, GDN_DK), BF16),
           pltpu.VMEM((rows, vw), F32),
           pltpu.VMEM((n_seg * n_groups, chunk, hpg * chunk), BF16),
           pltpu.VMEM((rows, vw), BF16)]
    )
    body = functools.partial(_odd_body, has_state=has_state, chunk=chunk, phase_major=phase_major)
    return pl.pallas_call(
        body,
        grid=(bsz // batch_block, n_tiles),
        in_specs=in_specs,
        out_specs=out_specs,
        out_shape=out_shape,
        scratch_shapes=scratch,
        compiler_params=pltpu.CompilerParams(
            dimension_semantics=("arbitrary", "arbitrary"), vmem_limit_bytes=VMEM_LIMIT_BYTES),
        name="odd_layer_state" if has_state else "odd_layer_prompt",
    )(*args)


def _gate_weights(w_a, w_x):
    heads, blk, _ = w_a.shape
    per_tile = MXU_TILE // blk
    tiles = []
    for i in range(heads // per_tile):
        sl = slice(i * per_tile, (i + 1) * per_tile)
        tiles.append(jnp.concatenate(
            [jax.scipy.linalg.block_diag(*w_a[sl]), jax.scipy.linalg.block_diag(*w_x[sl])], axis=1))
    return jnp.stack(tiles).astype(BF16)


def _rope_inv_row():
    half = ROT_DIM // 2
    inv = ROPE_THETA ** (-(jnp.arange(half, dtype=F32) * 2.0 / ROT_DIM))
    per_head = jnp.concatenate([inv, inv, jnp.zeros((HEAD_DIM - ROT_DIM,), F32)])
    return jnp.tile(per_head, LANES // HEAD_DIM).reshape(1, LANES)


def kernel(x_prompt, x_sample, state_lru_conv, state_lru_h, cache_swa_k, cache_swa_v, state_gdn_conv, state_gdn_s, c_prompt, c_sample, ev_mod_w, ev_mod_b, ev_norm_pre, ev_norm_post, ev_w_in, lru_conv_w, lru_conv_b, lru_w_a, lru_b_a, lru_w_x, lru_b_x, lru_lambda, swa_sinks, ev_w_out, od_mod_w, od_mod_b, od_norm_pre, od_norm_post, od_w_in, gdn_conv_w, gdn_a_log, gdn_dt_bias, gdn_head_norm, od_w_out):
    bp, seq, d = x_prompt.shape
    bs, dec_seq, _ = x_sample.shape
    lw = lru_conv_w.shape[-1]
    kvw = ATT_KV_HEADS * HEAD_DIM
    n_heads = gdn_a_log.shape[-1]
    cch = gdn_conv_w.shape[-1]
    vw = od_w_out.shape[1]

    n_c = bp + bs
    n_pad = -n_c % 16
    c_all = jnp.concatenate([c_prompt, c_sample, jnp.zeros((n_pad, d), F32)], axis=0)
    mod_ev, mod_od = _modulation(c_all, ev_mod_w[0], ev_mod_b[0], od_mod_w[0], od_mod_b[0])
    mod_ev = mod_ev.reshape(n_c + n_pad, 3, d)
    mod_od = mod_od.reshape(n_c + n_pad, 3, d)

    row = lambda a: a.reshape(1, -1)
    ev_consts = (row(ev_norm_pre[0]), row(ev_norm_post[0]), ev_w_in[0].astype(BF16), lru_conv_w[0],
                 row(lru_conv_b[0]), _gate_weights(lru_w_a[0], lru_w_x[0]), row(lru_b_a[0]), row(lru_b_x[0]),
                 row(lru_lambda[0]), _rope_inv_row(), swa_sinks[0], ev_w_out[0].astype(BF16))
    w_in_od = od_w_in[0]
    pad_lanes = lambda a: jnp.pad(a, ((0, 0), (0, LANES - a.shape[1])))
    head_row = lambda a: jnp.pad(a.reshape(1, -1), ((0, 0), (n_heads, LANES - 2 * n_heads)))
    od_consts = (row(od_norm_pre[0]), row(od_norm_post[0]), w_in_od[:, :cch + vw].astype(BF16),
                 pad_lanes(w_in_od[:, cch + vw:]).astype(BF16), gdn_conv_w[0],
                 head_row(gdn_a_log[0]), head_row(gdn_dt_bias[0]), row(gdn_head_norm[0]),
                 od_w_out[0].astype(BF16))

    tile = min(PROMPT_TILE, seq)
    xp, lru_conv_p, lru_h_p, swa_k_p, swa_v_p = _even_layer(
        x_prompt, mod_ev[:bp], ev_consts, None, tile=tile, chunk=min(CHUNK, seq), batch_block=1,
        n_past_valid=0, pos0=0, k_out_rows=WINDOW)
    ev_state = (state_lru_conv[0], state_lru_h[0].reshape(bs, 1, lw),
                cache_swa_k[0].reshape(bs, WINDOW, kvw), cache_swa_v[0].reshape(bs, WINDOW, kvw))
    xs, lru_conv_s, lru_h_s, swa_k_s, swa_v_s = _even_layer(
        x_sample, mod_ev[bp:n_c], ev_consts, ev_state, tile=dec_seq, chunk=min(CHUNK, dec_seq), batch_block=bs,
        n_past_valid=WINDOW, pos0=PAST_LEN, k_out_rows=dec_seq)

    xp, gdn_conv_p, gdn_s_p = _odd_layer(
        xp, mod_od[:bp], od_consts, None, tile=tile, chunk=min(CHUNK, seq), batch_block=1)
    xs, gdn_conv_s, gdn_s_s = _odd_layer(
        xs, mod_od[bp:n_c], od_consts, (state_gdn_conv[0], state_gdn_s[0]),
        tile=dec_seq, chunk=min(CHUNK, dec_seq), batch_block=bs)

    kv_shape = lambda a: a.reshape(1, a.shape[0], a.shape[1], ATT_KV_HEADS, HEAD_DIM)
    return (xp, xs,
            lru_conv_p[None], lru_conv_s[None],
            lru_h_p.reshape(1, bp, lw), lru_h_s.reshape(1, bs, lw),
            kv_shape(swa_k_p), kv_shape(swa_k_s), kv_shape(swa_v_p), kv_shape(swa_v_s),
            gdn_conv_p[None], gdn_conv_s[None], gdn_s_p[None], gdn_s_s[None])
```

```python
import functools

import jax
import jax.numpy as jnp
from jax import lax
from jax.experimental import pallas as pl
from jax.experimental.pallas import tpu as pltpu

F32 = jnp.float32
BF16 = jnp.bfloat16

CHUNK = 64
EPS = 1e-6
CONV_W = 4
NEG_INF = -1e30
LRU_HEADS = 8
LRU_C = 8.0
HEAD_DIM = 64
ATT_KV_HEADS = 2
WINDOW = 128
ROT_DIM = HEAD_DIM // 4
ROPE_THETA = 500000.0
GDN_DK = 128
GDN_DV = 128
PAST_LEN = 4096

LANES = 128
SUBLANES = 8
MXU_TILE = 256
VMEM_LIMIT_BYTES = 56 * 1024 * 1024

PROMPT_TILE = 512
MOD_TILE = 512
CONV_PAD = SUBLANES


def _silu(x):
    return x * jax.nn.sigmoid(x)


def _expm1(x):
    u = jnp.exp(x)
    d = u - 1.0
    return jnp.where(u == 1.0, x, jnp.where(d == -1.0, -1.0, d * x / jnp.log(u)))


def _rms(x, g):
    return x * lax.rsqrt(jnp.mean(x * x, axis=-1, keepdims=True) + EPS) * g


def _dot(a, b):
    return jnp.dot(a, b, preferred_element_type=F32)


def _dot_nt(a, b):
    return lax.dot_general(a, b, (((1,), (1,)), ((), ())), preferred_element_type=F32)


def _dot_tn(a, b):
    return lax.dot_general(a, b, (((0,), (0,)), ((), ())), preferred_element_type=F32)


def _for_each(n, body):
    if n == 1:
        body(0)
    else:
        def step(i, carry):
            body(i)
            return carry
        lax.fori_loop(0, n, step, 0)


def _rows(i, n, size=None):
    size = n if size is None else size
    if isinstance(i, int):
        return pl.ds(i * n, size)
    return pl.ds(pl.multiple_of(i * n, n), size)


def _mod_body(c_ref, w0_ref, b0_ref, w1_ref, b1_ref, o0_ref, o1_ref):
    c = _silu(c_ref[...]).astype(BF16)
    o0_ref[...] = _dot(c, w0_ref[...].astype(BF16)) + b0_ref[...]
    o1_ref[...] = _dot(c, w1_ref[...].astype(BF16)) + b1_ref[...]


def _modulation(c_all, w0, b0, w1, b1):
    n, d = c_all.shape
    d3 = w0.shape[1]
    wspec = pl.BlockSpec((d, MOD_TILE), lambda j: (0, j))
    bspec = pl.BlockSpec((1, MOD_TILE), lambda j: (0, j))
    ospec = pl.BlockSpec((n, MOD_TILE), lambda j: (0, j))
    return pl.pallas_call(
        _mod_body,
        grid=(d3 // MOD_TILE,),
        in_specs=[pl.BlockSpec((n, d), lambda j: (0, 0)), wspec, bspec, wspec, bspec],
        out_specs=[ospec, ospec],
        out_shape=[jax.ShapeDtypeStruct((n, d3), F32)] * 2,
        name="adaln_modulation",
    )(c_all, w0, b0.reshape(1, d3), w1, b1.reshape(1, d3))


def _pre_norm(x_ref, mod_ref, norm_pre_ref):
    bb, tc, d = x_ref.shape
    x = x_ref[...]
    h = _rms(x, norm_pre_ref[...]) * (1.0 + mod_ref[:, 1:2, :]) + mod_ref[:, 0:1, :]
    return h.reshape(bb * tc, d)


def _post_residual(x_ref, mod_ref, norm_post_ref, y, o_ref, rows=None):
    bb, tc, d = x_ref.shape
    if rows is None:
        yn = _rms(y, norm_post_ref[...]).reshape(bb, tc, d)
        o_ref[...] = x_ref[...] + mod_ref[:, 2:3, :] * yn
    else:
        assert bb == 1
        o_ref[0, rows, :] = x_ref[0, rows, :] + mod_ref[0, 2:3, :] * _rms(y, norm_post_ref[...])


def _run_interleaved(stage_lists):
    tagged = [((i + 0.5) / len(stages), k, i, stage)
              for k, stages in enumerate(stage_lists) for i, stage in enumerate(stages)]
    for _, _, _, stage in sorted(tagged, key=lambda entry: entry[:3]):
        stage()


def _causal_conv(ext_ref, bi, tc, w_ref):
    ext = ext_ref[bi]
    acc = pltpu.roll(ext, CONV_W - 1, axis=0)[CONV_PAD:] * w_ref[0:1, :]
    for j in range(1, CONV_W - 1):
        acc = acc + pltpu.roll(ext, CONV_W - 1 - j, axis=0)[CONV_PAD:] * w_ref[j:j + 1, :]
    return acc + ext[CONV_PAD:] * w_ref[CONV_W - 1:CONV_W, :]


def _scan_phase_major(a, b, h0, chunk):
    tc, width = a.shape
    row8 = lax.broadcasted_iota(jnp.int32, (SUBLANES, width), 0)
    carry = h0
    out = []
    for c in range(tc // chunk):
        acc_a, acc_b = [], []
        for r in range(SUBLANES):
            blk = slice(c * chunk + r * SUBLANES, c * chunk + (r + 1) * SUBLANES)
            if r == 0:
                acc_a.append(a[blk])
                acc_b.append(b[blk])
            else:
                acc_b.append(a[blk] * acc_b[-1] + b[blk])
                acc_a.append(a[blk] * acc_a[-1])
        tot_a, tot_b = acc_a[-1], acc_b[-1]
        for sft in (1, 2, 4):
            keep = row8 >= sft
            tot_b = jnp.where(keep, tot_a * pltpu.roll(tot_b, sft, axis=0) + tot_b, tot_b)
            tot_a = jnp.where(keep, tot_a * pltpu.roll(tot_a, sft, axis=0), tot_a)
        h_out = tot_b + tot_a * carry
        h_in = jnp.where(row8 == 0, carry, pltpu.roll(h_out, 1, axis=0))
        out += [acc_b[r] + acc_a[r] * h_in for r in range(SUBLANES)]
        carry = h_out[SUBLANES - 1:SUBLANES, :]
    return jnp.concatenate(out, axis=0), carry


def _even_body(*refs, has_state, n_past_valid, pos0, chunk, n_tiles, k_out_rows, phase_major):
    refs = list(refs)
    x_ref = refs.pop(0)
    perm_ref = refs.pop(0) if phase_major else None
    (mod_ref, npre_ref, npost_ref, win_ref, cw_ref, cb_ref, wg_ref, ba_ref, bx_ref,
     lam_ref, inv_ref, sinks_ref, wout_ref) = refs[:13]
    refs = refs[13:]
    if has_state:
        conv0_ref, h0_ref, k0_ref, v0_ref = refs[:4]
        refs = refs[4:]
    (y_ref, nconv_ref, nh_ref, nk_ref, nv_ref,
     hm_ref, u_ref, ext_ref, a_ref, b_ref, hs_ref, knat_ref, vnat_ref, q_ref, mix_ref, cosl_ref, sinl_ref) = refs
    bb, tc, _ = x_ref.shape
    lw = a_ref.shape[1]
    aw = q_ref.shape[1]
    kvw = knat_ref.shape[2]
    t = pl.program_id(1)
    cpt = tc // chunk
    m = WINDOW + chunk
    o_q, o_k, o_v, o_gb = 2 * lw, 2 * lw + aw, 2 * lw + aw + kvw, 2 * lw + aw + 2 * kvw

    @pl.when(t == 0)
    def _():
        ext_ref[...] = jnp.zeros(ext_ref.shape, F32)
        if has_state:
            ext_ref[:, CONV_PAD - 3:CONV_PAD, :] = conv0_ref[...]
            nh_ref[...] = h0_ref[...]
            knat_ref[:, 0:WINDOW, :] = k0_ref[...]
            vnat_ref[:, 0:WINDOW, :] = v0_ref[...]
        else:
            nh_ref[...] = jnp.zeros(nh_ref.shape, F32)
            knat_ref[:, 0:WINDOW, :] = jnp.zeros((bb, WINDOW, kvw), F32)
            vnat_ref[:, 0:WINDOW, :] = jnp.zeros((bb, WINDOW, kvw), F32)

    def permute_rows(v):
        n = perm_ref.shape[0]
        return jnp.concatenate(
            [_dot(perm_ref[...], v[i * n:(i + 1) * n]).astype(BF16) for i in range(v.shape[0] // n)], axis=0)

    hmod_time = _pre_norm(x_ref, mod_ref, npre_ref).astype(BF16)
    hm_ref[...] = permute_rows(hmod_time) if phase_major else hmod_time

    tile_start = (pos0 + t * tc).astype(F32) * inv_ref[...]
    cos_start, sin_start = jnp.cos(tile_start), jnp.sin(tile_start)

    def make_rope(cos_in_tile, sin_in_tile):
        cos_t = cos_start * cos_in_tile - sin_start * sin_in_tile
        sin_t = sin_start * cos_in_tile + cos_start * sin_in_tile
        lane = lax.broadcasted_iota(jnp.int32, cos_t.shape, 1) % HEAD_DIM
        half = ROT_DIM // 2
        sin_a = jnp.where(lane < half, -sin_t, 0.0)
        sin_b = jnp.where(lane >= half, sin_t, 0.0)

        def rope(xcol):
            return (xcol * cos_t + pltpu.roll(xcol, LANES - half, axis=1) * sin_a
                    + pltpu.roll(xcol, half, axis=1) * sin_b)
        return rope

    @pl.when(t == 0)
    def _():
        row_t = lax.broadcasted_iota(jnp.int32, (tc, LANES), 0)
        if phase_major:
            row_t = (row_t // chunk) * chunk + _time_of_row(row_t % chunk)
        in_tile = row_t.astype(F32) * inv_ref[...]
        cosl_ref[...] = jnp.cos(in_tile)
        sinl_ref[...] = jnp.sin(in_tile)

    rope = make_rope(cosl_ref[...], sinl_ref[...])

    neg_c_softplus = -LRU_C * jax.nn.softplus(-lam_ref[...])
    row8 = lax.broadcasted_iota(jnp.int32, (SUBLANES, lw), 0)

    def project(cols):
        u_ref[:, cols] = _dot(hm_ref[...], win_ref[:, cols])

    def recurrence_inputs(bi):
        rows = _rows(bi, tc)
        if phase_major:
            xc = _conv_phase_major(u_ref, ext_ref, nconv_ref, cw_ref, tc, chunk, slice(0, lw)) + cb_ref[...]
        else:
            ext_ref[bi, CONV_PAD:CONV_PAD + tc, :] = u_ref[rows, 0:lw]
            xc = _causal_conv(ext_ref, bi, tc, cw_ref) + cb_ref[...]
            nconv_ref[bi, :, :] = ext_ref[bi, CONV_PAD + tc - 3:CONV_PAD + tc, :]
            ext_ref[bi, CONV_PAD - 3:CONV_PAD, :] = ext_ref[bi, CONV_PAD + tc - 3:CONV_PAD + tc, :]
        xcb = xc.astype(BF16)
        halves = [_dot(xcb[:, i * MXU_TILE:(i + 1) * MXU_TILE], wg_ref[i]) for i in range(lw // MXU_TILE)]
        r = jax.nn.sigmoid(jnp.concatenate([g[:, :MXU_TILE] for g in halves], axis=1) + ba_ref[...])
        ig = jax.nn.sigmoid(jnp.concatenate([g[:, MXU_TILE:] for g in halves], axis=1) + bx_ref[...])
        log_a = r * neg_c_softplus
        a_ref[...] = jnp.exp(log_a)
        b_ref[...] = jnp.sqrt(-_expm1(2.0 * log_a)) * (ig * xc)

    def recurrence(bi):
        if phase_major:
            hs_ref[...], nh_ref[bi, :, :] = _scan_phase_major(a_ref[...], b_ref[...], nh_ref[bi, :, :], chunk)
            return

        def scan_block(j, hc):
            r0 = pl.multiple_of(j * SUBLANES, SUBLANES)
            a = a_ref[pl.ds(r0, SUBLANES), :]
            b = b_ref[pl.ds(r0, SUBLANES), :]
            for sft in (1, 2, 4):
                keep = row8 >= sft
                b = jnp.where(keep, a * pltpu.roll(b, sft, axis=0) + b, b)
                a = jnp.where(keep, a * pltpu.roll(a, sft, axis=0), a)
            h = a * hc + b
            hs_ref[pl.ds(r0, SUBLANES), :] = h
            return h[SUBLANES - 1:SUBLANES, :]

        nh_ref[bi, :, :] = lax.fori_loop(0, tc // SUBLANES, scan_block, nh_ref[bi, :, :])

    def recurrence_output(bi):
        rows = _rows(bi, tc)
        mix_ref[rows, 0:lw] = (hs_ref[...] * _silu(u_ref[rows, lw:2 * lw])).astype(BF16)

    def attention_inputs(bi):
        rows = _rows(bi, tc)
        for j in range(aw // LANES):
            qcol = rope(u_ref[rows, o_q + j * LANES:o_q + (j + 1) * LANES])
            q_ref[rows, j * LANES:(j + 1) * LANES] = (qcol * (HEAD_DIM ** -0.5)).astype(BF16)
        knat_ref[bi, WINDOW:WINDOW + tc, :] = rope(u_ref[rows, o_k:o_k + kvw])
        vnat_ref[bi, WINDOW:WINDOW + tc, :] = u_ref[rows, o_v:o_v + kvw]
        if not phase_major:
            nk_ref[bi, :, :] = knat_ref[bi, WINDOW + tc - k_out_rows:WINDOW + tc, :]
            nv_ref[bi, :, :] = vnat_ref[bi, WINDOW + tc - k_out_rows:WINDOW + tc, :]

    def per_batch(bi):
        recurrence_inputs(bi)
        recurrence(bi)
        recurrence_output(bi)
        attention_inputs(bi)

    if phase_major:
        project(slice(0, lw))
        later = [slice(lw, 2 * lw), slice(o_q, o_k), slice(o_k, o_gb), slice(o_gb, o_gb + aw)]
        _run_interleaved([[functools.partial(project, cols) for cols in later],
                          [functools.partial(recurrence_inputs, 0), functools.partial(recurrence, 0)]])
        recurrence_output(0)
        attention_inputs(0)
    else:
        project(slice(0, win_ref.shape[1]))
        _for_each(bb, per_batch)

    if phase_major:
        @pl.when(t == n_tiles - 1)
        def _():
            first = tc - k_out_rows
            kv = _dot(hmod_time[first:], win_ref[:, o_k:o_k + 2 * kvw])
            in_tile = (first + lax.broadcasted_iota(jnp.int32, (k_out_rows, LANES), 0)).astype(F32) * inv_ref[...]
            rope_last = make_rope(jnp.cos(in_tile), jnp.sin(in_tile))
            nk_ref[0] = rope_last(kv[:, 0:kvw])
            nv_ref[0] = kv[:, kvw:2 * kvw]

    lo_lane = lax.broadcasted_iota(jnp.int32, (m, LANES), 1) < HEAD_DIM
    lo_q = lax.broadcasted_iota(jnp.int32, (chunk, LANES), 1) < HEAD_DIM
    group = (aw // HEAD_DIM) // ATT_KV_HEADS

    sinks = [jnp.concatenate([jnp.full((chunk, 1), sinks_ref[h * group + g], F32) for g in range(group)], axis=0)
             for h in range(ATT_KV_HEADS)]

    lane_m = lax.broadcasted_iota(jnp.int32, (m, LANES), 1)

    def attend():
        units = []
        for bi, c in [(bi, c) for bi in range(bb) for c in range(cpt)]:
            rows = _rows(bi * cpt + c, chunk)
            krows = _rows(c, chunk, m)
            kseg = knat_ref[bi, krows, :]
            vseg = vnat_ref[bi, krows, :]
            krot = pltpu.roll(kseg, HEAD_DIM, axis=1)
            vrot = pltpu.roll(vseg, HEAD_DIM, axis=1)
            valid = None
            if n_past_valid < WINDOW:
                key_t = lax.broadcasted_iota(jnp.int32, (1, m), 1)
                if phase_major:
                    key_t = (key_t // chunk) * chunk + _time_of_row(key_t % chunk)
                valid = t * tc + c * chunk - WINDOW + key_t >= -n_past_valid
            for h in range(ATT_KV_HEADS):
                kd = (jnp.where(lo_lane, kseg, krot) if h == 0 else jnp.where(lo_lane, krot, kseg)).astype(BF16)
                v_lo = jnp.where(lo_lane, vseg if h == 0 else vrot, jnp.where(lane_m == HEAD_DIM, 1.0, 0.0))
                v_hi = jnp.where(lo_lane, jnp.where(lane_m == 0, 1.0, 0.0), vrot if h == 0 else vseg)
                units.append(dict(rows=rows, h=h, valid=valid, kd=kd, v_lo=v_lo.astype(BF16),
                                  v_hi=v_hi.astype(BF16)))

        for un in units:
            rows, h = un["rows"], un["h"]
            cols = [q_ref[rows, (h * group // 2 + j) * LANES:(h * group // 2 + j + 1) * LANES]
                    for j in range(group // 2)]
            zero = jnp.zeros_like(cols[0])
            qstack = jnp.concatenate(
                [part for qc in cols for part in (jnp.where(lo_q, qc, zero), jnp.where(lo_q, zero, qc))], axis=0)
            sc = _dot_nt(qstack, un["kd"])
            un["sc"] = sc if un["valid"] is None else jnp.where(un["valid"], sc, NEG_INF)
        for un in units:
            un["mx"] = jnp.maximum(jnp.max(un["sc"], axis=-1, keepdims=True), sinks[un["h"]])
        for un in units:
            un["p"] = jnp.exp(un["sc"] - un["mx"]).astype(BF16)
            un["sink_p"] = jnp.exp(sinks[un["h"]] - un["mx"])
        for un in units:
            rows, h, p, sink_p = un["rows"], un["h"], un["p"], un["sink_p"]
            for j in range(group // 2):
                r_lo = slice((2 * j) * chunk, (2 * j + 1) * chunk)
                r_hi = slice((2 * j + 1) * chunk, (2 * j + 2) * chunk)
                o_lo = _dot(p[r_lo], un["v_lo"])
                o_hi = _dot(p[r_hi], un["v_hi"])
                den_lo = o_lo[:, HEAD_DIM:HEAD_DIM + 1] + sink_p[r_lo]
                den_hi = o_hi[:, 0:1] + sink_p[r_hi]
                o = jnp.where(lo_q, o_lo / den_lo, o_hi / den_hi)
                col = h * group // 2 + j
                gate = _silu(u_ref[rows, o_gb + col * LANES:o_gb + (col + 1) * LANES])
                mix_ref[rows, lw + col * LANES:lw + (col + 1) * LANES] = (o * gate).astype(BF16)

    attend()

    if n_tiles > 1:
        knat_ref[:, 0:WINDOW, :] = knat_ref[:, tc:tc + WINDOW, :]
        vnat_ref[:, 0:WINDOW, :] = vnat_ref[:, tc:tc + WINDOW, :]

    mixed = mix_ref[...]
    if phase_major:
        mixed = permute_rows(mixed)
    y = _dot(mixed, wout_ref[...])
    _post_residual(x_ref, mod_ref, npost_ref, y, y_ref)


def _runs_phase_major(chunk, batch_block, has_state):
    return chunk == SUBLANES * SUBLANES and batch_block == 1 and not has_state


def _conv_history_shape(phase_major, batch_block, tile, channels):
    if phase_major:
        return (CONV_W - 1, SUBLANES, channels)
    return (batch_block, CONV_PAD + tile, channels)


def _const_spec(shape):
    zeros = (0,) * len(shape)
    return pl.BlockSpec(shape, lambda b, t: zeros)


def _even_layer(x, mod, consts, state, *, tile, chunk, batch_block, n_past_valid, pos0, k_out_rows):
    bsz, seq, d = x.shape
    (npre, npost, win, cw, cb, wg, b_a, b_x, lam, inv_row, sinks, wout) = consts
    lw = cw.shape[1]
    kvw = ATT_KV_HEADS * HEAD_DIM
    aw = (win.shape[1] - 2 * lw - 2 * kvw) // 2
    n_tiles = seq // tile
    assert seq % tile == 0 and tile % chunk == 0 and bsz % batch_block == 0
    assert n_tiles == 1 or tile >= WINDOW
    rows = batch_block * tile
    has_state = state is not None
    phase_major = _runs_phase_major(chunk, batch_block, has_state)
    assert not phase_major or tile >= k_out_rows

    def bspec(shape):
        nd = len(shape)
        return pl.BlockSpec((batch_block,) + shape, lambda b, t: (b,) + (0,) * nd)

    in_specs = [pl.BlockSpec((batch_block, tile, d), lambda b, t: (b, t, 0))]
    args = [x]
    if phase_major:
        perm = _phase_major_matrix(min(rows, MXU_TILE), chunk)
        assert rows % perm.shape[0] == 0
        in_specs.append(_const_spec(perm.shape))
        args.append(perm)
    in_specs += [
        bspec((3, d)),
        _const_spec(npre.shape), _const_spec(npost.shape), _const_spec(win.shape),
        _const_spec(cw.shape), _const_spec(cb.shape), _const_spec(wg.shape),
        _const_spec(b_a.shape), _const_spec(b_x.shape), _const_spec(lam.shape), _const_spec(inv_row.shape),
        pl.BlockSpec(memory_space=pltpu.SMEM),
        _const_spec(wout.shape),
    ]
    args += [mod, npre, npost, win, cw, cb, wg, b_a, b_x, lam, inv_row, sinks, wout]
    if has_state:
        in_specs += [bspec((CONV_W - 1, lw)), bspec((1, lw)), bspec((WINDOW, kvw)), bspec((WINDOW, kvw))]
        args += list(state)
    out_specs = [
        pl.BlockSpec((batch_block, tile, d), lambda b, t: (b, t, 0)),
        bspec((CONV_W - 1, lw)), bspec((1, lw)), bspec((k_out_rows, kvw)), bspec((k_out_rows, kvw)),
    ]
    out_shape = [
        jax.ShapeDtypeStruct((bsz, seq, d), F32),
        jax.ShapeDtypeStruct((bsz, CONV_W - 1, lw), F32),
        jax.ShapeDtypeStruct((bsz, 1, lw), F32),
        jax.ShapeDtypeStruct((bsz, k_out_rows, kvw), F32),
        jax.ShapeDtypeStruct((bsz, k_out_rows, kvw), F32),
    ]
    scratch = [
        pltpu.VMEM((rows, d), BF16),
        pltpu.VMEM((rows, win.shape[1]), F32),
        pltpu.VMEM(_conv_history_shape(phase_major, batch_block, tile, lw), F32),
        pltpu.VMEM((tile, lw), F32), pltpu.VMEM((tile, lw), F32), pltpu.VMEM((tile, lw), F32),
        pltpu.VMEM((batch_block, WINDOW + tile, kvw), F32),
        pltpu.VMEM((batch_block, WINDOW + tile, kvw), F32),
        pltpu.VMEM((rows, aw), BF16),
        pltpu.VMEM((rows, lw + aw), BF16),
        pltpu.VMEM((tile, LANES), F32), pltpu.VMEM((tile, LANES), F32),
    ]
    body = functools.partial(_even_body, has_state=has_state, n_past_valid=n_past_valid, pos0=pos0,
                             chunk=chunk, n_tiles=n_tiles, k_out_rows=k_out_rows, phase_major=phase_major)
    return pl.pallas_call(
        body,
        grid=(bsz // batch_block, n_tiles),
        in_specs=in_specs,
        out_specs=out_specs,
        out_shape=out_shape,
        scratch_shapes=scratch,
        compiler_params=pltpu.CompilerParams(
            dimension_semantics=("arbitrary", "arbitrary"), vmem_limit_bytes=VMEM_LIMIT_BYTES),
        name="even_layer_state" if has_state else "even_layer_prompt",
    )(*args)


def _gdn_groups(chunk, n_heads):
    hpg = max(1, min(n_heads, MXU_TILE // chunk))
    assert n_heads % hpg == 0
    return hpg, n_heads // hpg


def _time_of_row(i):
    return (i % SUBLANES) * SUBLANES + i // SUBLANES


def _phase_major_matrix(n_rows, chunk):
    i = jnp.arange(n_rows)
    src = (i // chunk) * chunk + _time_of_row(i % chunk)
    return (src[:, None] == i[None, :]).astype(BF16)


def _conv_phase_major(u_ref, hist_ref, nconv_ref, w_ref, tc, chunk, cols):
    cpt = tc // chunk
    blocks = [u_ref[SUBLANES * b:SUBLANES * (b + 1), cols] for b in range(tc // SUBLANES)]
    taps_w = [w_ref[j:j + 1, cols] for j in range(CONV_W)]
    top_row = lax.broadcasted_iota(jnp.int32, blocks[0].shape, 0) == 0
    first_late = SUBLANES - (CONV_W - 1)
    prev = {r: pltpu.roll(hist_ref[r - first_late, :, cols], 1, axis=0) for r in range(first_late, SUBLANES)}
    out = []
    for c in range(cpt):
        late = {}
        for r in range(first_late, SUBLANES):
            moved = pltpu.roll(blocks[SUBLANES * c + r], 1, axis=0)
            late[r] = jnp.where(top_row, prev[r], moved)
            prev[r] = moved
        for r in range(SUBLANES):
            acc = None
            for j in range(CONV_W):
                s = CONV_W - 1 - j
                tap = blocks[SUBLANES * c + r - s] if r >= s else late[r - s + SUBLANES]
                acc = tap * taps_w[j] if acc is None else acc + tap * taps_w[j]
            out.append(acc)
    for i in range(CONV_W - 1):
        last = blocks[SUBLANES * (cpt - 1) + first_late + i]
        hist_ref[i, :, cols] = last
        nconv_ref[0, i:i + 1, cols] = last[SUBLANES - 1:SUBLANES, :]
    return jnp.concatenate(out, axis=0)


def _odd_body(*refs, has_state, chunk, phase_major):
    refs = list(refs)
    x_ref = refs.pop(0)
    perm_ref = refs.pop(0) if phase_major else None
    (mod_ref, npre_ref, npost_ref, win_ref, wba_ref, cw_ref, alog_ref, dtb_ref, hn_ref, wout_ref) = refs[:10]
    refs = refs[10:]
    if has_state:
        conv0_ref, s0_ref = refs[:2]
        refs = refs[2:]
    (y_ref, nconv_ref, ns_ref,
     hm_ref, u_ref, ext_ref, q_ref, k_ref, kb_ref, kbe_ref, vb_ref, qe_ref, kd_ref,
     beta_ref, gcum_ref, eg_ref, dec_ref, wq_ref, uu_ref, qk_ref, mix_ref) = refs
    bb, tc, _ = x_ref.shape
    n_heads = ns_ref.shape[1]
    kw = n_heads * GDN_DK
    vw = n_heads * GDN_DV
    cch = 2 * kw + vw
    rows_all = bb * tc
    t = pl.program_id(1)
    cpt = tc // chunk
    n_steps = (chunk - 1).bit_length()
    assert n_steps >= 2
    hpg, n_groups = _gdn_groups(chunk, n_heads)
    lw = hpg * chunk

    @pl.when(t == 0)
    def _():
        ext_ref[...] = jnp.zeros(ext_ref.shape, F32)
        if has_state:
            ext_ref[:, CONV_PAD - 3:CONV_PAD, :] = conv0_ref[...]
            ns_ref[...] = s0_ref[...]
        else:
            ns_ref[...] = jnp.zeros(ns_ref.shape, F32)

    def permute_rows(v):
        n = perm_ref.shape[0]
        return jnp.concatenate(
            [_dot(perm_ref[...], v[i * n:(i + 1) * n]).astype(BF16) for i in range(v.shape[0] // n)], axis=0)

    hmod = _pre_norm(x_ref, mod_ref, npre_ref).astype(BF16)
    if phase_major:
        hmod = permute_rows(hmod)
    hm_ref[...] = hmod
    ba = _dot(hmod, wba_ref[...])
    beta_ref[...] = jax.nn.sigmoid(ba)
    g = -jnp.exp(alog_ref[...]) * jax.nn.softplus(ba + dtb_ref[...])
    if phase_major:
        row8 = lax.broadcasted_iota(jnp.int32, (SUBLANES, LANES), 0)
        parts = []
        for c in range(rows_all // chunk):
            run = []
            for r in range(SUBLANES):
                blk = g[c * chunk + r * SUBLANES:c * chunk + (r + 1) * SUBLANES]
                run.append(blk if r == 0 else run[-1] + blk)
            total = run[-1]
            incl = total
            for sft in (1, 2, 4):
                incl = jnp.where(row8 >= sft, incl + pltpu.roll(incl, sft, axis=0), incl)
            parts += [blk + (incl - total) for blk in run]
        g = jnp.concatenate(parts, axis=0)
    else:
        row_in_chunk = lax.broadcasted_iota(jnp.int32, (rows_all, LANES), 0) % chunk
        sft = 1
        while sft < chunk:
            g = jnp.where(row_in_chunk >= sft, g + pltpu.roll(g, sft, axis=0), g)
            sft *= 2
    gcum_ref[...] = g
    eg_ref[...] = jnp.exp(g)
    g3 = g.reshape(rows_all // chunk, chunk, LANES)
    dec_ref[...] = jnp.exp(g3[:, chunk - 1:chunk, :] - g3).reshape(rows_all, LANES)

    def project(cols):
        u_ref[:, cols] = _dot(hm_ref[...], win_ref[:, cols])

    def head_operands(rows, h, qh, kh, vh):
        hc = slice(h * GDN_DK, (h + 1) * GDN_DK)
        qh = qh * lax.rsqrt(jnp.sum(qh * qh, axis=-1, keepdims=True) + EPS) * (GDN_DK ** -0.5)
        kh = kh * lax.rsqrt(jnp.sum(kh * kh, axis=-1, keepdims=True) + EPS)
        beta = beta_ref[rows, :][:, h:h + 1]
        eg = eg_ref[rows, :][:, n_heads + h:n_heads + h + 1]
        kb = kh * beta
        q_ref[rows, hc] = qh.astype(BF16)
        k_ref[rows, hc] = kh.astype(BF16)
        kb_ref[rows, hc] = kb.astype(BF16)
        kbe_ref[rows, hc] = (kb * eg).astype(BF16)
        vb_ref[rows, hc] = (vh * beta).astype(BF16)
        qe_ref[rows, hc] = (qh * eg).astype(BF16)
        kd_ref[rows, hc] = (kh * dec_ref[rows, :][:, n_heads + h:n_heads + h + 1]).astype(BF16)

    def per_batch(bi):
        rows = _rows(bi, tc)
        ext_ref[bi, CONV_PAD:CONV_PAD + tc, :] = u_ref[rows, 0:cch]
        qkv = _silu(_causal_conv(ext_ref, bi, tc, cw_ref))
        nconv_ref[bi, :, :] = ext_ref[bi, CONV_PAD + tc - 3:CONV_PAD + tc, :]
        ext_ref[bi, CONV_PAD - 3:CONV_PAD, :] = ext_ref[bi, CONV_PAD + tc - 3:CONV_PAD + tc, :]
        for h in range(n_heads):
            head_operands(rows, h, qkv[:, h * GDN_DK:(h + 1) * GDN_DK],
                          qkv[:, kw + h * GDN_DK:kw + (h + 1) * GDN_DK],
                          qkv[:, 2 * kw + h * GDN_DV:2 * kw + (h + 1) * GDN_DV])

    pair = MXU_TILE // GDN_DK

    def pair_stages(j):
        col_sets = [slice(base + j * MXU_TILE, base + (j + 1) * MXU_TILE) for base in (0, kw, 2 * kw)]

        def operands():
            rows = pl.ds(0, tc)
            q2, k2, v2 = [_silu(_conv_phase_major(u_ref, ext_ref, nconv_ref, cw_ref, tc, chunk, cols))
                          for cols in col_sets]
            for i in range(pair):
                part = slice(i * GDN_DK, (i + 1) * GDN_DK)
                head_operands(rows, pair * j + i, q2[:, part], k2[:, part], v2[:, part])

        return [functools.partial(project, cols) for cols in col_sets] + [operands]

    if not phase_major:
        project(slice(0, cch + vw))
        _for_each(bb, per_batch)

    lane_w = lax.broadcasted_iota(jnp.int32, (chunk, lw), 1)
    row_w = lax.broadcasted_iota(jnp.int32, (chunk, lw), 0)
    blk_w = lane_w // chunk
    col_w = lane_w % chunk
    diag_w = row_w == col_w
    if phase_major:
        row_w, col_w = _time_of_row(row_w), _time_of_row(col_w)
    incl_w = row_w >= col_w
    strict_w = row_w > col_w
    eye_w = jnp.where(diag_w, 1.0, 0.0).astype(F32)
    feat_blk = lax.broadcasted_iota(jnp.int32, (chunk, hpg * GDN_DK), 1) // GDN_DK

    def block_diag(m_b):
        return jnp.concatenate([jnp.where(blk_w == hh, m_b, jnp.zeros_like(m_b)) for hh in range(hpg)], axis=0)

    def head_cols(h):
        return slice(h * GDN_DK, (h + 1) * GDN_DK)

    def wy_stages(segs, head_groups):
        units = []

        def build():
            for seg in segs:
                rows = _rows(seg, chunk)
                gt = gcum_ref[rows, :]
                for gi in head_groups:
                    gcols = slice(gi * hpg * GDN_DK, (gi + 1) * hpg * GDN_DK)
                    k4 = k_ref[rows, gcols]
                    k_bd = jnp.concatenate(
                        [jnp.where(feat_blk == hh, k4, jnp.zeros_like(k4)) for hh in range(hpg)], axis=0)
                    kk = _dot_nt(jnp.concatenate([kb_ref[rows, gcols], q_ref[rows, gcols]], axis=0), k_bd)
                    gcol = jnp.zeros((chunk, lw), F32)
                    for hh in range(hpg):
                        ln = n_heads + gi * hpg + hh
                        gcol = jnp.where(blk_w == hh, gt[:, ln:ln + 1], gcol)
                    grow = jnp.sum(jnp.where(diag_w, gcol, 0.0), axis=0, keepdims=True)
                    decay = jnp.exp(jnp.where(incl_w, gcol - grow, 0.0))
                    p_b = (-(kk[:chunk] * jnp.where(strict_w, decay, 0.0))).astype(BF16)
                    qk_ref[seg * n_groups + gi] = (kk[chunk:] * jnp.where(incl_w, decay, 0.0)).astype(BF16)
                    units.append(dict(seg=seg, rows=rows, gi=gi, p=p_b, t=eye_w + p_b.astype(F32)))

        def square():
            for un in units:
                un["p"] = _dot(un["p"], block_diag(un["p"])).astype(BF16)

        def extend_and_square():
            for un in units:
                out = _dot(jnp.concatenate([un["t"].astype(BF16), un["p"]], axis=0), block_diag(un["p"]))
                un["t"] = un["t"] + out[:chunk]
                un["p"] = out[chunk:].astype(BF16)

        def extend():
            for un in units:
                un["t"] = un["t"] + _dot(un["t"].astype(BF16), block_diag(un["p"]))

        def apply():
            for un in units:
                seg, rows, gi = un["seg"], un["rows"], un["gi"]
                heads = range(gi * hpg, (gi + 1) * hpg)
                rhs = jnp.concatenate(
                    [jnp.concatenate([kbe_ref[rows, head_cols(h)], vb_ref[rows, head_cols(h)]], axis=1)
                     for h in heads], axis=0)
                wu = _dot(block_diag(un["t"].astype(BF16)), rhs)
                for hh, h in enumerate(heads):
                    blk = wu[hh * chunk:(hh + 1) * chunk]
                    wq_ref[seg * n_heads + h] = jnp.concatenate(
                        [blk[:, :GDN_DK].astype(BF16), qe_ref[rows, head_cols(h)]], axis=0)
                    uu_ref[rows, head_cols(h)] = blk[:, GDN_DK:]

        return [build, square] + [extend_and_square] * (n_steps - 2) + [extend, apply]

    zero_b = jnp.zeros((chunk, GDN_DV), BF16)
    state = {(bi, h): ns_ref[bi, h] for bi in range(bb) for h in range(n_heads)}

    def recurrence_stages(segs):
        stages = []
        for c in sorted({seg % cpt for seg in segs}):
            wave = [seg for seg in segs if seg % cpt == c]
            pairs = [(seg, h) for seg in wave for h in range(n_heads)]
            held = {}

            def correct(wave=wave, pairs=pairs, held=held):
                for seg, h in pairs:
                    held["ws_qs", seg, h] = _dot(wq_ref[seg * n_heads + h], state[seg // cpt, h].astype(BF16))
                for seg, h in pairs:
                    held["v", seg, h] = (uu_ref[_rows(seg, chunk), head_cols(h)]
                                         - held["ws_qs", seg, h][:chunk]).astype(BF16)

            def advance(wave=wave, pairs=pairs, held=held):
                for seg in wave:
                    held["eg", seg] = jnp.exp(gcum_ref[pl.ds(seg * chunk + chunk - 1, 1), :])
                for seg, h in pairs:
                    state[seg // cpt, h] = (state[seg // cpt, h] * held["eg", seg][:, n_heads + h:n_heads + h + 1]
                                            + _dot_tn(kd_ref[_rows(seg, chunk), head_cols(h)], held["v", seg, h]))
                for seg in wave:
                    rows = _rows(seg, chunk)
                    for gi in range(n_groups):
                        heads = range(gi * hpg, (gi + 1) * hpg)
                        v_bd = jnp.concatenate(
                            [jnp.concatenate([held["v", seg, h] if j == hh else zero_b for j in range(hpg)], axis=1)
                             for hh, h in enumerate(heads)], axis=0)
                        o = (jnp.concatenate([held["ws_qs", seg, h][chunk:] for h in heads], axis=1)
                             + _dot(qk_ref[seg * n_groups + gi], v_bd))
                        for hh, h in enumerate(heads):
                            gate = _silu(u_ref[rows, cch + h * GDN_DV:cch + (h + 1) * GDN_DV])
                            o_h = _rms(o[:, hh * GDN_DV:(hh + 1) * GDN_DV], hn_ref[...])
                            mix_ref[rows, head_cols(h)] = (o_h * gate).astype(BF16)

            stages += [correct, advance]
        return stages

    def output_stages(rows):
        held = {}

        def gather():
            mixed = mix_ref[rows, :]
            held["mixed"] = permute_rows(mixed) if phase_major else mixed

        def project():
            held["y"] = _dot(held["mixed"], wout_ref[...])

        def finish():
            _post_residual(x_ref, mod_ref, npost_ref, held["y"], y_ref, rows if n_row_groups > 1 else None)

        return [gather, project, finish]

    group_rows = min(rows_all, MXU_TILE)
    n_row_groups = rows_all // group_rows
    assert rows_all % group_rows == 0 and group_rows % chunk == 0 and (bb == 1 or n_row_groups == 1)
    segs_of = [range(g * group_rows // chunk, (g + 1) * group_rows // chunk) for g in range(n_row_groups)]
    if phase_major:
        pairs_per_group = hpg // pair
        assert hpg % pair == 0
        work = [(segs, gi) for gi in range(n_groups) for segs in segs_of]
        ahead = [[stage for j in range(gi * pairs_per_group, (gi + 1) * pairs_per_group) for stage in pair_stages(j)]
                 for gi in range(n_groups)]
        ahead.append([functools.partial(project, slice(cch + i * MXU_TILE, cch + (i + 1) * MXU_TILE))
                      for i in range(vw // MXU_TILE)])
        for stage in ahead[0]:
            stage()
        for gi in range(n_groups):
            factors = [stage for segs, g in work if g == gi for stage in wy_stages(segs, [gi])]
            _run_interleaved([ahead[gi + 1], factors])
    else:
        for segs in segs_of:
            for stage in wy_stages(segs, range(n_groups)):
                stage()
    for segs in segs_of:
        for stage in recurrence_stages(segs):
            stage()
    for g in range(n_row_groups):
        for stage in output_stages(pl.ds(g * group_rows, group_rows)):
            stage()
    for bh, value in state.items():
        ns_ref[bh[0], bh[1]] = value


def _odd_layer(x, mod, consts, state, *, tile, chunk, batch_block):
    bsz, seq, d = x.shape
    (npre, npost, win, wba, cw, alog, dtb, hn, wout) = consts
    cch = cw.shape[1]
    vw = wout.shape[0]
    n_heads = vw // GDN_DV
    n_tiles = seq // tile
    assert seq % tile == 0 and tile % chunk == 0 and bsz % batch_block == 0
    rows = batch_block * tile
    has_state = state is not None
    phase_major = _runs_phase_major(chunk, batch_block, has_state)

    def bspec(shape):
        nd = len(shape)
        return pl.BlockSpec((batch_block,) + shape, lambda b, t: (b,) + (0,) * nd)

    in_specs = [pl.BlockSpec((batch_block, tile, d), lambda b, t: (b, t, 0))]
    args = [x]
    if phase_major:
        perm_rows = min(rows, MXU_TILE)
        assert rows % perm_rows == 0 and perm_rows % chunk == 0
        in_specs.append(_const_spec((perm_rows, perm_rows)))
        args.append(_phase_major_matrix(perm_rows, chunk))
    in_specs += [
        bspec((3, d)),
        _const_spec(npre.shape), _const_spec(npost.shape), _const_spec(win.shape), _const_spec(wba.shape),
        _const_spec(cw.shape), _const_spec(alog.shape), _const_spec(dtb.shape), _const_spec(hn.shape),
        _const_spec(wout.shape),
    ]
    args += [mod, npre, npost, win, wba, cw, alog, dtb, hn, wout]
    if has_state:
        in_specs += [bspec((CONV_W - 1, cch)), bspec((n_heads, GDN_DK, GDN_DV))]
        args += list(state)
    out_specs = [
        pl.BlockSpec((batch_block, tile, d), lambda b, t: (b, t, 0)),
        bspec((CONV_W - 1, cch)), bspec((n_heads, GDN_DK, GDN_DV)),
    ]
    out_shape = [
        jax.ShapeDtypeStruct((bsz, seq, d), F32),
        jax.ShapeDtypeStruct((bsz, CONV_W - 1, cch), F32),
        jax.ShapeDtypeStruct((bsz, n_heads, GDN_DK, GDN_DV), F32),
    ]
    hpg, n_groups = _gdn_groups(chunk, n_heads)
    n_seg = rows // chunk
    scratch = (
        [pltpu.VMEM((rows, d), BF16),
         pltpu.VMEM((rows, win.shape[1]), F32),
         pltpu.VMEM(_conv_history_shape(phase_major, batch_block, tile, cch), F32)]
        + [pltpu.VMEM((rows, vw), BF16)] * 7
        + [pltpu.VMEM((rows, LANES), F32)] * 4
        + [pltpu.VMEM((n_seg * n_heads, 2 * chunk, GDN_DK), BF16),
           pltpu.VMEM((rows, vw), F32),
           pltpu.VMEM((n_seg * n_groups, chunk, hpg * chunk), BF16),
           pltpu.VMEM((rows, vw), BF16)]
    )
    body = functools.partial(_odd_body, has_state=has_state, chunk=chunk, phase_major=phase_major)
    return pl.pallas_call(
        body,
        grid=(bsz // batch_block, n_tiles),
        in_specs=in_specs,
        out_specs=out_specs,
        out_shape=out_shape,
        scratch_shapes=scratch,
        compiler_params=pltpu.CompilerParams(
            dimension_semantics=("arbitrary", "arbitrary"), vmem_limit_bytes=VMEM_LIMIT_BYTES),
        name="odd_layer_state" if has_state else "odd_layer_prompt",
    )(*args)


def _gate_weights(w_a, w_x):
    heads, blk, _ = w_a.shape
    per_tile = MXU_TILE // blk
    tiles = []
    for i in range(heads // per_tile):
        sl = slice(i * per_tile, (i + 1) * per_tile)
        tiles.append(jnp.concatenate(
            [jax.scipy.linalg.block_diag(*w_a[sl]), jax.scipy.linalg.block_diag(*w_x[sl])], axis=1))
    return jnp.stack(tiles).astype(BF16)


def _rope_inv_row():
    half = ROT_DIM // 2
    inv = ROPE_THETA ** (-(jnp.arange(half, dtype=F32) * 2.0 / ROT_DIM))
    per_head = jnp.concatenate([inv, inv, jnp.zeros((HEAD_DIM - ROT_DIM,), F32)])
    return jnp.tile(per_head, LANES // HEAD_DIM).reshape(1, LANES)


def kernel(x_prompt, x_sample, state_lru_conv, state_lru_h, cache_swa_k, cache_swa_v, state_gdn_conv, state_gdn_s, c_prompt, c_sample, ev_mod_w, ev_mod_b, ev_norm_pre, ev_norm_post, ev_w_in, lru_conv_w, lru_conv_b, lru_w_a, lru_b_a, lru_w_x, lru_b_x, lru_lambda, swa_sinks, ev_w_out, od_mod_w, od_mod_b, od_norm_pre, od_norm_post, od_w_in, gdn_conv_w, gdn_a_log, gdn_dt_bias, gdn_head_norm, od_w_out):
    bp, seq, d = x_prompt.shape
    bs, dec_seq, _ = x_sample.shape
    lw = lru_conv_w.shape[-1]
    kvw = ATT_KV_HEADS * HEAD_DIM
    n_heads = gdn_a_log.shape[-1]
    cch = gdn_conv_w.shape[-1]
    vw = od_w_out.shape[1]

    n_c = bp + bs
    n_pad = -n_c % 16
    c_all = jnp.concatenate([c_prompt, c_sample, jnp.zeros((n_pad, d), F32)], axis=0)
    mod_ev, mod_od = _modulation(c_all, ev_mod_w[0], ev_mod_b[0], od_mod_w[0], od_mod_b[0])
    mod_ev = mod_ev.reshape(n_c + n_pad, 3, d)
    mod_od = mod_od.reshape(n_c + n_pad, 3, d)

    row = lambda a: a.reshape(1, -1)
    ev_consts = (row(ev_norm_pre[0]), row(ev_norm_post[0]), ev_w_in[0].astype(BF16), lru_conv_w[0],
                 row(lru_conv_b[0]), _gate_weights(lru_w_a[0], lru_w_x[0]), row(lru_b_a[0]), row(lru_b_x[0]),
                 row(lru_lambda[0]), _rope_inv_row(), swa_sinks[0], ev_w_out[0].astype(BF16))
    w_in_od = od_w_in[0]
    pad_lanes = lambda a: jnp.pad(a, ((0, 0), (0, LANES - a.shape[1])))
    head_row = lambda a: jnp.pad(a.reshape(1, -1), ((0, 0), (n_heads, LANES - 2 * n_heads)))
    od_consts = (row(od_norm_pre[0]), row(od_norm_post[0]), w_in_od[:, :cch + vw].astype(BF16),
                 pad_lanes(w_in_od[:, cch + vw:]).astype(BF16), gdn_conv_w[0],
                 head_row(gdn_a_log[0]), head_row(gdn_dt_bias[0]), row(gdn_head_norm[0]),
                 od_w_out[0].astype(BF16))

    tile = min(PROMPT_TILE, seq)
    xp, lru_conv_p, lru_h_p, swa_k_p, swa_v_p = _even_layer(
        x_prompt, mod_ev[:bp], ev_consts, None, tile=tile, chunk=min(CHUNK, seq), batch_block=1,
        n_past_valid=0, pos0=0, k_out_rows=WINDOW)
    ev_state = (state_lru_conv[0], state_lru_h[0].reshape(bs, 1, lw),
                cache_swa_k[0].reshape(bs, WINDOW, kvw), cache_swa_v[0].reshape(bs, WINDOW, kvw))
    xs, lru_conv_s, lru_h_s, swa_k_s, swa_v_s = _even_layer(
        x_sample, mod_ev[bp:n_c], ev_consts, ev_state, tile=dec_seq, chunk=min(CHUNK, dec_seq), batch_block=bs,
        n_past_valid=WINDOW, pos0=PAST_LEN, k_out_rows=dec_seq)

    xp, gdn_conv_p, gdn_s_p = _odd_layer(
        xp, mod_od[:bp], od_consts, None, tile=tile, chunk=min(CHUNK, seq), batch_block=1)
    xs, gdn_conv_s, gdn_s_s = _odd_layer(
        xs, mod_od[bp:n_c], od_consts, (state_gdn_conv[0], state_gdn_s[0]),
        tile=dec_seq, chunk=min(CHUNK, dec_seq), batch_block=bs)

    kv_shape = lambda a: a.reshape(1, a.shape[0], a.shape[1], ATT_KV_HEADS, HEAD_DIM)
    return (xp, xs,
            lru_conv_p[None], lru_conv_s[None],
            lru_h_p.reshape(1, bp, lw), lru_h_s.reshape(1, bs, lw),
            kv_shape(swa_k_p), kv_shape(swa_k_s), kv_shape(swa_v_p), kv_shape(swa_v_s),
            gdn_conv_p[None], gdn_conv_s[None], gdn_s_p[None], gdn_s_s[None])
```

```python
import functools

import jax
import jax.numpy as jnp
from jax import lax
from jax.experimental import pallas as pl
from jax.experimental.pallas import tpu as pltpu

F32 = jnp.float32
BF16 = jnp.bfloat16

CHUNK = 64
EPS = 1e-6
CONV_W = 4
NEG_INF = -1e30
LRU_HEADS = 8
LRU_C = 8.0
HEAD_DIM = 64
ATT_KV_HEADS = 2
WINDOW = 128
ROT_DIM = HEAD_DIM // 4
ROPE_THETA = 500000.0
GDN_DK = 128
GDN_DV = 128
PAST_LEN = 4096

LANES = 128
SUBLANES = 8
MXU_TILE = 256
VMEM_LIMIT_BYTES = 56 * 1024 * 1024

PROMPT_TILE = 512
MOD_TILE = 512
CONV_PAD = SUBLANES


def _silu(x):
    return x * jax.nn.sigmoid(x)


def _expm1(x):
    u = jnp.exp(x)
    d = u - 1.0
    return jnp.where(u == 1.0, x, jnp.where(d == -1.0, -1.0, d * x / jnp.log(u)))


def _rms_scale(x):
    return lax.rsqrt(jnp.mean(x * x, axis=-1, keepdims=True) + EPS)


def _rms(x, g):
    return x * _rms_scale(x) * g


def _dot(a, b):
    return jnp.dot(a, b, preferred_element_type=F32)


def _dot_nt(a, b):
    return lax.dot_general(a, b, (((1,), (1,)), ((), ())), preferred_element_type=F32)


def _dot_tn(a, b):
    return lax.dot_general(a, b, (((0,), (0,)), ((), ())), preferred_element_type=F32)


def _for_each(n, body):
    if n == 1:
        body(0)
    else:
        def step(i, carry):
            body(i)
            return carry
        lax.fori_loop(0, n, step, 0)


def _rows(i, n, size=None):
    size = n if size is None else size
    if isinstance(i, int):
        return pl.ds(i * n, size)
    return pl.ds(pl.multiple_of(i * n, n), size)


def _mod_body(c_ref, w0_ref, b0_ref, w1_ref, b1_ref, o0_ref, o1_ref):
    c = _silu(c_ref[...]).astype(BF16)
    o0_ref[...] = _dot(c, w0_ref[...].astype(BF16)) + b0_ref[...]
    o1_ref[...] = _dot(c, w1_ref[...].astype(BF16)) + b1_ref[...]


def _modulation(c_all, w0, b0, w1, b1):
    n, d = c_all.shape
    d3 = w0.shape[1]
    wspec = pl.BlockSpec((d, MOD_TILE), lambda j: (0, j))
    bspec = pl.BlockSpec((1, MOD_TILE), lambda j: (0, j))
    ospec = pl.BlockSpec((n, MOD_TILE), lambda j: (0, j))
    return pl.pallas_call(
        _mod_body,
        grid=(d3 // MOD_TILE,),
        in_specs=[pl.BlockSpec((n, d), lambda j: (0, 0)), wspec, bspec, wspec, bspec],
        out_specs=[ospec, ospec],
        out_shape=[jax.ShapeDtypeStruct((n, d3), F32)] * 2,
        name="adaln_modulation",
    )(c_all, w0, b0.reshape(1, d3), w1, b1.reshape(1, d3))


def _pre_norm(x_ref, mod_ref, norm_pre_ref):
    bb, tc, d = x_ref.shape
    x = x_ref[...]
    gain = norm_pre_ref[...] * (1.0 + mod_ref[:, 1:2, :])
    h = x * _rms_scale(x) * gain + mod_ref[:, 0:1, :]
    return h.reshape(bb * tc, d)


def _post_residual(x_ref, mod_ref, norm_post_ref, y, o_ref, rows=None):
    bb, tc, d = x_ref.shape
    yn = y * _rms_scale(y)
    if rows is None:
        gain = mod_ref[:, 2:3, :] * norm_post_ref[...]
        o_ref[...] = x_ref[...] + gain * yn.reshape(bb, tc, d)
    else:
        assert bb == 1
        o_ref[0, rows, :] = x_ref[0, rows, :] + (mod_ref[0, 2:3, :] * norm_post_ref[...]) * yn


def _run_interleaved(stage_lists):
    tagged = [((i + 0.5) / len(stages), k, i, stage)
              for k, stages in enumerate(stage_lists) for i, stage in enumerate(stages)]
    for _, _, _, stage in sorted(tagged, key=lambda entry: entry[:3]):
        stage()


def _causal_conv(ext_ref, bi, tc, w_ref):
    ext = ext_ref[bi]
    acc = pltpu.roll(ext, CONV_W - 1, axis=0)[CONV_PAD:] * w_ref[0:1, :]
    for j in range(1, CONV_W - 1):
        acc = acc + pltpu.roll(ext, CONV_W - 1 - j, axis=0)[CONV_PAD:] * w_ref[j:j + 1, :]
    return acc + ext[CONV_PAD:] * w_ref[CONV_W - 1:CONV_W, :]


def _scan_phase_major(a, b, h0, chunk):
    tc, width = a.shape
    row8 = lax.broadcasted_iota(jnp.int32, (SUBLANES, width), 0)
    carry = h0
    out = []
    for c in range(tc // chunk):
        acc_a, acc_b = [], []
        for r in range(SUBLANES):
            blk = slice(c * chunk + r * SUBLANES, c * chunk + (r + 1) * SUBLANES)
            if r == 0:
                acc_a.append(a[blk])
                acc_b.append(b[blk])
            else:
                acc_b.append(a[blk] * acc_b[-1] + b[blk])
                acc_a.append(a[blk] * acc_a[-1])
        tot_a, tot_b = acc_a[-1], acc_b[-1]
        for sft in (1, 2, 4):
            keep = row8 >= sft
            tot_b = jnp.where(keep, tot_a * pltpu.roll(tot_b, sft, axis=0) + tot_b, tot_b)
            tot_a = jnp.where(keep, tot_a * pltpu.roll(tot_a, sft, axis=0), tot_a)
        h_out = tot_b + tot_a * carry
        h_in = jnp.where(row8 == 0, carry, pltpu.roll(h_out, 1, axis=0))
        out += [acc_b[r] + acc_a[r] * h_in for r in range(SUBLANES)]
        carry = h_out[SUBLANES - 1:SUBLANES, :]
    return jnp.concatenate(out, axis=0), carry


def _even_body(*refs, has_state, n_past_valid, pos0, chunk, n_tiles, k_out_rows, phase_major):
    refs = list(refs)
    x_ref = refs.pop(0)
    perm_ref = refs.pop(0) if phase_major else None
    (mod_ref, npre_ref, npost_ref, win_ref, cw_ref, cb_ref, wg_ref, ba_ref, bx_ref,
     lam_ref, inv_ref, sinks_ref, wout_ref) = refs[:13]
    refs = refs[13:]
    if has_state:
        conv0_ref, h0_ref, k0_ref, v0_ref = refs[:4]
        refs = refs[4:]
    (y_ref, nconv_ref, nh_ref, nk_ref, nv_ref,
     hm_ref, u_ref, ext_ref, a_ref, b_ref, hs_ref, knat_ref, vnat_ref, q_ref, mix_ref, cosl_ref, sinl_ref) = refs
    bb, tc, _ = x_ref.shape
    lw = a_ref.shape[1]
    aw = q_ref.shape[1]
    kvw = knat_ref.shape[2]
    t = pl.program_id(1)
    cpt = tc // chunk
    m = WINDOW + chunk
    o_q, o_k, o_v, o_gb = 2 * lw, 2 * lw + aw, 2 * lw + aw + kvw, 2 * lw + aw + 2 * kvw

    @pl.when(t == 0)
    def _():
        ext_ref[...] = jnp.zeros(ext_ref.shape, F32)
        if has_state:
            ext_ref[:, CONV_PAD - 3:CONV_PAD, :] = conv0_ref[...]
            nh_ref[...] = h0_ref[...]
            knat_ref[:, 0:WINDOW, :] = k0_ref[...]
            vnat_ref[:, 0:WINDOW, :] = v0_ref[...]
        else:
            nh_ref[...] = jnp.zeros(nh_ref.shape, F32)
            knat_ref[:, 0:WINDOW, :] = jnp.zeros((bb, WINDOW, kvw), F32)
            vnat_ref[:, 0:WINDOW, :] = jnp.zeros((bb, WINDOW, kvw), F32)

    def permute_rows(v):
        n = perm_ref.shape[0]
        return jnp.concatenate(
            [_dot(perm_ref[...], v[i * n:(i + 1) * n]).astype(BF16) for i in range(v.shape[0] // n)], axis=0)

    hmod_time = _pre_norm(x_ref, mod_ref, npre_ref).astype(BF16)
    hm_ref[...] = permute_rows(hmod_time) if phase_major else hmod_time

    tile_start = (pos0 + t * tc).astype(F32) * inv_ref[...]
    cos_start, sin_start = jnp.cos(tile_start), jnp.sin(tile_start)

    def make_rope(cos_in_tile, sin_in_tile):
        cos_t = cos_start * cos_in_tile - sin_start * sin_in_tile
        sin_t = sin_start * cos_in_tile + cos_start * sin_in_tile
        lane = lax.broadcasted_iota(jnp.int32, cos_t.shape, 1) % HEAD_DIM
        half = ROT_DIM // 2
        sin_a = jnp.where(lane < half, -sin_t, 0.0)
        sin_b = jnp.where(lane >= half, sin_t, 0.0)

        def rope(xcol):
            return (xcol * cos_t + pltpu.roll(xcol, LANES - half, axis=1) * sin_a
                    + pltpu.roll(xcol, half, axis=1) * sin_b)
        return rope

    @pl.when(t == 0)
    def _():
        row_t = lax.broadcasted_iota(jnp.int32, (tc, LANES), 0)
        if phase_major:
            row_t = (row_t // chunk) * chunk + _time_of_row(row_t % chunk)
        in_tile = row_t.astype(F32) * inv_ref[...]
        cosl_ref[...] = jnp.cos(in_tile)
        sinl_ref[...] = jnp.sin(in_tile)

    rope = make_rope(cosl_ref[...], sinl_ref[...])

    neg_c_softplus = -LRU_C * jax.nn.softplus(-lam_ref[...])
    row8 = lax.broadcasted_iota(jnp.int32, (SUBLANES, lw), 0)

    def project(cols):
        u_ref[:, cols] = _dot(hm_ref[...], win_ref[:, cols])

    def recurrence_inputs(bi):
        rows = _rows(bi, tc)
        if phase_major:
            xc = _conv_phase_major(u_ref, ext_ref, nconv_ref, cw_ref, tc, chunk, slice(0, lw)) + cb_ref[...]
        else:
            ext_ref[bi, CONV_PAD:CONV_PAD + tc, :] = u_ref[rows, 0:lw]
            xc = _causal_conv(ext_ref, bi, tc, cw_ref) + cb_ref[...]
            nconv_ref[bi, :, :] = ext_ref[bi, CONV_PAD + tc - 3:CONV_PAD + tc, :]
            ext_ref[bi, CONV_PAD - 3:CONV_PAD, :] = ext_ref[bi, CONV_PAD + tc - 3:CONV_PAD + tc, :]
        xcb = xc.astype(BF16)
        halves = [_dot(xcb[:, i * MXU_TILE:(i + 1) * MXU_TILE], wg_ref[i]) for i in range(lw // MXU_TILE)]
        r = jax.nn.sigmoid(jnp.concatenate([g[:, :MXU_TILE] for g in halves], axis=1) + ba_ref[...])
        ig = jax.nn.sigmoid(jnp.concatenate([g[:, MXU_TILE:] for g in halves], axis=1) + bx_ref[...])
        log_a = r * neg_c_softplus
        a_ref[...] = jnp.exp(log_a)
        b_ref[...] = jnp.sqrt(-_expm1(2.0 * log_a)) * (ig * xc)

    def recurrence(bi):
        if phase_major:
            hs_ref[...], nh_ref[bi, :, :] = _scan_phase_major(a_ref[...], b_ref[...], nh_ref[bi, :, :], chunk)
            return

        def scan_block(j, hc):
            r0 = pl.multiple_of(j * SUBLANES, SUBLANES)
            a = a_ref[pl.ds(r0, SUBLANES), :]
            b = b_ref[pl.ds(r0, SUBLANES), :]
            for sft in (1, 2, 4):
                keep = row8 >= sft
                b = jnp.where(keep, a * pltpu.roll(b, sft, axis=0) + b, b)
                a = jnp.where(keep, a * pltpu.roll(a, sft, axis=0), a)
            h = a * hc + b
            hs_ref[pl.ds(r0, SUBLANES), :] = h
            return h[SUBLANES - 1:SUBLANES, :]

        nh_ref[bi, :, :] = lax.fori_loop(0, tc // SUBLANES, scan_block, nh_ref[bi, :, :])

    def recurrence_output(bi):
        rows = _rows(bi, tc)
        mix_ref[rows, 0:lw] = (hs_ref[...] * _silu(u_ref[rows, lw:2 * lw])).astype(BF16)

    def attention_inputs(bi):
        rows = _rows(bi, tc)
        for j in range(aw // LANES):
            qcol = rope(u_ref[rows, o_q + j * LANES:o_q + (j + 1) * LANES])
            q_ref[rows, j * LANES:(j + 1) * LANES] = (qcol * (HEAD_DIM ** -0.5)).astype(BF16)
        knat_ref[bi, WINDOW:WINDOW + tc, :] = rope(u_ref[rows, o_k:o_k + kvw])
        vnat_ref[bi, WINDOW:WINDOW + tc, :] = u_ref[rows, o_v:o_v + kvw]
        if not phase_major:
            nk_ref[bi, :, :] = knat_ref[bi, WINDOW + tc - k_out_rows:WINDOW + tc, :]
            nv_ref[bi, :, :] = vnat_ref[bi, WINDOW + tc - k_out_rows:WINDOW + tc, :]

    def per_batch(bi):
        recurrence_inputs(bi)
        recurrence(bi)
        recurrence_output(bi)
        attention_inputs(bi)

    if phase_major:
        project(slice(0, lw))
        later = [slice(lw, 2 * lw), slice(o_q, o_k), slice(o_k, o_gb), slice(o_gb, o_gb + aw)]
        _run_interleaved([[functools.partial(project, cols) for cols in later],
                          [functools.partial(recurrence_inputs, 0), functools.partial(recurrence, 0)]])
        recurrence_output(0)
        attention_inputs(0)
    else:
        project(slice(0, win_ref.shape[1]))
        _for_each(bb, per_batch)

    if phase_major:
        @pl.when(t == n_tiles - 1)
        def _():
            first = tc - k_out_rows
            kv = _dot(hmod_time[first:], win_ref[:, o_k:o_k + 2 * kvw])
            in_tile = (first + lax.broadcasted_iota(jnp.int32, (k_out_rows, LANES), 0)).astype(F32) * inv_ref[...]
            rope_last = make_rope(jnp.cos(in_tile), jnp.sin(in_tile))
            nk_ref[0] = rope_last(kv[:, 0:kvw])
            nv_ref[0] = kv[:, kvw:2 * kvw]

    lo_lane = lax.broadcasted_iota(jnp.int32, (m, LANES), 1) < HEAD_DIM
    lo_q = lax.broadcasted_iota(jnp.int32, (chunk, LANES), 1) < HEAD_DIM
    group = (aw // HEAD_DIM) // ATT_KV_HEADS

    sinks = [jnp.concatenate([jnp.full((chunk, 1), sinks_ref[h * group + g], F32) for g in range(group)], axis=0)
             for h in range(ATT_KV_HEADS)]

    lane_m = lax.broadcasted_iota(jnp.int32, (m, LANES), 1)

    def attend():
        units = []
        for bi, c in [(bi, c) for bi in range(bb) for c in range(cpt)]:
            rows = _rows(bi * cpt + c, chunk)
            krows = _rows(c, chunk, m)
            kseg = knat_ref[bi, krows, :]
            vseg = vnat_ref[bi, krows, :]
            krot = pltpu.roll(kseg, HEAD_DIM, axis=1)
            vrot = pltpu.roll(vseg, HEAD_DIM, axis=1)
            valid = None
            if n_past_valid < WINDOW:
                key_t = lax.broadcasted_iota(jnp.int32, (1, m), 1)
                if phase_major:
                    key_t = (key_t // chunk) * chunk + _time_of_row(key_t % chunk)
                valid = t * tc + c * chunk - WINDOW + key_t >= -n_past_valid
            for h in range(ATT_KV_HEADS):
                kd = (jnp.where(lo_lane, kseg, krot) if h == 0 else jnp.where(lo_lane, krot, kseg)).astype(BF16)
                v_lo = jnp.where(lo_lane, vseg if h == 0 else vrot, jnp.where(lane_m == HEAD_DIM, 1.0, 0.0))
                v_hi = jnp.where(lo_lane, jnp.where(lane_m == 0, 1.0, 0.0), vrot if h == 0 else vseg)
                units.append(dict(rows=rows, h=h, valid=valid, kd=kd, v_lo=v_lo.astype(BF16),
                                  v_hi=v_hi.astype(BF16)))

        for un in units:
            rows, h = un["rows"], un["h"]
            cols = [q_ref[rows, (h * group // 2 + j) * LANES:(h * group // 2 + j + 1) * LANES]
                    for j in range(group // 2)]
            zero = jnp.zeros_like(cols[0])
            qstack = jnp.concatenate(
                [part for qc in cols for part in (jnp.where(lo_q, qc, zero), jnp.where(lo_q, zero, qc))], axis=0)
            sc = _dot_nt(qstack, un["kd"])
            un["sc"] = sc if un["valid"] is None else jnp.where(un["valid"], sc, NEG_INF)
        for un in units:
            un["mx"] = jnp.maximum(jnp.max(un["sc"], axis=-1, keepdims=True), sinks[un["h"]])
        for un in units:
            un["p"] = jnp.exp(un["sc"] - un["mx"]).astype(BF16)
            un["sink_p"] = jnp.exp(sinks[un["h"]] - un["mx"])
        for un in units:
            rows, h, p, sink_p = un["rows"], un["h"], un["p"], un["sink_p"]
            for j in range(group // 2):
                r_lo = slice((2 * j) * chunk, (2 * j + 1) * chunk)
                r_hi = slice((2 * j + 1) * chunk, (2 * j + 2) * chunk)
                o_lo = _dot(p[r_lo], un["v_lo"])
                o_hi = _dot(p[r_hi], un["v_hi"])
                den_lo = o_lo[:, HEAD_DIM:HEAD_DIM + 1] + sink_p[r_lo]
                den_hi = o_hi[:, 0:1] + sink_p[r_hi]
                o = jnp.where(lo_q, o_lo / den_lo, o_hi / den_hi)
                col = h * group // 2 + j
                gate = _silu(u_ref[rows, o_gb + col * LANES:o_gb + (col + 1) * LANES])
                mix_ref[rows, lw + col * LANES:lw + (col + 1) * LANES] = (o * gate).astype(BF16)

    attend()

    if n_tiles > 1:
        knat_ref[:, 0:WINDOW, :] = knat_ref[:, tc:tc + WINDOW, :]
        vnat_ref[:, 0:WINDOW, :] = vnat_ref[:, tc:tc + WINDOW, :]

    mixed = mix_ref[...]
    if phase_major:
        mixed = permute_rows(mixed)
    y = _dot(mixed, wout_ref[...])
    _post_residual(x_ref, mod_ref, npost_ref, y, y_ref)


def _runs_phase_major(chunk, batch_block, has_state):
    return chunk == SUBLANES * SUBLANES and batch_block == 1 and not has_state


def _conv_history_shape(phase_major, batch_block, tile, channels):
    if phase_major:
        return (CONV_W - 1, SUBLANES, channels)
    return (batch_block, CONV_PAD + tile, channels)


def _const_spec(shape):
    zeros = (0,) * len(shape)
    return pl.BlockSpec(shape, lambda b, t: zeros)


def _even_layer(x, mod, consts, state, *, tile, chunk, batch_block, n_past_valid, pos0, k_out_rows):
    bsz, seq, d = x.shape
    (npre, npost, win, cw, cb, wg, b_a, b_x, lam, inv_row, sinks, wout) = consts
    lw = cw.shape[1]
    kvw = ATT_KV_HEADS * HEAD_DIM
    aw = (win.shape[1] - 2 * lw - 2 * kvw) // 2
    n_tiles = seq // tile
    assert seq % tile == 0 and tile % chunk == 0 and bsz % batch_block == 0
    assert n_tiles == 1 or tile >= WINDOW
    rows = batch_block * tile
    has_state = state is not None
    phase_major = _runs_phase_major(chunk, batch_block, has_state)
    assert not phase_major or tile >= k_out_rows

    def bspec(shape):
        nd = len(shape)
        return pl.BlockSpec((batch_block,) + shape, lambda b, t: (b,) + (0,) * nd)

    in_specs = [pl.BlockSpec((batch_block, tile, d), lambda b, t: (b, t, 0))]
    args = [x]
    if phase_major:
        perm = _phase_major_matrix(min(rows, MXU_TILE), chunk)
        assert rows % perm.shape[0] == 0
        in_specs.append(_const_spec(perm.shape))
        args.append(perm)
    in_specs += [
        bspec((3, d)),
        _const_spec(npre.shape), _const_spec(npost.shape), _const_spec(win.shape),
        _const_spec(cw.shape), _const_spec(cb.shape), _const_spec(wg.shape),
        _const_spec(b_a.shape), _const_spec(b_x.shape), _const_spec(lam.shape), _const_spec(inv_row.shape),
        pl.BlockSpec(memory_space=pltpu.SMEM),
        _const_spec(wout.shape),
    ]
    args += [mod, npre, npost, win, cw, cb, wg, b_a, b_x, lam, inv_row, sinks, wout]
    if has_state:
        in_specs += [bspec((CONV_W - 1, lw)), bspec((1, lw)), bspec((WINDOW, kvw)), bspec((WINDOW, kvw))]
        args += list(state)
    out_specs = [
        pl.BlockSpec((batch_block, tile, d), lambda b, t: (b, t, 0)),
        bspec((CONV_W - 1, lw)), bspec((1, lw)), bspec((k_out_rows, kvw)), bspec((k_out_rows, kvw)),
    ]
    out_shape = [
        jax.ShapeDtypeStruct((bsz, seq, d), F32),
        jax.ShapeDtypeStruct((bsz, CONV_W - 1, lw), F32),
        jax.ShapeDtypeStruct((bsz, 1, lw), F32),
        jax.ShapeDtypeStruct((bsz, k_out_rows, kvw), F32),
        jax.ShapeDtypeStruct((bsz, k_out_rows, kvw), F32),
    ]
    scratch = [
        pltpu.VMEM((rows, d), BF16),
        pltpu.VMEM((rows, win.shape[1]), F32),
        pltpu.VMEM(_conv_history_shape(phase_major, batch_block, tile, lw), F32),
        pltpu.VMEM((tile, lw), F32), pltpu.VMEM((tile, lw), F32), pltpu.VMEM((tile, lw), F32),
        pltpu.VMEM((batch_block, WINDOW + tile, kvw), F32),
        pltpu.VMEM((batch_block, WINDOW + tile, kvw), F32),
        pltpu.VMEM((rows, aw), BF16),
        pltpu.VMEM((rows, lw + aw), BF16),
        pltpu.VMEM((tile, LANES), F32), pltpu.VMEM((tile, LANES), F32),
    ]
    body = functools.partial(_even_body, has_state=has_state, n_past_valid=n_past_valid, pos0=pos0,
                             chunk=chunk, n_tiles=n_tiles, k_out_rows=k_out_rows, phase_major=phase_major)
    return pl.pallas_call(
        body,
        grid=(bsz // batch_block, n_tiles),
        in_specs=in_specs,
        out_specs=out_specs,
        out_shape=out_shape,
        scratch_shapes=scratch,
        compiler_params=pltpu.CompilerParams(
            dimension_semantics=("arbitrary", "arbitrary"), vmem_limit_bytes=VMEM_LIMIT_BYTES),
        name="even_layer_state" if has_state else "even_layer_prompt",
    )(*args)


def _gdn_groups(chunk, n_heads):
    hpg = max(1, min(n_heads, MXU_TILE // chunk))
    assert n_heads % hpg == 0
    return hpg, n_heads // hpg


def _time_of_row(i):
    return (i % SUBLANES) * SUBLANES + i // SUBLANES


def _phase_major_matrix(n_rows, chunk):
    i = jnp.arange(n_rows)
    src = (i // chunk) * chunk + _time_of_row(i % chunk)
    return (src[:, None] == i[None, :]).astype(BF16)


def _conv_phase_major(u_ref, hist_ref, nconv_ref, w_ref, tc, chunk, cols):
    cpt = tc // chunk
    blocks = [u_ref[SUBLANES * b:SUBLANES * (b + 1), cols] for b in range(tc // SUBLANES)]
    taps_w = [w_ref[j:j + 1, cols] for j in range(CONV_W)]
    top_row = lax.broadcasted_iota(jnp.int32, blocks[0].shape, 0) == 0
    first_late = SUBLANES - (CONV_W - 1)
    prev = {r: pltpu.roll(hist_ref[r - first_late, :, cols], 1, axis=0) for r in range(first_late, SUBLANES)}
    out = []
    for c in range(cpt):
        late = {}
        for r in range(first_late, SUBLANES):
            moved = pltpu.roll(blocks[SUBLANES * c + r], 1, axis=0)
            late[r] = jnp.where(top_row, prev[r], moved)
            prev[r] = moved
        for r in range(SUBLANES):
            acc = None
            for j in range(CONV_W):
                s = CONV_W - 1 - j
                tap = blocks[SUBLANES * c + r - s] if r >= s else late[r - s + SUBLANES]
                acc = tap * taps_w[j] if acc is None else acc + tap * taps_w[j]
            out.append(acc)
    for i in range(CONV_W - 1):
        last = blocks[SUBLANES * (cpt - 1) + first_late + i]
        hist_ref[i, :, cols] = last
        nconv_ref[0, i:i + 1, cols] = last[SUBLANES - 1:SUBLANES, :]
    return jnp.concatenate(out, axis=0)


def _odd_body(*refs, has_state, chunk, phase_major):
    refs = list(refs)
    x_ref = refs.pop(0)
    perm_ref = refs.pop(0) if phase_major else None
    (mod_ref, npre_ref, npost_ref, win_ref, wba_ref, cw_ref, alog_ref, dtb_ref, hn_ref, wout_ref) = refs[:10]
    refs = refs[10:]
    if has_state:
        conv0_ref, s0_ref = refs[:2]
        refs = refs[2:]
    (y_ref, nconv_ref, ns_ref,
     hm_ref, u_ref, ext_ref, q_ref, k_ref, kb_ref, kbe_ref, vb_ref, qe_ref, kd_ref,
     beta_ref, gcum_ref, eg_ref, dec_ref, wq_ref, uu_ref, qk_ref, mix_ref) = refs
    bb, tc, _ = x_ref.shape
    n_heads = ns_ref.shape[1]
    kw = n_heads * GDN_DK
    vw = n_heads * GDN_DV
    cch = 2 * kw + vw
    rows_all = bb * tc
    t = pl.program_id(1)
    cpt = tc // chunk
    n_steps = (chunk - 1).bit_length()
    assert n_steps >= 2
    hpg, n_groups = _gdn_groups(chunk, n_heads)
    lw = hpg * chunk

    @pl.when(t == 0)
    def _():
        ext_ref[...] = jnp.zeros(ext_ref.shape, F32)
        if has_state:
            ext_ref[:, CONV_PAD - 3:CONV_PAD, :] = conv0_ref[...]
            ns_ref[...] = s0_ref[...]
        else:
            ns_ref[...] = jnp.zeros(ns_ref.shape, F32)

    def permute_rows(v):
        n = perm_ref.shape[0]
        return jnp.concatenate(
            [_dot(perm_ref[...], v[i * n:(i + 1) * n]).astype(BF16) for i in range(v.shape[0] // n)], axis=0)

    hmod = _pre_norm(x_ref, mod_ref, npre_ref).astype(BF16)
    if phase_major:
        hmod = permute_rows(hmod)
    hm_ref[...] = hmod
    ba = _dot(hmod, wba_ref[...])
    beta_ref[...] = jax.nn.sigmoid(ba)
    g = -jnp.exp(alog_ref[...]) * jax.nn.softplus(ba + dtb_ref[...])
    if phase_major:
        row8 = lax.broadcasted_iota(jnp.int32, (SUBLANES, LANES), 0)
        parts = []
        for c in range(rows_all // chunk):
            run = []
            for r in range(SUBLANES):
                blk = g[c * chunk + r * SUBLANES:c * chunk + (r + 1) * SUBLANES]
                run.append(blk if r == 0 else run[-1] + blk)
            total = run[-1]
            incl = total
            for sft in (1, 2, 4):
                incl = jnp.where(row8 >= sft, incl + pltpu.roll(incl, sft, axis=0), incl)
            parts += [blk + (incl - total) for blk in run]
        g = jnp.concatenate(parts, axis=0)
    else:
        row_in_chunk = lax.broadcasted_iota(jnp.int32, (rows_all, LANES), 0) % chunk
        sft = 1
        while sft < chunk:
            g = jnp.where(row_in_chunk >= sft, g + pltpu.roll(g, sft, axis=0), g)
            sft *= 2
    gcum_ref[...] = g
    eg_ref[...] = jnp.exp(g)
    g3 = g.reshape(rows_all // chunk, chunk, LANES)
    dec_ref[...] = jnp.exp(g3[:, chunk - 1:chunk, :] - g3).reshape(rows_all, LANES)

    def project(cols):
        u_ref[:, cols] = _dot(hm_ref[...], win_ref[:, cols])

    def head_operands(rows, h, qh, kh, vh):
        hc = slice(h * GDN_DK, (h + 1) * GDN_DK)
        qh = qh * (lax.rsqrt(jnp.sum(qh * qh, axis=-1, keepdims=True) + EPS) * (GDN_DK ** -0.5))
        kh = kh * lax.rsqrt(jnp.sum(kh * kh, axis=-1, keepdims=True) + EPS)
        beta = beta_ref[rows, :][:, h:h + 1]
        eg = eg_ref[rows, :][:, n_heads + h:n_heads + h + 1]
        kb = kh * beta
        q_ref[rows, hc] = qh.astype(BF16)
        k_ref[rows, hc] = kh.astype(BF16)
        kb_ref[rows, hc] = kb.astype(BF16)
        kbe_ref[rows, hc] = (kb * eg).astype(BF16)
        vb_ref[rows, hc] = (vh * beta).astype(BF16)
        qe_ref[rows, hc] = (qh * eg).astype(BF16)
        kd_ref[rows, hc] = (kh * dec_ref[rows, :][:, n_heads + h:n_heads + h + 1]).astype(BF16)

    def per_batch(bi):
        rows = _rows(bi, tc)
        ext_ref[bi, CONV_PAD:CONV_PAD + tc, :] = u_ref[rows, 0:cch]
        qkv = _silu(_causal_conv(ext_ref, bi, tc, cw_ref))
        nconv_ref[bi, :, :] = ext_ref[bi, CONV_PAD + tc - 3:CONV_PAD + tc, :]
        ext_ref[bi, CONV_PAD - 3:CONV_PAD, :] = ext_ref[bi, CONV_PAD + tc - 3:CONV_PAD + tc, :]
        for h in range(n_heads):
            head_operands(rows, h, qkv[:, h * GDN_DK:(h + 1) * GDN_DK],
                          qkv[:, kw + h * GDN_DK:kw + (h + 1) * GDN_DK],
                          qkv[:, 2 * kw + h * GDN_DV:2 * kw + (h + 1) * GDN_DV])

    pair = MXU_TILE // GDN_DK

    def pair_stages(j):
        col_sets = [slice(base + j * MXU_TILE, base + (j + 1) * MXU_TILE) for base in (0, kw, 2 * kw)]

        def operands():
            rows = pl.ds(0, tc)
            q2, k2, v2 = [_silu(_conv_phase_major(u_ref, ext_ref, nconv_ref, cw_ref, tc, chunk, cols))
                          for cols in col_sets]
            for i in range(pair):
                part = slice(i * GDN_DK, (i + 1) * GDN_DK)
                head_operands(rows, pair * j + i, q2[:, part], k2[:, part], v2[:, part])

        return [functools.partial(project, cols) for cols in col_sets] + [operands]

    if not phase_major:
        project(slice(0, cch + vw))
        _for_each(bb, per_batch)

    lane_w = lax.broadcasted_iota(jnp.int32, (chunk, lw), 1)
    row_w = lax.broadcasted_iota(jnp.int32, (chunk, lw), 0)
    blk_w = lane_w // chunk
    col_w = lane_w % chunk
    diag_w = row_w == col_w
    if phase_major:
        row_w, col_w = _time_of_row(row_w), _time_of_row(col_w)
    incl_w = row_w >= col_w
    strict_w = row_w > col_w
    eye_w = jnp.where(diag_w, 1.0, 0.0).astype(F32)
    feat_blk = lax.broadcasted_iota(jnp.int32, (chunk, hpg * GDN_DK), 1) // GDN_DK

    def block_diag(m_b):
        return jnp.concatenate([jnp.where(blk_w == hh, m_b, jnp.zeros_like(m_b)) for hh in range(hpg)], axis=0)

    def head_cols(h):
        return slice(h * GDN_DK, (h + 1) * GDN_DK)

    def wy_stages(segs, head_groups):
        units = []

        def build():
            for seg in segs:
                rows = _rows(seg, chunk)
                gt = gcum_ref[rows, :]
                for gi in head_groups:
                    gcols = slice(gi * hpg * GDN_DK, (gi + 1) * hpg * GDN_DK)
                    k4 = k_ref[rows, gcols]
                    k_bd = jnp.concatenate(
                        [jnp.where(feat_blk == hh, k4, jnp.zeros_like(k4)) for hh in range(hpg)], axis=0)
                    kk = _dot_nt(jnp.concatenate([kb_ref[rows, gcols], q_ref[rows, gcols]], axis=0), k_bd)
                    gcol = jnp.zeros((chunk, lw), F32)
                    for hh in range(hpg):
                        ln = n_heads + gi * hpg + hh
                        gcol = jnp.where(blk_w == hh, gt[:, ln:ln + 1], gcol)
                    grow = jnp.sum(jnp.where(diag_w, gcol, 0.0), axis=0, keepdims=True)
                    decay = jnp.exp(jnp.where(incl_w, gcol - grow, 0.0))
                    p_b = (-(kk[:chunk] * jnp.where(strict_w, decay, 0.0))).astype(BF16)
                    qk_ref[seg * n_groups + gi] = (kk[chunk:] * jnp.where(incl_w, decay, 0.0)).astype(BF16)
                    units.append(dict(seg=seg, rows=rows, gi=gi, p=p_b, t=eye_w + p_b.astype(F32)))

        def square():
            for un in units:
                un["p"] = _dot(un["p"], block_diag(un["p"])).astype(BF16)

        def extend_and_square():
            for un in units:
                out = _dot(jnp.concatenate([un["t"].astype(BF16), un["p"]], axis=0), block_diag(un["p"]))
                un["t"] = un["t"] + out[:chunk]
                un["p"] = out[chunk:].astype(BF16)

        def extend():
            for un in units:
                un["t"] = un["t"] + _dot(un["t"].astype(BF16), block_diag(un["p"]))

        def apply():
            for un in units:
                seg, rows, gi = un["seg"], un["rows"], un["gi"]
                heads = range(gi * hpg, (gi + 1) * hpg)
                rhs = jnp.concatenate(
                    [jnp.concatenate([kbe_ref[rows, head_cols(h)], vb_ref[rows, head_cols(h)]], axis=1)
                     for h in heads], axis=0)
                wu = _dot(block_diag(un["t"].astype(BF16)), rhs)
                for hh, h in enumerate(heads):
                    blk = wu[hh * chunk:(hh + 1) * chunk]
                    wq_ref[seg * n_heads + h] = jnp.concatenate(
                        [blk[:, :GDN_DK].astype(BF16), qe_ref[rows, head_cols(h)]], axis=0)
                    uu_ref[rows, head_cols(h)] = blk[:, GDN_DK:]

        return [build, square] + [extend_and_square] * (n_steps - 2) + [extend, apply]

    zero_b = jnp.zeros((chunk, GDN_DV), BF16)
    state = {(bi, h): ns_ref[bi, h] for bi in range(bb) for h in range(n_heads)}

    def recurrence_stages(segs):
        stages = []
        for c in sorted({seg % cpt for seg in segs}):
            wave = [seg for seg in segs if seg % cpt == c]
            pairs = [(seg, h) for seg in wave for h in range(n_heads)]
            held = {}

            def correct(wave=wave, pairs=pairs, held=held):
                for seg, h in pairs:
                    held["ws_qs", seg, h] = _dot(wq_ref[seg * n_heads + h], state[seg // cpt, h].astype(BF16))
                for seg, h in pairs:
                    held["v", seg, h] = (uu_ref[_rows(seg, chunk), head_cols(h)]
                                         - held["ws_qs", seg, h][:chunk]).astype(BF16)

            def advance(wave=wave, pairs=pairs, held=held):
                for seg in wave:
                    held["eg", seg] = jnp.exp(gcum_ref[pl.ds(seg * chunk + chunk - 1, 1), :])
                for seg, h in pairs:
                    state[seg // cpt, h] = (state[seg // cpt, h] * held["eg", seg][:, n_heads + h:n_heads + h + 1]
                                            + _dot_tn(kd_ref[_rows(seg, chunk), head_cols(h)], held["v", seg, h]))
                for seg in wave:
                    rows = _rows(seg, chunk)
                    for gi in range(n_groups):
                        heads = range(gi * hpg, (gi + 1) * hpg)
                        v_bd = jnp.concatenate(
                            [jnp.concatenate([held["v", seg, h] if j == hh else zero_b for j in range(hpg)], axis=1)
                             for hh, h in enumerate(heads)], axis=0)
                        o = (jnp.concatenate([held["ws_qs", seg, h][chunk:] for h in heads], axis=1)
                             + _dot(qk_ref[seg * n_groups + gi], v_bd))
                        for hh, h in enumerate(heads):
                            gate = _silu(u_ref[rows, cch + h * GDN_DV:cch + (h + 1) * GDN_DV])
                            o_h = _rms(o[:, hh * GDN_DV:(hh + 1) * GDN_DV], hn_ref[...])
                            mix_ref[rows, head_cols(h)] = (o_h * gate).astype(BF16)

            stages += [correct, advance]
        return stages

    def output_stages(rows):
        held = {}

        def gather():
            mixed = mix_ref[rows, :]
            held["mixed"] = permute_rows(mixed) if phase_major else mixed

        def project():
            held["y"] = _dot(held["mixed"], wout_ref[...])

        def finish():
            _post_residual(x_ref, mod_ref, npost_ref, held["y"], y_ref, rows if n_row_groups > 1 else None)

        return [gather, project, finish]

    group_rows = min(rows_all, MXU_TILE)
    n_row_groups = rows_all // group_rows
    assert rows_all % group_rows == 0 and group_rows % chunk == 0 and (bb == 1 or n_row_groups == 1)
    segs_of = [range(g * group_rows // chunk, (g + 1) * group_rows // chunk) for g in range(n_row_groups)]
    if phase_major:
        pairs_per_group = hpg // pair
        assert hpg % pair == 0
        work = [(segs, gi) for gi in range(n_groups) for segs in segs_of]
        ahead = [[stage for j in range(gi * pairs_per_group, (gi + 1) * pairs_per_group) for stage in pair_stages(j)]
                 for gi in range(n_groups)]
        ahead.append([functools.partial(project, slice(cch + i * MXU_TILE, cch + (i + 1) * MXU_TILE))
                      for i in range(vw // MXU_TILE)])
        for stage in ahead[0]:
            stage()
        for gi in range(n_groups):
            factors = [stage for segs, g in work if g == gi for stage in wy_stages(segs, [gi])]
            _run_interleaved([ahead[gi + 1], factors])
    else:
        for segs in segs_of:
            for stage in wy_stages(segs, range(n_groups)):
                stage()
    for segs in segs_of:
        for stage in recurrence_stages(segs):
            stage()
    for g in range(n_row_groups):
        for stage in output_stages(pl.ds(g * group_rows, group_rows)):
            stage()
    for bh, value in state.items():
        ns_ref[bh[0], bh[1]] = value


def _odd_layer(x, mod, consts, state, *, tile, chunk, batch_block):
    bsz, seq, d = x.shape
    (npre, npost, win, wba, cw, alog, dtb, hn, wout) = consts
    cch = cw.shape[1]
    vw = wout.shape[0]
    n_heads = vw // GDN_DV
    n_tiles = seq // tile
    assert seq % tile == 0 and tile % chunk == 0 and bsz % batch_block == 0
    rows = batch_block * tile
    has_state = state is not None
    phase_major = _runs_phase_major(chunk, batch_block, has_state)

    def bspec(shape):
        nd = len(shape)
        return pl.BlockSpec((batch_block,) + shape, lambda b, t: (b,) + (0,) * nd)

    in_specs = [pl.BlockSpec((batch_block, tile, d), lambda b, t: (b, t, 0))]
    args = [x]
    if phase_major:
        perm_rows = min(rows, MXU_TILE)
        assert rows % perm_rows == 0 and perm_rows % chunk == 0
        in_specs.append(_const_spec((perm_rows, perm_rows)))
        args.append(_phase_major_matrix(perm_rows, chunk))
    in_specs += [
        bspec((3, d)),
        _const_spec(npre.shape), _const_spec(npost.shape), _const_spec((d, cch + vw)), _const_spec(wba.shape),
        _const_spec(cw.shape), _const_spec(alog.shape), _const_spec(dtb.shape), _const_spec(hn.shape),
        _const_spec(wout.shape),
    ]
    args += [mod, npre, npost, win, wba, cw, alog, dtb, hn, wout]
    if has_state:
        in_specs += [bspec((CONV_W - 1, cch)), bspec((n_heads, GDN_DK, GDN_DV))]
        args += list(state)
    out_specs = [
        pl.BlockSpec((batch_block, tile, d), lambda b, t: (b, t, 0)),
        bspec((CONV_W - 1, cch)), bspec((n_heads, GDN_DK, GDN_DV)),
    ]
    out_shape = [
        jax.ShapeDtypeStruct((bsz, seq, d), F32),
        jax.ShapeDtypeStruct((bsz, CONV_W - 1, cch), F32),
        jax.ShapeDtypeStruct((bsz, n_heads, GDN_DK, GDN_DV), F32),
    ]
    hpg, n_groups = _gdn_groups(chunk, n_heads)
    n_seg = rows // chunk
    scratch = (
        [pltpu.VMEM((rows, d), BF16),
         pltpu.VMEM((rows, cch + vw), F32),
         pltpu.VMEM(_conv_history_shape(phase_major, batch_block, tile, cch), F32)]
        + [pltpu.VMEM((rows, vw), BF16)] * 7
        + [pltpu.VMEM((rows, LANES), F32)] * 4
        + [pltpu.VMEM((n_seg * n_heads, 2 * chunk, GDN_DK), BF16),
           pltpu.VMEM((rows, vw), F32),
           pltpu.VMEM((n_seg * n_groups, chunk, hpg * chunk), BF16),
           pltpu.VMEM((rows, vw), BF16)]
    )
    body = functools.partial(_odd_body, has_state=has_state, chunk=chunk, phase_major=phase_major)
    return pl.pallas_call(
        body,
        grid=(bsz // batch_block, n_tiles),
        in_specs=in_specs,
        out_specs=out_specs,
        out_shape=out_shape,
        scratch_shapes=scratch,
        compiler_params=pltpu.CompilerParams(
            dimension_semantics=("arbitrary", "arbitrary"), vmem_limit_bytes=VMEM_LIMIT_BYTES),
        name="odd_layer_state" if has_state else "odd_layer_prompt",
    )(*args)


def _gate_weights(w_a, w_x):
    heads, blk, _ = w_a.shape
    per_tile = MXU_TILE // blk
    tiles = []
    for i in range(heads // per_tile):
        sl = slice(i * per_tile, (i + 1) * per_tile)
        tiles.append(jnp.concatenate(
            [jax.scipy.linalg.block_diag(*w_a[sl]), jax.scipy.linalg.block_diag(*w_x[sl])], axis=1))
    return jnp.stack(tiles).astype(BF16)


def _rope_inv_row():
    half = ROT_DIM // 2
    inv = ROPE_THETA ** (-(jnp.arange(half, dtype=F32) * 2.0 / ROT_DIM))
    per_head = jnp.concatenate([inv, inv, jnp.zeros((HEAD_DIM - ROT_DIM,), F32)])
    return jnp.tile(per_head, LANES // HEAD_DIM).reshape(1, LANES)


def kernel(x_prompt, x_sample, state_lru_conv, state_lru_h, cache_swa_k, cache_swa_v, state_gdn_conv, state_gdn_s, c_prompt, c_sample, ev_mod_w, ev_mod_b, ev_norm_pre, ev_norm_post, ev_w_in, lru_conv_w, lru_conv_b, lru_w_a, lru_b_a, lru_w_x, lru_b_x, lru_lambda, swa_sinks, ev_w_out, od_mod_w, od_mod_b, od_norm_pre, od_norm_post, od_w_in, gdn_conv_w, gdn_a_log, gdn_dt_bias, gdn_head_norm, od_w_out):
    bp, seq, d = x_prompt.shape
    bs, dec_seq, _ = x_sample.shape
    lw = lru_conv_w.shape[-1]
    kvw = ATT_KV_HEADS * HEAD_DIM
    n_heads = gdn_a_log.shape[-1]
    cch = gdn_conv_w.shape[-1]
    vw = od_w_out.shape[1]

    n_c = bp + bs
    n_pad = -n_c % 16
    c_all = jnp.concatenate([c_prompt, c_sample, jnp.zeros((n_pad, d), F32)], axis=0)
    mod_ev, mod_od = _modulation(c_all, ev_mod_w[0], ev_mod_b[0], od_mod_w[0], od_mod_b[0])
    mod_ev = mod_ev.reshape(n_c + n_pad, 3, d)
    mod_od = mod_od.reshape(n_c + n_pad, 3, d)

    row = lambda a: a.reshape(1, -1)
    ev_consts = (row(ev_norm_pre[0]), row(ev_norm_post[0]), ev_w_in[0].astype(BF16), lru_conv_w[0],
                 row(lru_conv_b[0]), _gate_weights(lru_w_a[0], lru_w_x[0]), row(lru_b_a[0]), row(lru_b_x[0]),
                 row(lru_lambda[0]), _rope_inv_row(), swa_sinks[0], ev_w_out[0].astype(BF16))
    w_in_od = od_w_in[0]
    pad_lanes = lambda a: jnp.pad(a, ((0, 0), (0, LANES - a.shape[1])))
    head_row = lambda a: jnp.pad(a.reshape(1, -1), ((0, 0), (n_heads, LANES - 2 * n_heads)))
    od_consts = (row(od_norm_pre[0]), row(od_norm_post[0]), w_in_od.astype(BF16),
                 pad_lanes(w_in_od[:, cch + vw:]).astype(BF16), gdn_conv_w[0],
                 head_row(gdn_a_log[0]), head_row(gdn_dt_bias[0]), row(gdn_head_norm[0]),
                 od_w_out[0].astype(BF16))

    tile = min(PROMPT_TILE, seq)
    xp, lru_conv_p, lru_h_p, swa_k_p, swa_v_p = _even_layer(
        x_prompt, mod_ev[:bp], ev_consts, None, tile=tile, chunk=min(CHUNK, seq), batch_block=1,
        n_past_valid=0, pos0=0, k_out_rows=WINDOW)
    ev_state = (state_lru_conv[0], state_lru_h[0].reshape(bs, 1, lw),
                cache_swa_k[0].reshape(bs, WINDOW, kvw), cache_swa_v[0].reshape(bs, WINDOW, kvw))
    xs, lru_conv_s, lru_h_s, swa_k_s, swa_v_s = _even_layer(
        x_sample, mod_ev[bp:n_c], ev_consts, ev_state, tile=dec_seq, chunk=min(CHUNK, dec_seq), batch_block=bs,
        n_past_valid=WINDOW, pos0=PAST_LEN, k_out_rows=dec_seq)

    xp, gdn_conv_p, gdn_s_p = _odd_layer(
        xp, mod_od[:bp], od_consts, None, tile=tile, chunk=min(CHUNK, seq), batch_block=1)
    xs, gdn_conv_s, gdn_s_s = _odd_layer(
        xs, mod_od[bp:n_c], od_consts, (state_gdn_conv[0], state_gdn_s[0]),
        tile=dec_seq, chunk=min(CHUNK, dec_seq), batch_block=bs)

    kv_shape = lambda a: a.reshape(1, a.shape[0], a.shape[1], ATT_KV_HEADS, HEAD_DIM)
    return (xp, xs,
            lru_conv_p[None], lru_conv_s[None],
            lru_h_p.reshape(1, bp, lw), lru_h_s.reshape(1, bs, lw),
            kv_shape(swa_k_p), kv_shape(swa_k_s), kv_shape(swa_v_p), kv_shape(swa_v_s),
            gdn_conv_p[None], gdn_conv_s[None], gdn_s_p[None], gdn_s_s[None])
```

```python
import functools

import jax
import jax.numpy as jnp
from jax import lax
from jax.experimental import pallas as pl
from jax.experimental.pallas import tpu as pltpu

F32 = jnp.float32
BF16 = jnp.bfloat16

CHUNK = 64
EPS = 1e-6
CONV_W = 4
NEG_INF = -1e30
LRU_HEADS = 8
LRU_C = 8.0
HEAD_DIM = 64
ATT_KV_HEADS = 2
WINDOW = 128
ROT_DIM = HEAD_DIM // 4
ROPE_THETA = 500000.0
GDN_DK = 128
GDN_DV = 128
PAST_LEN = 4096

LANES = 128
SUBLANES = 8
MXU_TILE = 256
VMEM_LIMIT_BYTES = 56 * 1024 * 1024

PROMPT_TILE = 512
MOD_TILE = 512
CONV_PAD = SUBLANES


def _silu(x):
    return x * jax.nn.sigmoid(x)


def _expm1(x):
    u = jnp.exp(x)
    d = u - 1.0
    return jnp.where(u == 1.0, x, jnp.where(d == -1.0, -1.0, d * x / jnp.log(u)))


def _rms_scale(x):
    return lax.rsqrt(jnp.mean(x * x, axis=-1, keepdims=True) + EPS)


def _rms(x, g):
    return x * _rms_scale(x) * g


def _dot(a, b):
    return jnp.dot(a, b, preferred_element_type=F32)


def _dot_nt(a, b):
    return lax.dot_general(a, b, (((1,), (1,)), ((), ())), preferred_element_type=F32)


def _dot_tn(a, b):
    return lax.dot_general(a, b, (((0,), (0,)), ((), ())), preferred_element_type=F32)


def _for_each(n, body):
    if n == 1:
        body(0)
    else:
        def step(i, carry):
            body(i)
            return carry
        lax.fori_loop(0, n, step, 0)


def _rows(i, n, size=None):
    size = n if size is None else size
    if isinstance(i, int):
        return pl.ds(i * n, size)
    return pl.ds(pl.multiple_of(i * n, n), size)


def _mod_body(c_ref, w0_ref, b0_ref, w1_ref, b1_ref, o0_ref, o1_ref):
    c = _silu(c_ref[...]).astype(BF16)
    o0_ref[...] = _dot(c, w0_ref[...].astype(BF16)) + b0_ref[...]
    o1_ref[...] = _dot(c, w1_ref[...].astype(BF16)) + b1_ref[...]


def _modulation(c_all, w0, b0, w1, b1):
    n, d = c_all.shape
    d3 = w0.shape[1]
    wspec = pl.BlockSpec((d, MOD_TILE), lambda j: (0, j))
    bspec = pl.BlockSpec((1, MOD_TILE), lambda j: (0, j))
    ospec = pl.BlockSpec((n, MOD_TILE), lambda j: (0, j))
    return pl.pallas_call(
        _mod_body,
        grid=(d3 // MOD_TILE,),
        in_specs=[pl.BlockSpec((n, d), lambda j: (0, 0)), wspec, bspec, wspec, bspec],
        out_specs=[ospec, ospec],
        out_shape=[jax.ShapeDtypeStruct((n, d3), F32)] * 2,
        name="adaln_modulation",
    )(c_all, w0, b0.reshape(1, d3), w1, b1.reshape(1, d3))


def _pre_norm(x_ref, mod_ref, norm_pre_ref):
    bb, tc, d = x_ref.shape
    x = x_ref[...]
    gain = norm_pre_ref[...] * (1.0 + mod_ref[:, 1:2, :])
    h = x * _rms_scale(x) * gain + mod_ref[:, 0:1, :]
    return h.reshape(bb * tc, d)


def _post_residual(x_ref, mod_ref, norm_post_ref, y, o_ref, rows=None):
    bb, tc, d = x_ref.shape
    yn = y * _rms_scale(y)
    if rows is None:
        gain = mod_ref[:, 2:3, :] * norm_post_ref[...]
        o_ref[...] = x_ref[...] + gain * yn.reshape(bb, tc, d)
    else:
        assert bb == 1
        o_ref[0, rows, :] = x_ref[0, rows, :] + (mod_ref[0, 2:3, :] * norm_post_ref[...]) * yn


def _run_interleaved(stage_lists):
    tagged = [((i + 0.5) / len(stages), k, i, stage)
              for k, stages in enumerate(stage_lists) for i, stage in enumerate(stages)]
    for _, _, _, stage in sorted(tagged, key=lambda entry: entry[:3]):
        stage()


def _causal_conv(ext_ref, bi, tc, w_ref):
    ext = ext_ref[bi]
    acc = pltpu.roll(ext, CONV_W - 1, axis=0)[CONV_PAD:] * w_ref[0:1, :]
    for j in range(1, CONV_W - 1):
        acc = acc + pltpu.roll(ext, CONV_W - 1 - j, axis=0)[CONV_PAD:] * w_ref[j:j + 1, :]
    return acc + ext[CONV_PAD:] * w_ref[CONV_W - 1:CONV_W, :]


def _scan_phase_major(a, b, h0, chunk):
    tc, width = a.shape
    row8 = lax.broadcasted_iota(jnp.int32, (SUBLANES, width), 0)
    carry = h0
    out = []
    for c in range(tc // chunk):
        acc_a, acc_b = [], []
        for r in range(SUBLANES):
            blk = slice(c * chunk + r * SUBLANES, c * chunk + (r + 1) * SUBLANES)
            if r == 0:
                acc_a.append(a[blk])
                acc_b.append(b[blk])
            else:
                acc_b.append(a[blk] * acc_b[-1] + b[blk])
                acc_a.append(a[blk] * acc_a[-1])
        tot_a, tot_b = acc_a[-1], acc_b[-1]
        for sft in (1, 2, 4):
            keep = row8 >= sft
            tot_b = jnp.where(keep, tot_a * pltpu.roll(tot_b, sft, axis=0) + tot_b, tot_b)
            tot_a = jnp.where(keep, tot_a * pltpu.roll(tot_a, sft, axis=0), tot_a)
        h_out = tot_b + tot_a * carry
        h_in = jnp.where(row8 == 0, carry, pltpu.roll(h_out, 1, axis=0))
        out += [acc_b[r] + acc_a[r] * h_in for r in range(SUBLANES)]
        carry = h_out[SUBLANES - 1:SUBLANES, :]
    return jnp.concatenate(out, axis=0), carry


def _even_body(*refs, has_state, n_past_valid, pos0, chunk, n_tiles, k_out_rows, phase_major):
    refs = list(refs)
    x_ref = refs.pop(0)
    perm_ref = refs.pop(0) if phase_major else None
    (mod_ref, npre_ref, npost_ref, win_ref, cw_ref, cb_ref, wg_ref, ba_ref, bx_ref,
     lam_ref, inv_ref, sinks_ref, wout_ref) = refs[:13]
    refs = refs[13:]
    if has_state:
        conv0_ref, h0_ref, k0_ref, v0_ref = refs[:4]
        refs = refs[4:]
    (y_ref, nconv_ref, nh_ref, nk_ref, nv_ref,
     hm_ref, u_ref, ext_ref, a_ref, b_ref, hs_ref, knat_ref, vnat_ref, q_ref, mix_ref, cosl_ref, sinl_ref) = refs
    bb, tc, _ = x_ref.shape
    lw = a_ref.shape[1]
    aw = q_ref.shape[1]
    kvw = knat_ref.shape[2]
    t = pl.program_id(1)
    cpt = tc // chunk
    m = WINDOW + chunk
    o_q, o_k, o_v, o_gb = 2 * lw, 2 * lw + aw, 2 * lw + aw + kvw, 2 * lw + aw + 2 * kvw

    @pl.when(t == 0)
    def _():
        ext_ref[...] = jnp.zeros(ext_ref.shape, F32)
        if has_state:
            ext_ref[:, CONV_PAD - 3:CONV_PAD, :] = conv0_ref[...]
            nh_ref[...] = h0_ref[...]
            knat_ref[:, 0:WINDOW, :] = k0_ref[...]
            vnat_ref[:, 0:WINDOW, :] = v0_ref[...]
        else:
            nh_ref[...] = jnp.zeros(nh_ref.shape, F32)
            knat_ref[:, 0:WINDOW, :] = jnp.zeros((bb, WINDOW, kvw), F32)
            vnat_ref[:, 0:WINDOW, :] = jnp.zeros((bb, WINDOW, kvw), F32)

    def permute_rows(v):
        n = perm_ref.shape[0]
        return jnp.concatenate(
            [_dot(perm_ref[...], v[i * n:(i + 1) * n]).astype(BF16) for i in range(v.shape[0] // n)], axis=0)

    hmod_time = _pre_norm(x_ref, mod_ref, npre_ref).astype(BF16)
    hm_ref[...] = permute_rows(hmod_time) if phase_major else hmod_time

    tile_start = (pos0 + t * tc).astype(F32) * inv_ref[...]
    cos_start, sin_start = jnp.cos(tile_start), jnp.sin(tile_start)

    def make_rope(cos_in_tile, sin_in_tile):
        cos_t = cos_start * cos_in_tile - sin_start * sin_in_tile
        sin_t = sin_start * cos_in_tile + cos_start * sin_in_tile
        lane = lax.broadcasted_iota(jnp.int32, cos_t.shape, 1) % HEAD_DIM
        half = ROT_DIM // 2
        sin_a = jnp.where(lane < half, -sin_t, 0.0)
        sin_b = jnp.where(lane >= half, sin_t, 0.0)

        def rope(xcol):
            return (xcol * cos_t + pltpu.roll(xcol, LANES - half, axis=1) * sin_a
                    + pltpu.roll(xcol, half, axis=1) * sin_b)
        return rope

    @pl.when(t == 0)
    def _():
        row_t = lax.broadcasted_iota(jnp.int32, (tc, LANES), 0)
        if phase_major:
            row_t = (row_t // chunk) * chunk + _time_of_row(row_t % chunk)
        in_tile = row_t.astype(F32) * inv_ref[...]
        cosl_ref[...] = jnp.cos(in_tile)
        sinl_ref[...] = jnp.sin(in_tile)

    rope = make_rope(cosl_ref[...], sinl_ref[...])

    neg_c_softplus = -LRU_C * jax.nn.softplus(-lam_ref[...])
    row8 = lax.broadcasted_iota(jnp.int32, (SUBLANES, lw), 0)

    def project(cols):
        u_ref[:, cols] = _dot(hm_ref[...], win_ref[:, cols])

    def recurrence_inputs(bi):
        rows = _rows(bi, tc)
        if phase_major:
            xc = _conv_phase_major(u_ref, ext_ref, nconv_ref, cw_ref, tc, chunk, slice(0, lw)) + cb_ref[...]
        else:
            ext_ref[bi, CONV_PAD:CONV_PAD + tc, :] = u_ref[rows, 0:lw]
            xc = _causal_conv(ext_ref, bi, tc, cw_ref) + cb_ref[...]
            nconv_ref[bi, :, :] = ext_ref[bi, CONV_PAD + tc - 3:CONV_PAD + tc, :]
            ext_ref[bi, CONV_PAD - 3:CONV_PAD, :] = ext_ref[bi, CONV_PAD + tc - 3:CONV_PAD + tc, :]
        xcb = xc.astype(BF16)
        halves = [_dot(xcb[:, i * MXU_TILE:(i + 1) * MXU_TILE], wg_ref[i]) for i in range(lw // MXU_TILE)]
        r = jax.nn.sigmoid(jnp.concatenate([g[:, :MXU_TILE] for g in halves], axis=1) + ba_ref[...])
        ig = jax.nn.sigmoid(jnp.concatenate([g[:, MXU_TILE:] for g in halves], axis=1) + bx_ref[...])
        log_a = r * neg_c_softplus
        a_ref[...] = jnp.exp(log_a)
        b_ref[...] = jnp.sqrt(-_expm1(2.0 * log_a)) * (ig * xc)

    def recurrence(bi):
        if phase_major:
            hs_ref[...], nh_ref[bi, :, :] = _scan_phase_major(a_ref[...], b_ref[...], nh_ref[bi, :, :], chunk)
            return

        def scan_block(j, hc):
            r0 = pl.multiple_of(j * SUBLANES, SUBLANES)
            a = a_ref[pl.ds(r0, SUBLANES), :]
            b = b_ref[pl.ds(r0, SUBLANES), :]
            for sft in (1, 2, 4):
                keep = row8 >= sft
                b = jnp.where(keep, a * pltpu.roll(b, sft, axis=0) + b, b)
                a = jnp.where(keep, a * pltpu.roll(a, sft, axis=0), a)
            h = a * hc + b
            hs_ref[pl.ds(r0, SUBLANES), :] = h
            return h[SUBLANES - 1:SUBLANES, :]

        nh_ref[bi, :, :] = lax.fori_loop(0, tc // SUBLANES, scan_block, nh_ref[bi, :, :])

    def recurrence_output(bi):
        rows = _rows(bi, tc)
        mix_ref[rows, 0:lw] = (hs_ref[...] * _silu(u_ref[rows, lw:2 * lw])).astype(BF16)

    def attention_inputs(bi):
        rows = _rows(bi, tc)
        for j in range(aw // LANES):
            qcol = rope(u_ref[rows, o_q + j * LANES:o_q + (j + 1) * LANES])
            q_ref[rows, j * LANES:(j + 1) * LANES] = (qcol * (HEAD_DIM ** -0.5)).astype(BF16)
        knat_ref[bi, WINDOW:WINDOW + tc, :] = rope(u_ref[rows, o_k:o_k + kvw])
        vnat_ref[bi, WINDOW:WINDOW + tc, :] = u_ref[rows, o_v:o_v + kvw]
        if not phase_major:
            nk_ref[bi, :, :] = knat_ref[bi, WINDOW + tc - k_out_rows:WINDOW + tc, :]
            nv_ref[bi, :, :] = vnat_ref[bi, WINDOW + tc - k_out_rows:WINDOW + tc, :]

    def per_batch(bi):
        recurrence_inputs(bi)
        recurrence(bi)
        recurrence_output(bi)
        attention_inputs(bi)

    if phase_major:
        project(slice(0, lw))
        later = [slice(lw, 2 * lw), slice(o_q, o_k), slice(o_k, o_gb), slice(o_gb, o_gb + aw)]
        _run_interleaved([[functools.partial(project, cols) for cols in later],
                          [functools.partial(recurrence_inputs, 0), functools.partial(recurrence, 0)]])
        recurrence_output(0)
        attention_inputs(0)
    else:
        project(slice(0, win_ref.shape[1]))
        _for_each(bb, per_batch)

    if phase_major:
        @pl.when(t == n_tiles - 1)
        def _():
            first = tc - k_out_rows
            kv = _dot(hmod_time[first:], win_ref[:, o_k:o_k + 2 * kvw])
            in_tile = (first + lax.broadcasted_iota(jnp.int32, (k_out_rows, LANES), 0)).astype(F32) * inv_ref[...]
            rope_last = make_rope(jnp.cos(in_tile), jnp.sin(in_tile))
            nk_ref[0] = rope_last(kv[:, 0:kvw])
            nv_ref[0] = kv[:, kvw:2 * kvw]

    lo_lane = lax.broadcasted_iota(jnp.int32, (m, LANES), 1) < HEAD_DIM
    lo_q = lax.broadcasted_iota(jnp.int32, (chunk, LANES), 1) < HEAD_DIM
    group = (aw // HEAD_DIM) // ATT_KV_HEADS

    sinks = [jnp.concatenate([jnp.full((chunk, 1), sinks_ref[h * group + g], F32) for g in range(group)], axis=0)
             for h in range(ATT_KV_HEADS)]

    lane_m = lax.broadcasted_iota(jnp.int32, (m, LANES), 1)

    def attend():
        units = []
        for bi, c in [(bi, c) for bi in range(bb) for c in range(cpt)]:
            rows = _rows(bi * cpt + c, chunk)
            krows = _rows(c, chunk, m)
            kseg = knat_ref[bi, krows, :]
            vseg = vnat_ref[bi, krows, :]
            krot = pltpu.roll(kseg, HEAD_DIM, axis=1)
            vrot = pltpu.roll(vseg, HEAD_DIM, axis=1)
            valid = None
            if n_past_valid < WINDOW:
                key_t = lax.broadcasted_iota(jnp.int32, (1, m), 1)
                if phase_major:
                    key_t = (key_t // chunk) * chunk + _time_of_row(key_t % chunk)
                valid = t * tc + c * chunk - WINDOW + key_t >= -n_past_valid
            for h in range(ATT_KV_HEADS):
                kd = (jnp.where(lo_lane, kseg, krot) if h == 0 else jnp.where(lo_lane, krot, kseg)).astype(BF16)
                v_lo = jnp.where(lo_lane, vseg if h == 0 else vrot, jnp.where(lane_m == HEAD_DIM, 1.0, 0.0))
                v_hi = jnp.where(lo_lane, jnp.where(lane_m == 0, 1.0, 0.0), vrot if h == 0 else vseg)
                units.append(dict(rows=rows, h=h, valid=valid, kd=kd, v_lo=v_lo.astype(BF16),
                                  v_hi=v_hi.astype(BF16)))

        for un in units:
            rows, h = un["rows"], un["h"]
            cols = [q_ref[rows, (h * group // 2 + j) * LANES:(h * group // 2 + j + 1) * LANES]
                    for j in range(group // 2)]
            zero = jnp.zeros_like(cols[0])
            qstack = jnp.concatenate(
                [part for qc in cols for part in (jnp.where(lo_q, qc, zero), jnp.where(lo_q, zero, qc))], axis=0)
            sc = _dot_nt(qstack, un["kd"])
            un["sc"] = sc if un["valid"] is None else jnp.where(un["valid"], sc, NEG_INF)
        for un in units:
            un["mx"] = jnp.maximum(jnp.max(un["sc"], axis=-1, keepdims=True), sinks[un["h"]])
        for un in units:
            un["p"] = jnp.exp(un["sc"] - un["mx"]).astype(BF16)
            un["sink_p"] = jnp.exp(sinks[un["h"]] - un["mx"])
        for un in units:
            rows, h, p, sink_p = un["rows"], un["h"], un["p"], un["sink_p"]
            for j in range(group // 2):
                r_lo = slice((2 * j) * chunk, (2 * j + 1) * chunk)
                r_hi = slice((2 * j + 1) * chunk, (2 * j + 2) * chunk)
                o_lo = _dot(p[r_lo], un["v_lo"])
                o_hi = _dot(p[r_hi], un["v_hi"])
                den_lo = o_lo[:, HEAD_DIM:HEAD_DIM + 1] + sink_p[r_lo]
                den_hi = o_hi[:, 0:1] + sink_p[r_hi]
                o = jnp.where(lo_q, o_lo / den_lo, o_hi / den_hi)
                col = h * group // 2 + j
                gate = _silu(u_ref[rows, o_gb + col * LANES:o_gb + (col + 1) * LANES])
                mix_ref[rows, lw + col * LANES:lw + (col + 1) * LANES] = (o * gate).astype(BF16)

    attend()

    if n_tiles > 1:
        knat_ref[:, 0:WINDOW, :] = knat_ref[:, tc:tc + WINDOW, :]
        vnat_ref[:, 0:WINDOW, :] = vnat_ref[:, tc:tc + WINDOW, :]

    mixed = mix_ref[...]
    if phase_major:
        mixed = permute_rows(mixed)
    y = _dot(mixed, wout_ref[...])
    _post_residual(x_ref, mod_ref, npost_ref, y, y_ref)


def _runs_phase_major(chunk, batch_block, has_state):
    return chunk == SUBLANES * SUBLANES and batch_block == 1 and not has_state


def _conv_history_shape(phase_major, batch_block, tile, channels):
    if phase_major:
        return (CONV_W - 1, SUBLANES, channels)
    return (batch_block, CONV_PAD + tile, channels)


def _const_spec(shape):
    zeros = (0,) * len(shape)
    return pl.BlockSpec(shape, lambda b, t: zeros)


def _even_layer(x, mod, consts, state, *, tile, chunk, batch_block, n_past_valid, pos0, k_out_rows):
    bsz, seq, d = x.shape
    (npre, npost, win, cw, cb, wg, b_a, b_x, lam, inv_row, sinks, wout) = consts
    lw = cw.shape[1]
    kvw = ATT_KV_HEADS * HEAD_DIM
    aw = (win.shape[1] - 2 * lw - 2 * kvw) // 2
    n_tiles = seq // tile
    assert seq % tile == 0 and tile % chunk == 0 and bsz % batch_block == 0
    assert n_tiles == 1 or tile >= WINDOW
    rows = batch_block * tile
    has_state = state is not None
    phase_major = _runs_phase_major(chunk, batch_block, has_state)
    assert not phase_major or tile >= k_out_rows

    def bspec(shape):
        nd = len(shape)
        return pl.BlockSpec((batch_block,) + shape, lambda b, t: (b,) + (0,) * nd)

    in_specs = [pl.BlockSpec((batch_block, tile, d), lambda b, t: (b, t, 0))]
    args = [x]
    if phase_major:
        perm = _phase_major_matrix(min(rows, MXU_TILE), chunk)
        assert rows % perm.shape[0] == 0
        in_specs.append(_const_spec(perm.shape))
        args.append(perm)
    in_specs += [
        bspec((3, d)),
        _const_spec(npre.shape), _const_spec(npost.shape), _const_spec(win.shape),
        _const_spec(cw.shape), _const_spec(cb.shape), _const_spec(wg.shape),
        _const_spec(b_a.shape), _const_spec(b_x.shape), _const_spec(lam.shape), _const_spec(inv_row.shape),
        pl.BlockSpec(memory_space=pltpu.SMEM),
        _const_spec(wout.shape),
    ]
    args += [mod, npre, npost, win, cw, cb, wg, b_a, b_x, lam, inv_row, sinks, wout]
    if has_state:
        in_specs += [bspec((CONV_W - 1, lw)), bspec((1, lw)), bspec((WINDOW, kvw)), bspec((WINDOW, kvw))]
        args += list(state)
    out_specs = [
        pl.BlockSpec((batch_block, tile, d), lambda b, t: (b, t, 0)),
        bspec((CONV_W - 1, lw)), bspec((1, lw)), bspec((k_out_rows, kvw)), bspec((k_out_rows, kvw)),
    ]
    out_shape = [
        jax.ShapeDtypeStruct((bsz, seq, d), F32),
        jax.ShapeDtypeStruct((bsz, CONV_W - 1, lw), F32),
        jax.ShapeDtypeStruct((bsz, 1, lw), F32),
        jax.ShapeDtypeStruct((bsz, k_out_rows, kvw), F32),
        jax.ShapeDtypeStruct((bsz, k_out_rows, kvw), F32),
    ]
    scratch = [
        pltpu.VMEM((rows, d), BF16),
        pltpu.VMEM((rows, win.shape[1]), F32),
        pltpu.VMEM(_conv_history_shape(phase_major, batch_block, tile, lw), F32),
        pltpu.VMEM((tile, lw), F32), pltpu.VMEM((tile, lw), F32), pltpu.VMEM((tile, lw), F32),
        pltpu.VMEM((batch_block, WINDOW + tile, kvw), F32),
        pltpu.VMEM((batch_block, WINDOW + tile, kvw), F32),
        pltpu.VMEM((rows, aw), BF16),
        pltpu.VMEM((rows, lw + aw), BF16),
        pltpu.VMEM((tile, LANES), F32), pltpu.VMEM((tile, LANES), F32),
    ]
    body = functools.partial(_even_body, has_state=has_state, n_past_valid=n_past_valid, pos0=pos0,
                             chunk=chunk, n_tiles=n_tiles, k_out_rows=k_out_rows, phase_major=phase_major)
    return pl.pallas_call(
        body,
        grid=(bsz // batch_block, n_tiles),
        in_specs=in_specs,
        out_specs=out_specs,
        out_shape=out_shape,
        scratch_shapes=scratch,
        compiler_params=pltpu.CompilerParams(
            dimension_semantics=("arbitrary", "arbitrary"), vmem_limit_bytes=VMEM_LIMIT_BYTES),
        name="even_layer_state" if has_state else "even_layer_prompt",
    )(*args)


def _gdn_groups(chunk, n_heads):
    hpg = max(1, min(n_heads, MXU_TILE // chunk))
    assert n_heads % hpg == 0
    return hpg, n_heads // hpg


def _time_of_row(i):
    return (i % SUBLANES) * SUBLANES + i // SUBLANES


def _phase_major_matrix(n_rows, chunk):
    i = jnp.arange(n_rows)
    src = (i // chunk) * chunk + _time_of_row(i % chunk)
    return (src[:, None] == i[None, :]).astype(BF16)


def _conv_phase_major(u_ref, hist_ref, nconv_ref, w_ref, tc, chunk, cols):
    cpt = tc // chunk
    blocks = [u_ref[SUBLANES * b:SUBLANES * (b + 1), cols] for b in range(tc // SUBLANES)]
    taps_w = [w_ref[j:j + 1, cols] for j in range(CONV_W)]
    top_row = lax.broadcasted_iota(jnp.int32, blocks[0].shape, 0) == 0
    first_late = SUBLANES - (CONV_W - 1)
    prev = {r: pltpu.roll(hist_ref[r - first_late, :, cols], 1, axis=0) for r in range(first_late, SUBLANES)}
    out = []
    for c in range(cpt):
        late = {}
        for r in range(first_late, SUBLANES):
            moved = pltpu.roll(blocks[SUBLANES * c + r], 1, axis=0)
            late[r] = jnp.where(top_row, prev[r], moved)
            prev[r] = moved
        for r in range(SUBLANES):
            acc = None
            for j in range(CONV_W):
                s = CONV_W - 1 - j
                tap = blocks[SUBLANES * c + r - s] if r >= s else late[r - s + SUBLANES]
                acc = tap * taps_w[j] if acc is None else acc + tap * taps_w[j]
            out.append(acc)
    for i in range(CONV_W - 1):
        last = blocks[SUBLANES * (cpt - 1) + first_late + i]
        hist_ref[i, :, cols] = last
        nconv_ref[0, i:i + 1, cols] = last[SUBLANES - 1:SUBLANES, :]
    return jnp.concatenate(out, axis=0)


def _odd_body(*refs, has_state, chunk, phase_major):
    refs = list(refs)
    x_ref = refs.pop(0)
    perm_ref = refs.pop(0) if phase_major else None
    (mod_ref, npre_ref, npost_ref, win_ref, wba_ref, cw_ref, alog_ref, dtb_ref, hn_ref, wout_ref) = refs[:10]
    refs = refs[10:]
    if has_state:
        conv0_ref, s0_ref = refs[:2]
        refs = refs[2:]
    (y_ref, nconv_ref, ns_ref,
     hm_ref, u_ref, ext_ref, q_ref, k_ref, kb_ref, kbe_ref, vb_ref, qe_ref, kd_ref,
     beta_ref, gcum_ref, eg_ref, dec_ref, wq_ref, uu_ref, qk_ref, mix_ref) = refs
    bb, tc, _ = x_ref.shape
    n_heads = ns_ref.shape[1]
    kw = n_heads * GDN_DK
    vw = n_heads * GDN_DV
    cch = 2 * kw + vw
    rows_all = bb * tc
    t = pl.program_id(1)
    cpt = tc // chunk
    n_steps = (chunk - 1).bit_length()
    assert n_steps >= 2
    hpg, n_groups = _gdn_groups(chunk, n_heads)
    lw = hpg * chunk

    @pl.when(t == 0)
    def _():
        ext_ref[...] = jnp.zeros(ext_ref.shape, F32)
        if has_state:
            ext_ref[:, CONV_PAD - 3:CONV_PAD, :] = conv0_ref[...]
            ns_ref[...] = s0_ref[...]
        else:
            ns_ref[...] = jnp.zeros(ns_ref.shape, F32)

    def permute_rows(v):
        n = perm_ref.shape[0]
        return jnp.concatenate(
            [_dot(perm_ref[...], v[i * n:(i + 1) * n]).astype(BF16) for i in range(v.shape[0] // n)], axis=0)

    hmod = _pre_norm(x_ref, mod_ref, npre_ref).astype(BF16)
    if phase_major:
        hmod = permute_rows(hmod)
    hm_ref[...] = hmod
    ba = _dot(hmod, wba_ref[...])
    beta_ref[...] = jax.nn.sigmoid(ba)
    g = -jnp.exp(alog_ref[...]) * jax.nn.softplus(ba + dtb_ref[...])
    if phase_major:
        row8 = lax.broadcasted_iota(jnp.int32, (SUBLANES, LANES), 0)
        parts = []
        for c in range(rows_all // chunk):
            run = []
            for r in range(SUBLANES):
                blk = g[c * chunk + r * SUBLANES:c * chunk + (r + 1) * SUBLANES]
                run.append(blk if r == 0 else run[-1] + blk)
            total = run[-1]
            incl = total
            for sft in (1, 2, 4):
                incl = jnp.where(row8 >= sft, incl + pltpu.roll(incl, sft, axis=0), incl)
            parts += [blk + (incl - total) for blk in run]
        g = jnp.concatenate(parts, axis=0)
    else:
        row_in_chunk = lax.broadcasted_iota(jnp.int32, (rows_all, LANES), 0) % chunk
        sft = 1
        while sft < chunk:
            g = jnp.where(row_in_chunk >= sft, g + pltpu.roll(g, sft, axis=0), g)
            sft *= 2
    gcum_ref[...] = g
    eg_ref[...] = jnp.exp(g)
    g3 = g.reshape(rows_all // chunk, chunk, LANES)
    dec_ref[...] = jnp.exp(g3[:, chunk - 1:chunk, :] - g3).reshape(rows_all, LANES)

    def project(cols):
        u_ref[:, cols] = _dot(hm_ref[...], win_ref[:, cols])

    def head_operands(rows, h, qh, kh, vh):
        hc = slice(h * GDN_DK, (h + 1) * GDN_DK)
        qh = qh * (lax.rsqrt(jnp.sum(qh * qh, axis=-1, keepdims=True) + EPS) * (GDN_DK ** -0.5))
        kh = kh * lax.rsqrt(jnp.sum(kh * kh, axis=-1, keepdims=True) + EPS)
        beta = beta_ref[rows, :][:, h:h + 1]
        eg = eg_ref[rows, :][:, n_heads + h:n_heads + h + 1]
        kb = kh * beta
        q_ref[rows, hc] = qh.astype(BF16)
        k_ref[rows, hc] = kh.astype(BF16)
        kb_ref[rows, hc] = kb.astype(BF16)
        kbe_ref[rows, hc] = (kb * eg).astype(BF16)
        vb_ref[rows, hc] = (vh * beta).astype(BF16)
        qe_ref[rows, hc] = (qh * eg).astype(BF16)
        kd_ref[rows, hc] = (kh * dec_ref[rows, :][:, n_heads + h:n_heads + h + 1]).astype(BF16)

    def per_batch(bi):
        rows = _rows(bi, tc)
        ext_ref[bi, CONV_PAD:CONV_PAD + tc, :] = u_ref[rows, 0:cch]
        qkv = _silu(_causal_conv(ext_ref, bi, tc, cw_ref))
        nconv_ref[bi, :, :] = ext_ref[bi, CONV_PAD + tc - 3:CONV_PAD + tc, :]
        ext_ref[bi, CONV_PAD - 3:CONV_PAD, :] = ext_ref[bi, CONV_PAD + tc - 3:CONV_PAD + tc, :]
        for h in range(n_heads):
            head_operands(rows, h, qkv[:, h * GDN_DK:(h + 1) * GDN_DK],
                          qkv[:, kw + h * GDN_DK:kw + (h + 1) * GDN_DK],
                          qkv[:, 2 * kw + h * GDN_DV:2 * kw + (h + 1) * GDN_DV])

    pair = MXU_TILE // GDN_DK

    def pair_stages(j):
        col_sets = [slice(base + j * MXU_TILE, base + (j + 1) * MXU_TILE) for base in (0, kw, 2 * kw)]

        def operands():
            rows = pl.ds(0, tc)
            q2, k2, v2 = [_silu(_conv_phase_major(u_ref, ext_ref, nconv_ref, cw_ref, tc, chunk, cols))
                          for cols in col_sets]
            for i in range(pair):
                part = slice(i * GDN_DK, (i + 1) * GDN_DK)
                head_operands(rows, pair * j + i, q2[:, part], k2[:, part], v2[:, part])

        return [functools.partial(project, cols) for cols in col_sets] + [operands]

    if not phase_major:
        project(slice(0, cch + vw))
        _for_each(bb, per_batch)

    lane_w = lax.broadcasted_iota(jnp.int32, (chunk, lw), 1)
    row_w = lax.broadcasted_iota(jnp.int32, (chunk, lw), 0)
    blk_w = lane_w // chunk
    col_w = lane_w % chunk
    diag_w = row_w == col_w
    if phase_major:
        row_w, col_w = _time_of_row(row_w), _time_of_row(col_w)
    incl_w = row_w >= col_w
    strict_w = row_w > col_w
    eye_w = jnp.where(diag_w, 1.0, 0.0).astype(F32)
    feat_blk = lax.broadcasted_iota(jnp.int32, (chunk, hpg * GDN_DK), 1) // GDN_DK

    def block_diag(m_b):
        return jnp.concatenate([jnp.where(blk_w == hh, m_b, jnp.zeros_like(m_b)) for hh in range(hpg)], axis=0)

    def head_cols(h):
        return slice(h * GDN_DK, (h + 1) * GDN_DK)

    def wy_stages(segs, head_groups):
        units = []

        def build():
            for seg in segs:
                rows = _rows(seg, chunk)
                gt = gcum_ref[rows, :]
                for gi in head_groups:
                    gcols = slice(gi * hpg * GDN_DK, (gi + 1) * hpg * GDN_DK)
                    k4 = k_ref[rows, gcols]
                    k_bd = jnp.concatenate(
                        [jnp.where(feat_blk == hh, k4, jnp.zeros_like(k4)) for hh in range(hpg)], axis=0)
                    kk = _dot_nt(jnp.concatenate([kb_ref[rows, gcols], q_ref[rows, gcols]], axis=0), k_bd)
                    gcol = jnp.zeros((chunk, lw), F32)
                    for hh in range(hpg):
                        ln = n_heads + gi * hpg + hh
                        gcol = jnp.where(blk_w == hh, gt[:, ln:ln + 1], gcol)
                    grow = jnp.sum(jnp.where(diag_w, gcol, 0.0), axis=0, keepdims=True)
                    decay = jnp.exp(jnp.where(incl_w, gcol - grow, 0.0))
                    p_b = (-(kk[:chunk] * jnp.where(strict_w, decay, 0.0))).astype(BF16)
                    qk_ref[seg * n_groups + gi] = (kk[chunk:] * jnp.where(incl_w, decay, 0.0)).astype(BF16)
                    units.append(dict(seg=seg, rows=rows, gi=gi, p=p_b, t=eye_w + p_b.astype(F32)))

        def square():
            for un in units:
                un["p"] = _dot(un["p"], block_diag(un["p"])).astype(BF16)

        def extend_and_square():
            for un in units:
                out = _dot(jnp.concatenate([un["t"].astype(BF16), un["p"]], axis=0), block_diag(un["p"]))
                un["t"] = un["t"] + out[:chunk]
                un["p"] = out[chunk:].astype(BF16)

        def extend():
            for un in units:
                un["t"] = un["t"] + _dot(un["t"].astype(BF16), block_diag(un["p"]))

        def apply():
            for un in units:
                seg, rows, gi = un["seg"], un["rows"], un["gi"]
                heads = range(gi * hpg, (gi + 1) * hpg)
                rhs = jnp.concatenate(
                    [jnp.concatenate([kbe_ref[rows, head_cols(h)], vb_ref[rows, head_cols(h)]], axis=1)
                     for h in heads], axis=0)
                wu = _dot(block_diag(un["t"].astype(BF16)), rhs)
                for hh, h in enumerate(heads):
                    blk = wu[hh * chunk:(hh + 1) * chunk]
                    wq_ref[seg * n_heads + h] = jnp.concatenate(
                        [blk[:, :GDN_DK].astype(BF16), qe_ref[rows, head_cols(h)]], axis=0)
                    uu_ref[rows, head_cols(h)] = blk[:, GDN_DK:]

        return [build, square] + [extend_and_square] * (n_steps - 2) + [extend, apply]

    zero_b = jnp.zeros((chunk, GDN_DV), BF16)
    state = {(bi, h): ns_ref[bi, h] for bi in range(bb) for h in range(n_heads)}

    def recurrence_stages(segs):
        stages = []
        for c in sorted({seg % cpt for seg in segs}):
            wave = [seg for seg in segs if seg % cpt == c]
            pairs = [(seg, h) for seg in wave for h in range(n_heads)]
            held = {}

            def correct(wave=wave, pairs=pairs, held=held):
                for seg, h in pairs:
                    held["ws_qs", seg, h] = _dot(wq_ref[seg * n_heads + h], state[seg // cpt, h].astype(BF16))
                for seg, h in pairs:
                    held["v", seg, h] = (uu_ref[_rows(seg, chunk), head_cols(h)]
                                         - held["ws_qs", seg, h][:chunk]).astype(BF16)

            def advance(wave=wave, pairs=pairs, held=held):
                for seg in wave:
                    held["eg", seg] = jnp.exp(gcum_ref[pl.ds(seg * chunk + chunk - 1, 1), :])
                for seg, h in pairs:
                    state[seg // cpt, h] = (state[seg // cpt, h] * held["eg", seg][:, n_heads + h:n_heads + h + 1]
                                            + _dot_tn(kd_ref[_rows(seg, chunk), head_cols(h)], held["v", seg, h]))
                for seg in wave:
                    rows = _rows(seg, chunk)
                    for gi in range(n_groups):
                        heads = range(gi * hpg, (gi + 1) * hpg)
                        v_bd = jnp.concatenate(
                            [jnp.concatenate([held["v", seg, h] if j == hh else zero_b for j in range(hpg)], axis=1)
                             for hh, h in enumerate(heads)], axis=0)
                        o = (jnp.concatenate([held["ws_qs", seg, h][chunk:] for h in heads], axis=1)
                             + _dot(qk_ref[seg * n_groups + gi], v_bd))
                        for hh, h in enumerate(heads):
                            gate = _silu(u_ref[rows, cch + h * GDN_DV:cch + (h + 1) * GDN_DV])
                            o_h = _rms(o[:, hh * GDN_DV:(hh + 1) * GDN_DV], hn_ref[...])
                            mix_ref[rows, head_cols(h)] = (o_h * gate).astype(BF16)

            stages += [correct, advance]
        return stages

    def output_stages(rows):
        held = {}

        def gather():
            mixed = mix_ref[rows, :]
            held["mixed"] = permute_rows(mixed) if phase_major else mixed

        def project():
            held["y"] = _dot(held["mixed"], wout_ref[...])

        def finish():
            _post_residual(x_ref, mod_ref, npost_ref, held["y"], y_ref, rows if n_row_groups > 1 else None)

        return [gather, project, finish]

    group_rows = min(rows_all, MXU_TILE)
    n_row_groups = rows_all // group_rows
    assert rows_all % group_rows == 0 and group_rows % chunk == 0 and (bb == 1 or n_row_groups == 1)
    segs_of = [range(g * group_rows // chunk, (g + 1) * group_rows // chunk) for g in range(n_row_groups)]
    if phase_major:
        pairs_per_group = hpg // pair
        assert hpg % pair == 0
        work = [(segs, gi) for gi in range(n_groups) for segs in segs_of]
        ahead = [[stage for j in range(gi * pairs_per_group, (gi + 1) * pairs_per_group) for stage in pair_stages(j)]
                 for gi in range(n_groups)]
        ahead.append([functools.partial(project, slice(cch + i * MXU_TILE, cch + (i + 1) * MXU_TILE))
                      for i in range(vw // MXU_TILE)])
        for stage in ahead[0]:
            stage()
        for gi in range(n_groups):
            factors = [stage for segs, g in work if g == gi for stage in wy_stages(segs, [gi])]
            _run_interleaved([ahead[gi + 1], factors])
    else:
        for segs in segs_of:
            for stage in wy_stages(segs, range(n_groups)):
                stage()
    for g in range(n_row_groups + 1):
        lists = []
        if g < n_row_groups:
            lists.append(recurrence_stages(segs_of[g]))
        if g >= 1:
            lists.append(output_stages(pl.ds((g - 1) * group_rows, group_rows)))
        _run_interleaved(lists)
    for bh, value in state.items():
        ns_ref[bh[0], bh[1]] = value


def _odd_layer(x, mod, consts, state, *, tile, chunk, batch_block):
    bsz, seq, d = x.shape
    (npre, npost, win, wba, cw, alog, dtb, hn, wout) = consts
    cch = cw.shape[1]
    vw = wout.shape[0]
    n_heads = vw // GDN_DV
    n_tiles = seq // tile
    assert seq % tile == 0 and tile % chunk == 0 and bsz % batch_block == 0
    rows = batch_block * tile
    has_state = state is not None
    phase_major = _runs_phase_major(chunk, batch_block, has_state)

    def bspec(shape):
        nd = len(shape)
        return pl.BlockSpec((batch_block,) + shape, lambda b, t: (b,) + (0,) * nd)

    in_specs = [pl.BlockSpec((batch_block, tile, d), lambda b, t: (b, t, 0))]
    args = [x]
    if phase_major:
        perm_rows = min(rows, MXU_TILE)
        assert rows % perm_rows == 0 and perm_rows % chunk == 0
        in_specs.append(_const_spec((perm_rows, perm_rows)))
        args.append(_phase_major_matrix(perm_rows, chunk))
    in_specs += [
        bspec((3, d)),
        _const_spec(npre.shape), _const_spec(npost.shape), _const_spec((d, cch + vw)), _const_spec(wba.shape),
        _const_spec(cw.shape), _const_spec(alog.shape), _const_spec(dtb.shape), _const_spec(hn.shape),
        _const_spec(wout.shape),
    ]
    args += [mod, npre, npost, win, wba, cw, alog, dtb, hn, wout]
    if has_state:
        in_specs += [bspec((CONV_W - 1, cch)), bspec((n_heads, GDN_DK, GDN_DV))]
        args += list(state)
    out_specs = [
        pl.BlockSpec((batch_block, tile, d), lambda b, t: (b, t, 0)),
        bspec((CONV_W - 1, cch)), bspec((n_heads, GDN_DK, GDN_DV)),
    ]
    out_shape = [
        jax.ShapeDtypeStruct((bsz, seq, d), F32),
        jax.ShapeDtypeStruct((bsz, CONV_W - 1, cch), F32),
        jax.ShapeDtypeStruct((bsz, n_heads, GDN_DK, GDN_DV), F32),
    ]
    hpg, n_groups = _gdn_groups(chunk, n_heads)
    n_seg = rows // chunk
    scratch = (
        [pltpu.VMEM((rows, d), BF16),
         pltpu.VMEM((rows, cch + vw), F32),
         pltpu.VMEM(_conv_history_shape(phase_major, batch_block, tile, cch), F32)]
        + [pltpu.VMEM((rows, vw), BF16)] * 7
        + [pltpu.VMEM((rows, LANES), F32)] * 4
        + [pltpu.VMEM((n_seg * n_heads, 2 * chunk, GDN_DK), BF16),
           pltpu.VMEM((rows, vw), F32),
           pltpu.VMEM((n_seg * n_groups, chunk, hpg * chunk), BF16),
           pltpu.VMEM((rows, vw), BF16)]
    )
    body = functools.partial(_odd_body, has_state=has_state, chunk=chunk, phase_major=phase_major)
    return pl.pallas_call(
        body,
        grid=(bsz // batch_block, n_tiles),
        in_specs=in_specs,
        out_specs=out_specs,
        out_shape=out_shape,
        scratch_shapes=scratch,
        compiler_params=pltpu.CompilerParams(
            dimension_semantics=("arbitrary", "arbitrary"), vmem_limit_bytes=VMEM_LIMIT_BYTES),
        name="odd_layer_state" if has_state else "odd_layer_prompt",
    )(*args)


def _gate_weights(w_a, w_x):
    heads, blk, _ = w_a.shape
    per_tile = MXU_TILE // blk
    tiles = []
    for i in range(heads // per_tile):
        sl = slice(i * per_tile, (i + 1) * per_tile)
        tiles.append(jnp.concatenate(
            [jax.scipy.linalg.block_diag(*w_a[sl]), jax.scipy.linalg.block_diag(*w_x[sl])], axis=1))
    return jnp.stack(tiles).astype(BF16)


def _rope_inv_row():
    half = ROT_DIM // 2
    inv = ROPE_THETA ** (-(jnp.arange(half, dtype=F32) * 2.0 / ROT_DIM))
    per_head = jnp.concatenate([inv, inv, jnp.zeros((HEAD_DIM - ROT_DIM,), F32)])
    return jnp.tile(per_head, LANES // HEAD_DIM).reshape(1, LANES)


def kernel(x_prompt, x_sample, state_lru_conv, state_lru_h, cache_swa_k, cache_swa_v, state_gdn_conv, state_gdn_s, c_prompt, c_sample, ev_mod_w, ev_mod_b, ev_norm_pre, ev_norm_post, ev_w_in, lru_conv_w, lru_conv_b, lru_w_a, lru_b_a, lru_w_x, lru_b_x, lru_lambda, swa_sinks, ev_w_out, od_mod_w, od_mod_b, od_norm_pre, od_norm_post, od_w_in, gdn_conv_w, gdn_a_log, gdn_dt_bias, gdn_head_norm, od_w_out):
    bp, seq, d = x_prompt.shape
    bs, dec_seq, _ = x_sample.shape
    lw = lru_conv_w.shape[-1]
    kvw = ATT_KV_HEADS * HEAD_DIM
    n_heads = gdn_a_log.shape[-1]
    cch = gdn_conv_w.shape[-1]
    vw = od_w_out.shape[1]

    n_c = bp + bs
    n_pad = -n_c % 16
    c_all = jnp.concatenate([c_prompt, c_sample, jnp.zeros((n_pad, d), F32)], axis=0)
    mod_ev, mod_od = _modulation(c_all, ev_mod_w[0], ev_mod_b[0], od_mod_w[0], od_mod_b[0])
    mod_ev = mod_ev.reshape(n_c + n_pad, 3, d)
    mod_od = mod_od.reshape(n_c + n_pad, 3, d)

    row = lambda a: a.reshape(1, -1)
    ev_consts = (row(ev_norm_pre[0]), row(ev_norm_post[0]), ev_w_in[0].astype(BF16), lru_conv_w[0],
                 row(lru_conv_b[0]), _gate_weights(lru_w_a[0], lru_w_x[0]), row(lru_b_a[0]), row(lru_b_x[0]),
                 row(lru_lambda[0]), _rope_inv_row(), swa_sinks[0], ev_w_out[0].astype(BF16))
    w_in_od = od_w_in[0]
    pad_lanes = lambda a: jnp.pad(a, ((0, 0), (0, LANES - a.shape[1])))
    head_row = lambda a: jnp.pad(a.reshape(1, -1), ((0, 0), (n_heads, LANES - 2 * n_heads)))
    od_consts = (row(od_norm_pre[0]), row(od_norm_post[0]), w_in_od.astype(BF16),
                 pad_lanes(w_in_od[:, cch + vw:]).astype(BF16), gdn_conv_w[0],
                 head_row(gdn_a_log[0]), head_row(gdn_dt_bias[0]), row(gdn_head_norm[0]),
                 od_w_out[0].astype(BF16))

    tile = min(PROMPT_TILE, seq)
    xp, lru_conv_p, lru_h_p, swa_k_p, swa_v_p = _even_layer(
        x_prompt, mod_ev[:bp], ev_consts, None, tile=tile, chunk=min(CHUNK, seq), batch_block=1,
        n_past_valid=0, pos0=0, k_out_rows=WINDOW)
    ev_state = (state_lru_conv[0], state_lru_h[0].reshape(bs, 1, lw),
                cache_swa_k[0].reshape(bs, WINDOW, kvw), cache_swa_v[0].reshape(bs, WINDOW, kvw))
    xs, lru_conv_s, lru_h_s, swa_k_s, swa_v_s = _even_layer(
        x_sample, mod_ev[bp:n_c], ev_consts, ev_state, tile=dec_seq, chunk=min(CHUNK, dec_seq), batch_block=bs,
        n_past_valid=WINDOW, pos0=PAST_LEN, k_out_rows=dec_seq)

    xp, gdn_conv_p, gdn_s_p = _odd_layer(
        xp, mod_od[:bp], od_consts, None, tile=tile, chunk=min(CHUNK, seq), batch_block=1)
    xs, gdn_conv_s, gdn_s_s = _odd_layer(
        xs, mod_od[bp:n_c], od_consts, (state_gdn_conv[0], state_gdn_s[0]),
        tile=dec_seq, chunk=min(CHUNK, dec_seq), batch_block=bs)

    kv_shape = lambda a: a.reshape(1, a.shape[0], a.shape[1], ATT_KV_HEADS, HEAD_DIM)
    return (xp, xs,
            lru_conv_p[None], lru_conv_s[None],
            lru_h_p.reshape(1, bp, lw), lru_h_s.reshape(1, bs, lw),
            kv_shape(swa_k_p), kv_shape(swa_k_s), kv_shape(swa_v_p), kv_shape(swa_v_s),
            gdn_conv_p[None], gdn_conv_s[None], gdn_s_p[None], gdn_s_s[None])
```

```python
import functools

import jax
import jax.numpy as jnp
from jax import lax
from jax.experimental import pallas as pl
from jax.experimental.pallas import tpu as pltpu

F32 = jnp.float32
BF16 = jnp.bfloat16

CHUNK = 64
EPS = 1e-6
CONV_W = 4
NEG_INF = -1e30
LRU_HEADS = 8
LRU_C = 8.0
HEAD_DIM = 64
ATT_KV_HEADS = 2
WINDOW = 128
ROT_DIM = HEAD_DIM // 4
ROPE_THETA = 500000.0
GDN_DK = 128
GDN_DV = 128
PAST_LEN = 4096

LANES = 128
SUBLANES = 8
MXU_TILE = 256
VMEM_LIMIT_BYTES = 56 * 1024 * 1024

PROMPT_TILE = 512
MOD_TILE = 512
CONV_PAD = SUBLANES


def _silu(x):
    return x * jax.nn.sigmoid(x)


def _expm1(x):
    u = jnp.exp(x)
    d = u - 1.0
    return jnp.where(u == 1.0, x, jnp.where(d == -1.0, -1.0, d * x / jnp.log(u)))


def _rms_scale(x):
    return lax.rsqrt(jnp.mean(x * x, axis=-1, keepdims=True) + EPS)


def _rms(x, g):
    return x * _rms_scale(x) * g


def _dot(a, b):
    return jnp.dot(a, b, preferred_element_type=F32)


def _dot_nt(a, b):
    return lax.dot_general(a, b, (((1,), (1,)), ((), ())), preferred_element_type=F32)


def _dot_tn(a, b):
    return lax.dot_general(a, b, (((0,), (0,)), ((), ())), preferred_element_type=F32)


def _for_each(n, body):
    if n == 1:
        body(0)
    else:
        def step(i, carry):
            body(i)
            return carry
        lax.fori_loop(0, n, step, 0)


def _rows(i, n, size=None):
    size = n if size is None else size
    if isinstance(i, int):
        return pl.ds(i * n, size)
    return pl.ds(pl.multiple_of(i * n, n), size)


def _mod_body(c_ref, w0_ref, b0_ref, w1_ref, b1_ref, o0_ref, o1_ref):
    c = _silu(c_ref[...]).astype(BF16)
    o0_ref[...] = _dot(c, w0_ref[...].astype(BF16)) + b0_ref[...]
    o1_ref[...] = _dot(c, w1_ref[...].astype(BF16)) + b1_ref[...]


def _modulation(c_all, w0, b0, w1, b1):
    n, d = c_all.shape
    d3 = w0.shape[1]
    wspec = pl.BlockSpec((d, MOD_TILE), lambda j: (0, j))
    bspec = pl.BlockSpec((1, MOD_TILE), lambda j: (0, j))
    ospec = pl.BlockSpec((n, MOD_TILE), lambda j: (0, j))
    return pl.pallas_call(
        _mod_body,
        grid=(d3 // MOD_TILE,),
        in_specs=[pl.BlockSpec((n, d), lambda j: (0, 0)), wspec, bspec, wspec, bspec],
        out_specs=[ospec, ospec],
        out_shape=[jax.ShapeDtypeStruct((n, d3), F32)] * 2,
        name="adaln_modulation",
    )(c_all, w0, b0.reshape(1, d3), w1, b1.reshape(1, d3))


def _pre_norm(x_ref, mod_ref, norm_pre_ref):
    bb, tc, d = x_ref.shape
    x = x_ref[...]
    gain = norm_pre_ref[...] * (1.0 + mod_ref[:, 1:2, :])
    h = x * _rms_scale(x) * gain + mod_ref[:, 0:1, :]
    return h.reshape(bb * tc, d)


def _post_residual(x_ref, mod_ref, norm_post_ref, y, o_ref, rows=None):
    bb, tc, d = x_ref.shape
    yn = y * _rms_scale(y)
    if rows is None:
        gain = mod_ref[:, 2:3, :] * norm_post_ref[...]
        o_ref[...] = x_ref[...] + gain * yn.reshape(bb, tc, d)
    else:
        assert bb == 1
        o_ref[0, rows, :] = x_ref[0, rows, :] + (mod_ref[0, 2:3, :] * norm_post_ref[...]) * yn


def _run_interleaved(stage_lists):
    tagged = [((i + 0.5) / len(stages), k, i, stage)
              for k, stages in enumerate(stage_lists) for i, stage in enumerate(stages)]
    for _, _, _, stage in sorted(tagged, key=lambda entry: entry[:3]):
        stage()


def _causal_conv(ext_ref, bi, tc, w_ref):
    ext = ext_ref[bi]
    acc = pltpu.roll(ext, CONV_W - 1, axis=0)[CONV_PAD:] * w_ref[0:1, :]
    for j in range(1, CONV_W - 1):
        acc = acc + pltpu.roll(ext, CONV_W - 1 - j, axis=0)[CONV_PAD:] * w_ref[j:j + 1, :]
    return acc + ext[CONV_PAD:] * w_ref[CONV_W - 1:CONV_W, :]


def _scan_phase_major(a, b, h0, chunk):
    tc, width = a.shape
    row8 = lax.broadcasted_iota(jnp.int32, (SUBLANES, width), 0)
    carry = h0
    out = []
    for c in range(tc // chunk):
        acc_a, acc_b = [], []
        for r in range(SUBLANES):
            blk = slice(c * chunk + r * SUBLANES, c * chunk + (r + 1) * SUBLANES)
            if r == 0:
                acc_a.append(a[blk])
                acc_b.append(b[blk])
            else:
                acc_b.append(a[blk] * acc_b[-1] + b[blk])
                acc_a.append(a[blk] * acc_a[-1])
        tot_a, tot_b = acc_a[-1], acc_b[-1]
        for sft in (1, 2, 4):
            keep = row8 >= sft
            tot_b = jnp.where(keep, tot_a * pltpu.roll(tot_b, sft, axis=0) + tot_b, tot_b)
            tot_a = jnp.where(keep, tot_a * pltpu.roll(tot_a, sft, axis=0), tot_a)
        h_out = tot_b + tot_a * carry
        h_in = jnp.where(row8 == 0, carry, pltpu.roll(h_out, 1, axis=0))
        out += [acc_b[r] + acc_a[r] * h_in for r in range(SUBLANES)]
        carry = h_out[SUBLANES - 1:SUBLANES, :]
    return jnp.concatenate(out, axis=0), carry


def _even_body(*refs, has_state, n_past_valid, pos0, chunk, n_tiles, k_out_rows, phase_major):
    refs = list(refs)
    x_ref = refs.pop(0)
    perm_ref = refs.pop(0) if phase_major else None
    (mod_ref, npre_ref, npost_ref, win_ref, cw_ref, cb_ref, wg_ref, ba_ref, bx_ref,
     lam_ref, inv_ref, sinks_ref, wout_ref) = refs[:13]
    refs = refs[13:]
    if has_state:
        conv0_ref, h0_ref, k0_ref, v0_ref = refs[:4]
        refs = refs[4:]
    (y_ref, nconv_ref, nh_ref, nk_ref, nv_ref,
     hm_ref, u_ref, ext_ref, a_ref, b_ref, hs_ref, knat_ref, vnat_ref, q_ref, mix_ref, cosl_ref, sinl_ref) = refs
    bb, tc, _ = x_ref.shape
    lw = a_ref.shape[1]
    aw = q_ref.shape[1]
    kvw = knat_ref.shape[2]
    t = pl.program_id(1)
    cpt = tc // chunk
    m = WINDOW + chunk
    o_q, o_k, o_v, o_gb = 2 * lw, 2 * lw + aw, 2 * lw + aw + kvw, 2 * lw + aw + 2 * kvw

    @pl.when(t == 0)
    def _():
        ext_ref[...] = jnp.zeros(ext_ref.shape, F32)
        if has_state:
            ext_ref[:, CONV_PAD - 3:CONV_PAD, :] = conv0_ref[...]
            nh_ref[...] = h0_ref[...]
            knat_ref[:, 0:WINDOW, :] = k0_ref[...]
            vnat_ref[:, 0:WINDOW, :] = v0_ref[...]
        else:
            nh_ref[...] = jnp.zeros(nh_ref.shape, F32)
            knat_ref[:, 0:WINDOW, :] = jnp.zeros((bb, WINDOW, kvw), F32)
            vnat_ref[:, 0:WINDOW, :] = jnp.zeros((bb, WINDOW, kvw), F32)

    def permute_rows(v):
        n = perm_ref.shape[0]
        return jnp.concatenate(
            [_dot(perm_ref[...], v[i * n:(i + 1) * n]).astype(BF16) for i in range(v.shape[0] // n)], axis=0)

    hmod_time = _pre_norm(x_ref, mod_ref, npre_ref).astype(BF16)
    hm_ref[...] = permute_rows(hmod_time) if phase_major else hmod_time

    tile_start = (pos0 + t * tc).astype(F32) * inv_ref[...]
    cos_start, sin_start = jnp.cos(tile_start), jnp.sin(tile_start)

    def make_rope(cos_in_tile, sin_in_tile):
        cos_t = cos_start * cos_in_tile - sin_start * sin_in_tile
        sin_t = sin_start * cos_in_tile + cos_start * sin_in_tile
        lane = lax.broadcasted_iota(jnp.int32, cos_t.shape, 1) % HEAD_DIM
        half = ROT_DIM // 2
        sin_a = jnp.where(lane < half, -sin_t, 0.0)
        sin_b = jnp.where(lane >= half, sin_t, 0.0)

        def rope(xcol):
            return (xcol * cos_t + pltpu.roll(xcol, LANES - half, axis=1) * sin_a
                    + pltpu.roll(xcol, half, axis=1) * sin_b)
        return rope

    @pl.when(t == 0)
    def _():
        row_t = lax.broadcasted_iota(jnp.int32, (tc, LANES), 0)
        if phase_major:
            row_t = (row_t // chunk) * chunk + _time_of_row(row_t % chunk)
        in_tile = row_t.astype(F32) * inv_ref[...]
        cosl_ref[...] = jnp.cos(in_tile)
        sinl_ref[...] = jnp.sin(in_tile)

    rope = make_rope(cosl_ref[...], sinl_ref[...])

    neg_c_softplus = -LRU_C * jax.nn.softplus(-lam_ref[...])
    row8 = lax.broadcasted_iota(jnp.int32, (SUBLANES, lw), 0)

    def project(cols):
        u_ref[:, cols] = _dot(hm_ref[...], win_ref[:, cols])

    def recurrence_inputs(bi):
        rows = _rows(bi, tc)
        if phase_major:
            xc = _conv_phase_major(u_ref, ext_ref, nconv_ref, cw_ref, tc, chunk, slice(0, lw)) + cb_ref[...]
        else:
            ext_ref[bi, CONV_PAD:CONV_PAD + tc, :] = u_ref[rows, 0:lw]
            xc = _causal_conv(ext_ref, bi, tc, cw_ref) + cb_ref[...]
            nconv_ref[bi, :, :] = ext_ref[bi, CONV_PAD + tc - 3:CONV_PAD + tc, :]
            ext_ref[bi, CONV_PAD - 3:CONV_PAD, :] = ext_ref[bi, CONV_PAD + tc - 3:CONV_PAD + tc, :]
        xcb = xc.astype(BF16)
        halves = [_dot(xcb[:, i * MXU_TILE:(i + 1) * MXU_TILE], wg_ref[i]) for i in range(lw // MXU_TILE)]
        r = jax.nn.sigmoid(jnp.concatenate([g[:, :MXU_TILE] for g in halves], axis=1) + ba_ref[...])
        ig = jax.nn.sigmoid(jnp.concatenate([g[:, MXU_TILE:] for g in halves], axis=1) + bx_ref[...])
        log_a = r * neg_c_softplus
        a_ref[...] = jnp.exp(log_a)
        b_ref[...] = jnp.sqrt(-_expm1(2.0 * log_a)) * (ig * xc)

    def recurrence(bi):
        if phase_major:
            hs_ref[...], nh_ref[bi, :, :] = _scan_phase_major(a_ref[...], b_ref[...], nh_ref[bi, :, :], chunk)
            return

        def scan_block(j, hc):
            r0 = pl.multiple_of(j * SUBLANES, SUBLANES)
            a = a_ref[pl.ds(r0, SUBLANES), :]
            b = b_ref[pl.ds(r0, SUBLANES), :]
            for sft in (1, 2, 4):
                keep = row8 >= sft
                b = jnp.where(keep, a * pltpu.roll(b, sft, axis=0) + b, b)
                a = jnp.where(keep, a * pltpu.roll(a, sft, axis=0), a)
            h = a * hc + b
            hs_ref[pl.ds(r0, SUBLANES), :] = h
            return h[SUBLANES - 1:SUBLANES, :]

        nh_ref[bi, :, :] = lax.fori_loop(0, tc // SUBLANES, scan_block, nh_ref[bi, :, :])

    def recurrence_output(bi):
        rows = _rows(bi, tc)
        mix_ref[rows, 0:lw] = (hs_ref[...] * _silu(u_ref[rows, lw:2 * lw])).astype(BF16)

    def attention_inputs(bi):
        rows = _rows(bi, tc)
        for j in range(aw // LANES):
            qcol = rope(u_ref[rows, o_q + j * LANES:o_q + (j + 1) * LANES])
            q_ref[rows, j * LANES:(j + 1) * LANES] = (qcol * (HEAD_DIM ** -0.5)).astype(BF16)
        knat_ref[bi, WINDOW:WINDOW + tc, :] = rope(u_ref[rows, o_k:o_k + kvw])
        vnat_ref[bi, WINDOW:WINDOW + tc, :] = u_ref[rows, o_v:o_v + kvw]
        if not phase_major:
            nk_ref[bi, :, :] = knat_ref[bi, WINDOW + tc - k_out_rows:WINDOW + tc, :]
            nv_ref[bi, :, :] = vnat_ref[bi, WINDOW + tc - k_out_rows:WINDOW + tc, :]

    def per_batch(bi):
        recurrence_inputs(bi)
        recurrence(bi)
        recurrence_output(bi)
        attention_inputs(bi)

    if phase_major:
        project(slice(0, lw))
        later = [slice(lw, 2 * lw), slice(o_q, o_k), slice(o_k, o_gb), slice(o_gb, o_gb + aw)]
        _run_interleaved([[functools.partial(project, cols) for cols in later],
                          [functools.partial(recurrence_inputs, 0), functools.partial(recurrence, 0)]])
        recurrence_output(0)
        attention_inputs(0)
    else:
        project(slice(0, win_ref.shape[1]))
        _for_each(bb, per_batch)

    if phase_major:
        @pl.when(t == n_tiles - 1)
        def _():
            first = tc - k_out_rows
            kv = _dot(hmod_time[first:], win_ref[:, o_k:o_k + 2 * kvw])
            in_tile = (first + lax.broadcasted_iota(jnp.int32, (k_out_rows, LANES), 0)).astype(F32) * inv_ref[...]
            rope_last = make_rope(jnp.cos(in_tile), jnp.sin(in_tile))
            nk_ref[0] = rope_last(kv[:, 0:kvw])
            nv_ref[0] = kv[:, kvw:2 * kvw]

    lo_lane = lax.broadcasted_iota(jnp.int32, (m, LANES), 1) < HEAD_DIM
    lo_q = lax.broadcasted_iota(jnp.int32, (chunk, LANES), 1) < HEAD_DIM
    group = (aw // HEAD_DIM) // ATT_KV_HEADS

    sinks = [jnp.concatenate([jnp.full((chunk, 1), sinks_ref[h * group + g], F32) for g in range(group)], axis=0)
             for h in range(ATT_KV_HEADS)]

    lane_m = lax.broadcasted_iota(jnp.int32, (m, LANES), 1)

    def attend(segments):
        units = []
        for bi, c in segments:
            rows = _rows(bi * cpt + c, chunk)
            krows = _rows(c, chunk, m)
            kseg = knat_ref[bi, krows, :]
            vseg = vnat_ref[bi, krows, :]
            krot = pltpu.roll(kseg, HEAD_DIM, axis=1)
            vrot = pltpu.roll(vseg, HEAD_DIM, axis=1)
            valid = None
            if n_past_valid < WINDOW:
                key_t = lax.broadcasted_iota(jnp.int32, (1, m), 1)
                if phase_major:
                    key_t = (key_t // chunk) * chunk + _time_of_row(key_t % chunk)
                valid = t * tc + c * chunk - WINDOW + key_t >= -n_past_valid
            for h in range(ATT_KV_HEADS):
                kd = (jnp.where(lo_lane, kseg, krot) if h == 0 else jnp.where(lo_lane, krot, kseg)).astype(BF16)
                v_lo = jnp.where(lo_lane, vseg if h == 0 else vrot, jnp.where(lane_m == HEAD_DIM, 1.0, 0.0))
                v_hi = jnp.where(lo_lane, jnp.where(lane_m == 0, 1.0, 0.0), vrot if h == 0 else vseg)
                units.append(dict(rows=rows, h=h, valid=valid, kd=kd, v_lo=v_lo.astype(BF16),
                                  v_hi=v_hi.astype(BF16)))
        yield
        for un in units:
            rows, h = un["rows"], un["h"]
            cols = [q_ref[rows, (h * group // 2 + j) * LANES:(h * group // 2 + j + 1) * LANES]
                    for j in range(group // 2)]
            zero = jnp.zeros_like(cols[0])
            qstack = jnp.concatenate(
                [part for qc in cols for part in (jnp.where(lo_q, qc, zero), jnp.where(lo_q, zero, qc))], axis=0)
            sc = _dot_nt(qstack, un["kd"])
            un["sc"] = sc if un["valid"] is None else jnp.where(un["valid"], sc, NEG_INF)
        yield
        for un in units:
            un["mx"] = jnp.maximum(jnp.max(un["sc"], axis=-1, keepdims=True), sinks[un["h"]])
        yield
        for un in units:
            un["p"] = jnp.exp(un["sc"] - un["mx"]).astype(BF16)
            un["sink_p"] = jnp.exp(sinks[un["h"]] - un["mx"])
        yield
        for un in units:
            rows, h, p, sink_p = un["rows"], un["h"], un["p"], un["sink_p"]
            for j in range(group // 2):
                r_lo = slice((2 * j) * chunk, (2 * j + 1) * chunk)
                r_hi = slice((2 * j + 1) * chunk, (2 * j + 2) * chunk)
                o_lo = _dot(p[r_lo], un["v_lo"])
                o_hi = _dot(p[r_hi], un["v_hi"])
                den_lo = o_lo[:, HEAD_DIM:HEAD_DIM + 1] + sink_p[r_lo]
                den_hi = o_hi[:, 0:1] + sink_p[r_hi]
                o = jnp.where(lo_q, o_lo / den_lo, o_hi / den_hi)
                col = h * group // 2 + j
                gate = _silu(u_ref[rows, o_gb + col * LANES:o_gb + (col + 1) * LANES])
                mix_ref[rows, lw + col * LANES:lw + (col + 1) * LANES] = (o * gate).astype(BF16)

    def output_stages(rows):
        held = {}

        def gather():
            mixed = mix_ref[rows, :]
            held["mixed"] = permute_rows(mixed) if phase_major else mixed

        def project_out():
            held["y"] = _dot(held["mixed"], wout_ref[...])

        def finish():
            _post_residual(x_ref, mod_ref, npost_ref, held["y"], y_ref, rows if n_row_groups > 1 else None)

        return [gather, project_out, finish]

    group_rows = min(bb * tc, MXU_TILE)
    n_row_groups = bb * tc // group_rows
    assert (bb * tc) % group_rows == 0 and group_rows % chunk == 0 and (bb == 1 or n_row_groups == 1)
    segments = [(bi, c) for bi in range(bb) for c in range(cpt)]
    per_group = len(segments) // n_row_groups
    for g in range(n_row_groups + 1):
        lists = []
        if g < n_row_groups:
            wave = attend(segments[g * per_group:(g + 1) * per_group])
            lists.append([functools.partial(next, wave, None)] * 5)
        if g >= 1:
            lists.append(output_stages(pl.ds((g - 1) * group_rows, group_rows)))
        _run_interleaved(lists)

    if n_tiles > 1:
        knat_ref[:, 0:WINDOW, :] = knat_ref[:, tc:tc + WINDOW, :]
        vnat_ref[:, 0:WINDOW, :] = vnat_ref[:, tc:tc + WINDOW, :]


def _runs_phase_major(chunk, batch_block, has_state):
    return chunk == SUBLANES * SUBLANES and batch_block == 1 and not has_state


def _conv_history_shape(phase_major, batch_block, tile, channels):
    if phase_major:
        return (CONV_W - 1, SUBLANES, channels)
    return (batch_block, CONV_PAD + tile, channels)


def _const_spec(shape):
    zeros = (0,) * len(shape)
    return pl.BlockSpec(shape, lambda b, t: zeros)


def _even_layer(x, mod, consts, state, *, tile, chunk, batch_block, n_past_valid, pos0, k_out_rows):
    bsz, seq, d = x.shape
    (npre, npost, win, cw, cb, wg, b_a, b_x, lam, inv_row, sinks, wout) = consts
    lw = cw.shape[1]
    kvw = ATT_KV_HEADS * HEAD_DIM
    aw = (win.shape[1] - 2 * lw - 2 * kvw) // 2
    n_tiles = seq // tile
    assert seq % tile == 0 and tile % chunk == 0 and bsz % batch_block == 0
    assert n_tiles == 1 or tile >= WINDOW
    rows = batch_block * tile
    has_state = state is not None
    phase_major = _runs_phase_major(chunk, batch_block, has_state)
    assert not phase_major or tile >= k_out_rows

    def bspec(shape):
        nd = len(shape)
        return pl.BlockSpec((batch_block,) + shape, lambda b, t: (b,) + (0,) * nd)

    in_specs = [pl.BlockSpec((batch_block, tile, d), lambda b, t: (b, t, 0))]
    args = [x]
    if phase_major:
        perm = _phase_major_matrix(min(rows, MXU_TILE), chunk)
        assert rows % perm.shape[0] == 0
        in_specs.append(_const_spec(perm.shape))
        args.append(perm)
    in_specs += [
        bspec((3, d)),
        _const_spec(npre.shape), _const_spec(npost.shape), _const_spec(win.shape),
        _const_spec(cw.shape), _const_spec(cb.shape), _const_spec(wg.shape),
        _const_spec(b_a.shape), _const_spec(b_x.shape), _const_spec(lam.shape), _const_spec(inv_row.shape),
        pl.BlockSpec(memory_space=pltpu.SMEM),
        _const_spec(wout.shape),
    ]
    args += [mod, npre, npost, win, cw, cb, wg, b_a, b_x, lam, inv_row, sinks, wout]
    if has_state:
        in_specs += [bspec((CONV_W - 1, lw)), bspec((1, lw)), bspec((WINDOW, kvw)), bspec((WINDOW, kvw))]
        args += list(state)
    out_specs = [
        pl.BlockSpec((batch_block, tile, d), lambda b, t: (b, t, 0)),
        bspec((CONV_W - 1, lw)), bspec((1, lw)), bspec((k_out_rows, kvw)), bspec((k_out_rows, kvw)),
    ]
    out_shape = [
        jax.ShapeDtypeStruct((bsz, seq, d), F32),
        jax.ShapeDtypeStruct((bsz, CONV_W - 1, lw), F32),
        jax.ShapeDtypeStruct((bsz, 1, lw), F32),
        jax.ShapeDtypeStruct((bsz, k_out_rows, kvw), F32),
        jax.ShapeDtypeStruct((bsz, k_out_rows, kvw), F32),
    ]
    scratch = [
        pltpu.VMEM((rows, d), BF16),
        pltpu.VMEM((rows, win.shape[1]), F32),
        pltpu.VMEM(_conv_history_shape(phase_major, batch_block, tile, lw), F32),
        pltpu.VMEM((tile, lw), F32), pltpu.VMEM((tile, lw), F32), pltpu.VMEM((tile, lw), F32),
        pltpu.VMEM((batch_block, WINDOW + tile, kvw), F32),
        pltpu.VMEM((batch_block, WINDOW + tile, kvw), F32),
        pltpu.VMEM((rows, aw), BF16),
        pltpu.VMEM((rows, lw + aw), BF16),
        pltpu.VMEM((tile, LANES), F32), pltpu.VMEM((tile, LANES), F32),
    ]
    body = functools.partial(_even_body, has_state=has_state, n_past_valid=n_past_valid, pos0=pos0,
                             chunk=chunk, n_tiles=n_tiles, k_out_rows=k_out_rows, phase_major=phase_major)
    return pl.pallas_call(
        body,
        grid=(bsz // batch_block, n_tiles),
        in_specs=in_specs,
        out_specs=out_specs,
        out_shape=out_shape,
        scratch_shapes=scratch,
        compiler_params=pltpu.CompilerParams(
            dimension_semantics=("arbitrary", "arbitrary"), vmem_limit_bytes=VMEM_LIMIT_BYTES),
        name="even_layer_state" if has_state else "even_layer_prompt",
    )(*args)


def _gdn_groups(chunk, n_heads):
    hpg = max(1, min(n_heads, MXU_TILE // chunk))
    assert n_heads % hpg == 0
    return hpg, n_heads // hpg


def _time_of_row(i):
    return (i % SUBLANES) * SUBLANES + i // SUBLANES


def _phase_major_matrix(n_rows, chunk):
    i = jnp.arange(n_rows)
    src = (i // chunk) * chunk + _time_of_row(i % chunk)
    return (src[:, None] == i[None, :]).astype(BF16)


def _conv_phase_major(u_ref, hist_ref, nconv_ref, w_ref, tc, chunk, cols):
    cpt = tc // chunk
    blocks = [u_ref[SUBLANES * b:SUBLANES * (b + 1), cols] for b in range(tc // SUBLANES)]
    taps_w = [w_ref[j:j + 1, cols] for j in range(CONV_W)]
    top_row = lax.broadcasted_iota(jnp.int32, blocks[0].shape, 0) == 0
    first_late = SUBLANES - (CONV_W - 1)
    prev = {r: pltpu.roll(hist_ref[r - first_late, :, cols], 1, axis=0) for r in range(first_late, SUBLANES)}
    out = []
    for c in range(cpt):
        late = {}
        for r in range(first_late, SUBLANES):
            moved = pltpu.roll(blocks[SUBLANES * c + r], 1, axis=0)
            late[r] = jnp.where(top_row, prev[r], moved)
            prev[r] = moved
        for r in range(SUBLANES):
            acc = None
            for j in range(CONV_W):
                s = CONV_W - 1 - j
                tap = blocks[SUBLANES * c + r - s] if r >= s else late[r - s + SUBLANES]
                acc = tap * taps_w[j] if acc is None else acc + tap * taps_w[j]
            out.append(acc)
    for i in range(CONV_W - 1):
        last = blocks[SUBLANES * (cpt - 1) + first_late + i]
        hist_ref[i, :, cols] = last
        nconv_ref[0, i:i + 1, cols] = last[SUBLANES - 1:SUBLANES, :]
    return jnp.concatenate(out, axis=0)


def _odd_body(*refs, has_state, chunk, phase_major):
    refs = list(refs)
    x_ref = refs.pop(0)
    perm_ref = refs.pop(0) if phase_major else None
    (mod_ref, npre_ref, npost_ref, win_ref, wba_ref, cw_ref, alog_ref, dtb_ref, hn_ref, wout_ref) = refs[:10]
    refs = refs[10:]
    if has_state:
        conv0_ref, s0_ref = refs[:2]
        refs = refs[2:]
    (y_ref, nconv_ref, ns_ref,
     hm_ref, u_ref, ext_ref, q_ref, k_ref, kb_ref, kbe_ref, vb_ref, qe_ref, kd_ref,
     beta_ref, gcum_ref, eg_ref, dec_ref, wq_ref, uu_ref, qk_ref, mix_ref) = refs
    bb, tc, _ = x_ref.shape
    n_heads = ns_ref.shape[1]
    kw = n_heads * GDN_DK
    vw = n_heads * GDN_DV
    cch = 2 * kw + vw
    rows_all = bb * tc
    t = pl.program_id(1)
    cpt = tc // chunk
    n_steps = (chunk - 1).bit_length()
    assert n_steps >= 2
    hpg, n_groups = _gdn_groups(chunk, n_heads)
    lw = hpg * chunk

    @pl.when(t == 0)
    def _():
        ext_ref[...] = jnp.zeros(ext_ref.shape, F32)
        if has_state:
            ext_ref[:, CONV_PAD - 3:CONV_PAD, :] = conv0_ref[...]
            ns_ref[...] = s0_ref[...]
        else:
            ns_ref[...] = jnp.zeros(ns_ref.shape, F32)

    def permute_rows(v):
        n = perm_ref.shape[0]
        return jnp.concatenate(
            [_dot(perm_ref[...], v[i * n:(i + 1) * n]).astype(BF16) for i in range(v.shape[0] // n)], axis=0)

    hmod = _pre_norm(x_ref, mod_ref, npre_ref).astype(BF16)
    if phase_major:
        hmod = permute_rows(hmod)
    hm_ref[...] = hmod
    ba = _dot(hmod, wba_ref[...])
    beta_ref[...] = jax.nn.sigmoid(ba)
    g = -jnp.exp(alog_ref[...]) * jax.nn.softplus(ba + dtb_ref[...])
    if phase_major:
        row8 = lax.broadcasted_iota(jnp.int32, (SUBLANES, LANES), 0)
        parts = []
        for c in range(rows_all // chunk):
            run = []
            for r in range(SUBLANES):
                blk = g[c * chunk + r * SUBLANES:c * chunk + (r + 1) * SUBLANES]
                run.append(blk if r == 0 else run[-1] + blk)
            total = run[-1]
            incl = total
            for sft in (1, 2, 4):
                incl = jnp.where(row8 >= sft, incl + pltpu.roll(incl, sft, axis=0), incl)
            parts += [blk + (incl - total) for blk in run]
        g = jnp.concatenate(parts, axis=0)
    else:
        row_in_chunk = lax.broadcasted_iota(jnp.int32, (rows_all, LANES), 0) % chunk
        sft = 1
        while sft < chunk:
            g = jnp.where(row_in_chunk >= sft, g + pltpu.roll(g, sft, axis=0), g)
            sft *= 2
    gcum_ref[...] = g
    eg_ref[...] = jnp.exp(g)
    g3 = g.reshape(rows_all // chunk, chunk, LANES)
    dec_ref[...] = jnp.exp(g3[:, chunk - 1:chunk, :] - g3).reshape(rows_all, LANES)

    def project(cols):
        u_ref[:, cols] = _dot(hm_ref[...], win_ref[:, cols])

    def head_operands(rows, h, qh, kh, vh):
        hc = slice(h * GDN_DK, (h + 1) * GDN_DK)
        qh = qh * (lax.rsqrt(jnp.sum(qh * qh, axis=-1, keepdims=True) + EPS) * (GDN_DK ** -0.5))
        kh = kh * lax.rsqrt(jnp.sum(kh * kh, axis=-1, keepdims=True) + EPS)
        beta = beta_ref[rows, :][:, h:h + 1]
        eg = eg_ref[rows, :][:, n_heads + h:n_heads + h + 1]
        kb = kh * beta
        q_ref[rows, hc] = qh.astype(BF16)
        k_ref[rows, hc] = kh.astype(BF16)
        kb_ref[rows, hc] = kb.astype(BF16)
        kbe_ref[rows, hc] = (kb * eg).astype(BF16)
        vb_ref[rows, hc] = (vh * beta).astype(BF16)
        qe_ref[rows, hc] = (qh * eg).astype(BF16)
        kd_ref[rows, hc] = (kh * dec_ref[rows, :][:, n_heads + h:n_heads + h + 1]).astype(BF16)

    def per_batch(bi):
        rows = _rows(bi, tc)
        ext_ref[bi, CONV_PAD:CONV_PAD + tc, :] = u_ref[rows, 0:cch]
        qkv = _silu(_causal_conv(ext_ref, bi, tc, cw_ref))
        nconv_ref[bi, :, :] = ext_ref[bi, CONV_PAD + tc - 3:CONV_PAD + tc, :]
        ext_ref[bi, CONV_PAD - 3:CONV_PAD, :] = ext_ref[bi, CONV_PAD + tc - 3:CONV_PAD + tc, :]
        for h in range(n_heads):
            head_operands(rows, h, qkv[:, h * GDN_DK:(h + 1) * GDN_DK],
                          qkv[:, kw + h * GDN_DK:kw + (h + 1) * GDN_DK],
                          qkv[:, 2 * kw + h * GDN_DV:2 * kw + (h + 1) * GDN_DV])

    pair = MXU_TILE // GDN_DK

    def pair_stages(j):
        col_sets = [slice(base + j * MXU_TILE, base + (j + 1) * MXU_TILE) for base in (0, kw, 2 * kw)]

        def operands():
            rows = pl.ds(0, tc)
            q2, k2, v2 = [_silu(_conv_phase_major(u_ref, ext_ref, nconv_ref, cw_ref, tc, chunk, cols))
                          for cols in col_sets]
            for i in range(pair):
                part = slice(i * GDN_DK, (i + 1) * GDN_DK)
                head_operands(rows, pair * j + i, q2[:, part], k2[:, part], v2[:, part])

        return [functools.partial(project, cols) for cols in col_sets] + [operands]

    if not phase_major:
        project(slice(0, cch + vw))
        _for_each(bb, per_batch)

    lane_w = lax.broadcasted_iota(jnp.int32, (chunk, lw), 1)
    row_w = lax.broadcasted_iota(jnp.int32, (chunk, lw), 0)
    blk_w = lane_w // chunk
    col_w = lane_w % chunk
    diag_w = row_w == col_w
    if phase_major:
        row_w, col_w = _time_of_row(row_w), _time_of_row(col_w)
    incl_w = row_w >= col_w
    strict_w = row_w > col_w
    eye_w = jnp.where(diag_w, 1.0, 0.0).astype(F32)
    feat_blk = lax.broadcasted_iota(jnp.int32, (chunk, hpg * GDN_DK), 1) // GDN_DK

    def block_diag(m_b):
        return jnp.concatenate([jnp.where(blk_w == hh, m_b, jnp.zeros_like(m_b)) for hh in range(hpg)], axis=0)

    def head_cols(h):
        return slice(h * GDN_DK, (h + 1) * GDN_DK)

    def wy_stages(segs, head_groups):
        units = []

        def build():
            for seg in segs:
                rows = _rows(seg, chunk)
                gt = gcum_ref[rows, :]
                for gi in head_groups:
                    gcols = slice(gi * hpg * GDN_DK, (gi + 1) * hpg * GDN_DK)
                    k4 = k_ref[rows, gcols]
                    k_bd = jnp.concatenate(
                        [jnp.where(feat_blk == hh, k4, jnp.zeros_like(k4)) for hh in range(hpg)], axis=0)
                    kk = _dot_nt(jnp.concatenate([kb_ref[rows, gcols], q_ref[rows, gcols]], axis=0), k_bd)
                    gcol = jnp.zeros((chunk, lw), F32)
                    for hh in range(hpg):
                        ln = n_heads + gi * hpg + hh
                        gcol = jnp.where(blk_w == hh, gt[:, ln:ln + 1], gcol)
                    grow = jnp.sum(jnp.where(diag_w, gcol, 0.0), axis=0, keepdims=True)
                    decay = jnp.exp(jnp.where(incl_w, gcol - grow, 0.0))
                    p_b = (-(kk[:chunk] * jnp.where(strict_w, decay, 0.0))).astype(BF16)
                    qk_ref[seg * n_groups + gi] = (kk[chunk:] * jnp.where(incl_w, decay, 0.0)).astype(BF16)
                    units.append(dict(seg=seg, rows=rows, gi=gi, p=p_b, t=eye_w + p_b.astype(F32)))

        def square():
            for un in units:
                un["p"] = _dot(un["p"], block_diag(un["p"])).astype(BF16)

        def extend_and_square():
            for un in units:
                out = _dot(jnp.concatenate([un["t"].astype(BF16), un["p"]], axis=0), block_diag(un["p"]))
                un["t"] = un["t"] + out[:chunk]
                un["p"] = out[chunk:].astype(BF16)

        def extend():
            for un in units:
                un["t"] = un["t"] + _dot(un["t"].astype(BF16), block_diag(un["p"]))

        def apply():
            for un in units:
                seg, rows, gi = un["seg"], un["rows"], un["gi"]
                heads = range(gi * hpg, (gi + 1) * hpg)
                rhs = jnp.concatenate(
                    [jnp.concatenate([kbe_ref[rows, head_cols(h)], vb_ref[rows, head_cols(h)]], axis=1)
                     for h in heads], axis=0)
                wu = _dot(block_diag(un["t"].astype(BF16)), rhs)
                for hh, h in enumerate(heads):
                    blk = wu[hh * chunk:(hh + 1) * chunk]
                    wq_ref[seg * n_heads + h] = jnp.concatenate(
                        [blk[:, :GDN_DK].astype(BF16), qe_ref[rows, head_cols(h)]], axis=0)
                    uu_ref[rows, head_cols(h)] = blk[:, GDN_DK:]

        return [build, square] + [extend_and_square] * (n_steps - 2) + [extend, apply]

    zero_b = jnp.zeros((chunk, GDN_DV), BF16)
    state = {(bi, h): ns_ref[bi, h] for bi in range(bb) for h in range(n_heads)}

    def recurrence_stages(segs):
        stages = []
        for c in sorted({seg % cpt for seg in segs}):
            wave = [seg for seg in segs if seg % cpt == c]
            pairs = [(seg, h) for seg in wave for h in range(n_heads)]
            held = {}

            def correct(wave=wave, pairs=pairs, held=held):
                for seg, h in pairs:
                    held["ws_qs", seg, h] = _dot(wq_ref[seg * n_heads + h], state[seg // cpt, h].astype(BF16))
                for seg, h in pairs:
                    held["v", seg, h] = (uu_ref[_rows(seg, chunk), head_cols(h)]
                                         - held["ws_qs", seg, h][:chunk]).astype(BF16)

            def advance(wave=wave, pairs=pairs, held=held):
                for seg in wave:
                    held["eg", seg] = jnp.exp(gcum_ref[pl.ds(seg * chunk + chunk - 1, 1), :])
                for seg, h in pairs:
                    state[seg // cpt, h] = (state[seg // cpt, h] * held["eg", seg][:, n_heads + h:n_heads + h + 1]
                                            + _dot_tn(kd_ref[_rows(seg, chunk), head_cols(h)], held["v", seg, h]))
                for seg in wave:
                    rows = _rows(seg, chunk)
                    for gi in range(n_groups):
                        heads = range(gi * hpg, (gi + 1) * hpg)
                        v_bd = jnp.concatenate(
                            [jnp.concatenate([held["v", seg, h] if j == hh else zero_b for j in range(hpg)], axis=1)
                             for hh, h in enumerate(heads)], axis=0)
                        o = (jnp.concatenate([held["ws_qs", seg, h][chunk:] for h in heads], axis=1)
                             + _dot(qk_ref[seg * n_groups + gi], v_bd))
                        for hh, h in enumerate(heads):
                            gate = _silu(u_ref[rows, cch + h * GDN_DV:cch + (h + 1) * GDN_DV])
                            o_h = _rms(o[:, hh * GDN_DV:(hh + 1) * GDN_DV], hn_ref[...])
                            mix_ref[rows, head_cols(h)] = (o_h * gate).astype(BF16)

            stages += [correct, advance]
        return stages

    def output_stages(rows):
        held = {}

        def gather():
            mixed = mix_ref[rows, :]
            held["mixed"] = permute_rows(mixed) if phase_major else mixed

        def project():
            held["y"] = _dot(held["mixed"], wout_ref[...])

        def finish():
            _post_residual(x_ref, mod_ref, npost_ref, held["y"], y_ref, rows if n_row_groups > 1 else None)

        return [gather, project, finish]

    group_rows = min(rows_all, MXU_TILE)
    n_row_groups = rows_all // group_rows
    assert rows_all % group_rows == 0 and group_rows % chunk == 0 and (bb == 1 or n_row_groups == 1)
    segs_of = [range(g * group_rows // chunk, (g + 1) * group_rows // chunk) for g in range(n_row_groups)]
    if phase_major:
        pairs_per_group = hpg // pair
        assert hpg % pair == 0
        work = [(segs, gi) for gi in range(n_groups) for segs in segs_of]
        ahead = [[stage for j in range(gi * pairs_per_group, (gi + 1) * pairs_per_group) for stage in pair_stages(j)]
                 for gi in range(n_groups)]
        ahead.append([functools.partial(project, slice(cch + i * MXU_TILE, cch + (i + 1) * MXU_TILE))
                      for i in range(vw // MXU_TILE)])
        for stage in ahead[0]:
            stage()
        for gi in range(n_groups):
            factors = [stage for segs, g in work if g == gi for stage in wy_stages(segs, [gi])]
            _run_interleaved([ahead[gi + 1], factors])
    else:
        for segs in segs_of:
            for stage in wy_stages(segs, range(n_groups)):
                stage()
    for segs in segs_of:
        for stage in recurrence_stages(segs):
            stage()
    for g in range(n_row_groups):
        for stage in output_stages(pl.ds(g * group_rows, group_rows)):
            stage()
    for bh, value in state.items():
        ns_ref[bh[0], bh[1]] = value


def _odd_layer(x, mod, consts, state, *, tile, chunk, batch_block):
    bsz, seq, d = x.shape
    (npre, npost, win, wba, cw, alog, dtb, hn, wout) = consts
    cch = cw.shape[1]
    vw = wout.shape[0]
    n_heads = vw // GDN_DV
    n_tiles = seq // tile
    assert seq % tile == 0 and tile % chunk == 0 and bsz % batch_block == 0
    rows = batch_block * tile
    has_state = state is not None
    phase_major = _runs_phase_major(chunk, batch_block, has_state)

    def bspec(shape):
        nd = len(shape)
        return pl.BlockSpec((batch_block,) + shape, lambda b, t: (b,) + (0,) * nd)

    in_specs = [pl.BlockSpec((batch_block, tile, d), lambda b, t: (b, t, 0))]
    args = [x]
    if phase_major:
        perm_rows = min(rows, MXU_TILE)
        assert rows % perm_rows == 0 and perm_rows % chunk == 0
        in_specs.append(_const_spec((perm_rows, perm_rows)))
        args.append(_phase_major_matrix(perm_rows, chunk))
    in_specs += [
        bspec((3, d)),
        _const_spec(npre.shape), _const_spec(npost.shape), _const_spec((d, cch + vw)), _const_spec(wba.shape),
        _const_spec(cw.shape), _const_spec(alog.shape), _const_spec(dtb.shape), _const_spec(hn.shape),
        _const_spec(wout.shape),
    ]
    args += [mod, npre, npost, win, wba, cw, alog, dtb, hn, wout]
    if has_state:
        in_specs += [bspec((CONV_W - 1, cch)), bspec((n_heads, GDN_DK, GDN_DV))]
        args += list(state)
    out_specs = [
        pl.BlockSpec((batch_block, tile, d), lambda b, t: (b, t, 0)),
        bspec((CONV_W - 1, cch)), bspec((n_heads, GDN_DK, GDN_DV)),
    ]
    out_shape = [
        jax.ShapeDtypeStruct((bsz, seq, d), F32),
        jax.ShapeDtypeStruct((bsz, CONV_W - 1, cch), F32),
        jax.ShapeDtypeStruct((bsz, n_heads, GDN_DK, GDN_DV), F32),
    ]
    hpg, n_groups = _gdn_groups(chunk, n_heads)
    n_seg = rows // chunk
    scratch = (
        [pltpu.VMEM((rows, d), BF16),
         pltpu.VMEM((rows, cch + vw), F32),
         pltpu.VMEM(_conv_history_shape(phase_major, batch_block, tile, cch), F32)]
        + [pltpu.VMEM((rows, vw), BF16)] * 7
        + [pltpu.VMEM((rows, LANES), F32)] * 4
        + [pltpu.VMEM((n_seg * n_heads, 2 * chunk, GDN_DK), BF16),
           pltpu.VMEM((rows, vw), F32),
           pltpu.VMEM((n_seg * n_groups, chunk, hpg * chunk), BF16),
           pltpu.VMEM((rows, vw), BF16)]
    )
    body = functools.partial(_odd_body, has_state=has_state, chunk=chunk, phase_major=phase_major)
    return pl.pallas_call(
        body,
        grid=(bsz // batch_block, n_tiles),
        in_specs=in_specs,
        out_specs=out_specs,
        out_shape=out_shape,
        scratch_shapes=scratch,
        compiler_params=pltpu.CompilerParams(
            dimension_semantics=("arbitrary", "arbitrary"), vmem_limit_bytes=VMEM_LIMIT_BYTES),
        name="odd_layer_state" if has_state else "odd_layer_prompt",
    )(*args)


def _gate_weights(w_a, w_x):
    heads, blk, _ = w_a.shape
    per_tile = MXU_TILE // blk
    tiles = []
    for i in range(heads // per_tile):
        sl = slice(i * per_tile, (i + 1) * per_tile)
        tiles.append(jnp.concatenate(
            [jax.scipy.linalg.block_diag(*w_a[sl]), jax.scipy.linalg.block_diag(*w_x[sl])], axis=1))
    return jnp.stack(tiles).astype(BF16)


def _rope_inv_row():
    half = ROT_DIM // 2
    inv = ROPE_THETA ** (-(jnp.arange(half, dtype=F32) * 2.0 / ROT_DIM))
    per_head = jnp.concatenate([inv, inv, jnp.zeros((HEAD_DIM - ROT_DIM,), F32)])
    return jnp.tile(per_head, LANES // HEAD_DIM).reshape(1, LANES)


def kernel(x_prompt, x_sample, state_lru_conv, state_lru_h, cache_swa_k, cache_swa_v, state_gdn_conv, state_gdn_s, c_prompt, c_sample, ev_mod_w, ev_mod_b, ev_norm_pre, ev_norm_post, ev_w_in, lru_conv_w, lru_conv_b, lru_w_a, lru_b_a, lru_w_x, lru_b_x, lru_lambda, swa_sinks, ev_w_out, od_mod_w, od_mod_b, od_norm_pre, od_norm_post, od_w_in, gdn_conv_w, gdn_a_log, gdn_dt_bias, gdn_head_norm, od_w_out):
    bp, seq, d = x_prompt.shape
    bs, dec_seq, _ = x_sample.shape
    lw = lru_conv_w.shape[-1]
    kvw = ATT_KV_HEADS * HEAD_DIM
    n_heads = gdn_a_log.shape[-1]
    cch = gdn_conv_w.shape[-1]
    vw = od_w_out.shape[1]

    n_c = bp + bs
    n_pad = -n_c % 16
    c_all = jnp.concatenate([c_prompt, c_sample, jnp.zeros((n_pad, d), F32)], axis=0)
    mod_ev, mod_od = _modulation(c_all, ev_mod_w[0], ev_mod_b[0], od_mod_w[0], od_mod_b[0])
    mod_ev = mod_ev.reshape(n_c + n_pad, 3, d)
    mod_od = mod_od.reshape(n_c + n_pad, 3, d)

    row = lambda a: a.reshape(1, -1)
    ev_consts = (row(ev_norm_pre[0]), row(ev_norm_post[0]), ev_w_in[0].astype(BF16), lru_conv_w[0],
                 row(lru_conv_b[0]), _gate_weights(lru_w_a[0], lru_w_x[0]), row(lru_b_a[0]), row(lru_b_x[0]),
                 row(lru_lambda[0]), _rope_inv_row(), swa_sinks[0], ev_w_out[0].astype(BF16))
    w_in_od = od_w_in[0]
    pad_lanes = lambda a: jnp.pad(a, ((0, 0), (0, LANES - a.shape[1])))
    head_row = lambda a: jnp.pad(a.reshape(1, -1), ((0, 0), (n_heads, LANES - 2 * n_heads)))
    od_consts = (row(od_norm_pre[0]), row(od_norm_post[0]), w_in_od.astype(BF16),
                 pad_lanes(w_in_od[:, cch + vw:]).astype(BF16), gdn_conv_w[0],
                 head_row(gdn_a_log[0]), head_row(gdn_dt_bias[0]), row(gdn_head_norm[0]),
                 od_w_out[0].astype(BF16))

    tile = min(PROMPT_TILE, seq)
    xp, lru_conv_p, lru_h_p, swa_k_p, swa_v_p = _even_layer(
        x_prompt, mod_ev[:bp], ev_consts, None, tile=tile, chunk=min(CHUNK, seq), batch_block=1,
        n_past_valid=0, pos0=0, k_out_rows=WINDOW)
    ev_state = (state_lru_conv[0], state_lru_h[0].reshape(bs, 1, lw),
                cache_swa_k[0].reshape(bs, WINDOW, kvw), cache_swa_v[0].reshape(bs, WINDOW, kvw))
    xs, lru_conv_s, lru_h_s, swa_k_s, swa_v_s = _even_layer(
        x_sample, mod_ev[bp:n_c], ev_consts, ev_state, tile=dec_seq, chunk=min(CHUNK, dec_seq), batch_block=bs,
        n_past_valid=WINDOW, pos0=PAST_LEN, k_out_rows=dec_seq)

    xp, gdn_conv_p, gdn_s_p = _odd_layer(
        xp, mod_od[:bp], od_consts, None, tile=tile, chunk=min(CHUNK, seq), batch_block=1)
    xs, gdn_conv_s, gdn_s_s = _odd_layer(
        xs, mod_od[bp:n_c], od_consts, (state_gdn_conv[0], state_gdn_s[0]),
        tile=dec_seq, chunk=min(CHUNK, dec_seq), batch_block=bs)

    kv_shape = lambda a: a.reshape(1, a.shape[0], a.shape[1], ATT_KV_HEADS, HEAD_DIM)
    return (xp, xs,
            lru_conv_p[None], lru_conv_s[None],
            lru_h_p.reshape(1, bp, lw), lru_h_s.reshape(1, bs, lw),
            kv_shape(swa_k_p), kv_shape(swa_k_s), kv_shape(swa_v_p), kv_shape(swa_v_s),
            gdn_conv_p[None], gdn_conv_s[None], gdn_s_p[None], gdn_s_s[None])
```

```python
import functools

import jax
import jax.numpy as jnp
from jax import lax
from jax.experimental import pallas as pl
from jax.experimental.pallas import tpu as pltpu

F32 = jnp.float32
BF16 = jnp.bfloat16

CHUNK = 64
EPS = 1e-6
CONV_W = 4
NEG_INF = -1e30
LRU_HEADS = 8
LRU_C = 8.0
HEAD_DIM = 64
ATT_KV_HEADS = 2
WINDOW = 128
ROT_DIM = HEAD_DIM // 4
ROPE_THETA = 500000.0
GDN_DK = 128
GDN_DV = 128
PAST_LEN = 4096

LANES = 128
SUBLANES = 8
MXU_TILE = 256
VMEM_LIMIT_BYTES = 56 * 1024 * 1024

PROMPT_TILE = 512
MOD_TILE = 512
CONV_PAD = SUBLANES


def _silu(x):
    return x * jax.nn.sigmoid(x)


def _expm1(x):
    u = jnp.exp(x)
    d = u - 1.0
    return jnp.where(u == 1.0, x, jnp.where(d == -1.0, -1.0, d * x / jnp.log(u)))


def _rms_scale(x):
    return lax.rsqrt(jnp.mean(x * x, axis=-1, keepdims=True) + EPS)


def _rms(x, g):
    return x * _rms_scale(x) * g


def _dot(a, b):
    return jnp.dot(a, b, preferred_element_type=F32)


def _dot_nt(a, b):
    return lax.dot_general(a, b, (((1,), (1,)), ((), ())), preferred_element_type=F32)


def _dot_tn(a, b):
    return lax.dot_general(a, b, (((0,), (0,)), ((), ())), preferred_element_type=F32)


def _for_each(n, body):
    if n == 1:
        body(0)
    else:
        def step(i, carry):
            body(i)
            return carry
        lax.fori_loop(0, n, step, 0)


def _rows(i, n, size=None):
    size = n if size is None else size
    if isinstance(i, int):
        return pl.ds(i * n, size)
    return pl.ds(pl.multiple_of(i * n, n), size)


def _mod_body(c_ref, w0_ref, b0_ref, w1_ref, b1_ref, o0_ref, o1_ref):
    c = _silu(c_ref[...]).astype(BF16)
    o0_ref[...] = _dot(c, w0_ref[...].astype(BF16)) + b0_ref[...]
    o1_ref[...] = _dot(c, w1_ref[...].astype(BF16)) + b1_ref[...]


def _modulation(c_all, w0, b0, w1, b1):
    n, d = c_all.shape
    d3 = w0.shape[1]
    wspec = pl.BlockSpec((d, MOD_TILE), lambda j: (0, j))
    bspec = pl.BlockSpec((1, MOD_TILE), lambda j: (0, j))
    ospec = pl.BlockSpec((n, MOD_TILE), lambda j: (0, j))
    return pl.pallas_call(
        _mod_body,
        grid=(d3 // MOD_TILE,),
        in_specs=[pl.BlockSpec((n, d), lambda j: (0, 0)), wspec, bspec, wspec, bspec],
        out_specs=[ospec, ospec],
        out_shape=[jax.ShapeDtypeStruct((n, d3), F32)] * 2,
        name="adaln_modulation",
    )(c_all, w0, b0.reshape(1, d3), w1, b1.reshape(1, d3))


def _pre_norm(x_ref, mod_ref, norm_pre_ref):
    bb, tc, d = x_ref.shape
    x = x_ref[...]
    gain = norm_pre_ref[...] * (1.0 + mod_ref[:, 1:2, :])
    h = x * _rms_scale(x) * gain + mod_ref[:, 0:1, :]
    return h.reshape(bb * tc, d)


def _post_residual(x_ref, mod_ref, norm_post_ref, y, o_ref, rows=None):
    bb, tc, d = x_ref.shape
    yn = y * _rms_scale(y)
    if rows is None:
        gain = mod_ref[:, 2:3, :] * norm_post_ref[...]
        o_ref[...] = x_ref[...] + gain * yn.reshape(bb, tc, d)
    else:
        assert bb == 1
        o_ref[0, rows, :] = x_ref[0, rows, :] + (mod_ref[0, 2:3, :] * norm_post_ref[...]) * yn


def _run_interleaved(stage_lists):
    tagged = [((i + 0.5) / len(stages), k, i, stage)
              for k, stages in enumerate(stage_lists) for i, stage in enumerate(stages)]
    for _, _, _, stage in sorted(tagged, key=lambda entry: entry[:3]):
        stage()


def _causal_conv(ext_ref, bi, tc, w_ref):
    ext = ext_ref[bi]
    acc = pltpu.roll(ext, CONV_W - 1, axis=0)[CONV_PAD:] * w_ref[0:1, :]
    for j in range(1, CONV_W - 1):
        acc = acc + pltpu.roll(ext, CONV_W - 1 - j, axis=0)[CONV_PAD:] * w_ref[j:j + 1, :]
    return acc + ext[CONV_PAD:] * w_ref[CONV_W - 1:CONV_W, :]


def _scan_phase_major(a, b, h0, chunk):
    tc, width = a.shape
    row8 = lax.broadcasted_iota(jnp.int32, (SUBLANES, width), 0)
    carry = h0
    out = []
    for c in range(tc // chunk):
        acc_a, acc_b = [], []
        for r in range(SUBLANES):
            blk = slice(c * chunk + r * SUBLANES, c * chunk + (r + 1) * SUBLANES)
            if r == 0:
                acc_a.append(a[blk])
                acc_b.append(b[blk])
            else:
                acc_b.append(a[blk] * acc_b[-1] + b[blk])
                acc_a.append(a[blk] * acc_a[-1])
        tot_a, tot_b = acc_a[-1], acc_b[-1]
        for sft in (1, 2, 4):
            keep = row8 >= sft
            tot_b = jnp.where(keep, tot_a * pltpu.roll(tot_b, sft, axis=0) + tot_b, tot_b)
            tot_a = jnp.where(keep, tot_a * pltpu.roll(tot_a, sft, axis=0), tot_a)
        h_out = tot_b + tot_a * carry
        h_in = jnp.where(row8 == 0, carry, pltpu.roll(h_out, 1, axis=0))
        out += [acc_b[r] + acc_a[r] * h_in for r in range(SUBLANES)]
        carry = h_out[SUBLANES - 1:SUBLANES, :]
    return jnp.concatenate(out, axis=0), carry


def _even_body(*refs, has_state, n_past_valid, pos0, chunk, n_tiles, k_out_rows, phase_major):
    refs = list(refs)
    x_ref = refs.pop(0)
    perm_ref = refs.pop(0) if phase_major else None
    (mod_ref, npre_ref, npost_ref, win_ref, cw_ref, cb_ref, wg_ref, ba_ref, bx_ref,
     lam_ref, inv_ref, sinks_ref, wout_ref) = refs[:13]
    refs = refs[13:]
    if has_state:
        conv0_ref, h0_ref, k0_ref, v0_ref = refs[:4]
        refs = refs[4:]
    (y_ref, nconv_ref, nh_ref, nk_ref, nv_ref,
     hm_ref, u_ref, ext_ref, a_ref, b_ref, hs_ref, knat_ref, vnat_ref, q_ref, mix_ref, cosl_ref, sinl_ref) = refs
    bb, tc, _ = x_ref.shape
    lw = a_ref.shape[1]
    aw = q_ref.shape[1]
    kvw = knat_ref.shape[2]
    t = pl.program_id(1)
    cpt = tc // chunk
    m = WINDOW + chunk
    o_q, o_k, o_v, o_gb = 2 * lw, 2 * lw + aw, 2 * lw + aw + kvw, 2 * lw + aw + 2 * kvw

    @pl.when(t == 0)
    def _():
        ext_ref[...] = jnp.zeros(ext_ref.shape, F32)
        if has_state:
            ext_ref[:, CONV_PAD - 3:CONV_PAD, :] = conv0_ref[...]
            nh_ref[...] = h0_ref[...]
            knat_ref[:, 0:WINDOW, :] = k0_ref[...]
            vnat_ref[:, 0:WINDOW, :] = v0_ref[...]
        else:
            nh_ref[...] = jnp.zeros(nh_ref.shape, F32)
            knat_ref[:, 0:WINDOW, :] = jnp.zeros((bb, WINDOW, kvw), F32)
            vnat_ref[:, 0:WINDOW, :] = jnp.zeros((bb, WINDOW, kvw), F32)

    def permute_rows(v):
        n = perm_ref.shape[0]
        return jnp.concatenate(
            [_dot(perm_ref[...], v[i * n:(i + 1) * n]).astype(BF16) for i in range(v.shape[0] // n)], axis=0)

    hmod_time = _pre_norm(x_ref, mod_ref, npre_ref).astype(BF16)
    hm_ref[...] = permute_rows(hmod_time) if phase_major else hmod_time

    tile_start = (pos0 + t * tc).astype(F32) * inv_ref[...]
    cos_start, sin_start = jnp.cos(tile_start), jnp.sin(tile_start)

    def make_rope(cos_in_tile, sin_in_tile):
        cos_t = cos_start * cos_in_tile - sin_start * sin_in_tile
        sin_t = sin_start * cos_in_tile + cos_start * sin_in_tile
        lane = lax.broadcasted_iota(jnp.int32, cos_t.shape, 1) % HEAD_DIM
        half = ROT_DIM // 2
        sin_a = jnp.where(lane < half, -sin_t, 0.0)
        sin_b = jnp.where(lane >= half, sin_t, 0.0)

        def rope(xcol):
            return (xcol * cos_t + pltpu.roll(xcol, LANES - half, axis=1) * sin_a
                    + pltpu.roll(xcol, half, axis=1) * sin_b)
        return rope

    @pl.when(t == 0)
    def _():
        row_t = lax.broadcasted_iota(jnp.int32, (tc, LANES), 0)
        if phase_major:
            row_t = (row_t // chunk) * chunk + _time_of_row(row_t % chunk)
        in_tile = row_t.astype(F32) * inv_ref[...]
        cosl_ref[...] = jnp.cos(in_tile)
        sinl_ref[...] = jnp.sin(in_tile)

    rope = make_rope(cosl_ref[...], sinl_ref[...])

    neg_c_softplus = -LRU_C * jax.nn.softplus(-lam_ref[...])
    row8 = lax.broadcasted_iota(jnp.int32, (SUBLANES, lw), 0)

    def project(cols):
        u_ref[:, cols] = _dot(hm_ref[...], win_ref[:, cols])

    def recurrence_inputs(bi):
        rows = _rows(bi, tc)
        if phase_major:
            xc = _conv_phase_major(u_ref, ext_ref, nconv_ref, cw_ref, tc, chunk, slice(0, lw)) + cb_ref[...]
        else:
            ext_ref[bi, CONV_PAD:CONV_PAD + tc, :] = u_ref[rows, 0:lw]
            xc = _causal_conv(ext_ref, bi, tc, cw_ref) + cb_ref[...]
            nconv_ref[bi, :, :] = ext_ref[bi, CONV_PAD + tc - 3:CONV_PAD + tc, :]
            ext_ref[bi, CONV_PAD - 3:CONV_PAD, :] = ext_ref[bi, CONV_PAD + tc - 3:CONV_PAD + tc, :]
        xcb = xc.astype(BF16)
        halves = [_dot(xcb[:, i * MXU_TILE:(i + 1) * MXU_TILE], wg_ref[i]) for i in range(lw // MXU_TILE)]
        r = jax.nn.sigmoid(jnp.concatenate([g[:, :MXU_TILE] for g in halves], axis=1) + ba_ref[...])
        ig = jax.nn.sigmoid(jnp.concatenate([g[:, MXU_TILE:] for g in halves], axis=1) + bx_ref[...])
        log_a = r * neg_c_softplus
        a_ref[...] = jnp.exp(log_a)
        b_ref[...] = jnp.sqrt(-_expm1(2.0 * log_a)) * (ig * xc)

    def recurrence(bi):
        if phase_major:
            hs_ref[...], nh_ref[bi, :, :] = _scan_phase_major(a_ref[...], b_ref[...], nh_ref[bi, :, :], chunk)
            return

        def scan_block(j, hc):
            r0 = pl.multiple_of(j * SUBLANES, SUBLANES)
            a = a_ref[pl.ds(r0, SUBLANES), :]
            b = b_ref[pl.ds(r0, SUBLANES), :]
            for sft in (1, 2, 4):
                keep = row8 >= sft
                b = jnp.where(keep, a * pltpu.roll(b, sft, axis=0) + b, b)
                a = jnp.where(keep, a * pltpu.roll(a, sft, axis=0), a)
            h = a * hc + b
            hs_ref[pl.ds(r0, SUBLANES), :] = h
            return h[SUBLANES - 1:SUBLANES, :]

        nh_ref[bi, :, :] = lax.fori_loop(0, tc // SUBLANES, scan_block, nh_ref[bi, :, :])

    def recurrence_output(bi):
        rows = _rows(bi, tc)
        mix_ref[rows, 0:lw] = (hs_ref[...] * _silu(u_ref[rows, lw:2 * lw])).astype(BF16)

    def attention_inputs(bi):
        rows = _rows(bi, tc)
        for j in range(aw // LANES):
            qcol = rope(u_ref[rows, o_q + j * LANES:o_q + (j + 1) * LANES])
            q_ref[rows, j * LANES:(j + 1) * LANES] = (qcol * (HEAD_DIM ** -0.5)).astype(BF16)
        knat_ref[bi, WINDOW:WINDOW + tc, :] = rope(u_ref[rows, o_k:o_k + kvw])
        vnat_ref[bi, WINDOW:WINDOW + tc, :] = u_ref[rows, o_v:o_v + kvw]
        if not phase_major:
            nk_ref[bi, :, :] = knat_ref[bi, WINDOW + tc - k_out_rows:WINDOW + tc, :]
            nv_ref[bi, :, :] = vnat_ref[bi, WINDOW + tc - k_out_rows:WINDOW + tc, :]

    def per_batch(bi):
        recurrence_inputs(bi)
        recurrence(bi)
        recurrence_output(bi)
        attention_inputs(bi)

    if phase_major:
        project(slice(0, lw))
        later = [slice(lw, 2 * lw), slice(o_q, o_k), slice(o_k, o_gb), slice(o_gb, o_gb + aw)]
        _run_interleaved([[functools.partial(project, cols) for cols in later],
                          [functools.partial(recurrence_inputs, 0), functools.partial(recurrence, 0)]])
        recurrence_output(0)
        attention_inputs(0)
    else:
        project(slice(0, win_ref.shape[1]))
        _for_each(bb, per_batch)

    if phase_major:
        @pl.when(t == n_tiles - 1)
        def _():
            first = tc - k_out_rows
            kv = _dot(hmod_time[first:], win_ref[:, o_k:o_k + 2 * kvw])
            in_tile = (first + lax.broadcasted_iota(jnp.int32, (k_out_rows, LANES), 0)).astype(F32) * inv_ref[...]
            rope_last = make_rope(jnp.cos(in_tile), jnp.sin(in_tile))
            nk_ref[0] = rope_last(kv[:, 0:kvw])
            nv_ref[0] = kv[:, kvw:2 * kvw]

    lo_q = lax.broadcasted_iota(jnp.int32, (chunk, LANES), 1) < HEAD_DIM
    group = (aw // HEAD_DIM) // ATT_KV_HEADS

    sinks = [jnp.concatenate([jnp.full((chunk, 1), sinks_ref[h * group + g], F32) for g in range(group)], axis=0)
             for h in range(ATT_KV_HEADS)]


    def attend():
        units = []
        for bi in range(bb):
            k_all = knat_ref[bi]
            v_all = vnat_ref[bi]
            k_rot = pltpu.roll(k_all, HEAD_DIM, axis=1)
            v_rot = pltpu.roll(v_all, HEAD_DIM, axis=1)
            lane_all = lax.broadcasted_iota(jnp.int32, k_all.shape, 1)
            lo_all = lane_all < HEAD_DIM
            kd = [jnp.where(lo_all, k_all, k_rot).astype(BF16), jnp.where(lo_all, k_rot, k_all).astype(BF16)]
            ones_hi = jnp.where(lane_all == HEAD_DIM, 1.0, 0.0)
            ones_lo = jnp.where(lane_all == 0, 1.0, 0.0)
            v_lo = [jnp.where(lo_all, v_all, ones_hi).astype(BF16), jnp.where(lo_all, v_rot, ones_hi).astype(BF16)]
            v_hi = [jnp.where(lo_all, ones_lo, v_rot).astype(BF16), jnp.where(lo_all, ones_lo, v_all).astype(BF16)]
            for c in range(cpt):
                rows = _rows(bi * cpt + c, chunk)
                window = slice(c * chunk, c * chunk + m)
                valid = None
                if n_past_valid < WINDOW:
                    key_t = lax.broadcasted_iota(jnp.int32, (1, m), 1)
                    if phase_major:
                        key_t = (key_t // chunk) * chunk + _time_of_row(key_t % chunk)
                    valid = t * tc + c * chunk - WINDOW + key_t >= -n_past_valid
                for h in range(ATT_KV_HEADS):
                    units.append(dict(rows=rows, h=h, valid=valid, kd=kd[h][window], v_lo=v_lo[h][window],
                                      v_hi=v_hi[h][window]))
        yield
        for un in units:
            rows, h = un["rows"], un["h"]
            cols = [q_ref[rows, (h * group // 2 + j) * LANES:(h * group // 2 + j + 1) * LANES]
                    for j in range(group // 2)]
            zero = jnp.zeros_like(cols[0])
            qstack = jnp.concatenate(
                [part for qc in cols for part in (jnp.where(lo_q, qc, zero), jnp.where(lo_q, zero, qc))], axis=0)
            sc = _dot_nt(qstack, un["kd"])
            un["sc"] = sc if un["valid"] is None else jnp.where(un["valid"], sc, NEG_INF)
        yield
        for un in units:
            un["mx"] = jnp.maximum(jnp.max(un["sc"], axis=-1, keepdims=True), sinks[un["h"]])
        yield
        for un in units:
            un["p"] = jnp.exp(un["sc"] - un["mx"]).astype(BF16)
            un["sink_p"] = jnp.exp(sinks[un["h"]] - un["mx"])
        yield
        for un in units:
            rows, h, p, sink_p = un["rows"], un["h"], un["p"], un["sink_p"]
            for j in range(group // 2):
                r_lo = slice((2 * j) * chunk, (2 * j + 1) * chunk)
                r_hi = slice((2 * j + 1) * chunk, (2 * j + 2) * chunk)
                o_lo = _dot(p[r_lo], un["v_lo"])
                o_hi = _dot(p[r_hi], un["v_hi"])
                den_lo = o_lo[:, HEAD_DIM:HEAD_DIM + 1] + sink_p[r_lo]
                den_hi = o_hi[:, 0:1] + sink_p[r_hi]
                o = jnp.where(lo_q, o_lo / den_lo, o_hi / den_hi)
                col = h * group // 2 + j
                gate = _silu(u_ref[rows, o_gb + col * LANES:o_gb + (col + 1) * LANES])
                mix_ref[rows, lw + col * LANES:lw + (col + 1) * LANES] = (o * gate).astype(BF16)

    for _ in attend():
        pass

    if n_tiles > 1:
        knat_ref[:, 0:WINDOW, :] = knat_ref[:, tc:tc + WINDOW, :]
        vnat_ref[:, 0:WINDOW, :] = vnat_ref[:, tc:tc + WINDOW, :]

    mixed = mix_ref[...]
    if phase_major:
        mixed = permute_rows(mixed)
    y = _dot(mixed, wout_ref[...])
    _post_residual(x_ref, mod_ref, npost_ref, y, y_ref)


def _runs_phase_major(chunk, batch_block, has_state):
    return chunk == SUBLANES * SUBLANES and batch_block == 1 and not has_state


def _conv_history_shape(phase_major, batch_block, tile, channels):
    if phase_major:
        return (CONV_W - 1, SUBLANES, channels)
    return (batch_block, CONV_PAD + tile, channels)


def _const_spec(shape):
    zeros = (0,) * len(shape)
    return pl.BlockSpec(shape, lambda b, t: zeros)


def _even_layer(x, mod, consts, state, *, tile, chunk, batch_block, n_past_valid, pos0, k_out_rows):
    bsz, seq, d = x.shape
    (npre, npost, win, cw, cb, wg, b_a, b_x, lam, inv_row, sinks, wout) = consts
    lw = cw.shape[1]
    kvw = ATT_KV_HEADS * HEAD_DIM
    aw = (win.shape[1] - 2 * lw - 2 * kvw) // 2
    n_tiles = seq // tile
    assert seq % tile == 0 and tile % chunk == 0 and bsz % batch_block == 0
    assert n_tiles == 1 or tile >= WINDOW
    rows = batch_block * tile
    has_state = state is not None
    phase_major = _runs_phase_major(chunk, batch_block, has_state)
    assert not phase_major or tile >= k_out_rows

    def bspec(shape):
        nd = len(shape)
        return pl.BlockSpec((batch_block,) + shape, lambda b, t: (b,) + (0,) * nd)

    in_specs = [pl.BlockSpec((batch_block, tile, d), lambda b, t: (b, t, 0))]
    args = [x]
    if phase_major:
        perm = _phase_major_matrix(min(rows, MXU_TILE), chunk)
        assert rows % perm.shape[0] == 0
        in_specs.append(_const_spec(perm.shape))
        args.append(perm)
    in_specs += [
        bspec((3, d)),
        _const_spec(npre.shape), _const_spec(npost.shape), _const_spec(win.shape),
        _const_spec(cw.shape), _const_spec(cb.shape), _const_spec(wg.shape),
        _const_spec(b_a.shape), _const_spec(b_x.shape), _const_spec(lam.shape), _const_spec(inv_row.shape),
        pl.BlockSpec(memory_space=pltpu.SMEM),
        _const_spec(wout.shape),
    ]
    args += [mod, npre, npost, win, cw, cb, wg, b_a, b_x, lam, inv_row, sinks, wout]
    if has_state:
        in_specs += [bspec((CONV_W - 1, lw)), bspec((1, lw)), bspec((WINDOW, kvw)), bspec((WINDOW, kvw))]
        args += list(state)
    out_specs = [
        pl.BlockSpec((batch_block, tile, d), lambda b, t: (b, t, 0)),
        bspec((CONV_W - 1, lw)), bspec((1, lw)), bspec((k_out_rows, kvw)), bspec((k_out_rows, kvw)),
    ]
    out_shape = [
        jax.ShapeDtypeStruct((bsz, seq, d), F32),
        jax.ShapeDtypeStruct((bsz, CONV_W - 1, lw), F32),
        jax.ShapeDtypeStruct((bsz, 1, lw), F32),
        jax.ShapeDtypeStruct((bsz, k_out_rows, kvw), F32),
        jax.ShapeDtypeStruct((bsz, k_out_rows, kvw), F32),
    ]
    scratch = [
        pltpu.VMEM((rows, d), BF16),
        pltpu.VMEM((rows, win.shape[1]), F32),
        pltpu.VMEM(_conv_history_shape(phase_major, batch_block, tile, lw), F32),
        pltpu.VMEM((tile, lw), F32), pltpu.VMEM((tile, lw), F32), pltpu.VMEM((tile, lw), F32),
        pltpu.VMEM((batch_block, WINDOW + tile, kvw), F32),
        pltpu.VMEM((batch_block, WINDOW + tile, kvw), F32),
        pltpu.VMEM((rows, aw), BF16),
        pltpu.VMEM((rows, lw + aw), BF16),
        pltpu.VMEM((tile, LANES), F32), pltpu.VMEM((tile, LANES), F32),
    ]
    body = functools.partial(_even_body, has_state=has_state, n_past_valid=n_past_valid, pos0=pos0,
                             chunk=chunk, n_tiles=n_tiles, k_out_rows=k_out_rows, phase_major=phase_major)
    return pl.pallas_call(
        body,
        grid=(bsz // batch_block, n_tiles),
        in_specs=in_specs,
        out_specs=out_specs,
        out_shape=out_shape,
        scratch_shapes=scratch,
        compiler_params=pltpu.CompilerParams(
            dimension_semantics=("arbitrary", "arbitrary"), vmem_limit_bytes=VMEM_LIMIT_BYTES),
        name="even_layer_state" if has_state else "even_layer_prompt",
    )(*args)


def _gdn_groups(chunk, n_heads):
    hpg = max(1, min(n_heads, MXU_TILE // chunk))
    assert n_heads % hpg == 0
    return hpg, n_heads // hpg


def _time_of_row(i):
    return (i % SUBLANES) * SUBLANES + i // SUBLANES


def _phase_major_matrix(n_rows, chunk):
    i = jnp.arange(n_rows)
    src = (i // chunk) * chunk + _time_of_row(i % chunk)
    return (src[:, None] == i[None, :]).astype(BF16)


def _conv_phase_major(u_ref, hist_ref, nconv_ref, w_ref, tc, chunk, cols):
    cpt = tc // chunk
    blocks = [u_ref[SUBLANES * b:SUBLANES * (b + 1), cols] for b in range(tc // SUBLANES)]
    taps_w = [w_ref[j:j + 1, cols] for j in range(CONV_W)]
    top_row = lax.broadcasted_iota(jnp.int32, blocks[0].shape, 0) == 0
    first_late = SUBLANES - (CONV_W - 1)
    prev = {r: pltpu.roll(hist_ref[r - first_late, :, cols], 1, axis=0) for r in range(first_late, SUBLANES)}
    out = []
    for c in range(cpt):
        late = {}
        for r in range(first_late, SUBLANES):
            moved = pltpu.roll(blocks[SUBLANES * c + r], 1, axis=0)
            late[r] = jnp.where(top_row, prev[r], moved)
            prev[r] = moved
        for r in range(SUBLANES):
            acc = None
            for j in range(CONV_W):
                s = CONV_W - 1 - j
                tap = blocks[SUBLANES * c + r - s] if r >= s else late[r - s + SUBLANES]
                acc = tap * taps_w[j] if acc is None else acc + tap * taps_w[j]
            out.append(acc)
    for i in range(CONV_W - 1):
        last = blocks[SUBLANES * (cpt - 1) + first_late + i]
        hist_ref[i, :, cols] = last
        nconv_ref[0, i:i + 1, cols] = last[SUBLANES - 1:SUBLANES, :]
    return jnp.concatenate(out, axis=0)


def _odd_body(*refs, has_state, chunk, phase_major):
    refs = list(refs)
    x_ref = refs.pop(0)
    perm_ref = refs.pop(0) if phase_major else None
    (mod_ref, npre_ref, npost_ref, win_ref, wba_ref, cw_ref, alog_ref, dtb_ref, hn_ref, wout_ref) = refs[:10]
    refs = refs[10:]
    if has_state:
        conv0_ref, s0_ref = refs[:2]
        refs = refs[2:]
    (y_ref, nconv_ref, ns_ref,
     hm_ref, u_ref, ext_ref, q_ref, k_ref, kb_ref, kbe_ref, vb_ref, qe_ref, kd_ref,
     beta_ref, gcum_ref, eg_ref, dec_ref, wq_ref, uu_ref, qk_ref, mix_ref) = refs
    bb, tc, _ = x_ref.shape
    n_heads = ns_ref.shape[1]
    kw = n_heads * GDN_DK
    vw = n_heads * GDN_DV
    cch = 2 * kw + vw
    rows_all = bb * tc
    t = pl.program_id(1)
    cpt = tc // chunk
    n_steps = (chunk - 1).bit_length()
    assert n_steps >= 2
    hpg, n_groups = _gdn_groups(chunk, n_heads)
    lw = hpg * chunk

    @pl.when(t == 0)
    def _():
        ext_ref[...] = jnp.zeros(ext_ref.shape, F32)
        if has_state:
            ext_ref[:, CONV_PAD - 3:CONV_PAD, :] = conv0_ref[...]
            ns_ref[...] = s0_ref[...]
        else:
            ns_ref[...] = jnp.zeros(ns_ref.shape, F32)

    def permute_rows(v):
        n = perm_ref.shape[0]
        return jnp.concatenate(
            [_dot(perm_ref[...], v[i * n:(i + 1) * n]).astype(BF16) for i in range(v.shape[0] // n)], axis=0)

    hmod = _pre_norm(x_ref, mod_ref, npre_ref).astype(BF16)
    if phase_major:
        hmod = permute_rows(hmod)
    hm_ref[...] = hmod
    ba = _dot(hmod, wba_ref[...])
    beta_ref[...] = jax.nn.sigmoid(ba)
    g = -jnp.exp(alog_ref[...]) * jax.nn.softplus(ba + dtb_ref[...])
    if phase_major:
        row8 = lax.broadcasted_iota(jnp.int32, (SUBLANES, LANES), 0)
        parts = []
        for c in range(rows_all // chunk):
            run = []
            for r in range(SUBLANES):
                blk = g[c * chunk + r * SUBLANES:c * chunk + (r + 1) * SUBLANES]
                run.append(blk if r == 0 else run[-1] + blk)
            total = run[-1]
            incl = total
            for sft in (1, 2, 4):
                incl = jnp.where(row8 >= sft, incl + pltpu.roll(incl, sft, axis=0), incl)
            parts += [blk + (incl - total) for blk in run]
        g = jnp.concatenate(parts, axis=0)
    else:
        row_in_chunk = lax.broadcasted_iota(jnp.int32, (rows_all, LANES), 0) % chunk
        sft = 1
        while sft < chunk:
            g = jnp.where(row_in_chunk >= sft, g + pltpu.roll(g, sft, axis=0), g)
            sft *= 2
    gcum_ref[...] = g
    eg_ref[...] = jnp.exp(g)
    g3 = g.reshape(rows_all // chunk, chunk, LANES)
    dec_ref[...] = jnp.exp(g3[:, chunk - 1:chunk, :] - g3).reshape(rows_all, LANES)

    def project(cols):
        u_ref[:, cols] = _dot(hm_ref[...], win_ref[:, cols])

    def head_operands(rows, h, qh, kh, vh):
        hc = slice(h * GDN_DK, (h + 1) * GDN_DK)
        qh = qh * (lax.rsqrt(jnp.sum(qh * qh, axis=-1, keepdims=True) + EPS) * (GDN_DK ** -0.5))
        kh = kh * lax.rsqrt(jnp.sum(kh * kh, axis=-1, keepdims=True) + EPS)
        beta = beta_ref[rows, :][:, h:h + 1]
        eg = eg_ref[rows, :][:, n_heads + h:n_heads + h + 1]
        kb = kh * beta
        q_ref[rows, hc] = qh.astype(BF16)
        k_ref[rows, hc] = kh.astype(BF16)
        kb_ref[rows, hc] = kb.astype(BF16)
        kbe_ref[rows, hc] = (kb * eg).astype(BF16)
        vb_ref[rows, hc] = (vh * beta).astype(BF16)
        qe_ref[rows, hc] = (qh * eg).astype(BF16)
        kd_ref[rows, hc] = (kh * dec_ref[rows, :][:, n_heads + h:n_heads + h + 1]).astype(BF16)

    def per_batch(bi):
        rows = _rows(bi, tc)
        ext_ref[bi, CONV_PAD:CONV_PAD + tc, :] = u_ref[rows, 0:cch]
        qkv = _silu(_causal_conv(ext_ref, bi, tc, cw_ref))
        nconv_ref[bi, :, :] = ext_ref[bi, CONV_PAD + tc - 3:CONV_PAD + tc, :]
        ext_ref[bi, CONV_PAD - 3:CONV_PAD, :] = ext_ref[bi, CONV_PAD + tc - 3:CONV_PAD + tc, :]
        for h in range(n_heads):
            head_operands(rows, h, qkv[:, h * GDN_DK:(h + 1) * GDN_DK],
                          qkv[:, kw + h * GDN_DK:kw + (h + 1) * GDN_DK],
                          qkv[:, 2 * kw + h * GDN_DV:2 * kw + (h + 1) * GDN_DV])

    pair = MXU_TILE // GDN_DK

    def pair_stages(j):
        col_sets = [slice(base + j * MXU_TILE, base + (j + 1) * MXU_TILE) for base in (0, kw, 2 * kw)]

        def operands():
            rows = pl.ds(0, tc)
            q2, k2, v2 = [_silu(_conv_phase_major(u_ref, ext_ref, nconv_ref, cw_ref, tc, chunk, cols))
                          for cols in col_sets]
            for i in range(pair):
                part = slice(i * GDN_DK, (i + 1) * GDN_DK)
                head_operands(rows, pair * j + i, q2[:, part], k2[:, part], v2[:, part])

        return [functools.partial(project, cols) for cols in col_sets] + [operands]

    if not phase_major:
        project(slice(0, cch + vw))
        _for_each(bb, per_batch)

    lane_w = lax.broadcasted_iota(jnp.int32, (chunk, lw), 1)
    row_w = lax.broadcasted_iota(jnp.int32, (chunk, lw), 0)
    blk_w = lane_w // chunk
    col_w = lane_w % chunk
    diag_w = row_w == col_w
    if phase_major:
        row_w, col_w = _time_of_row(row_w), _time_of_row(col_w)
    incl_w = row_w >= col_w
    strict_w = row_w > col_w
    eye_w = jnp.where(diag_w, 1.0, 0.0).astype(F32)
    feat_blk = lax.broadcasted_iota(jnp.int32, (chunk, hpg * GDN_DK), 1) // GDN_DK

    def block_diag(m_b):
        return jnp.concatenate([jnp.where(blk_w == hh, m_b, jnp.zeros_like(m_b)) for hh in range(hpg)], axis=0)

    def head_cols(h):
        return slice(h * GDN_DK, (h + 1) * GDN_DK)

    def wy_stages(segs, head_groups):
        units = []

        def build():
            for seg in segs:
                rows = _rows(seg, chunk)
                gt = gcum_ref[rows, :]
                for gi in head_groups:
                    gcols = slice(gi * hpg * GDN_DK, (gi + 1) * hpg * GDN_DK)
                    k4 = k_ref[rows, gcols]
                    k_bd = jnp.concatenate(
                        [jnp.where(feat_blk == hh, k4, jnp.zeros_like(k4)) for hh in range(hpg)], axis=0)
                    kk = _dot_nt(jnp.concatenate([kb_ref[rows, gcols], q_ref[rows, gcols]], axis=0), k_bd)
                    gcol = jnp.zeros((chunk, lw), F32)
                    for hh in range(hpg):
                        ln = n_heads + gi * hpg + hh
                        gcol = jnp.where(blk_w == hh, gt[:, ln:ln + 1], gcol)
                    grow = jnp.sum(jnp.where(diag_w, gcol, 0.0), axis=0, keepdims=True)
                    decay = jnp.exp(jnp.where(incl_w, gcol - grow, 0.0))
                    p_b = (-(kk[:chunk] * jnp.where(strict_w, decay, 0.0))).astype(BF16)
                    qk_ref[seg * n_groups + gi] = (kk[chunk:] * jnp.where(incl_w, decay, 0.0)).astype(BF16)
                    units.append(dict(seg=seg, rows=rows, gi=gi, p=p_b, t=eye_w + p_b.astype(F32)))

        def square():
            for un in units:
                un["p"] = _dot(un["p"], block_diag(un["p"])).astype(BF16)

        def extend_and_square():
            for un in units:
                out = _dot(jnp.concatenate([un["t"].astype(BF16), un["p"]], axis=0), block_diag(un["p"]))
                un["t"] = un["t"] + out[:chunk]
                un["p"] = out[chunk:].astype(BF16)

        def extend():
            for un in units:
                un["t"] = un["t"] + _dot(un["t"].astype(BF16), block_diag(un["p"]))

        def apply():
            for un in units:
                seg, rows, gi = un["seg"], un["rows"], un["gi"]
                heads = range(gi * hpg, (gi + 1) * hpg)
                rhs = jnp.concatenate(
                    [jnp.concatenate([kbe_ref[rows, head_cols(h)], vb_ref[rows, head_cols(h)]], axis=1)
                     for h in heads], axis=0)
                wu = _dot(block_diag(un["t"].astype(BF16)), rhs)
                for hh, h in enumerate(heads):
                    blk = wu[hh * chunk:(hh + 1) * chunk]
                    wq_ref[seg * n_heads + h] = jnp.concatenate(
                        [blk[:, :GDN_DK].astype(BF16), qe_ref[rows, head_cols(h)]], axis=0)
                    uu_ref[rows, head_cols(h)] = blk[:, GDN_DK:]

        return [build, square] + [extend_and_square] * (n_steps - 2) + [extend, apply]

    zero_b = jnp.zeros((chunk, GDN_DV), BF16)
    state = {(bi, h): ns_ref[bi, h] for bi in range(bb) for h in range(n_heads)}

    def recurrence_stages(segs):
        stages = []
        for c in sorted({seg % cpt for seg in segs}):
            wave = [seg for seg in segs if seg % cpt == c]
            pairs = [(seg, h) for seg in wave for h in range(n_heads)]
            held = {}

            def correct(wave=wave, pairs=pairs, held=held):
                for seg, h in pairs:
                    held["ws_qs", seg, h] = _dot(wq_ref[seg * n_heads + h], state[seg // cpt, h].astype(BF16))
                for seg, h in pairs:
                    held["v", seg, h] = (uu_ref[_rows(seg, chunk), head_cols(h)]
                                         - held["ws_qs", seg, h][:chunk]).astype(BF16)

            def advance(wave=wave, pairs=pairs, held=held):
                for seg in wave:
                    held["eg", seg] = jnp.exp(gcum_ref[pl.ds(seg * chunk + chunk - 1, 1), :])
                for seg, h in pairs:
                    state[seg // cpt, h] = (state[seg // cpt, h] * held["eg", seg][:, n_heads + h:n_heads + h + 1]
                                            + _dot_tn(kd_ref[_rows(seg, chunk), head_cols(h)], held["v", seg, h]))
                for seg in wave:
                    rows = _rows(seg, chunk)
                    for gi in range(n_groups):
                        heads = range(gi * hpg, (gi + 1) * hpg)
                        v_bd = jnp.concatenate(
                            [jnp.concatenate([held["v", seg, h] if j == hh else zero_b for j in range(hpg)], axis=1)
                             for hh, h in enumerate(heads)], axis=0)
                        o = (jnp.concatenate([held["ws_qs", seg, h][chunk:] for h in heads], axis=1)
                             + _dot(qk_ref[seg * n_groups + gi], v_bd))
                        for hh, h in enumerate(heads):
                            gate = _silu(u_ref[rows, cch + h * GDN_DV:cch + (h + 1) * GDN_DV])
                            o_h = _rms(o[:, hh * GDN_DV:(hh + 1) * GDN_DV], hn_ref[...])
                            mix_ref[rows, head_cols(h)] = (o_h * gate).astype(BF16)

            stages += [correct, advance]
        return stages

    def output_stages(rows):
        held = {}

        def gather():
            mixed = mix_ref[rows, :]
            held["mixed"] = permute_rows(mixed) if phase_major else mixed

        def project():
            held["y"] = _dot(held["mixed"], wout_ref[...])

        def finish():
            _post_residual(x_ref, mod_ref, npost_ref, held["y"], y_ref, rows if n_row_groups > 1 else None)

        return [gather, project, finish]

    group_rows = min(rows_all, MXU_TILE)
    n_row_groups = rows_all // group_rows
    assert rows_all % group_rows == 0 and group_rows % chunk == 0 and (bb == 1 or n_row_groups == 1)
    segs_of = [range(g * group_rows // chunk, (g + 1) * group_rows // chunk) for g in range(n_row_groups)]
    if phase_major:
        pairs_per_group = hpg // pair
        assert hpg % pair == 0
        work = [(segs, gi) for gi in range(n_groups) for segs in segs_of]
        ahead = [[stage for j in range(gi * pairs_per_group, (gi + 1) * pairs_per_group) for stage in pair_stages(j)]
                 for gi in range(n_groups)]
        ahead.append([functools.partial(project, slice(cch + i * MXU_TILE, cch + (i + 1) * MXU_TILE))
                      for i in range(vw // MXU_TILE)])
        for stage in ahead[0]:
            stage()
        for gi in range(n_groups):
            factors = [stage for segs, g in work if g == gi for stage in wy_stages(segs, [gi])]
            _run_interleaved([ahead[gi + 1], factors])
    else:
        for segs in segs_of:
            for stage in wy_stages(segs, range(n_groups)):
                stage()
    for segs in segs_of:
        for stage in recurrence_stages(segs):
            stage()
    for g in range(n_row_groups):
        for stage in output_stages(pl.ds(g * group_rows, group_rows)):
            stage()
    for bh, value in state.items():
        ns_ref[bh[0], bh[1]] = value


def _odd_layer(x, mod, consts, state, *, tile, chunk, batch_block):
    bsz, seq, d = x.shape
    (npre, npost, win, wba, cw, alog, dtb, hn, wout) = consts
    cch = cw.shape[1]
    vw = wout.shape[0]
    n_heads = vw // GDN_DV
    n_tiles = seq // tile
    assert seq % tile == 0 and tile % chunk == 0 and bsz % batch_block == 0
    rows = batch_block * tile
    has_state = state is not None
    phase_major = _runs_phase_major(chunk, batch_block, has_state)

    def bspec(shape):
        nd = len(shape)
        return pl.BlockSpec((batch_block,) + shape, lambda b, t: (b,) + (0,) * nd)

    in_specs = [pl.BlockSpec((batch_block, tile, d), lambda b, t: (b, t, 0))]
    args = [x]
    if phase_major:
        perm_rows = min(rows, MXU_TILE)
        assert rows % perm_rows == 0 and perm_rows % chunk == 0
        in_specs.append(_const_spec((perm_rows, perm_rows)))
        args.append(_phase_major_matrix(perm_rows, chunk))
    in_specs += [
        bspec((3, d)),
        _const_spec(npre.shape), _const_spec(npost.shape), _const_spec((d, cch + vw)), _const_spec(wba.shape),
        _const_spec(cw.shape), _const_spec(alog.shape), _const_spec(dtb.shape), _const_spec(hn.shape),
        _const_spec(wout.shape),
    ]
    args += [mod, npre, npost, win, wba, cw, alog, dtb, hn, wout]
    if has_state:
        in_specs += [bspec((CONV_W - 1, cch)), bspec((n_heads, GDN_DK, GDN_DV))]
        args += list(state)
    out_specs = [
        pl.BlockSpec((batch_block, tile, d), lambda b, t: (b, t, 0)),
        bspec((CONV_W - 1, cch)), bspec((n_heads, GDN_DK, GDN_DV)),
    ]
    out_shape = [
        jax.ShapeDtypeStruct((bsz, seq, d), F32),
        jax.ShapeDtypeStruct((bsz, CONV_W - 1, cch), F32),
        jax.ShapeDtypeStruct((bsz, n_heads, GDN_DK, GDN_DV), F32),
    ]
    hpg, n_groups = _gdn_groups(chunk, n_heads)
    n_seg = rows // chunk
    scratch = (
        [pltpu.VMEM((rows, d), BF16),
         pltpu.VMEM((rows, cch + vw), F32),
         pltpu.VMEM(_conv_history_shape(phase_major, batch_block, tile, cch), F32)]
        + [pltpu.VMEM((rows, vw), BF16)] * 7
        + [pltpu.VMEM((rows, LANES), F32)] * 4
        + [pltpu.VMEM((n_seg * n_heads, 2 * chunk, GDN_DK), BF16),
           pltpu.VMEM((rows, vw), F32),
           pltpu.VMEM((n_seg * n_groups, chunk, hpg * chunk), BF16),
           pltpu.VMEM((rows, vw), BF16)]
    )
    body = functools.partial(_odd_body, has_state=has_state, chunk=chunk, phase_major=phase_major)
    return pl.pallas_call(
        body,
        grid=(bsz // batch_block, n_tiles),
        in_specs=in_specs,
        out_specs=out_specs,
        out_shape=out_shape,
        scratch_shapes=scratch,
        compiler_params=pltpu.CompilerParams(
            dimension_semantics=("arbitrary", "arbitrary"), vmem_limit_bytes=VMEM_LIMIT_BYTES),
        name="odd_layer_state" if has_state else "odd_layer_prompt",
    )(*args)


def _gate_weights(w_a, w_x):
    heads, blk, _ = w_a.shape
    per_tile = MXU_TILE // blk
    tiles = []
    for i in range(heads // per_tile):
        sl = slice(i * per_tile, (i + 1) * per_tile)
        tiles.append(jnp.concatenate(
            [jax.scipy.linalg.block_diag(*w_a[sl]), jax.scipy.linalg.block_diag(*w_x[sl])], axis=1))
    return jnp.stack(tiles).astype(BF16)


def _rope_inv_row():
    half = ROT_DIM // 2
    inv = ROPE_THETA ** (-(jnp.arange(half, dtype=F32) * 2.0 / ROT_DIM))
    per_head = jnp.concatenate([inv, inv, jnp.zeros((HEAD_DIM - ROT_DIM,), F32)])
    return jnp.tile(per_head, LANES // HEAD_DIM).reshape(1, LANES)


def kernel(x_prompt, x_sample, state_lru_conv, state_lru_h, cache_swa_k, cache_swa_v, state_gdn_conv, state_gdn_s, c_prompt, c_sample, ev_mod_w, ev_mod_b, ev_norm_pre, ev_norm_post, ev_w_in, lru_conv_w, lru_conv_b, lru_w_a, lru_b_a, lru_w_x, lru_b_x, lru_lambda, swa_sinks, ev_w_out, od_mod_w, od_mod_b, od_norm_pre, od_norm_post, od_w_in, gdn_conv_w, gdn_a_log, gdn_dt_bias, gdn_head_norm, od_w_out):
    bp, seq, d = x_prompt.shape
    bs, dec_seq, _ = x_sample.shape
    lw = lru_conv_w.shape[-1]
    kvw = ATT_KV_HEADS * HEAD_DIM
    n_heads = gdn_a_log.shape[-1]
    cch = gdn_conv_w.shape[-1]
    vw = od_w_out.shape[1]

    n_c = bp + bs
    n_pad = -n_c % 16
    c_all = jnp.concatenate([c_prompt, c_sample, jnp.zeros((n_pad, d), F32)], axis=0)
    mod_ev, mod_od = _modulation(c_all, ev_mod_w[0], ev_mod_b[0], od_mod_w[0], od_mod_b[0])
    mod_ev = mod_ev.reshape(n_c + n_pad, 3, d)
    mod_od = mod_od.reshape(n_c + n_pad, 3, d)

    row = lambda a: a.reshape(1, -1)
    ev_consts = (row(ev_norm_pre[0]), row(ev_norm_post[0]), ev_w_in[0].astype(BF16), lru_conv_w[0],
                 row(lru_conv_b[0]), _gate_weights(lru_w_a[0], lru_w_x[0]), row(lru_b_a[0]), row(lru_b_x[0]),
                 row(lru_lambda[0]), _rope_inv_row(), swa_sinks[0], ev_w_out[0].astype(BF16))
    w_in_od = od_w_in[0]
    pad_lanes = lambda a: jnp.pad(a, ((0, 0), (0, LANES - a.shape[1])))
    head_row = lambda a: jnp.pad(a.reshape(1, -1), ((0, 0), (n_heads, LANES - 2 * n_heads)))
    od_consts = (row(od_norm_pre[0]), row(od_norm_post[0]), w_in_od.astype(BF16),
                 pad_lanes(w_in_od[:, cch + vw:]).astype(BF16), gdn_conv_w[0],
                 head_row(gdn_a_log[0]), head_row(gdn_dt_bias[0]), row(gdn_head_norm[0]),
                 od_w_out[0].astype(BF16))

    tile = min(PROMPT_TILE, seq)
    xp, lru_conv_p, lru_h_p, swa_k_p, swa_v_p = _even_layer(
        x_prompt, mod_ev[:bp], ev_consts, None, tile=tile, chunk=min(CHUNK, seq), batch_block=1,
        n_past_valid=0, pos0=0, k_out_rows=WINDOW)
    ev_state = (state_lru_conv[0], state_lru_h[0].reshape(bs, 1, lw),
                cache_swa_k[0].reshape(bs, WINDOW, kvw), cache_swa_v[0].reshape(bs, WINDOW, kvw))
    xs, lru_conv_s, lru_h_s, swa_k_s, swa_v_s = _even_layer(
        x_sample, mod_ev[bp:n_c], ev_consts, ev_state, tile=dec_seq, chunk=min(CHUNK, dec_seq), batch_block=bs,
        n_past_valid=WINDOW, pos0=PAST_LEN, k_out_rows=dec_seq)

    xp, gdn_conv_p, gdn_s_p = _odd_layer(
        xp, mod_od[:bp], od_consts, None, tile=tile, chunk=min(CHUNK, seq), batch_block=1)
    xs, gdn_conv_s, gdn_s_s = _odd_layer(
        xs, mod_od[bp:n_c], od_consts, (state_gdn_conv[0], state_gdn_s[0]),
        tile=dec_seq, chunk=min(CHUNK, dec_seq), batch_block=bs)

    kv_shape = lambda a: a.reshape(1, a.shape[0], a.shape[1], ATT_KV_HEADS, HEAD_DIM)
    return (xp, xs,
            lru_conv_p[None], lru_conv_s[None],
            lru_h_p.reshape(1, bp, lw), lru_h_s.reshape(1, bs, lw),
            kv_shape(swa_k_p), kv_shape(swa_k_s), kv_shape(swa_v_p), kv_shape(swa_v_s),
            gdn_conv_p[None], gdn_conv_s[None], gdn_s_p[None], gdn_s_s[None])
```

```python
import functools

import jax
import jax.numpy as jnp
from jax import lax
from jax.experimental import pallas as pl
from jax.experimental.pallas import tpu as pltpu

F32 = jnp.float32
BF16 = jnp.bfloat16

CHUNK = 64
EPS = 1e-6
CONV_W = 4
NEG_INF = -1e30
LRU_HEADS = 8
LRU_C = 8.0
HEAD_DIM = 64
ATT_KV_HEADS = 2
WINDOW = 128
ROT_DIM = HEAD_DIM // 4
ROPE_THETA = 500000.0
GDN_DK = 128
GDN_DV = 128
PAST_LEN = 4096

LANES = 128
SUBLANES = 8
MXU_TILE = 256
VMEM_LIMIT_BYTES = 56 * 1024 * 1024

PROMPT_TILE = 512
MOD_TILE = 512
CONV_PAD = SUBLANES


def _silu(x):
    return x * jax.nn.sigmoid(x)


def _expm1(x):
    u = jnp.exp(x)
    d = u - 1.0
    return jnp.where(u == 1.0, x, jnp.where(d == -1.0, -1.0, d * x / jnp.log(u)))


def _rms_scale(x):
    return lax.rsqrt(jnp.mean(x * x, axis=-1, keepdims=True) + EPS)


def _rms(x, g):
    return x * _rms_scale(x) * g


def _dot(a, b):
    return jnp.dot(a, b, preferred_element_type=F32)


def _dot_nt(a, b):
    return lax.dot_general(a, b, (((1,), (1,)), ((), ())), preferred_element_type=F32)


def _dot_tn(a, b):
    return lax.dot_general(a, b, (((0,), (0,)), ((), ())), preferred_element_type=F32)


def _for_each(n, body):
    if n == 1:
        body(0)
    else:
        def step(i, carry):
            body(i)
            return carry
        lax.fori_loop(0, n, step, 0)


def _rows(i, n, size=None):
    size = n if size is None else size
    if isinstance(i, int):
        return pl.ds(i * n, size)
    return pl.ds(pl.multiple_of(i * n, n), size)


def _mod_body(c_ref, w0_ref, b0_ref, w1_ref, b1_ref, o0_ref, o1_ref):
    c = _silu(c_ref[...]).astype(BF16)
    o0_ref[...] = _dot(c, w0_ref[...].astype(BF16)) + b0_ref[...]
    o1_ref[...] = _dot(c, w1_ref[...].astype(BF16)) + b1_ref[...]


def _modulation(c_all, w0, b0, w1, b1):
    n, d = c_all.shape
    d3 = w0.shape[1]
    wspec = pl.BlockSpec((d, MOD_TILE), lambda j: (0, j))
    bspec = pl.BlockSpec((1, MOD_TILE), lambda j: (0, j))
    ospec = pl.BlockSpec((n, MOD_TILE), lambda j: (0, j))
    return pl.pallas_call(
        _mod_body,
        grid=(d3 // MOD_TILE,),
        in_specs=[pl.BlockSpec((n, d), lambda j: (0, 0)), wspec, bspec, wspec, bspec],
        out_specs=[ospec, ospec],
        out_shape=[jax.ShapeDtypeStruct((n, d3), F32)] * 2,
        name="adaln_modulation",
    )(c_all, w0, b0.reshape(1, d3), w1, b1.reshape(1, d3))


def _pre_norm(x_ref, mod_ref, norm_pre_ref, rows=None):
    bb, tc, d = x_ref.shape
    if rows is not None:
        assert bb == 1
        x = x_ref[0, rows, :]
        gain = norm_pre_ref[...] * (1.0 + mod_ref[0, 1:2, :])
        return x * _rms_scale(x) * gain + mod_ref[0, 0:1, :]
    x = x_ref[...]
    gain = norm_pre_ref[...] * (1.0 + mod_ref[:, 1:2, :])
    h = x * _rms_scale(x) * gain + mod_ref[:, 0:1, :]
    return h.reshape(bb * tc, d)


def _post_residual(x_ref, mod_ref, norm_post_ref, y, o_ref, rows=None):
    bb, tc, d = x_ref.shape
    yn = y * _rms_scale(y)
    if rows is None:
        gain = mod_ref[:, 2:3, :] * norm_post_ref[...]
        o_ref[...] = x_ref[...] + gain * yn.reshape(bb, tc, d)
    else:
        assert bb == 1
        o_ref[0, rows, :] = x_ref[0, rows, :] + (mod_ref[0, 2:3, :] * norm_post_ref[...]) * yn


def _run_interleaved(stage_lists):
    tagged = [((i + 0.5) / len(stages), k, i, stage)
              for k, stages in enumerate(stage_lists) for i, stage in enumerate(stages)]
    for _, _, _, stage in sorted(tagged, key=lambda entry: entry[:3]):
        stage()


def _causal_conv(ext_ref, bi, tc, w_ref):
    ext = ext_ref[bi]
    acc = pltpu.roll(ext, CONV_W - 1, axis=0)[CONV_PAD:] * w_ref[0:1, :]
    for j in range(1, CONV_W - 1):
        acc = acc + pltpu.roll(ext, CONV_W - 1 - j, axis=0)[CONV_PAD:] * w_ref[j:j + 1, :]
    return acc + ext[CONV_PAD:] * w_ref[CONV_W - 1:CONV_W, :]


def _scan_phase_major(a, b, h0, chunk):
    tc, width = a.shape
    row8 = lax.broadcasted_iota(jnp.int32, (SUBLANES, width), 0)
    carry = h0
    out = []
    for c in range(tc // chunk):
        acc_a, acc_b = [], []
        for r in range(SUBLANES):
            blk = slice(c * chunk + r * SUBLANES, c * chunk + (r + 1) * SUBLANES)
            if r == 0:
                acc_a.append(a[blk])
                acc_b.append(b[blk])
            else:
                acc_b.append(a[blk] * acc_b[-1] + b[blk])
                acc_a.append(a[blk] * acc_a[-1])
        tot_a, tot_b = acc_a[-1], acc_b[-1]
        for sft in (1, 2, 4):
            keep = row8 >= sft
            tot_b = jnp.where(keep, tot_a * pltpu.roll(tot_b, sft, axis=0) + tot_b, tot_b)
            tot_a = jnp.where(keep, tot_a * pltpu.roll(tot_a, sft, axis=0), tot_a)
        h_out = tot_b + tot_a * carry
        h_in = jnp.where(row8 == 0, carry, pltpu.roll(h_out, 1, axis=0))
        out += [acc_b[r] + acc_a[r] * h_in for r in range(SUBLANES)]
        carry = h_out[SUBLANES - 1:SUBLANES, :]
    return jnp.concatenate(out, axis=0), carry


def _even_body(*refs, has_state, n_past_valid, pos0, chunk, n_tiles, k_out_rows, phase_major):
    refs = list(refs)
    x_ref = refs.pop(0)
    perm_ref = refs.pop(0) if phase_major else None
    (mod_ref, npre_ref, npost_ref, win_ref, cw_ref, cb_ref, wg_ref, ba_ref, bx_ref,
     lam_ref, inv_ref, sinks_ref, wout_ref) = refs[:13]
    refs = refs[13:]
    if has_state:
        conv0_ref, h0_ref, k0_ref, v0_ref = refs[:4]
        refs = refs[4:]
    (y_ref, nconv_ref, nh_ref, nk_ref, nv_ref,
     hm_ref, u_ref, ext_ref, a_ref, b_ref, hs_ref, knat_ref, vnat_ref, q_ref, mix_ref, cosl_ref, sinl_ref) = refs
    bb, tc, _ = x_ref.shape
    lw = a_ref.shape[1]
    aw = q_ref.shape[1]
    kvw = knat_ref.shape[2]
    t = pl.program_id(1)
    cpt = tc // chunk
    m = WINDOW + chunk
    o_q, o_k, o_v, o_gb = 2 * lw, 2 * lw + aw, 2 * lw + aw + kvw, 2 * lw + aw + 2 * kvw

    @pl.when(t == 0)
    def _():
        ext_ref[...] = jnp.zeros(ext_ref.shape, F32)
        if has_state:
            ext_ref[:, CONV_PAD - 3:CONV_PAD, :] = conv0_ref[...]
            nh_ref[...] = h0_ref[...]
            knat_ref[:, 0:WINDOW, :] = k0_ref[...]
            vnat_ref[:, 0:WINDOW, :] = v0_ref[...]
        else:
            nh_ref[...] = jnp.zeros(nh_ref.shape, F32)
            knat_ref[:, 0:WINDOW, :] = jnp.zeros((bb, WINDOW, kvw), F32)
            vnat_ref[:, 0:WINDOW, :] = jnp.zeros((bb, WINDOW, kvw), F32)

    def permute_rows(v):
        n = perm_ref.shape[0]
        return jnp.concatenate(
            [_dot(perm_ref[...], v[i * n:(i + 1) * n]).astype(BF16) for i in range(v.shape[0] // n)], axis=0)

    if phase_major:
        n = perm_ref.shape[0]
        for i in range(bb * tc // n):
            rows_i = pl.ds(i * n, n)
            hmod_time = _pre_norm(x_ref, mod_ref, npre_ref, rows_i).astype(BF16)
            hm_ref[rows_i, :] = _dot(perm_ref[...], hmod_time).astype(BF16)
            u_ref[rows_i, 0:lw] = _dot(hm_ref[rows_i, :], win_ref[:, 0:lw])
    else:
        hm_ref[...] = _pre_norm(x_ref, mod_ref, npre_ref).astype(BF16)

    tile_start = (pos0 + t * tc).astype(F32) * inv_ref[...]
    cos_start, sin_start = jnp.cos(tile_start), jnp.sin(tile_start)

    def make_rope(cos_in_tile, sin_in_tile):
        cos_t = cos_start * cos_in_tile - sin_start * sin_in_tile
        sin_t = sin_start * cos_in_tile + cos_start * sin_in_tile
        lane = lax.broadcasted_iota(jnp.int32, cos_t.shape, 1) % HEAD_DIM
        half = ROT_DIM // 2
        sin_a = jnp.where(lane < half, -sin_t, 0.0)
        sin_b = jnp.where(lane >= half, sin_t, 0.0)

        def rope(xcol):
            return (xcol * cos_t + pltpu.roll(xcol, LANES - half, axis=1) * sin_a
                    + pltpu.roll(xcol, half, axis=1) * sin_b)
        return rope

    @pl.when(t == 0)
    def _():
        row_t = lax.broadcasted_iota(jnp.int32, (tc, LANES), 0)
        if phase_major:
            row_t = (row_t // chunk) * chunk + _time_of_row(row_t % chunk)
        in_tile = row_t.astype(F32) * inv_ref[...]
        cosl_ref[...] = jnp.cos(in_tile)
        sinl_ref[...] = jnp.sin(in_tile)

    rope = make_rope(cosl_ref[...], sinl_ref[...])

    neg_c_softplus = -LRU_C * jax.nn.softplus(-lam_ref[...])
    row8 = lax.broadcasted_iota(jnp.int32, (SUBLANES, lw), 0)

    def project(cols):
        u_ref[:, cols] = _dot(hm_ref[...], win_ref[:, cols])

    def recurrence_inputs(bi):
        rows = _rows(bi, tc)
        if phase_major:
            xc = _conv_phase_major(u_ref, ext_ref, nconv_ref, cw_ref, tc, chunk, slice(0, lw)) + cb_ref[...]
        else:
            ext_ref[bi, CONV_PAD:CONV_PAD + tc, :] = u_ref[rows, 0:lw]
            xc = _causal_conv(ext_ref, bi, tc, cw_ref) + cb_ref[...]
            nconv_ref[bi, :, :] = ext_ref[bi, CONV_PAD + tc - 3:CONV_PAD + tc, :]
            ext_ref[bi, CONV_PAD - 3:CONV_PAD, :] = ext_ref[bi, CONV_PAD + tc - 3:CONV_PAD + tc, :]
        xcb = xc.astype(BF16)
        halves = [_dot(xcb[:, i * MXU_TILE:(i + 1) * MXU_TILE], wg_ref[i]) for i in range(lw // MXU_TILE)]
        r = jax.nn.sigmoid(jnp.concatenate([g[:, :MXU_TILE] for g in halves], axis=1) + ba_ref[...])
        ig = jax.nn.sigmoid(jnp.concatenate([g[:, MXU_TILE:] for g in halves], axis=1) + bx_ref[...])
        log_a = r * neg_c_softplus
        a_ref[...] = jnp.exp(log_a)
        b_ref[...] = jnp.sqrt(-_expm1(2.0 * log_a)) * (ig * xc)

    def recurrence(bi):
        if phase_major:
            hs_ref[...], nh_ref[bi, :, :] = _scan_phase_major(a_ref[...], b_ref[...], nh_ref[bi, :, :], chunk)
            return

        def scan_block(j, hc):
            r0 = pl.multiple_of(j * SUBLANES, SUBLANES)
            a = a_ref[pl.ds(r0, SUBLANES), :]
            b = b_ref[pl.ds(r0, SUBLANES), :]
            for sft in (1, 2, 4):
                keep = row8 >= sft
                b = jnp.where(keep, a * pltpu.roll(b, sft, axis=0) + b, b)
                a = jnp.where(keep, a * pltpu.roll(a, sft, axis=0), a)
            h = a * hc + b
            hs_ref[pl.ds(r0, SUBLANES), :] = h
            return h[SUBLANES - 1:SUBLANES, :]

        nh_ref[bi, :, :] = lax.fori_loop(0, tc // SUBLANES, scan_block, nh_ref[bi, :, :])

    def recurrence_output(bi):
        rows = _rows(bi, tc)
        mix_ref[rows, 0:lw] = (hs_ref[...] * _silu(u_ref[rows, lw:2 * lw])).astype(BF16)

    def attention_inputs(bi):
        rows = _rows(bi, tc)
        for j in range(aw // LANES):
            qcol = rope(u_ref[rows, o_q + j * LANES:o_q + (j + 1) * LANES])
            q_ref[rows, j * LANES:(j + 1) * LANES] = (qcol * (HEAD_DIM ** -0.5)).astype(BF16)
        knat_ref[bi, WINDOW:WINDOW + tc, :] = rope(u_ref[rows, o_k:o_k + kvw])
        vnat_ref[bi, WINDOW:WINDOW + tc, :] = u_ref[rows, o_v:o_v + kvw]
        if not phase_major:
            nk_ref[bi, :, :] = knat_ref[bi, WINDOW + tc - k_out_rows:WINDOW + tc, :]
            nv_ref[bi, :, :] = vnat_ref[bi, WINDOW + tc - k_out_rows:WINDOW + tc, :]

    def per_batch(bi):
        recurrence_inputs(bi)
        recurrence(bi)
        recurrence_output(bi)
        attention_inputs(bi)

    if phase_major:
        later = [slice(lw, 2 * lw), slice(o_q, o_k), slice(o_k, o_gb), slice(o_gb, o_gb + aw)]
        _run_interleaved([[functools.partial(project, cols) for cols in later],
                          [functools.partial(recurrence_inputs, 0), functools.partial(recurrence, 0)]])
        recurrence_output(0)
        attention_inputs(0)
    else:
        project(slice(0, win_ref.shape[1]))
        _for_each(bb, per_batch)

    if phase_major:
        @pl.when(t == n_tiles - 1)
        def _():
            first = tc - k_out_rows
            kv = _dot(hmod_time[hmod_time.shape[0] - k_out_rows:], win_ref[:, o_k:o_k + 2 * kvw])
            in_tile = (first + lax.broadcasted_iota(jnp.int32, (k_out_rows, LANES), 0)).astype(F32) * inv_ref[...]
            rope_last = make_rope(jnp.cos(in_tile), jnp.sin(in_tile))
            nk_ref[0] = rope_last(kv[:, 0:kvw])
            nv_ref[0] = kv[:, kvw:2 * kvw]

    lo_q = lax.broadcasted_iota(jnp.int32, (chunk, LANES), 1) < HEAD_DIM
    group = (aw // HEAD_DIM) // ATT_KV_HEADS

    sinks = [jnp.concatenate([jnp.full((chunk, 1), sinks_ref[h * group + g], F32) for g in range(group)], axis=0)
             for h in range(ATT_KV_HEADS)]


    def attend():
        units = []
        for bi in range(bb):
            k_all = knat_ref[bi]
            v_all = vnat_ref[bi]
            k_rot = pltpu.roll(k_all, HEAD_DIM, axis=1)
            v_rot = pltpu.roll(v_all, HEAD_DIM, axis=1)
            lane_all = lax.broadcasted_iota(jnp.int32, k_all.shape, 1)
            lo_all = lane_all < HEAD_DIM
            kd = [jnp.where(lo_all, k_all, k_rot).astype(BF16), jnp.where(lo_all, k_rot, k_all).astype(BF16)]
            ones_hi = jnp.where(lane_all == HEAD_DIM, 1.0, 0.0)
            ones_lo = jnp.where(lane_all == 0, 1.0, 0.0)
            v_lo = [jnp.where(lo_all, v_all, ones_hi).astype(BF16), jnp.where(lo_all, v_rot, ones_hi).astype(BF16)]
            v_hi = [jnp.where(lo_all, ones_lo, v_rot).astype(BF16), jnp.where(lo_all, ones_lo, v_all).astype(BF16)]
            for c in range(cpt):
                rows = _rows(bi * cpt + c, chunk)
                window = slice(c * chunk, c * chunk + m)
                valid = None
                if n_past_valid < WINDOW:
                    key_t = lax.broadcasted_iota(jnp.int32, (1, m), 1)
                    if phase_major:
                        key_t = (key_t // chunk) * chunk + _time_of_row(key_t % chunk)
                    valid = t * tc + c * chunk - WINDOW + key_t >= -n_past_valid
                for h in range(ATT_KV_HEADS):
                    units.append(dict(rows=rows, h=h, valid=valid, kd=kd[h][window], v_lo=v_lo[h][window],
                                      v_hi=v_hi[h][window]))
        yield
        for un in units:
            rows, h = un["rows"], un["h"]
            cols = [q_ref[rows, (h * group // 2 + j) * LANES:(h * group // 2 + j + 1) * LANES]
                    for j in range(group // 2)]
            zero = jnp.zeros_like(cols[0])
            qstack = jnp.concatenate(
                [part for qc in cols for part in (jnp.where(lo_q, qc, zero), jnp.where(lo_q, zero, qc))], axis=0)
            sc = _dot_nt(qstack, un["kd"])
            un["sc"] = sc if un["valid"] is None else jnp.where(un["valid"], sc, NEG_INF)
        yield
        for un in units:
            un["mx"] = jnp.maximum(jnp.max(un["sc"], axis=-1, keepdims=True), sinks[un["h"]])
        yield
        for un in units:
            un["p"] = jnp.exp(un["sc"] - un["mx"]).astype(BF16)
            un["sink_p"] = jnp.exp(sinks[un["h"]] - un["mx"])
        yield
        for un in units:
            rows, h, p, sink_p = un["rows"], un["h"], un["p"], un["sink_p"]
            for j in range(group // 2):
                r_lo = slice((2 * j) * chunk, (2 * j + 1) * chunk)
                r_hi = slice((2 * j + 1) * chunk, (2 * j + 2) * chunk)
                o_lo = _dot(p[r_lo], un["v_lo"])
                o_hi = _dot(p[r_hi], un["v_hi"])
                den_lo = o_lo[:, HEAD_DIM:HEAD_DIM + 1] + sink_p[r_lo]
                den_hi = o_hi[:, 0:1] + sink_p[r_hi]
                o = jnp.where(lo_q, o_lo / den_lo, o_hi / den_hi)
                col = h * group // 2 + j
                gate = _silu(u_ref[rows, o_gb + col * LANES:o_gb + (col + 1) * LANES])
                mix_ref[rows, lw + col * LANES:lw + (col + 1) * LANES] = (o * gate).astype(BF16)

    for _ in attend():
        pass

    if n_tiles > 1:
        knat_ref[:, 0:WINDOW, :] = knat_ref[:, tc:tc + WINDOW, :]
        vnat_ref[:, 0:WINDOW, :] = vnat_ref[:, tc:tc + WINDOW, :]

    mixed = mix_ref[...]
    if phase_major:
        mixed = permute_rows(mixed)
    y = _dot(mixed, wout_ref[...])
    _post_residual(x_ref, mod_ref, npost_ref, y, y_ref)


def _runs_phase_major(chunk, batch_block, has_state):
    return chunk == SUBLANES * SUBLANES and batch_block == 1 and not has_state


def _conv_history_shape(phase_major, batch_block, tile, channels):
    if phase_major:
        return (CONV_W - 1, SUBLANES, channels)
    return (batch_block, CONV_PAD + tile, channels)


def _const_spec(shape):
    zeros = (0,) * len(shape)
    return pl.BlockSpec(shape, lambda b, t: zeros)


def _even_layer(x, mod, consts, state, *, tile, chunk, batch_block, n_past_valid, pos0, k_out_rows):
    bsz, seq, d = x.shape
    (npre, npost, win, cw, cb, wg, b_a, b_x, lam, inv_row, sinks, wout) = consts
    lw = cw.shape[1]
    kvw = ATT_KV_HEADS * HEAD_DIM
    aw = (win.shape[1] - 2 * lw - 2 * kvw) // 2
    n_tiles = seq // tile
    assert seq % tile == 0 and tile % chunk == 0 and bsz % batch_block == 0
    assert n_tiles == 1 or tile >= WINDOW
    rows = batch_block * tile
    has_state = state is not None
    phase_major = _runs_phase_major(chunk, batch_block, has_state)
    assert not phase_major or tile >= k_out_rows

    def bspec(shape):
        nd = len(shape)
        return pl.BlockSpec((batch_block,) + shape, lambda b, t: (b,) + (0,) * nd)

    in_specs = [pl.BlockSpec((batch_block, tile, d), lambda b, t: (b, t, 0))]
    args = [x]
    if phase_major:
        perm = _phase_major_matrix(min(rows, MXU_TILE), chunk)
        assert rows % perm.shape[0] == 0
        in_specs.append(_const_spec(perm.shape))
        args.append(perm)
    in_specs += [
        bspec((3, d)),
        _const_spec(npre.shape), _const_spec(npost.shape), _const_spec(win.shape),
        _const_spec(cw.shape), _const_spec(cb.shape), _const_spec(wg.shape),
        _const_spec(b_a.shape), _const_spec(b_x.shape), _const_spec(lam.shape), _const_spec(inv_row.shape),
        pl.BlockSpec(memory_space=pltpu.SMEM),
        _const_spec(wout.shape),
    ]
    args += [mod, npre, npost, win, cw, cb, wg, b_a, b_x, lam, inv_row, sinks, wout]
    if has_state:
        in_specs += [bspec((CONV_W - 1, lw)), bspec((1, lw)), bspec((WINDOW, kvw)), bspec((WINDOW, kvw))]
        args += list(state)
    out_specs = [
        pl.BlockSpec((batch_block, tile, d), lambda b, t: (b, t, 0)),
        bspec((CONV_W - 1, lw)), bspec((1, lw)), bspec((k_out_rows, kvw)), bspec((k_out_rows, kvw)),
    ]
    out_shape = [
        jax.ShapeDtypeStruct((bsz, seq, d), F32),
        jax.ShapeDtypeStruct((bsz, CONV_W - 1, lw), F32),
        jax.ShapeDtypeStruct((bsz, 1, lw), F32),
        jax.ShapeDtypeStruct((bsz, k_out_rows, kvw), F32),
        jax.ShapeDtypeStruct((bsz, k_out_rows, kvw), F32),
    ]
    scratch = [
        pltpu.VMEM((rows, d), BF16),
        pltpu.VMEM((rows, win.shape[1]), F32),
        pltpu.VMEM(_conv_history_shape(phase_major, batch_block, tile, lw), F32),
        pltpu.VMEM((tile, lw), F32), pltpu.VMEM((tile, lw), F32), pltpu.VMEM((tile, lw), F32),
        pltpu.VMEM((batch_block, WINDOW + tile, kvw), F32),
        pltpu.VMEM((batch_block, WINDOW + tile, kvw), F32),
        pltpu.VMEM((rows, aw), BF16),
        pltpu.VMEM((rows, lw + aw), BF16),
        pltpu.VMEM((tile, LANES), F32), pltpu.VMEM((tile, LANES), F32),
    ]
    body = functools.partial(_even_body, has_state=has_state, n_past_valid=n_past_valid, pos0=pos0,
                             chunk=chunk, n_tiles=n_tiles, k_out_rows=k_out_rows, phase_major=phase_major)
    return pl.pallas_call(
        body,
        grid=(bsz // batch_block, n_tiles),
        in_specs=in_specs,
        out_specs=out_specs,
        out_shape=out_shape,
        scratch_shapes=scratch,
        compiler_params=pltpu.CompilerParams(
            dimension_semantics=("arbitrary", "arbitrary"), vmem_limit_bytes=VMEM_LIMIT_BYTES),
        name="even_layer_state" if has_state else "even_layer_prompt",
    )(*args)


def _gdn_groups(chunk, n_heads):
    hpg = max(1, min(n_heads, MXU_TILE // chunk))
    assert n_heads % hpg == 0
    return hpg, n_heads // hpg


def _time_of_row(i):
    return (i % SUBLANES) * SUBLANES + i // SUBLANES


def _phase_major_matrix(n_rows, chunk):
    i = jnp.arange(n_rows)
    src = (i // chunk) * chunk + _time_of_row(i % chunk)
    return (src[:, None] == i[None, :]).astype(BF16)


def _conv_phase_major(u_ref, hist_ref, nconv_ref, w_ref, tc, chunk, cols):
    cpt = tc // chunk
    blocks = [u_ref[SUBLANES * b:SUBLANES * (b + 1), cols] for b in range(tc // SUBLANES)]
    taps_w = [w_ref[j:j + 1, cols] for j in range(CONV_W)]
    top_row = lax.broadcasted_iota(jnp.int32, blocks[0].shape, 0) == 0
    first_late = SUBLANES - (CONV_W - 1)
    prev = {r: pltpu.roll(hist_ref[r - first_late, :, cols], 1, axis=0) for r in range(first_late, SUBLANES)}
    out = []
    for c in range(cpt):
        late = {}
        for r in range(first_late, SUBLANES):
            moved = pltpu.roll(blocks[SUBLANES * c + r], 1, axis=0)
            late[r] = jnp.where(top_row, prev[r], moved)
            prev[r] = moved
        for r in range(SUBLANES):
            acc = None
            for j in range(CONV_W):
                s = CONV_W - 1 - j
                tap = blocks[SUBLANES * c + r - s] if r >= s else late[r - s + SUBLANES]
                acc = tap * taps_w[j] if acc is None else acc + tap * taps_w[j]
            out.append(acc)
    for i in range(CONV_W - 1):
        last = blocks[SUBLANES * (cpt - 1) + first_late + i]
        hist_ref[i, :, cols] = last
        nconv_ref[0, i:i + 1, cols] = last[SUBLANES - 1:SUBLANES, :]
    return jnp.concatenate(out, axis=0)


def _odd_body(*refs, has_state, chunk, phase_major):
    refs = list(refs)
    x_ref = refs.pop(0)
    perm_ref = refs.pop(0) if phase_major else None
    (mod_ref, npre_ref, npost_ref, win_ref, wba_ref, cw_ref, alog_ref, dtb_ref, hn_ref, wout_ref) = refs[:10]
    refs = refs[10:]
    if has_state:
        conv0_ref, s0_ref = refs[:2]
        refs = refs[2:]
    (y_ref, nconv_ref, ns_ref,
     hm_ref, u_ref, ext_ref, q_ref, k_ref, kb_ref, kbe_ref, vb_ref, qe_ref, kd_ref,
     beta_ref, gcum_ref, eg_ref, dec_ref, wq_ref, uu_ref, qk_ref, mix_ref) = refs
    bb, tc, _ = x_ref.shape
    n_heads = ns_ref.shape[1]
    kw = n_heads * GDN_DK
    vw = n_heads * GDN_DV
    cch = 2 * kw + vw
    rows_all = bb * tc
    t = pl.program_id(1)
    cpt = tc // chunk
    n_steps = (chunk - 1).bit_length()
    assert n_steps >= 2
    hpg, n_groups = _gdn_groups(chunk, n_heads)
    lw = hpg * chunk

    @pl.when(t == 0)
    def _():
        ext_ref[...] = jnp.zeros(ext_ref.shape, F32)
        if has_state:
            ext_ref[:, CONV_PAD - 3:CONV_PAD, :] = conv0_ref[...]
            ns_ref[...] = s0_ref[...]
        else:
            ns_ref[...] = jnp.zeros(ns_ref.shape, F32)

    def permute_rows(v):
        n = perm_ref.shape[0]
        return jnp.concatenate(
            [_dot(perm_ref[...], v[i * n:(i + 1) * n]).astype(BF16) for i in range(v.shape[0] // n)], axis=0)

    if phase_major:
        n = perm_ref.shape[0]
        ba_parts = []
        for i in range(rows_all // n):
            rows_i = pl.ds(i * n, n)
            hmod = _pre_norm(x_ref, mod_ref, npre_ref, rows_i).astype(BF16)
            hm_ref[rows_i, :] = _dot(perm_ref[...], hmod).astype(BF16)
            ba_parts.append(_dot(hm_ref[rows_i, :], wba_ref[...]))
        ba = jnp.concatenate(ba_parts, axis=0)
    else:
        hm_ref[...] = _pre_norm(x_ref, mod_ref, npre_ref).astype(BF16)
        ba = _dot(hm_ref[...], wba_ref[...])
    beta_ref[...] = jax.nn.sigmoid(ba)
    g = -jnp.exp(alog_ref[...]) * jax.nn.softplus(ba + dtb_ref[...])
    if phase_major:
        row8 = lax.broadcasted_iota(jnp.int32, (SUBLANES, LANES), 0)
        parts = []
        for c in range(rows_all // chunk):
            run = []
            for r in range(SUBLANES):
                blk = g[c * chunk + r * SUBLANES:c * chunk + (r + 1) * SUBLANES]
                run.append(blk if r == 0 else run[-1] + blk)
            total = run[-1]
            incl = total
            for sft in (1, 2, 4):
                incl = jnp.where(row8 >= sft, incl + pltpu.roll(incl, sft, axis=0), incl)
            parts += [blk + (incl - total) for blk in run]
        g = jnp.concatenate(parts, axis=0)
    else:
        row_in_chunk = lax.broadcasted_iota(jnp.int32, (rows_all, LANES), 0) % chunk
        sft = 1
        while sft < chunk:
            g = jnp.where(row_in_chunk >= sft, g + pltpu.roll(g, sft, axis=0), g)
            sft *= 2
    gcum_ref[...] = g
    eg_ref[...] = jnp.exp(g)
    g3 = g.reshape(rows_all // chunk, chunk, LANES)
    dec_ref[...] = jnp.exp(g3[:, chunk - 1:chunk, :] - g3).reshape(rows_all, LANES)

    def project(cols):
        u_ref[:, cols] = _dot(hm_ref[...], win_ref[:, cols])

    def head_operands(rows, h, qh, kh, vh):
        hc = slice(h * GDN_DK, (h + 1) * GDN_DK)
        qh = qh * (lax.rsqrt(jnp.sum(qh * qh, axis=-1, keepdims=True) + EPS) * (GDN_DK ** -0.5))
        kh = kh * lax.rsqrt(jnp.sum(kh * kh, axis=-1, keepdims=True) + EPS)
        beta = beta_ref[rows, :][:, h:h + 1]
        eg = eg_ref[rows, :][:, n_heads + h:n_heads + h + 1]
        kb = kh * beta
        q_ref[rows, hc] = qh.astype(BF16)
        k_ref[rows, hc] = kh.astype(BF16)
        kb_ref[rows, hc] = kb.astype(BF16)
        kbe_ref[rows, hc] = (kb * eg).astype(BF16)
        vb_ref[rows, hc] = (vh * beta).astype(BF16)
        qe_ref[rows, hc] = (qh * eg).astype(BF16)
        kd_ref[rows, hc] = (kh * dec_ref[rows, :][:, n_heads + h:n_heads + h + 1]).astype(BF16)

    def per_batch(bi):
        rows = _rows(bi, tc)
        ext_ref[bi, CONV_PAD:CONV_PAD + tc, :] = u_ref[rows, 0:cch]
        qkv = _silu(_causal_conv(ext_ref, bi, tc, cw_ref))
        nconv_ref[bi, :, :] = ext_ref[bi, CONV_PAD + tc - 3:CONV_PAD + tc, :]
        ext_ref[bi, CONV_PAD - 3:CONV_PAD, :] = ext_ref[bi, CONV_PAD + tc - 3:CONV_PAD + tc, :]
        for h in range(n_heads):
            head_operands(rows, h, qkv[:, h * GDN_DK:(h + 1) * GDN_DK],
                          qkv[:, kw + h * GDN_DK:kw + (h + 1) * GDN_DK],
                          qkv[:, 2 * kw + h * GDN_DV:2 * kw + (h + 1) * GDN_DV])

    pair = MXU_TILE // GDN_DK

    def pair_stages(j):
        col_sets = [slice(base + j * MXU_TILE, base + (j + 1) * MXU_TILE) for base in (0, kw, 2 * kw)]

        def operands():
            rows = pl.ds(0, tc)
            q2, k2, v2 = [_silu(_conv_phase_major(u_ref, ext_ref, nconv_ref, cw_ref, tc, chunk, cols))
                          for cols in col_sets]
            for i in range(pair):
                part = slice(i * GDN_DK, (i + 1) * GDN_DK)
                head_operands(rows, pair * j + i, q2[:, part], k2[:, part], v2[:, part])

        return [functools.partial(project, cols) for cols in col_sets] + [operands]

    if not phase_major:
        project(slice(0, cch + vw))
        _for_each(bb, per_batch)

    lane_w = lax.broadcasted_iota(jnp.int32, (chunk, lw), 1)
    row_w = lax.broadcasted_iota(jnp.int32, (chunk, lw), 0)
    blk_w = lane_w // chunk
    col_w = lane_w % chunk
    diag_w = row_w == col_w
    if phase_major:
        row_w, col_w = _time_of_row(row_w), _time_of_row(col_w)
    incl_w = row_w >= col_w
    strict_w = row_w > col_w
    eye_w = jnp.where(diag_w, 1.0, 0.0).astype(F32)
    feat_blk = lax.broadcasted_iota(jnp.int32, (chunk, hpg * GDN_DK), 1) // GDN_DK

    def block_diag(m_b):
        return jnp.concatenate([jnp.where(blk_w == hh, m_b, jnp.zeros_like(m_b)) for hh in range(hpg)], axis=0)

    def head_cols(h):
        return slice(h * GDN_DK, (h + 1) * GDN_DK)

    def wy_stages(segs, head_groups):
        units = []

        def build():
            for seg in segs:
                rows = _rows(seg, chunk)
                gt = gcum_ref[rows, :]
                for gi in head_groups:
                    gcols = slice(gi * hpg * GDN_DK, (gi + 1) * hpg * GDN_DK)
                    k4 = k_ref[rows, gcols]
                    k_bd = jnp.concatenate(
                        [jnp.where(feat_blk == hh, k4, jnp.zeros_like(k4)) for hh in range(hpg)], axis=0)
                    kk = _dot_nt(jnp.concatenate([kb_ref[rows, gcols], q_ref[rows, gcols]], axis=0), k_bd)
                    gcol = jnp.zeros((chunk, lw), F32)
                    for hh in range(hpg):
                        ln = n_heads + gi * hpg + hh
                        gcol = jnp.where(blk_w == hh, gt[:, ln:ln + 1], gcol)
                    grow = jnp.sum(jnp.where(diag_w, gcol, 0.0), axis=0, keepdims=True)
                    decay = jnp.exp(jnp.where(incl_w, gcol - grow, 0.0))
                    p_b = (-(kk[:chunk] * jnp.where(strict_w, decay, 0.0))).astype(BF16)
                    qk_ref[seg * n_groups + gi] = (kk[chunk:] * jnp.where(incl_w, decay, 0.0)).astype(BF16)
                    units.append(dict(seg=seg, rows=rows, gi=gi, p=p_b, t=eye_w + p_b.astype(F32)))

        def square():
            for un in units:
                un["p"] = _dot(un["p"], block_diag(un["p"])).astype(BF16)

        def extend_and_square():
            for un in units:
                out = _dot(jnp.concatenate([un["t"].astype(BF16), un["p"]], axis=0), block_diag(un["p"]))
                un["t"] = un["t"] + out[:chunk]
                un["p"] = out[chunk:].astype(BF16)

        def extend():
            for un in units:
                un["t"] = un["t"] + _dot(un["t"].astype(BF16), block_diag(un["p"]))

        def apply():
            for un in units:
                seg, rows, gi = un["seg"], un["rows"], un["gi"]
                heads = range(gi * hpg, (gi + 1) * hpg)
                rhs = jnp.concatenate(
                    [jnp.concatenate([kbe_ref[rows, head_cols(h)], vb_ref[rows, head_cols(h)]], axis=1)
                     for h in heads], axis=0)
                wu = _dot(block_diag(un["t"].astype(BF16)), rhs)
                for hh, h in enumerate(heads):
                    blk = wu[hh * chunk:(hh + 1) * chunk]
                    wq_ref[seg * n_heads + h] = jnp.concatenate(
                        [blk[:, :GDN_DK].astype(BF16), qe_ref[rows, head_cols(h)]], axis=0)
                    uu_ref[rows, head_cols(h)] = blk[:, GDN_DK:]

        return [build, square] + [extend_and_square] * (n_steps - 2) + [extend, apply]

    zero_b = jnp.zeros((chunk, GDN_DV), BF16)
    state = {(bi, h): ns_ref[bi, h] for bi in range(bb) for h in range(n_heads)}

    def recurrence_stages(segs):
        stages = []
        for c in sorted({seg % cpt for seg in segs}):
            wave = [seg for seg in segs if seg % cpt == c]
            pairs = [(seg, h) for seg in wave for h in range(n_heads)]
            held = {}

            def correct(wave=wave, pairs=pairs, held=held):
                for seg, h in pairs:
                    held["ws_qs", seg, h] = _dot(wq_ref[seg * n_heads + h], state[seg // cpt, h].astype(BF16))
                for seg, h in pairs:
                    held["v", seg, h] = (uu_ref[_rows(seg, chunk), head_cols(h)]
                                         - held["ws_qs", seg, h][:chunk]).astype(BF16)

            def advance(wave=wave, pairs=pairs, held=held):
                for seg in wave:
                    held["eg", seg] = jnp.exp(gcum_ref[pl.ds(seg * chunk + chunk - 1, 1), :])
                for seg, h in pairs:
                    state[seg // cpt, h] = (state[seg // cpt, h] * held["eg", seg][:, n_heads + h:n_heads + h + 1]
                                            + _dot_tn(kd_ref[_rows(seg, chunk), head_cols(h)], held["v", seg, h]))
                for seg in wave:
                    rows = _rows(seg, chunk)
                    for gi in range(n_groups):
                        heads = range(gi * hpg, (gi + 1) * hpg)
                        v_bd = jnp.concatenate(
                            [jnp.concatenate([held["v", seg, h] if j == hh else zero_b for j in range(hpg)], axis=1)
                             for hh, h in enumerate(heads)], axis=0)
                        o = (jnp.concatenate([held["ws_qs", seg, h][chunk:] for h in heads], axis=1)
                             + _dot(qk_ref[seg * n_groups + gi], v_bd))
                        for hh, h in enumerate(heads):
                            gate = _silu(u_ref[rows, cch + h * GDN_DV:cch + (h + 1) * GDN_DV])
                            o_h = _rms(o[:, hh * GDN_DV:(hh + 1) * GDN_DV], hn_ref[...])
                            mix_ref[rows, head_cols(h)] = (o_h * gate).astype(BF16)

            stages += [correct, advance]
        return stages

    def output_stages(rows):
        held = {}

        def gather():
            mixed = mix_ref[rows, :]
            held["mixed"] = permute_rows(mixed) if phase_major else mixed

        def project():
            held["y"] = _dot(held["mixed"], wout_ref[...])

        def finish():
            _post_residual(x_ref, mod_ref, npost_ref, held["y"], y_ref, rows if n_row_groups > 1 else None)

        return [gather, project, finish]

    group_rows = min(rows_all, MXU_TILE)
    n_row_groups = rows_all // group_rows
    assert rows_all % group_rows == 0 and group_rows % chunk == 0 and (bb == 1 or n_row_groups == 1)
    segs_of = [range(g * group_rows // chunk, (g + 1) * group_rows // chunk) for g in range(n_row_groups)]
    if phase_major:
        pairs_per_group = hpg // pair
        assert hpg % pair == 0
        work = [(segs, gi) for gi in range(n_groups) for segs in segs_of]
        ahead = [[stage for j in range(gi * pairs_per_group, (gi + 1) * pairs_per_group) for stage in pair_stages(j)]
                 for gi in range(n_groups)]
        ahead.append([functools.partial(project, slice(cch + i * MXU_TILE, cch + (i + 1) * MXU_TILE))
                      for i in range(vw // MXU_TILE)])
        for stage in ahead[0]:
            stage()
        for gi in range(n_groups):
            factors = [stage for segs, g in work if g == gi for stage in wy_stages(segs, [gi])]
            _run_interleaved([ahead[gi + 1], factors])
    else:
        for segs in segs_of:
            for stage in wy_stages(segs, range(n_groups)):
                stage()
    for segs in segs_of:
        for stage in recurrence_stages(segs):
            stage()
    for g in range(n_row_groups):
        for stage in output_stages(pl.ds(g * group_rows, group_rows)):
            stage()
    for bh, value in state.items():
        ns_ref[bh[0], bh[1]] = value


def _odd_layer(x, mod, consts, state, *, tile, chunk, batch_block):
    bsz, seq, d = x.shape
    (npre, npost, win, wba, cw, alog, dtb, hn, wout) = consts
    cch = cw.shape[1]
    vw = wout.shape[0]
    n_heads = vw // GDN_DV
    n_tiles = seq // tile
    assert seq % tile == 0 and tile % chunk == 0 and bsz % batch_block == 0
    rows = batch_block * tile
    has_state = state is not None
    phase_major = _runs_phase_major(chunk, batch_block, has_state)

    def bspec(shape):
        nd = len(shape)
        return pl.BlockSpec((batch_block,) + shape, lambda b, t: (b,) + (0,) * nd)

    in_specs = [pl.BlockSpec((batch_block, tile, d), lambda b, t: (b, t, 0))]
    args = [x]
    if phase_major:
        perm_rows = min(rows, MXU_TILE)
        assert rows % perm_rows == 0 and perm_rows % chunk == 0
        in_specs.append(_const_spec((perm_rows, perm_rows)))
        args.append(_phase_major_matrix(perm_rows, chunk))
    in_specs += [
        bspec((3, d)),
        _const_spec(npre.shape), _const_spec(npost.shape), _const_spec((d, cch + vw)), _const_spec(wba.shape),
        _const_spec(cw.shape), _const_spec(alog.shape), _const_spec(dtb.shape), _const_spec(hn.shape),
        _const_spec(wout.shape),
    ]
    args += [mod, npre, npost, win, wba, cw, alog, dtb, hn, wout]
    if has_state:
        in_specs += [bspec((CONV_W - 1, cch)), bspec((n_heads, GDN_DK, GDN_DV))]
        args += list(state)
    out_specs = [
        pl.BlockSpec((batch_block, tile, d), lambda b, t: (b, t, 0)),
        bspec((CONV_W - 1, cch)), bspec((n_heads, GDN_DK, GDN_DV)),
    ]
    out_shape = [
        jax.ShapeDtypeStruct((bsz, seq, d), F32),
        jax.ShapeDtypeStruct((bsz, CONV_W - 1, cch), F32),
        jax.ShapeDtypeStruct((bsz, n_heads, GDN_DK, GDN_DV), F32),
    ]
    hpg, n_groups = _gdn_groups(chunk, n_heads)
    n_seg = rows // chunk
    scratch = (
        [pltpu.VMEM((rows, d), BF16),
         pltpu.VMEM((rows, cch + vw), F32),
         pltpu.VMEM(_conv_history_shape(phase_major, batch_block, tile, cch), F32)]
        + [pltpu.VMEM((rows, vw), BF16)] * 7
        + [pltpu.VMEM((rows, LANES), F32)] * 4
        + [pltpu.VMEM((n_seg * n_heads, 2 * chunk, GDN_DK), BF16),
           pltpu.VMEM((rows, vw), F32),
           pltpu.VMEM((n_seg * n_groups, chunk, hpg * chunk), BF16),
           pltpu.VMEM((rows, vw), BF16)]
    )
    body = functools.partial(_odd_body, has_state=has_state, chunk=chunk, phase_major=phase_major)
    return pl.pallas_call(
        body,
        grid=(bsz // batch_block, n_tiles),
        in_specs=in_specs,
        out_specs=out_specs,
        out_shape=out_shape,
        scratch_shapes=scratch,
        compiler_params=pltpu.CompilerParams(
            dimension_semantics=("arbitrary", "arbitrary"), vmem_limit_bytes=VMEM_LIMIT_BYTES),
        name="odd_layer_state" if has_state else "odd_layer_prompt",
    )(*args)


def _gate_weights(w_a, w_x):
    heads, blk, _ = w_a.shape
    per_tile = MXU_TILE // blk
    tiles = []
    for i in range(heads // per_tile):
        sl = slice(i * per_tile, (i + 1) * per_tile)
        tiles.append(jnp.concatenate(
            [jax.scipy.linalg.block_diag(*w_a[sl]), jax.scipy.linalg.block_diag(*w_x[sl])], axis=1))
    return jnp.stack(tiles).astype(BF16)


def _rope_inv_row():
    half = ROT_DIM // 2
    inv = ROPE_THETA ** (-(jnp.arange(half, dtype=F32) * 2.0 / ROT_DIM))
    per_head = jnp.concatenate([inv, inv, jnp.zeros((HEAD_DIM - ROT_DIM,), F32)])
    return jnp.tile(per_head, LANES // HEAD_DIM).reshape(1, LANES)


def kernel(x_prompt, x_sample, state_lru_conv, state_lru_h, cache_swa_k, cache_swa_v, state_gdn_conv, state_gdn_s, c_prompt, c_sample, ev_mod_w, ev_mod_b, ev_norm_pre, ev_norm_post, ev_w_in, lru_conv_w, lru_conv_b, lru_w_a, lru_b_a, lru_w_x, lru_b_x, lru_lambda, swa_sinks, ev_w_out, od_mod_w, od_mod_b, od_norm_pre, od_norm_post, od_w_in, gdn_conv_w, gdn_a_log, gdn_dt_bias, gdn_head_norm, od_w_out):
    bp, seq, d = x_prompt.shape
    bs, dec_seq, _ = x_sample.shape
    lw = lru_conv_w.shape[-1]
    kvw = ATT_KV_HEADS * HEAD_DIM
    n_heads = gdn_a_log.shape[-1]
    cch = gdn_conv_w.shape[-1]
    vw = od_w_out.shape[1]

    n_c = bp + bs
    n_pad = -n_c % 16
    c_all = jnp.concatenate([c_prompt, c_sample, jnp.zeros((n_pad, d), F32)], axis=0)
    mod_ev, mod_od = _modulation(c_all, ev_mod_w[0], ev_mod_b[0], od_mod_w[0], od_mod_b[0])
    mod_ev = mod_ev.reshape(n_c + n_pad, 3, d)
    mod_od = mod_od.reshape(n_c + n_pad, 3, d)

    row = lambda a: a.reshape(1, -1)
    ev_consts = (row(ev_norm_pre[0]), row(ev_norm_post[0]), ev_w_in[0].astype(BF16), lru_conv_w[0],
                 row(lru_conv_b[0]), _gate_weights(lru_w_a[0], lru_w_x[0]), row(lru_b_a[0]), row(lru_b_x[0]),
                 row(lru_lambda[0]), _rope_inv_row(), swa_sinks[0], ev_w_out[0].astype(BF16))
    w_in_od = od_w_in[0]
    pad_lanes = lambda a: jnp.pad(a, ((0, 0), (0, LANES - a.shape[1])))
    head_row = lambda a: jnp.pad(a.reshape(1, -1), ((0, 0), (n_heads, LANES - 2 * n_heads)))
    od_consts = (row(od_norm_pre[0]), row(od_norm_post[0]), w_in_od.astype(BF16),
                 pad_lanes(w_in_od[:, cch + vw:]).astype(BF16), gdn_conv_w[0],
                 head_row(gdn_a_log[0]), head_row(gdn_dt_bias[0]), row(gdn_head_norm[0]),
                 od_w_out[0].astype(BF16))

    tile = min(PROMPT_TILE, seq)
    xp, lru_conv_p, lru_h_p, swa_k_p, swa_v_p = _even_layer(
        x_prompt, mod_ev[:bp], ev_consts, None, tile=tile, chunk=min(CHUNK, seq), batch_block=1,
        n_past_valid=0, pos0=0, k_out_rows=WINDOW)
    ev_state = (state_lru_conv[0], state_lru_h[0].reshape(bs, 1, lw),
                cache_swa_k[0].reshape(bs, WINDOW, kvw), cache_swa_v[0].reshape(bs, WINDOW, kvw))
    xs, lru_conv_s, lru_h_s, swa_k_s, swa_v_s = _even_layer(
        x_sample, mod_ev[bp:n_c], ev_consts, ev_state, tile=dec_seq, chunk=min(CHUNK, dec_seq), batch_block=bs,
        n_past_valid=WINDOW, pos0=PAST_LEN, k_out_rows=dec_seq)

    xp, gdn_conv_p, gdn_s_p = _odd_layer(
        xp, mod_od[:bp], od_consts, None, tile=tile, chunk=min(CHUNK, seq), batch_block=1)
    xs, gdn_conv_s, gdn_s_s = _odd_layer(
        xs, mod_od[bp:n_c], od_consts, (state_gdn_conv[0], state_gdn_s[0]),
        tile=dec_seq, chunk=min(CHUNK, dec_seq), batch_block=bs)

    kv_shape = lambda a: a.reshape(1, a.shape[0], a.shape[1], ATT_KV_HEADS, HEAD_DIM)
    return (xp, xs,
            lru_conv_p[None], lru_conv_s[None],
            lru_h_p.reshape(1, bp, lw), lru_h_s.reshape(1, bs, lw),
            kv_shape(swa_k_p), kv_shape(swa_k_s), kv_shape(swa_v_p), kv_shape(swa_v_s),
            gdn_conv_p[None], gdn_conv_s[None], gdn_s_p[None], gdn_s_s[None])
```

```python
import functools

import jax
import jax.numpy as jnp
from jax import lax
from jax.experimental import pallas as pl
from jax.experimental.pallas import tpu as pltpu

F32 = jnp.float32
BF16 = jnp.bfloat16

CHUNK = 64
EPS = 1e-6
CONV_W = 4
NEG_INF = -1e30
LRU_HEADS = 8
LRU_C = 8.0
HEAD_DIM = 64
ATT_KV_HEADS = 2
WINDOW = 128
ROT_DIM = HEAD_DIM // 4
ROPE_THETA = 500000.0
GDN_DK = 128
GDN_DV = 128
PAST_LEN = 4096

LANES = 128
SUBLANES = 8
MXU_TILE = 256
VMEM_LIMIT_BYTES = 56 * 1024 * 1024

PROMPT_TILE = 512
MOD_TILE = 512
CONV_PAD = SUBLANES


def _silu(x):
    return x * jax.nn.sigmoid(x)


def _expm1(x):
    u = jnp.exp(x)
    d = u - 1.0
    return jnp.where(u == 1.0, x, jnp.where(d == -1.0, -1.0, d * x / jnp.log(u)))


def _rms_scale(x):
    return lax.rsqrt(jnp.mean(x * x, axis=-1, keepdims=True) + EPS)


def _rms(x, g):
    return x * _rms_scale(x) * g


def _dot(a, b):
    return jnp.dot(a, b, preferred_element_type=F32)


def _dot_nt(a, b):
    return lax.dot_general(a, b, (((1,), (1,)), ((), ())), preferred_element_type=F32)


def _dot_tn(a, b):
    return lax.dot_general(a, b, (((0,), (0,)), ((), ())), preferred_element_type=F32)


def _for_each(n, body):
    if n == 1:
        body(0)
    else:
        def step(i, carry):
            body(i)
            return carry
        lax.fori_loop(0, n, step, 0)


def _rows(i, n, size=None):
    size = n if size is None else size
    if isinstance(i, int):
        return pl.ds(i * n, size)
    return pl.ds(pl.multiple_of(i * n, n), size)


def _mod_body(c_ref, w0_ref, b0_ref, w1_ref, b1_ref, o0_ref, o1_ref):
    c = _silu(c_ref[...]).astype(BF16)
    o0_ref[...] = _dot(c, w0_ref[...].astype(BF16)) + b0_ref[...]
    o1_ref[...] = _dot(c, w1_ref[...].astype(BF16)) + b1_ref[...]


def _modulation(c_all, w0, b0, w1, b1):
    n, d = c_all.shape
    d3 = w0.shape[1]
    wspec = pl.BlockSpec((d, MOD_TILE), lambda j: (0, j))
    bspec = pl.BlockSpec((1, MOD_TILE), lambda j: (0, j))
    ospec = pl.BlockSpec((n, MOD_TILE), lambda j: (0, j))
    return pl.pallas_call(
        _mod_body,
        grid=(d3 // MOD_TILE,),
        in_specs=[pl.BlockSpec((n, d), lambda j: (0, 0)), wspec, bspec, wspec, bspec],
        out_specs=[ospec, ospec],
        out_shape=[jax.ShapeDtypeStruct((n, d3), F32)] * 2,
        name="adaln_modulation",
    )(c_all, w0, b0.reshape(1, d3), w1, b1.reshape(1, d3))


def _pre_norm(x_ref, mod_ref, norm_pre_ref):
    bb, tc, d = x_ref.shape
    x = x_ref[...]
    gain = norm_pre_ref[...] * (1.0 + mod_ref[:, 1:2, :])
    h = x * _rms_scale(x) * gain + mod_ref[:, 0:1, :]
    return h.reshape(bb * tc, d)


def _post_residual(x_ref, mod_ref, norm_post_ref, y, o_ref, rows=None):
    bb, tc, d = x_ref.shape
    yn = y * _rms_scale(y)
    if rows is None:
        gain = mod_ref[:, 2:3, :] * norm_post_ref[...]
        o_ref[...] = x_ref[...] + gain * yn.reshape(bb, tc, d)
    else:
        assert bb == 1
        o_ref[0, rows, :] = x_ref[0, rows, :] + (mod_ref[0, 2:3, :] * norm_post_ref[...]) * yn


def _run_interleaved(stage_lists):
    tagged = [((i + 0.5) / len(stages), k, i, stage)
              for k, stages in enumerate(stage_lists) for i, stage in enumerate(stages)]
    for _, _, _, stage in sorted(tagged, key=lambda entry: entry[:3]):
        stage()


def _causal_conv(ext_ref, bi, tc, w_ref):
    ext = ext_ref[bi]
    acc = pltpu.roll(ext, CONV_W - 1, axis=0)[CONV_PAD:] * w_ref[0:1, :]
    for j in range(1, CONV_W - 1):
        acc = acc + pltpu.roll(ext, CONV_W - 1 - j, axis=0)[CONV_PAD:] * w_ref[j:j + 1, :]
    return acc + ext[CONV_PAD:] * w_ref[CONV_W - 1:CONV_W, :]


def _scan_phase_major(a, b, h0, chunk):
    tc, width = a.shape
    row8 = lax.broadcasted_iota(jnp.int32, (SUBLANES, width), 0)
    carry = h0
    out = []
    for c in range(tc // chunk):
        acc_a, acc_b = [], []
        for r in range(SUBLANES):
            blk = slice(c * chunk + r * SUBLANES, c * chunk + (r + 1) * SUBLANES)
            if r == 0:
                acc_a.append(a[blk])
                acc_b.append(b[blk])
            else:
                acc_b.append(a[blk] * acc_b[-1] + b[blk])
                acc_a.append(a[blk] * acc_a[-1])
        tot_a, tot_b = acc_a[-1], acc_b[-1]
        for sft in (1, 2, 4):
            keep = row8 >= sft
            tot_b = jnp.where(keep, tot_a * pltpu.roll(tot_b, sft, axis=0) + tot_b, tot_b)
            tot_a = jnp.where(keep, tot_a * pltpu.roll(tot_a, sft, axis=0), tot_a)
        h_out = tot_b + tot_a * carry
        h_in = jnp.where(row8 == 0, carry, pltpu.roll(h_out, 1, axis=0))
        out += [acc_b[r] + acc_a[r] * h_in for r in range(SUBLANES)]
        carry = h_out[SUBLANES - 1:SUBLANES, :]
    return jnp.concatenate(out, axis=0), carry


def _even_body(*refs, has_state, n_past_valid, pos0, chunk, n_tiles, k_out_rows, phase_major):
    refs = list(refs)
    x_ref = refs.pop(0)
    perm_ref = refs.pop(0) if phase_major else None
    (mod_ref, npre_ref, npost_ref, win_ref, cw_ref, cb_ref, wg_ref, ba_ref, bx_ref,
     lam_ref, inv_ref, sinks_ref, wout_ref) = refs[:13]
    refs = refs[13:]
    if has_state:
        conv0_ref, h0_ref, k0_ref, v0_ref = refs[:4]
        refs = refs[4:]
    (y_ref, nconv_ref, nh_ref, nk_ref, nv_ref,
     hm_ref, u_ref, ext_ref, a_ref, b_ref, hs_ref, knat_ref, vnat_ref, q_ref, mix_ref, cosl_ref, sinl_ref) = refs
    bb, tc, _ = x_ref.shape
    lw = a_ref.shape[1]
    aw = q_ref.shape[1]
    kvw = knat_ref.shape[2]
    t = pl.program_id(1)
    cpt = tc // chunk
    m = WINDOW + chunk
    o_q, o_k, o_v, o_gb = 2 * lw, 2 * lw + aw, 2 * lw + aw + kvw, 2 * lw + aw + 2 * kvw

    @pl.when(t == 0)
    def _():
        ext_ref[...] = jnp.zeros(ext_ref.shape, F32)
        if has_state:
            ext_ref[:, CONV_PAD - 3:CONV_PAD, :] = conv0_ref[...]
            nh_ref[...] = h0_ref[...]
            knat_ref[:, 0:WINDOW, :] = k0_ref[...]
            vnat_ref[:, 0:WINDOW, :] = v0_ref[...]
        else:
            nh_ref[...] = jnp.zeros(nh_ref.shape, F32)
            knat_ref[:, 0:WINDOW, :] = jnp.zeros((bb, WINDOW, kvw), F32)
            vnat_ref[:, 0:WINDOW, :] = jnp.zeros((bb, WINDOW, kvw), F32)

    def permute_rows(v):
        n = perm_ref.shape[0]
        return jnp.concatenate(
            [_dot(perm_ref[...], v[i * n:(i + 1) * n]).astype(BF16) for i in range(v.shape[0] // n)], axis=0)

    hmod_time = _pre_norm(x_ref, mod_ref, npre_ref).astype(BF16)
    hm_ref[...] = permute_rows(hmod_time) if phase_major else hmod_time

    tile_start = (pos0 + t * tc).astype(F32) * inv_ref[...]
    cos_start, sin_start = jnp.cos(tile_start), jnp.sin(tile_start)

    def make_rope(cos_in_tile, sin_in_tile):
        cos_t = cos_start * cos_in_tile - sin_start * sin_in_tile
        sin_t = sin_start * cos_in_tile + cos_start * sin_in_tile
        lane = lax.broadcasted_iota(jnp.int32, cos_t.shape, 1) % HEAD_DIM
        half = ROT_DIM // 2
        sin_a = jnp.where(lane < half, -sin_t, 0.0)
        sin_b = jnp.where(lane >= half, sin_t, 0.0)

        def rope(xcol):
            return (xcol * cos_t + pltpu.roll(xcol, LANES - half, axis=1) * sin_a
                    + pltpu.roll(xcol, half, axis=1) * sin_b)
        return rope

    @pl.when(t == 0)
    def _():
        row_t = lax.broadcasted_iota(jnp.int32, (tc, LANES), 0)
        if phase_major:
            row_t = (row_t // chunk) * chunk + _time_of_row(row_t % chunk)
        in_tile = row_t.astype(F32) * inv_ref[...]
        cosl_ref[...] = jnp.cos(in_tile)
        sinl_ref[...] = jnp.sin(in_tile)

    rope = make_rope(cosl_ref[...], sinl_ref[...])

    neg_c_softplus = -LRU_C * jax.nn.softplus(-lam_ref[...])
    row8 = lax.broadcasted_iota(jnp.int32, (SUBLANES, lw), 0)

    def project(cols):
        u_ref[:, cols] = _dot(hm_ref[...], win_ref[:, cols])

    def recurrence_inputs(bi):
        rows = _rows(bi, tc)
        if phase_major:
            xc = _conv_phase_major(u_ref, ext_ref, nconv_ref, cw_ref, tc, chunk, slice(0, lw)) + cb_ref[...]
        else:
            ext_ref[bi, CONV_PAD:CONV_PAD + tc, :] = u_ref[rows, 0:lw]
            xc = _causal_conv(ext_ref, bi, tc, cw_ref) + cb_ref[...]
            nconv_ref[bi, :, :] = ext_ref[bi, CONV_PAD + tc - 3:CONV_PAD + tc, :]
            ext_ref[bi, CONV_PAD - 3:CONV_PAD, :] = ext_ref[bi, CONV_PAD + tc - 3:CONV_PAD + tc, :]
        xcb = xc.astype(BF16)
        halves = [_dot(xcb[:, i * MXU_TILE:(i + 1) * MXU_TILE], wg_ref[i]) for i in range(lw // MXU_TILE)]
        r = jax.nn.sigmoid(jnp.concatenate([g[:, :MXU_TILE] for g in halves], axis=1) + ba_ref[...])
        ig = jax.nn.sigmoid(jnp.concatenate([g[:, MXU_TILE:] for g in halves], axis=1) + bx_ref[...])
        log_a = r * neg_c_softplus
        a_ref[...] = jnp.exp(log_a)
        b_ref[...] = jnp.sqrt(-_expm1(2.0 * log_a)) * (ig * xc)

    def recurrence(bi):
        if phase_major:
            hs_ref[...], nh_ref[bi, :, :] = _scan_phase_major(a_ref[...], b_ref[...], nh_ref[bi, :, :], chunk)
            return

        def scan_block(j, hc):
            r0 = pl.multiple_of(j * SUBLANES, SUBLANES)
            a = a_ref[pl.ds(r0, SUBLANES), :]
            b = b_ref[pl.ds(r0, SUBLANES), :]
            for sft in (1, 2, 4):
                keep = row8 >= sft
                b = jnp.where(keep, a * pltpu.roll(b, sft, axis=0) + b, b)
                a = jnp.where(keep, a * pltpu.roll(a, sft, axis=0), a)
            h = a * hc + b
            hs_ref[pl.ds(r0, SUBLANES), :] = h
            return h[SUBLANES - 1:SUBLANES, :]

        nh_ref[bi, :, :] = lax.fori_loop(0, tc // SUBLANES, scan_block, nh_ref[bi, :, :])

    def recurrence_output(bi):
        rows = _rows(bi, tc)
        mix_ref[rows, 0:lw] = (hs_ref[...] * _silu(u_ref[rows, lw:2 * lw])).astype(BF16)

    def attention_inputs(bi):
        rows = _rows(bi, tc)
        for j in range(aw // LANES):
            qcol = rope(u_ref[rows, o_q + j * LANES:o_q + (j + 1) * LANES])
            q_ref[rows, j * LANES:(j + 1) * LANES] = (qcol * (HEAD_DIM ** -0.5)).astype(BF16)
        knat_ref[bi, WINDOW:WINDOW + tc, :] = rope(u_ref[rows, o_k:o_k + kvw])
        vnat_ref[bi, WINDOW:WINDOW + tc, :] = u_ref[rows, o_v:o_v + kvw]
        if not phase_major:
            nk_ref[bi, :, :] = knat_ref[bi, WINDOW + tc - k_out_rows:WINDOW + tc, :]
            nv_ref[bi, :, :] = vnat_ref[bi, WINDOW + tc - k_out_rows:WINDOW + tc, :]

    def per_batch(bi):
        recurrence_inputs(bi)
        recurrence(bi)
        recurrence_output(bi)
        attention_inputs(bi)

    if phase_major:
        project(slice(0, lw))
        later = [slice(lw, 2 * lw), slice(o_q, o_k), slice(o_k, o_gb), slice(o_gb, o_gb + aw)]
        _run_interleaved([[functools.partial(project, cols) for cols in later],
                          [functools.partial(recurrence_inputs, 0), functools.partial(recurrence, 0)]])
        recurrence_output(0)
        attention_inputs(0)
    else:
        project(slice(0, win_ref.shape[1]))
        _for_each(bb, per_batch)

    if phase_major:
        @pl.when(t == n_tiles - 1)
        def _():
            first = tc - k_out_rows
            kv = _dot(hmod_time[first:], win_ref[:, o_k:o_k + 2 * kvw])
            in_tile = (first + lax.broadcasted_iota(jnp.int32, (k_out_rows, LANES), 0)).astype(F32) * inv_ref[...]
            rope_last = make_rope(jnp.cos(in_tile), jnp.sin(in_tile))
            nk_ref[0] = rope_last(kv[:, 0:kvw])
            nv_ref[0] = kv[:, kvw:2 * kvw]

    lo_q = lax.broadcasted_iota(jnp.int32, (chunk, LANES), 1) < HEAD_DIM
    group = (aw // HEAD_DIM) // ATT_KV_HEADS

    sinks = [jnp.concatenate([jnp.full((chunk, 1), sinks_ref[h * group + g], F32) for g in range(group)], axis=0)
             for h in range(ATT_KV_HEADS)]


    def attend():
        units = []
        for bi in range(bb):
            k_all = knat_ref[bi]
            v_all = vnat_ref[bi]
            k_rot = pltpu.roll(k_all, HEAD_DIM, axis=1)
            v_rot = pltpu.roll(v_all, HEAD_DIM, axis=1)
            lane_all = lax.broadcasted_iota(jnp.int32, k_all.shape, 1)
            lo_all = lane_all < HEAD_DIM
            kd = [jnp.where(lo_all, k_all, k_rot).astype(BF16), jnp.where(lo_all, k_rot, k_all).astype(BF16)]
            ones_hi = jnp.where(lane_all == HEAD_DIM, 1.0, 0.0)
            ones_lo = jnp.where(lane_all == 0, 1.0, 0.0)
            v_lo = [jnp.where(lo_all, v_all, ones_hi).astype(BF16), jnp.where(lo_all, v_rot, ones_hi).astype(BF16)]
            v_hi = [jnp.where(lo_all, ones_lo, v_rot).astype(BF16), jnp.where(lo_all, ones_lo, v_all).astype(BF16)]
            for c in range(cpt):
                rows = _rows(bi * cpt + c, chunk)
                window = slice(c * chunk, c * chunk + m)
                valid = None
                if n_past_valid < WINDOW:
                    key_t = lax.broadcasted_iota(jnp.int32, (1, m), 1)
                    if phase_major:
                        key_t = (key_t // chunk) * chunk + _time_of_row(key_t % chunk)
                    valid = t * tc + c * chunk - WINDOW + key_t >= -n_past_valid
                for h in range(ATT_KV_HEADS):
                    units.append(dict(rows=rows, h=h, valid=valid, kd=kd[h][window], v_lo=v_lo[h][window],
                                      v_hi=v_hi[h][window]))
        yield
        for un in units:
            rows, h = un["rows"], un["h"]
            cols = [q_ref[rows, (h * group // 2 + j) * LANES:(h * group // 2 + j + 1) * LANES]
                    for j in range(group // 2)]
            zero = jnp.zeros_like(cols[0])
            qstack = jnp.concatenate(
                [part for qc in cols for part in (jnp.where(lo_q, qc, zero), jnp.where(lo_q, zero, qc))], axis=0)
            sc = _dot_nt(qstack, un["kd"])
            un["sc"] = sc if un["valid"] is None else jnp.where(un["valid"], sc, NEG_INF)
        yield
        for un in units:
            un["mx"] = jnp.maximum(jnp.max(un["sc"], axis=-1, keepdims=True), sinks[un["h"]])
        yield
        for un in units:
            un["p"] = jnp.exp(un["sc"] - un["mx"]).astype(BF16)
            un["sink_p"] = jnp.exp(sinks[un["h"]] - un["mx"])
        yield
        for un in units:
            rows, h, p, sink_p = un["rows"], un["h"], un["p"], un["sink_p"]
            for j in range(group // 2):
                r_lo = slice((2 * j) * chunk, (2 * j + 1) * chunk)
                r_hi = slice((2 * j + 1) * chunk, (2 * j + 2) * chunk)
                o_lo = _dot(p[r_lo], un["v_lo"])
                o_hi = _dot(p[r_hi], un["v_hi"])
                den_lo = o_lo[:, HEAD_DIM:HEAD_DIM + 1] + sink_p[r_lo]
                den_hi = o_hi[:, 0:1] + sink_p[r_hi]
                o = jnp.where(lo_q, o_lo / den_lo, o_hi / den_hi)
                col = h * group // 2 + j
                gate = _silu(u_ref[rows, o_gb + col * LANES:o_gb + (col + 1) * LANES])
                mix_ref[rows, lw + col * LANES:lw + (col + 1) * LANES] = (o * gate).astype(BF16)

    for _ in attend():
        pass

    if n_tiles > 1:
        knat_ref[:, 0:WINDOW, :] = knat_ref[:, tc:tc + WINDOW, :]
        vnat_ref[:, 0:WINDOW, :] = vnat_ref[:, tc:tc + WINDOW, :]

    mixed = mix_ref[...]
    if phase_major:
        mixed = permute_rows(mixed)
    y = _dot(mixed, wout_ref[...])
    _post_residual(x_ref, mod_ref, npost_ref, y, y_ref)


def _runs_phase_major(chunk, batch_block, has_state):
    return chunk == SUBLANES * SUBLANES and batch_block == 1 and not has_state


def _conv_history_shape(phase_major, batch_block, tile, channels):
    if phase_major:
        return (CONV_W - 1, SUBLANES, channels)
    return (batch_block, CONV_PAD + tile, channels)


def _const_spec(shape):
    zeros = (0,) * len(shape)
    return pl.BlockSpec(shape, lambda b, t: zeros)


def _even_layer(x, mod, consts, state, *, tile, chunk, batch_block, n_past_valid, pos0, k_out_rows):
    bsz, seq, d = x.shape
    (npre, npost, win, cw, cb, wg, b_a, b_x, lam, inv_row, sinks, wout) = consts
    lw = cw.shape[1]
    kvw = ATT_KV_HEADS * HEAD_DIM
    aw = (win.shape[1] - 2 * lw - 2 * kvw) // 2
    n_tiles = seq // tile
    assert seq % tile == 0 and tile % chunk == 0 and bsz % batch_block == 0
    assert n_tiles == 1 or tile >= WINDOW
    rows = batch_block * tile
    has_state = state is not None
    phase_major = _runs_phase_major(chunk, batch_block, has_state)
    assert not phase_major or tile >= k_out_rows

    def bspec(shape):
        nd = len(shape)
        return pl.BlockSpec((batch_block,) + shape, lambda b, t: (b,) + (0,) * nd)

    in_specs = [pl.BlockSpec((batch_block, tile, d), lambda b, t: (b, t, 0))]
    args = [x]
    if phase_major:
        perm = _phase_major_matrix(min(rows, MXU_TILE), chunk)
        assert rows % perm.shape[0] == 0
        in_specs.append(_const_spec(perm.shape))
        args.append(perm)
    in_specs += [
        bspec((3, d)),
        _const_spec(npre.shape), _const_spec(npost.shape), _const_spec(win.shape),
        _const_spec(cw.shape), _const_spec(cb.shape), _const_spec(wg.shape),
        _const_spec(b_a.shape), _const_spec(b_x.shape), _const_spec(lam.shape), _const_spec(inv_row.shape),
        pl.BlockSpec(memory_space=pltpu.SMEM),
        _const_spec(wout.shape),
    ]
    args += [mod, npre, npost, win, cw, cb, wg, b_a, b_x, lam, inv_row, sinks, wout]
    if has_state:
        in_specs += [bspec((CONV_W - 1, lw)), bspec((1, lw)), bspec((WINDOW, kvw)), bspec((WINDOW, kvw))]
        args += list(state)
    out_specs = [
        pl.BlockSpec((batch_block, tile, d), lambda b, t: (b, t, 0)),
        bspec((CONV_W - 1, lw)), bspec((1, lw)), bspec((k_out_rows, kvw)), bspec((k_out_rows, kvw)),
    ]
    out_shape = [
        jax.ShapeDtypeStruct((bsz, seq, d), F32),
        jax.ShapeDtypeStruct((bsz, CONV_W - 1, lw), F32),
        jax.ShapeDtypeStruct((bsz, 1, lw), F32),
        jax.ShapeDtypeStruct((bsz, k_out_rows, kvw), F32),
        jax.ShapeDtypeStruct((bsz, k_out_rows, kvw), F32),
    ]
    scratch = [
        pltpu.VMEM((rows, d), BF16),
        pltpu.VMEM((rows, win.shape[1]), F32),
        pltpu.VMEM(_conv_history_shape(phase_major, batch_block, tile, lw), F32),
        pltpu.VMEM((tile, lw), F32), pltpu.VMEM((tile, lw), F32), pltpu.VMEM((tile, lw), F32),
        pltpu.VMEM((batch_block, WINDOW + tile, kvw), F32),
        pltpu.VMEM((batch_block, WINDOW + tile, kvw), F32),
        pltpu.VMEM((rows, aw), BF16),
        pltpu.VMEM((rows, lw + aw), BF16),
        pltpu.VMEM((tile, LANES), F32), pltpu.VMEM((tile, LANES), F32),
    ]
    body = functools.partial(_even_body, has_state=has_state, n_past_valid=n_past_valid, pos0=pos0,
                             chunk=chunk, n_tiles=n_tiles, k_out_rows=k_out_rows, phase_major=phase_major)
    return pl.pallas_call(
        body,
        grid=(bsz // batch_block, n_tiles),
        in_specs=in_specs,
        out_specs=out_specs,
        out_shape=out_shape,
        scratch_shapes=scratch,
        compiler_params=pltpu.CompilerParams(
            dimension_semantics=("arbitrary", "arbitrary"), vmem_limit_bytes=VMEM_LIMIT_BYTES),
        name="even_layer_state" if has_state else "even_layer_prompt",
    )(*args)


def _gdn_groups(chunk, n_heads):
    hpg = max(1, min(n_heads, MXU_TILE // chunk))
    assert n_heads % hpg == 0
    return hpg, n_heads // hpg


def _time_of_row(i):
    return (i % SUBLANES) * SUBLANES + i // SUBLANES


def _phase_major_matrix(n_rows, chunk):
    i = jnp.arange(n_rows)
    src = (i // chunk) * chunk + _time_of_row(i % chunk)
    return (src[:, None] == i[None, :]).astype(BF16)


def _conv_phase_major(u_ref, hist_ref, nconv_ref, w_ref, tc, chunk, cols):
    cpt = tc // chunk
    blocks = [u_ref[SUBLANES * b:SUBLANES * (b + 1), cols] for b in range(tc // SUBLANES)]
    taps_w = [w_ref[j:j + 1, cols] for j in range(CONV_W)]
    top_row = lax.broadcasted_iota(jnp.int32, blocks[0].shape, 0) == 0
    first_late = SUBLANES - (CONV_W - 1)
    prev = {r: pltpu.roll(hist_ref[r - first_late, :, cols], 1, axis=0) for r in range(first_late, SUBLANES)}
    out = []
    for c in range(cpt):
        late = {}
        for r in range(first_late, SUBLANES):
            moved = pltpu.roll(blocks[SUBLANES * c + r], 1, axis=0)
            late[r] = jnp.where(top_row, prev[r], moved)
            prev[r] = moved
        for r in range(SUBLANES):
            acc = None
            for j in range(CONV_W):
                s = CONV_W - 1 - j
                tap = blocks[SUBLANES * c + r - s] if r >= s else late[r - s + SUBLANES]
                acc = tap * taps_w[j] if acc is None else acc + tap * taps_w[j]
            out.append(acc)
    for i in range(CONV_W - 1):
        last = blocks[SUBLANES * (cpt - 1) + first_late + i]
        hist_ref[i, :, cols] = last
        nconv_ref[0, i:i + 1, cols] = last[SUBLANES - 1:SUBLANES, :]
    return jnp.concatenate(out, axis=0)


def _odd_body(*refs, has_state, chunk, phase_major):
    refs = list(refs)
    x_ref = refs.pop(0)
    perm_ref = refs.pop(0) if phase_major else None
    (mod_ref, npre_ref, npost_ref, win_ref, wba_ref, cw_ref, alog_ref, dtb_ref, hn_ref, wout_ref) = refs[:10]
    refs = refs[10:]
    if has_state:
        conv0_ref, s0_ref = refs[:2]
        refs = refs[2:]
    (y_ref, nconv_ref, ns_ref,
     hm_ref, u_ref, ext_ref, q_ref, k_ref, kb_ref, kbe_ref, vb_ref, qe_ref, kd_ref,
     beta_ref, gcum_ref, eg_ref, dec_ref, wq_ref, uu_ref, qk_ref, mix_ref) = refs
    bb, tc, _ = x_ref.shape
    n_heads = ns_ref.shape[1]
    kw = n_heads * GDN_DK
    vw = n_heads * GDN_DV
    cch = 2 * kw + vw
    rows_all = bb * tc
    t = pl.program_id(1)
    cpt = tc // chunk
    n_steps = (chunk - 1).bit_length()
    assert n_steps >= 2
    hpg, n_groups = _gdn_groups(chunk, n_heads)
    lw = hpg * chunk

    @pl.when(t == 0)
    def _():
        ext_ref[...] = jnp.zeros(ext_ref.shape, F32)
        if has_state:
            ext_ref[:, CONV_PAD - 3:CONV_PAD, :] = conv0_ref[...]
            ns_ref[...] = s0_ref[...]
        else:
            ns_ref[...] = jnp.zeros(ns_ref.shape, F32)

    def permute_rows(v):
        n = perm_ref.shape[0]
        return jnp.concatenate(
            [_dot(perm_ref[...], v[i * n:(i + 1) * n]).astype(BF16) for i in range(v.shape[0] // n)], axis=0)

    hmod = _pre_norm(x_ref, mod_ref, npre_ref).astype(BF16)
    if phase_major:
        hmod = permute_rows(hmod)
    hm_ref[...] = hmod
    ba = _dot(hmod, wba_ref[...])
    beta_ref[...] = jax.nn.sigmoid(ba)
    g = -jnp.exp(alog_ref[...]) * jax.nn.softplus(ba + dtb_ref[...])
    if phase_major:
        row8 = lax.broadcasted_iota(jnp.int32, (SUBLANES, LANES), 0)
        parts = []
        for c in range(rows_all // chunk):
            run = []
            for r in range(SUBLANES):
                blk = g[c * chunk + r * SUBLANES:c * chunk + (r + 1) * SUBLANES]
                run.append(blk if r == 0 else run[-1] + blk)
            total = run[-1]
            incl = total
            for sft in (1, 2, 4):
                incl = jnp.where(row8 >= sft, incl + pltpu.roll(incl, sft, axis=0), incl)
            parts += [blk + (incl - total) for blk in run]
        g = jnp.concatenate(parts, axis=0)
    else:
        row_in_chunk = lax.broadcasted_iota(jnp.int32, (rows_all, LANES), 0) % chunk
        sft = 1
        while sft < chunk:
            g = jnp.where(row_in_chunk >= sft, g + pltpu.roll(g, sft, axis=0), g)
            sft *= 2
    gcum_ref[...] = g
    eg_ref[...] = jnp.exp(g)
    g3 = g.reshape(rows_all // chunk, chunk, LANES)
    dec_ref[...] = jnp.exp(g3[:, chunk - 1:chunk, :] - g3).reshape(rows_all, LANES)

    def project(cols):
        u_ref[:, cols] = _dot(hm_ref[...], win_ref[:, cols])

    def head_operands(rows, h, qh, kh, vh):
        hc = slice(h * GDN_DK, (h + 1) * GDN_DK)
        qh = qh * (lax.rsqrt(jnp.sum(qh * qh, axis=-1, keepdims=True) + EPS) * (GDN_DK ** -0.5))
        kh = kh * lax.rsqrt(jnp.sum(kh * kh, axis=-1, keepdims=True) + EPS)
        beta = beta_ref[rows, :][:, h:h + 1]
        eg = eg_ref[rows, :][:, n_heads + h:n_heads + h + 1]
        kb = kh * beta
        q_ref[rows, hc] = qh.astype(BF16)
        k_ref[rows, hc] = kh.astype(BF16)
        kb_ref[rows, hc] = kb.astype(BF16)
        kbe_ref[rows, hc] = (kb * eg).astype(BF16)
        vb_ref[rows, hc] = (vh * beta).astype(BF16)
        qe_ref[rows, hc] = (qh * eg).astype(BF16)
        kd_ref[rows, hc] = (kh * dec_ref[rows, :][:, n_heads + h:n_heads + h + 1]).astype(BF16)

    def per_batch(bi):
        rows = _rows(bi, tc)
        ext_ref[bi, CONV_PAD:CONV_PAD + tc, :] = u_ref[rows, 0:cch]
        qkv = _silu(_causal_conv(ext_ref, bi, tc, cw_ref))
        nconv_ref[bi, :, :] = ext_ref[bi, CONV_PAD + tc - 3:CONV_PAD + tc, :]
        ext_ref[bi, CONV_PAD - 3:CONV_PAD, :] = ext_ref[bi, CONV_PAD + tc - 3:CONV_PAD + tc, :]
        for h in range(n_heads):
            head_operands(rows, h, qkv[:, h * GDN_DK:(h + 1) * GDN_DK],
                          qkv[:, kw + h * GDN_DK:kw + (h + 1) * GDN_DK],
                          qkv[:, 2 * kw + h * GDN_DV:2 * kw + (h + 1) * GDN_DV])

    pair = MXU_TILE // GDN_DK

    def pair_stages(j):
        col_sets = [slice(base + j * MXU_TILE, base + (j + 1) * MXU_TILE) for base in (0, kw, 2 * kw)]

        def operands():
            rows = pl.ds(0, tc)
            for h in range(pair * j, pair * (j + 1)):
                qh, kh, vh = [
                    _silu(_conv_phase_major(u_ref, ext_ref, nconv_ref, cw_ref, tc, chunk,
                                            slice(base + h * GDN_DK, base + (h + 1) * GDN_DK)))
                    for base in (0, kw, 2 * kw)]
                head_operands(rows, h, qh, kh, vh)

        return [functools.partial(project, cols) for cols in col_sets] + [operands]

    if not phase_major:
        project(slice(0, cch + vw))
        _for_each(bb, per_batch)

    lane_w = lax.broadcasted_iota(jnp.int32, (chunk, lw), 1)
    row_w = lax.broadcasted_iota(jnp.int32, (chunk, lw), 0)
    blk_w = lane_w // chunk
    col_w = lane_w % chunk
    diag_w = row_w == col_w
    if phase_major:
        row_w, col_w = _time_of_row(row_w), _time_of_row(col_w)
    incl_w = row_w >= col_w
    strict_w = row_w > col_w
    eye_w = jnp.where(diag_w, 1.0, 0.0).astype(F32)
    feat_blk = lax.broadcasted_iota(jnp.int32, (chunk, hpg * GDN_DK), 1) // GDN_DK

    def block_diag(m_b):
        return jnp.concatenate([jnp.where(blk_w == hh, m_b, jnp.zeros_like(m_b)) for hh in range(hpg)], axis=0)

    def head_cols(h):
        return slice(h * GDN_DK, (h + 1) * GDN_DK)

    def wy_stages(segs, head_groups):
        units = []

        def build():
            for seg in segs:
                rows = _rows(seg, chunk)
                gt = gcum_ref[rows, :]
                for gi in head_groups:
                    gcols = slice(gi * hpg * GDN_DK, (gi + 1) * hpg * GDN_DK)
                    k4 = k_ref[rows, gcols]
                    k_bd = jnp.concatenate(
                        [jnp.where(feat_blk == hh, k4, jnp.zeros_like(k4)) for hh in range(hpg)], axis=0)
                    kk = _dot_nt(jnp.concatenate([kb_ref[rows, gcols], q_ref[rows, gcols]], axis=0), k_bd)
                    gcol = jnp.zeros((chunk, lw), F32)
                    for hh in range(hpg):
                        ln = n_heads + gi * hpg + hh
                        gcol = jnp.where(blk_w == hh, gt[:, ln:ln + 1], gcol)
                    grow = jnp.sum(jnp.where(diag_w, gcol, 0.0), axis=0, keepdims=True)
                    decay = jnp.exp(jnp.where(incl_w, gcol - grow, 0.0))
                    p_b = (-(kk[:chunk] * jnp.where(strict_w, decay, 0.0))).astype(BF16)
                    qk_ref[seg * n_groups + gi] = (kk[chunk:] * jnp.where(incl_w, decay, 0.0)).astype(BF16)
                    units.append(dict(seg=seg, rows=rows, gi=gi, p=p_b, t=eye_w + p_b.astype(F32)))

        def square():
            for un in units:
                un["p"] = _dot(un["p"], block_diag(un["p"])).astype(BF16)

        def extend_and_square():
            for un in units:
                out = _dot(jnp.concatenate([un["t"].astype(BF16), un["p"]], axis=0), block_diag(un["p"]))
                un["t"] = un["t"] + out[:chunk]
                un["p"] = out[chunk:].astype(BF16)

        def extend():
            for un in units:
                un["t"] = un["t"] + _dot(un["t"].astype(BF16), block_diag(un["p"]))

        def apply():
            for un in units:
                seg, rows, gi = un["seg"], un["rows"], un["gi"]
                heads = range(gi * hpg, (gi + 1) * hpg)
                rhs = jnp.concatenate(
                    [jnp.concatenate([kbe_ref[rows, head_cols(h)], vb_ref[rows, head_cols(h)]], axis=1)
                     for h in heads], axis=0)
                wu = _dot(block_diag(un["t"].astype(BF16)), rhs)
                for hh, h in enumerate(heads):
                    blk = wu[hh * chunk:(hh + 1) * chunk]
                    wq_ref[seg * n_heads + h] = jnp.concatenate(
                        [blk[:, :GDN_DK].astype(BF16), qe_ref[rows, head_cols(h)]], axis=0)
                    uu_ref[rows, head_cols(h)] = blk[:, GDN_DK:]

        return [build, square] + [extend_and_square] * (n_steps - 2) + [extend, apply]

    zero_b = jnp.zeros((chunk, GDN_DV), BF16)
    state = {(bi, h): ns_ref[bi, h] for bi in range(bb) for h in range(n_heads)}

    def recurrence_stages(segs):
        stages = []
        for c in sorted({seg % cpt for seg in segs}):
            wave = [seg for seg in segs if seg % cpt == c]
            pairs = [(seg, h) for seg in wave for h in range(n_heads)]
            held = {}

            def correct(wave=wave, pairs=pairs, held=held):
                for seg, h in pairs:
                    held["ws_qs", seg, h] = _dot(wq_ref[seg * n_heads + h], state[seg // cpt, h].astype(BF16))
                for seg, h in pairs:
                    held["v", seg, h] = (uu_ref[_rows(seg, chunk), head_cols(h)]
                                         - held["ws_qs", seg, h][:chunk]).astype(BF16)

            def advance(wave=wave, pairs=pairs, held=held):
                for seg in wave:
                    held["eg", seg] = jnp.exp(gcum_ref[pl.ds(seg * chunk + chunk - 1, 1), :])
                for seg, h in pairs:
                    state[seg // cpt, h] = (state[seg // cpt, h] * held["eg", seg][:, n_heads + h:n_heads + h + 1]
                                            + _dot_tn(kd_ref[_rows(seg, chunk), head_cols(h)], held["v", seg, h]))
                for seg in wave:
                    rows = _rows(seg, chunk)
                    for gi in range(n_groups):
                        heads = range(gi * hpg, (gi + 1) * hpg)
                        v_bd = jnp.concatenate(
                            [jnp.concatenate([held["v", seg, h] if j == hh else zero_b for j in range(hpg)], axis=1)
                             for hh, h in enumerate(heads)], axis=0)
                        o = (jnp.concatenate([held["ws_qs", seg, h][chunk:] for h in heads], axis=1)
                             + _dot(qk_ref[seg * n_groups + gi], v_bd))
                        for hh, h in enumerate(heads):
                            gate = _silu(u_ref[rows, cch + h * GDN_DV:cch + (h + 1) * GDN_DV])
                            o_h = _rms(o[:, hh * GDN_DV:(hh + 1) * GDN_DV], hn_ref[...])
                            mix_ref[rows, head_cols(h)] = (o_h * gate).astype(BF16)

            stages += [correct, advance]
        return stages

    def output_stages(rows):
        held = {}

        def gather():
            mixed = mix_ref[rows, :]
            held["mixed"] = permute_rows(mixed) if phase_major else mixed

        def project():
            held["y"] = _dot(held["mixed"], wout_ref[...])

        def finish():
            _post_residual(x_ref, mod_ref, npost_ref, held["y"], y_ref, rows if n_row_groups > 1 else None)

        return [gather, project, finish]

    group_rows = min(rows_all, MXU_TILE)
    n_row_groups = rows_all // group_rows
    assert rows_all % group_rows == 0 and group_rows % chunk == 0 and (bb == 1 or n_row_groups == 1)
    segs_of = [range(g * group_rows // chunk, (g + 1) * group_rows // chunk) for g in range(n_row_groups)]
    if phase_major:
        pairs_per_group = hpg // pair
        assert hpg % pair == 0
        work = [(segs, gi) for gi in range(n_groups) for segs in segs_of]
        ahead = [[stage for j in range(gi * pairs_per_group, (gi + 1) * pairs_per_group) for stage in pair_stages(j)]
                 for gi in range(n_groups)]
        ahead.append([functools.partial(project, slice(cch + i * MXU_TILE, cch + (i + 1) * MXU_TILE))
                      for i in range(vw // MXU_TILE)])
        for stage in ahead[0]:
            stage()
        for gi in range(n_groups):
            factors = [stage for segs, g in work if g == gi for stage in wy_stages(segs, [gi])]
            _run_interleaved([ahead[gi + 1], factors])
    else:
        for segs in segs_of:
            for stage in wy_stages(segs, range(n_groups)):
                stage()
    for segs in segs_of:
        for stage in recurrence_stages(segs):
            stage()
    for g in range(n_row_groups):
        for stage in output_stages(pl.ds(g * group_rows, group_rows)):
            stage()
    for bh, value in state.items():
        ns_ref[bh[0], bh[1]] = value


def _odd_layer(x, mod, consts, state, *, tile, chunk, batch_block):
    bsz, seq, d = x.shape
    (npre, npost, win, wba, cw, alog, dtb, hn, wout) = consts
    cch = cw.shape[1]
    vw = wout.shape[0]
    n_heads = vw // GDN_DV
    n_tiles = seq // tile
    assert seq % tile == 0 and tile % chunk == 0 and bsz % batch_block == 0
    rows = batch_block * tile
    has_state = state is not None
    phase_major = _runs_phase_major(chunk, batch_block, has_state)

    def bspec(shape):
        nd = len(shape)
        return pl.BlockSpec((batch_block,) + shape, lambda b, t: (b,) + (0,) * nd)

    in_specs = [pl.BlockSpec((batch_block, tile, d), lambda b, t: (b, t, 0))]
    args = [x]
    if phase_major:
        perm_rows = min(rows, MXU_TILE)
        assert rows % perm_rows == 0 and perm_rows % chunk == 0
        in_specs.append(_const_spec((perm_rows, perm_rows)))
        args.append(_phase_major_matrix(perm_rows, chunk))
    in_specs += [
        bspec((3, d)),
        _const_spec(npre.shape), _const_spec(npost.shape), _const_spec((d, cch + vw)), _const_spec(wba.shape),
        _const_spec(cw.shape), _const_spec(alog.shape), _const_spec(dtb.shape), _const_spec(hn.shape),
        _const_spec(wout.shape),
    ]
    args += [mod, npre, npost, win, wba, cw, alog, dtb, hn, wout]
    if has_state:
        in_specs += [bspec((CONV_W - 1, cch)), bspec((n_heads, GDN_DK, GDN_DV))]
        args += list(state)
    out_specs = [
        pl.BlockSpec((batch_block, tile, d), lambda b, t: (b, t, 0)),
        bspec((CONV_W - 1, cch)), bspec((n_heads, GDN_DK, GDN_DV)),
    ]
    out_shape = [
        jax.ShapeDtypeStruct((bsz, seq, d), F32),
        jax.ShapeDtypeStruct((bsz, CONV_W - 1, cch), F32),
        jax.ShapeDtypeStruct((bsz, n_heads, GDN_DK, GDN_DV), F32),
    ]
    hpg, n_groups = _gdn_groups(chunk, n_heads)
    n_seg = rows // chunk
    scratch = (
        [pltpu.VMEM((rows, d), BF16),
         pltpu.VMEM((rows, cch + vw), F32),
         pltpu.VMEM(_conv_history_shape(phase_major, batch_block, tile, cch), F32)]
        + [pltpu.VMEM((rows, vw), BF16)] * 7
        + [pltpu.VMEM((rows, LANES), F32)] * 4
        + [pltpu.VMEM((n_seg * n_heads, 2 * chunk, GDN_DK), BF16),
           pltpu.VMEM((rows, vw), F32),
           pltpu.VMEM((n_seg * n_groups, chunk, hpg * chunk), BF16),
           pltpu.VMEM((rows, vw), BF16)]
    )
    body = functools.partial(_odd_body, has_state=has_state, chunk=chunk, phase_major=phase_major)
    return pl.pallas_call(
        body,
        grid=(bsz // batch_block, n_tiles),
        in_specs=in_specs,
        out_specs=out_specs,
        out_shape=out_shape,
        scratch_shapes=scratch,
        compiler_params=pltpu.CompilerParams(
            dimension_semantics=("arbitrary", "arbitrary"), vmem_limit_bytes=VMEM_LIMIT_BYTES),
        name="odd_layer_state" if has_state else "odd_layer_prompt",
    )(*args)


def _gate_weights(w_a, w_x):
    heads, blk, _ = w_a.shape
    per_tile = MXU_TILE // blk
    tiles = []
    for i in range(heads // per_tile):
        sl = slice(i * per_tile, (i + 1) * per_tile)
        tiles.append(jnp.concatenate(
            [jax.scipy.linalg.block_diag(*w_a[sl]), jax.scipy.linalg.block_diag(*w_x[sl])], axis=1))
    return jnp.stack(tiles).astype(BF16)


def _rope_inv_row():
    half = ROT_DIM // 2
    inv = ROPE_THETA ** (-(jnp.arange(half, dtype=F32) * 2.0 / ROT_DIM))
    per_head = jnp.concatenate([inv, inv, jnp.zeros((HEAD_DIM - ROT_DIM,), F32)])
    return jnp.tile(per_head, LANES // HEAD_DIM).reshape(1, LANES)


def kernel(x_prompt, x_sample, state_lru_conv, state_lru_h, cache_swa_k, cache_swa_v, state_gdn_conv, state_gdn_s, c_prompt, c_sample, ev_mod_w, ev_mod_b, ev_norm_pre, ev_norm_post, ev_w_in, lru_conv_w, lru_conv_b, lru_w_a, lru_b_a, lru_w_x, lru_b_x, lru_lambda, swa_sinks, ev_w_out, od_mod_w, od_mod_b, od_norm_pre, od_norm_post, od_w_in, gdn_conv_w, gdn_a_log, gdn_dt_bias, gdn_head_norm, od_w_out):
    bp, seq, d = x_prompt.shape
    bs, dec_seq, _ = x_sample.shape
    lw = lru_conv_w.shape[-1]
    kvw = ATT_KV_HEADS * HEAD_DIM
    n_heads = gdn_a_log.shape[-1]
    cch = gdn_conv_w.shape[-1]
    vw = od_w_out.shape[1]

    n_c = bp + bs
    n_pad = -n_c % 16
    c_all = jnp.concatenate([c_prompt, c_sample, jnp.zeros((n_pad, d), F32)], axis=0)
    mod_ev, mod_od = _modulation(c_all, ev_mod_w[0], ev_mod_b[0], od_mod_w[0], od_mod_b[0])
    mod_ev = mod_ev.reshape(n_c + n_pad, 3, d)
    mod_od = mod_od.reshape(n_c + n_pad, 3, d)

    row = lambda a: a.reshape(1, -1)
    ev_consts = (row(ev_norm_pre[0]), row(ev_norm_post[0]), ev_w_in[0].astype(BF16), lru_conv_w[0],
                 row(lru_conv_b[0]), _gate_weights(lru_w_a[0], lru_w_x[0]), row(lru_b_a[0]), row(lru_b_x[0]),
                 row(lru_lambda[0]), _rope_inv_row(), swa_sinks[0], ev_w_out[0].astype(BF16))
    w_in_od = od_w_in[0]
    pad_lanes = lambda a: jnp.pad(a, ((0, 0), (0, LANES - a.shape[1])))
    head_row = lambda a: jnp.pad(a.reshape(1, -1), ((0, 0), (n_heads, LANES - 2 * n_heads)))
    od_consts = (row(od_norm_pre[0]), row(od_norm_post[0]), w_in_od.astype(BF16),
                 pad_lanes(w_in_od[:, cch + vw:]).astype(BF16), gdn_conv_w[0],
                 head_row(gdn_a_log[0]), head_row(gdn_dt_bias[0]), row(gdn_head_norm[0]),
                 od_w_out[0].astype(BF16))

    tile = min(PROMPT_TILE, seq)
    xp, lru_conv_p, lru_h_p, swa_k_p, swa_v_p = _even_layer(
        x_prompt, mod_ev[:bp], ev_consts, None, tile=tile, chunk=min(CHUNK, seq), batch_block=1,
        n_past_valid=0, pos0=0, k_out_rows=WINDOW)
    ev_state = (state_lru_conv[0], state_lru_h[0].reshape(bs, 1, lw),
                cache_swa_k[0].reshape(bs, WINDOW, kvw), cache_swa_v[0].reshape(bs, WINDOW, kvw))
    xs, lru_conv_s, lru_h_s, swa_k_s, swa_v_s = _even_layer(
        x_sample, mod_ev[bp:n_c], ev_consts, ev_state, tile=dec_seq, chunk=min(CHUNK, dec_seq), batch_block=bs,
        n_past_valid=WINDOW, pos0=PAST_LEN, k_out_rows=dec_seq)

    xp, gdn_conv_p, gdn_s_p = _odd_layer(
        xp, mod_od[:bp], od_consts, None, tile=tile, chunk=min(CHUNK, seq), batch_block=1)
    xs, gdn_conv_s, gdn_s_s = _odd_layer(
        xs, mod_od[bp:n_c], od_consts, (state_gdn_conv[0], state_gdn_s[0]),
        tile=dec_seq, chunk=min(CHUNK, dec_seq), batch_block=bs)

    kv_shape = lambda a: a.reshape(1, a.shape[0], a.shape[1], ATT_KV_HEADS, HEAD_DIM)
    return (xp, xs,
            lru_conv_p[None], lru_conv_s[None],
            lru_h_p.reshape(1, bp, lw), lru_h_s.reshape(1, bs, lw),
            kv_shape(swa_k_p), kv_shape(swa_k_s), kv_shape(swa_v_p), kv_shape(swa_v_s),
            gdn_conv_p[None], gdn_conv_s[None], gdn_s_p[None], gdn_s_s[None])
```
